```python
import jax, jax.numpy as jnp
from jax import lax
import numpy as np

D_MODEL = 1024
BATCH = 16
SEQ = 2048
DEPTH = 4

HEAD_DIM = 64
POOL_WIDTH = D_MODEL // 2
POOL_WINDOWS = (2, 4, 8, 16)
POOL_GROUPS = len(POOL_WINDOWS)
POOL_GROUP_DIM = POOL_WIDTH // POOL_GROUPS
CONV_WIDTH = D_MODEL // 2
CONV_K = 3
EVEN_IN = POOL_WIDTH + 3 * CONV_WIDTH
EVEN_CAT = POOL_WIDTH + CONV_WIDTH
C_HEADS = (D_MODEL // 2) // HEAD_DIM
C_PATTERNS = ((128, 1), (512, 4), (2048, 16))
D_HEADS = (D_MODEL // 2) // HEAD_DIM
D_KV_HEADS = 2
D_GROUP = D_HEADS // D_KV_HEADS
D_RADIUS = 128
C_WIDTH = C_HEADS * HEAD_DIM
D_Q_WIDTH = D_HEADS * HEAD_DIM
D_KV_WIDTH = D_KV_HEADS * HEAD_DIM
ODD_IN = 3 * C_WIDTH + D_Q_WIDTH + 2 * D_KV_WIDTH
ODD_CAT = C_WIDTH + D_Q_WIDTH
N_EXPERTS = 32
TOP_K = 4
D_EXPERT = D_MODEL
SWIGLU_LIMIT = 7.0
SWIGLU_ALPHA = 1.702
MOE_BLOCK = 256
N_EVEN = (DEPTH + 1) // 2
N_ODD = DEPTH // 2
DEEPNORM_ALPHA = (2 * DEPTH) ** 0.25
DEEPNORM_BETA = (8 * DEPTH) ** -0.25
LN_EPS = 1e-5
NEG_INF = -1e30

kernel_name = 'hybrid_pool_conv_dilated_swa_moe_encoder'


def split_cols(h, sizes):
    return jnp.split(h, [int(i) for i in np.cumsum(sizes)[:-1]], axis=-1)


def layer_norm(x, g, b):
    xf = x.astype(jnp.float32)
    mu = jnp.mean(xf, axis=-1, keepdims=True)
    xc = xf - mu
    var = jnp.mean(xc * xc, axis=-1, keepdims=True)
    y = xc * lax.rsqrt(var + LN_EPS) * g.astype(jnp.float32) + b.astype(jnp.float32)
    return y.astype(x.dtype)


def alibi_slopes(n):
    return jnp.asarray([2.0 ** (-8.0 * (i + 1) / n) for i in range(n)], jnp.float32)


def pool_mixer(u, pool_w, pool_scale):
    b_, s_, _ = u.shape
    uf = u.astype(jnp.float32)
    cs = jnp.concatenate([jnp.zeros_like(uf[:, :1]), jnp.cumsum(uf, axis=1)], axis=1)
    t = jnp.arange(s_)
    outs = []
    for g, w in enumerate(POOL_WINDOWS):
        sl = slice(g * POOL_GROUP_DIM, (g + 1) * POOL_GROUP_DIM)
        lo = jnp.clip(t - w // 2, 0, s_)
        hi = jnp.clip(t + w - w // 2, 0, s_)
        csg = cs[..., sl]
        win_sum = jnp.take(csg, hi, axis=1) - jnp.take(csg, lo, axis=1)
        cnt = (hi - lo).astype(jnp.float32)[None, :, None]
        outs.append(win_sum / cnt - uf[..., sl])
    pooled = jnp.stack(outs, axis=2).astype(u.dtype)
    mixed = jnp.einsum('bsgc,gcd->bsgd', pooled, pool_w).reshape(b_, s_, POOL_WIDTH)
    return mixed * pool_scale


def short_conv_mixer(b_gate, c_gate, v, conv_w):
    u = c_gate * v
    s_ = u.shape[1]
    half = CONV_K // 2
    up = jnp.pad(u, ((0, 0), (half, CONV_K - 1 - half), (0, 0)))
    conv = up[:, 0:s_] * conv_w[0]
    for k in range(1, CONV_K):
        conv = conv + up[:, k:k + s_] * conv_w[k]
    return b_gate * conv


def even_mixer(x, w_in, pool_w, pool_scale, conv_w, w_out):
    h = x @ w_in
    a_in, b_gate, c_gate, v = split_cols(h, (POOL_WIDTH, CONV_WIDTH, CONV_WIDTH, CONV_WIDTH))
    ya = pool_mixer(a_in, pool_w, pool_scale)
    yb = short_conv_mixer(b_gate, c_gate, v, conv_w)
    return jnp.concatenate([ya, yb], axis=-1) @ w_out


def banded_attention(q, k, v, radius, slopes, dist_unit):
    n_, l_, hkv, g_, hd = q.shape
    nb = -(-l_ // radius)
    lp = nb * radius
    pad = lp - l_
    qb = jnp.pad(q, ((0, 0), (0, pad), (0, 0), (0, 0), (0, 0))).reshape(n_, nb, radius, hkv, g_, hd)

    def key_span(t):
        tp = jnp.pad(t, ((0, 0), (radius, radius + pad), (0, 0), (0, 0)))
        tb = tp.reshape(n_, nb + 2, radius, hkv, hd)
        return jnp.concatenate([tb[:, :-2], tb[:, 1:-1], tb[:, 2:]], axis=2)

    kb, vb = key_span(k), key_span(v)
    q_pos = jnp.arange(lp).reshape(nb, radius)
    k_pos = jnp.arange(nb)[:, None] * radius - radius + jnp.arange(3 * radius)[None, :]
    dist = jnp.abs(k_pos[:, None, :] - q_pos[:, :, None])
    valid = (dist <= radius) & (k_pos >= 0)[:, None, :] & (k_pos < l_)[:, None, :]
    s = jnp.einsum('nbqhgd,nbkhd->nbhgqk', qb, kb).astype(jnp.float32) * (hd ** -0.5)
    s = s - slopes[None, None, :, :, None, None] * (dist * dist_unit).astype(jnp.float32)[None, :, None, None]
    s = jnp.where(valid[None, :, None, None], s, NEG_INF)
    lse = jax.nn.logsumexp(s, axis=-1)
    p = jnp.exp(s - lse[..., None]).astype(v.dtype)
    o = jnp.einsum('nbhgqk,nbkhd->nbqhgd', p, vb).reshape(n_, lp, hkv, g_, hd)[:, :l_]
    lse = lse.transpose(0, 1, 4, 2, 3).reshape(n_, lp, hkv, g_)[:, :l_]
    return o, lse


def dilated_attention(q, k, v):
    b_, s_, h_, hd = q.shape
    slopes = alibi_slopes(h_)[:, None]
    outs, lses = [], []
    for window, dil in C_PATTERNS:
        radius = window // 2 // dil
        l_ = s_ // dil

        def to_sub(t):
            return t.reshape(b_, l_, dil, h_, hd).transpose(0, 2, 1, 3, 4).reshape(b_ * dil, l_, h_, hd)

        o, lse = banded_attention(to_sub(q)[:, :, :, None], to_sub(k), to_sub(v), radius, slopes, dil)
        outs.append(o.reshape(b_, dil, l_, h_, hd).transpose(0, 2, 1, 3, 4).reshape(b_, s_, h_, hd))
        lses.append(lse.reshape(b_, dil, l_, h_).transpose(0, 2, 1, 3).reshape(b_, s_, h_))
    w = jax.nn.softmax(jnp.stack(lses, axis=0), axis=0).astype(q.dtype)
    return jnp.sum(w[..., None] * jnp.stack(outs, axis=0), axis=0)


def odd_mixer(x, w_in, sink, w_out):
    b_, s_, _ = x.shape
    h = x @ w_in
    qc, kc, vc, qd, kd, vd = split_cols(h, (C_WIDTH, C_WIDTH, C_WIDTH, D_Q_WIDTH, D_KV_WIDTH, D_KV_WIDTH))
    heads = lambda t, n: t.reshape(b_, s_, n, HEAD_DIM)
    yc = dilated_attention(heads(qc, C_HEADS), heads(kc, C_HEADS), heads(vc, C_HEADS)).reshape(b_, s_, C_WIDTH)
    od, lse = banded_attention(qd.reshape(b_, s_, D_KV_HEADS, D_GROUP, HEAD_DIM),
                               heads(kd, D_KV_HEADS), heads(vd, D_KV_HEADS), D_RADIUS,
                               alibi_slopes(D_HEADS).reshape(D_KV_HEADS, D_GROUP), 1)
    sink_gate = jax.nn.sigmoid(lse - sink.astype(jnp.float32).reshape(D_KV_HEADS, D_GROUP))
    yd = (od * sink_gate[..., None].astype(od.dtype)).reshape(b_, s_, D_Q_WIDTH)
    return jnp.concatenate([yc, yd], axis=-1) @ w_out


def moe_ffn(x, router_w, router_b, w_gu, b_gu, w_down, b_down):
    b_, s_, d_ = x.shape
    t_ = b_ * s_
    xt = x.reshape(t_, d_)
    logits = (xt @ router_w).astype(jnp.float32) + router_b.astype(jnp.float32)
    top_logits, top_idx = lax.top_k(logits, TOP_K)
    gates = jax.nn.softmax(top_logits, axis=-1)
    a_ = t_ * TOP_K
    flat_e = top_idx.reshape(a_)
    order = jnp.argsort(flat_e)
    e_sorted = flat_e[order]
    counts = jnp.bincount(flat_e, length=N_EXPERTS)
    padded = (counts + MOE_BLOCK - 1) // MOE_BLOCK * MOE_BLOCK
    pad_end = jnp.cumsum(padded)
    grp_start = jnp.cumsum(counts) - counts
    dest = (pad_end - padded)[e_sorted] + jnp.arange(a_) - grp_start[e_sorted]
    n_blocks = -(-a_ // MOE_BLOCK) + N_EXPERTS
    rows = n_blocks * MOE_BLOCK
    slot_tok = jnp.full((rows,), t_, jnp.int32).at[dest].set((order // TOP_K).astype(jnp.int32))
    slot_gate = jnp.zeros((rows,), jnp.float32).at[dest].set(gates.reshape(a_)[order])
    block_e = jnp.minimum(jnp.searchsorted(pad_end, jnp.arange(n_blocks) * MOE_BLOCK, side='right'), N_EXPERTS - 1)
    x_rows = jnp.concatenate([xt, jnp.zeros((1, d_), xt.dtype)], axis=0)[slot_tok].reshape(n_blocks, MOE_BLOCK, d_)

    def expert_block(args):
        xb, e = args
        h = xb @ w_gu[e] + b_gu[e]
        glu = jnp.minimum(h[:, 0::2], SWIGLU_LIMIT)
        lin = jnp.clip(h[:, 1::2], -SWIGLU_LIMIT, SWIGLU_LIMIT)
        act = glu * jax.nn.sigmoid(SWIGLU_ALPHA * glu) * (lin + 1.0)
        return act @ w_down[e] + b_down[e]

    y = lax.map(expert_block, (x_rows, block_e)).reshape(rows, d_)
    y = y * slot_gate[:, None].astype(y.dtype)
    out = jnp.zeros((t_ + 1, d_), y.dtype).at[slot_tok].add(y)[:t_]
    return out.reshape(b_, s_, d_)


def setup_inputs(seed: int = 0) -> dict:
    key = jax.random.key(seed)
    ks = jax.random.split(key, 17)

    def normal(k, shape, scale):
        return jax.random.normal(k, shape, jnp.float32) * scale

    return {
        'x': normal(ks[0], (BATCH, SEQ, D_MODEL), 1.0),
        'ev_w_in': normal(ks[1], (N_EVEN, D_MODEL, EVEN_IN), D_MODEL ** -0.5),
        'ev_pool_w': normal(ks[2], (N_EVEN, POOL_GROUPS, POOL_GROUP_DIM, POOL_GROUP_DIM), POOL_GROUP_DIM ** -0.5),
        'ev_pool_scale': 1.0 + normal(ks[3], (N_EVEN, POOL_WIDTH), 0.1),
        'ev_conv_w': normal(ks[4], (N_EVEN, CONV_K, CONV_WIDTH), CONV_K ** -0.5),
        'ev_w_out': normal(ks[5], (N_EVEN, EVEN_CAT, D_MODEL), DEEPNORM_BETA * EVEN_CAT ** -0.5),
        'od_w_in': normal(ks[6], (N_ODD, D_MODEL, ODD_IN), D_MODEL ** -0.5),
        'od_sink': normal(ks[7], (N_ODD, D_HEADS), 1.0),
        'od_w_out': normal(ks[8], (N_ODD, ODD_CAT, D_MODEL), DEEPNORM_BETA * ODD_CAT ** -0.5),
        'router_w': normal(ks[9], (DEPTH, D_MODEL, N_EXPERTS), D_MODEL ** -0.5),
        'router_b': normal(ks[10], (DEPTH, N_EXPERTS), 0.01),
        'exp_w_gu': normal(ks[11], (DEPTH, N_EXPERTS, D_MODEL, 2 * D_EXPERT), D_MODEL ** -0.5),
        'exp_b_gu': normal(ks[12], (DEPTH, N_EXPERTS, 2 * D_EXPERT), 0.02),
        'exp_w_down': normal(ks[13], (DEPTH, N_EXPERTS, D_EXPERT, D_MODEL), DEEPNORM_BETA * D_EXPERT ** -0.5),
        'exp_b_down': normal(ks[14], (DEPTH, N_EXPERTS, D_MODEL), 0.02),
        'ln_g': 1.0 + normal(ks[15], (DEPTH, 2, D_MODEL), 0.02),
        'ln_b': normal(ks[16], (DEPTH, 2, D_MODEL), 0.02),
    }


def reference(x, ev_w_in, ev_pool_w, ev_pool_scale, ev_conv_w, ev_w_out, od_w_in, od_sink, od_w_out,
              router_w, router_b, exp_w_gu, exp_b_gu, exp_w_down, exp_b_down, ln_g, ln_b):
    for layer in range(DEPTH):
        i = layer // 2
        if layer % 2 == 0:
            mix = even_mixer(x, ev_w_in[i], ev_pool_w[i], ev_pool_scale[i], ev_conv_w[i], ev_w_out[i])
        else:
            mix = odd_mixer(x, od_w_in[i], od_sink[i], od_w_out[i])
        x = layer_norm(DEEPNORM_ALPHA * x + mix, ln_g[layer, 0], ln_b[layer, 0])
        ffn = moe_ffn(x, router_w[layer], router_b[layer], exp_w_gu[layer], exp_b_gu[layer],
                      exp_w_down[layer], exp_b_down[layer])
        x = layer_norm(DEEPNORM_ALPHA * x + ffn, ln_g[layer, 1], ln_b[layer, 1])
    return x
```

```python
import functools

import jax
import jax.numpy as jnp
from jax import lax
from jax.experimental import pallas as pl
from jax.experimental.pallas import tpu as pltpu

HEAD_DIM = 64
POOL_WINDOWS = (2, 4, 8, 16)
C_PATTERNS = ((128, 1), (512, 4), (2048, 16))
D_KV_HEADS = 2
D_RADIUS = 128
TOP_K = 4
SWIGLU_LIMIT = 7.0
SWIGLU_ALPHA = 1.702
MOE_BLOCK = 256
LN_EPS = 1e-5
NEG_INF = -1e30

ROW_TILE = 512
ATTN_Q_BLOCK = 128
VMEM_LIMIT_BYTES = 56 * 1024 * 1024

F32 = jnp.float32
BF16 = jnp.bfloat16


def _params(n_axes=1):
    return pltpu.CompilerParams(dimension_semantics=("arbitrary",) * n_axes,
                                vmem_limit_bytes=VMEM_LIMIT_BYTES)


def _dot(a, b):
    return jnp.dot(a, b, preferred_element_type=F32)


def _dot_nt(a, b):
    return lax.dot_general(a, b, (((1,), (1,)), ((), ())), preferred_element_type=F32)


def _shift_down(a, k, row):
    return jnp.where(row >= k, pltpu.roll(a, k, axis=0), 0.0)


def _shift_up(a, k, row):
    n = a.shape[0]
    return jnp.where(row < n - k, pltpu.roll(a, n - k, axis=0), 0.0)


def _even_mix_kernel(x_ref, w_in_ref, pool_w_ref, pool_scale_ref, conv_w_ref, cat_ref, xb_ref):
    s = x_ref.shape[1]
    pool_width = pool_scale_ref.shape[1]
    gd = pool_width // len(POOL_WINDOWS)
    conv_width = conv_w_ref.shape[1]
    xb_ref[...] = x_ref[0].astype(BF16)
    xb = xb_ref[...]

    row = lax.broadcasted_iota(jnp.int32, (s, gd), 0)
    for g, w in enumerate(POOL_WINDOWS):
        lo = g * gd
        u = _dot(xb, w_in_ref[:, lo:lo + gd])
        half = w // 2
        back, fwd, span = u, u, 1
        while span < half:
            back = back + _shift_down(back, span, row)
            fwd = fwd + _shift_up(fwd, span, row)
            span *= 2
        win = _shift_down(back, 1, row) + fwd
        cnt = (jnp.minimum(row + (w - half), s) - jnp.maximum(row - half, 0)).astype(F32)
        pooled = win / cnt - u
        mixed = _dot(pooled.astype(BF16), pool_w_ref[g])
        cat_ref[0, :, lo:lo + gd] = (mixed * pool_scale_ref[:, lo:lo + gd]).astype(BF16)

    cw = 256
    rowc = lax.broadcasted_iota(jnp.int32, (s, cw), 0)
    for j in range(conv_width // cw):
        c0 = j * cw
        b_gate = _dot(xb, w_in_ref[:, pool_width + c0:pool_width + c0 + cw])
        c_gate = _dot(xb, w_in_ref[:, pool_width + conv_width + c0:pool_width + conv_width + c0 + cw])
        v = _dot(xb, w_in_ref[:, pool_width + 2 * conv_width + c0:pool_width + 2 * conv_width + c0 + cw])
        u = c_gate * v
        conv = (_shift_down(u, 1, rowc) * conv_w_ref[0:1, c0:c0 + cw] + u * conv_w_ref[1:2, c0:c0 + cw]
                + _shift_up(u, 1, rowc) * conv_w_ref[2:3, c0:c0 + cw])
        cat_ref[0, :, pool_width + c0:pool_width + c0 + cw] = (b_gate * conv).astype(BF16)


def _even_mix(x, w_in, pool_w, pool_scale, conv_w):
    b, s, d = x.shape
    pool_width = pool_scale.shape[0]
    conv_width = conv_w.shape[1]
    assert conv_width % 256 == 0 and w_in.shape[1] == pool_width + 3 * conv_width
    cat_width = pool_width + conv_width
    return pl.pallas_call(
        _even_mix_kernel,
        grid=(b,),
        in_specs=[
            pl.BlockSpec((1, s, d), lambda i: (i, 0, 0)),
            pl.BlockSpec(w_in.shape, lambda i: (0, 0)),
            pl.BlockSpec(pool_w.shape, lambda i: (0, 0, 0)),
            pl.BlockSpec((1, pool_width), lambda i: (0, 0)),
            pl.BlockSpec(conv_w.shape, lambda i: (0, 0)),
        ],
        out_specs=pl.BlockSpec((1, s, cat_width), lambda i: (i, 0, 0)),
        out_shape=jax.ShapeDtypeStruct((b, s, cat_width), BF16),
        scratch_shapes=[pltpu.VMEM((s, d), BF16)],
        compiler_params=_params(1),
        name="even_mix",
    )(x, w_in.astype(BF16), pool_w.astype(BF16), pool_scale.reshape(1, pool_width), conv_w)


def _odd_proj_kernel(x_ref, w_ref, qc_ref, kc_ref, vc_ref, qd_ref, kd_ref, vd_ref):
    h = _dot(x_ref[...].astype(BF16), w_ref[...])
    scale = HEAD_DIM ** -0.5
    c0 = 0
    for ref, sc in ((qc_ref, scale), (kc_ref, 1.0), (vc_ref, 1.0), (qd_ref, scale), (kd_ref, 1.0), (vd_ref, 1.0)):
        wd = ref.shape[1]
        ref[...] = (h[:, c0:c0 + wd] * sc).astype(BF16)
        c0 += wd


def _odd_proj(xt, w_in, c_width, dq_width, dkv_width):
    t, d = xt.shape
    widths = (c_width, c_width, c_width, dq_width, dkv_width, dkv_width)
    assert sum(widths) == w_in.shape[1]
    return pl.pallas_call(
        _odd_proj_kernel,
        grid=(t // ROW_TILE,),
        in_specs=[pl.BlockSpec((ROW_TILE, d), lambda i: (i, 0)),
                  pl.BlockSpec(w_in.shape, lambda i: (0, 0))],
        out_specs=[pl.BlockSpec((ROW_TILE, wd), lambda i: (i, 0)) for wd in widths],
        out_shape=[jax.ShapeDtypeStruct((t, wd), BF16) for wd in widths],
        compiler_params=_params(1),
        name="odd_proj",
    )(xt, w_in.astype(BF16))


def _band_geometry(i, length, radius):
    qb = min(ATTN_Q_BLOCK, length)
    span = min(length, qb + 2 * radius)
    q0 = pl.multiple_of(i * qb, qb)
    align = 8
    for cand in (128, 64, 32, 16):
        if qb % cand == 0 and radius % cand == 0 and (length - span) % cand == 0:
            align = cand
            break
    start = pl.multiple_of(jnp.clip(q0 - radius, 0, length - span), align)
    qpos = q0 + lax.broadcasted_iota(jnp.int32, (qb, span), 0)
    kpos = start + lax.broadcasted_iota(jnp.int32, (qb, span), 1)
    dist = jnp.abs(kpos - qpos)
    mask = jnp.where(dist <= radius, 0.0, NEG_INF).astype(F32)
    return q0, qb, start, span, dist.astype(F32), mask


def _softmax_av(s, vh):
    m = jnp.max(s, axis=-1, keepdims=True)
    p = jnp.exp(s - m)
    l = jnp.sum(p, axis=-1, keepdims=True)
    o = _dot(p.astype(BF16), vh) / l
    return o, m + jnp.log(l)


def _attn_c_kernel(q_ref, k_ref, v_ref, o_ref, lse_ref, *, dil, radius, n_heads):
    length = q_ref.shape[1]
    qb = min(ATTN_Q_BLOCK, length)

    def block(i, carry):
        q0, _, start, span, dist, mask = _band_geometry(i, length, radius)
        dist = dist * float(dil)
        for hh in range(n_heads):
            lo = hh * HEAD_DIM
            qh = q_ref[0, pl.ds(q0, qb), lo:lo + HEAD_DIM]
            kh = k_ref[0, pl.ds(start, span), lo:lo + HEAD_DIM]
            vh = v_ref[0, pl.ds(start, span), lo:lo + HEAD_DIM]
            slope = 2.0 ** (-8.0 * (hh + 1) / n_heads)
            s = _dot_nt(qh, kh) - slope * dist + mask
            o, lse = _softmax_av(s, vh)
            o_ref[0, pl.ds(q0, qb), lo:lo + HEAD_DIM] = o.astype(o_ref.dtype)
            lse_ref[0, pl.ds(q0, qb), lo:lo + HEAD_DIM] = jnp.broadcast_to(lse, (qb, HEAD_DIM))
        return carry

    lax.fori_loop(0, length // qb, block, 0)


def _attn_c(q, k, v, window, dil):
    b, s, w = q.shape
    length = s // dil
    radius = window // 2 // dil
    assert s % dil == 0 and length % min(ATTN_Q_BLOCK, length) == 0
    view = lambda a: a.reshape(b, length, dil * w)
    spec = pl.BlockSpec((1, length, w), lambda i, r: (i, 0, r))
    o, lse = pl.pallas_call(
        functools.partial(_attn_c_kernel, dil=dil, radius=radius, n_heads=w // HEAD_DIM),
        grid=(b, dil),
        in_specs=[spec, spec, spec],
        out_specs=[spec, spec],
        out_shape=[jax.ShapeDtypeStruct((b, length, dil * w), BF16),
                   jax.ShapeDtypeStruct((b, length, dil * w), F32)],
        compiler_params=_params(2),
        name=f"attn_c_d{dil}",
    )(view(q), view(k), view(v))
    return o.reshape(b, s, w), lse.reshape(b, s, w)


def _attn_d_kernel(sink_ref, q_ref, k_ref, v_ref, y_ref, *, radius, n_heads, group):
    length = q_ref.shape[1]
    qb = min(ATTN_Q_BLOCK, length)

    def block(i, carry):
        q0, _, start, span, dist, mask = _band_geometry(i, length, radius)
        for hh in range(n_heads):
            lo = hh * HEAD_DIM
            kv = (hh // group) * HEAD_DIM
            qh = q_ref[0, pl.ds(q0, qb), lo:lo + HEAD_DIM]
            kh = k_ref[0, pl.ds(start, span), kv:kv + HEAD_DIM]
            vh = v_ref[0, pl.ds(start, span), kv:kv + HEAD_DIM]
            slope = 2.0 ** (-8.0 * (hh + 1) / n_heads)
            s = _dot_nt(qh, kh) - slope * dist + mask
            o, lse = _softmax_av(s, vh)
            y_ref[0, pl.ds(q0, qb), lo:lo + HEAD_DIM] = (o * jax.nn.sigmoid(lse - sink_ref[hh])).astype(y_ref.dtype)
        return carry

    lax.fori_loop(0, length // qb, block, 0)


def _attn_d(q, k, v, sink):
    b, s, w = q.shape
    kvw = k.shape[2]
    n_heads = w // HEAD_DIM
    return pl.pallas_call(
        functools.partial(_attn_d_kernel, radius=D_RADIUS, n_heads=n_heads, group=n_heads // D_KV_HEADS),
        grid=(b,),
        in_specs=[pl.BlockSpec(memory_space=pltpu.SMEM),
                  pl.BlockSpec((1, s, w), lambda i: (i, 0, 0)),
                  pl.BlockSpec((1, s, kvw), lambda i: (i, 0, 0)),
                  pl.BlockSpec((1, s, kvw), lambda i: (i, 0, 0))],
        out_specs=pl.BlockSpec((1, s, w), lambda i: (i, 0, 0)),
        out_shape=jax.ShapeDtypeStruct((b, s, w), BF16),
        compiler_params=_params(1),
        name="attn_d",
    )(sink.astype(F32), q, k, v)


def _merge_kernel(o1_ref, o2_ref, o3_ref, l1_ref, l2_ref, l3_ref, yd_ref, cat_ref):
    l1, l2, l3 = l1_ref[...], l2_ref[...], l3_ref[...]
    m = jnp.maximum(jnp.maximum(l1, l2), l3)
    e1, e2, e3 = jnp.exp(l1 - m), jnp.exp(l2 - m), jnp.exp(l3 - m)
    num = e1 * o1_ref[...].astype(F32) + e2 * o2_ref[...].astype(F32) + e3 * o3_ref[...].astype(F32)
    cw = o1_ref.shape[1]
    cat_ref[:, :cw] = (num / (e1 + e2 + e3)).astype(BF16)
    cat_ref[:, cw:] = yd_ref[...]


def _merge(os_, lses, yd):
    t, cw = os_[0].shape
    dw = yd.shape[1]
    cspec = pl.BlockSpec((ROW_TILE, cw), lambda i: (i, 0))
    return pl.pallas_call(
        _merge_kernel,
        grid=(t // ROW_TILE,),
        in_specs=[cspec] * 6 + [pl.BlockSpec((ROW_TILE, dw), lambda i: (i, 0))],
        out_specs=pl.BlockSpec((ROW_TILE, cw + dw), lambda i: (i, 0)),
        out_shape=jax.ShapeDtypeStruct((t, cw + dw), BF16),
        compiler_params=_params(1),
        name="merge",
    )(*os_, *lses, yd)


def _layer_norm(z, g, b):
    mu = jnp.mean(z, axis=-1, keepdims=True)
    zc = z - mu
    var = jnp.mean(zc * zc, axis=-1, keepdims=True)
    return zc * lax.rsqrt(var + LN_EPS) * g + b


def _out_ln_kernel(cat_ref, w_ref, x_ref, g_ref, b_ref, rw_hi_ref, rw_lo_ref, rb_ref,
                   x1_ref, x1b_ref, logit_ref, *, alpha):
    mix = _dot(cat_ref[...], w_ref[...])
    x1 = _layer_norm(alpha * x_ref[...] + mix, g_ref[...], b_ref[...])
    x1_ref[...] = x1
    hi = x1.astype(BF16)
    x1b_ref[...] = hi
    lo = (x1 - hi.astype(F32)).astype(BF16)
    logit_ref[...] = (_dot(hi, rw_hi_ref[...]) + _dot(lo, rw_hi_ref[...]) + _dot(hi, rw_lo_ref[...])
                      + rb_ref[...])


def _out_ln(cat, w_out, xt, g, b, router_w, router_b, alpha):
    t, d = xt.shape
    n_exp = router_w.shape[1]
    rw_hi = router_w.astype(BF16)
    rw_lo = (router_w - rw_hi.astype(F32)).astype(BF16)
    row = lambda wd: pl.BlockSpec((ROW_TILE, wd), lambda i: (i, 0))
    full = lambda a: pl.BlockSpec(a.shape, lambda i: (0,) * a.ndim)
    args = (cat, w_out.astype(BF16), xt, g.reshape(1, d), b.reshape(1, d), rw_hi, rw_lo, router_b.reshape(1, n_exp))
    return pl.pallas_call(
        functools.partial(_out_ln_kernel, alpha=alpha),
        grid=(t // ROW_TILE,),
        in_specs=[row(cat.shape[1]), full(args[1]), row(d)] + [full(a) for a in args[3:]],
        out_specs=[row(d), row(d), row(n_exp)],
        out_shape=[jax.ShapeDtypeStruct((t, d), F32), jax.ShapeDtypeStruct((t, d), BF16),
                   jax.ShapeDtypeStruct((t, n_exp), F32)],
        compiler_params=_params(1),
        name="out_ln",
    )(*args)


def _experts_kernel(block_e_ref, n_used_ref, x_ref, wg_ref, wl_ref, bg_ref, bl_ref, wd_ref, bd_ref, y_ref):
    i = pl.program_id(0)

    @pl.when(i < n_used_ref[0])
    def _():
        xb = x_ref[...]
        glu = jnp.minimum(_dot(xb, wg_ref[0]) + bg_ref[0], SWIGLU_LIMIT)
        lin = jnp.clip(_dot(xb, wl_ref[0]) + bl_ref[0], -SWIGLU_LIMIT, SWIGLU_LIMIT)
        act = glu * jax.nn.sigmoid(SWIGLU_ALPHA * glu) * (lin + 1.0)
        y_ref[...] = (_dot(act.astype(BF16), wd_ref[0]) + bd_ref[0]).astype(y_ref.dtype)

    @pl.when(i >= n_used_ref[0])
    def _():
        y_ref[...] = jnp.zeros_like(y_ref)


def _experts(x_rows, block_e, n_used, w_glu, w_lin, b_glu, b_lin, w_down, b_down):
    rows, d = x_rows.shape
    n_exp, _, de = w_glu.shape
    n_blocks = rows // MOE_BLOCK
    wspec = lambda a: pl.BlockSpec((1,) + a.shape[1:], lambda i, be, nu: (be[i], 0, 0))
    grid_spec = pltpu.PrefetchScalarGridSpec(
        num_scalar_prefetch=2,
        grid=(n_blocks,),
        in_specs=[pl.BlockSpec((MOE_BLOCK, d), lambda i, be, nu: (i, 0)),
                  wspec(w_glu), wspec(w_lin), wspec(b_glu), wspec(b_lin), wspec(w_down), wspec(b_down)],
        out_specs=pl.BlockSpec((MOE_BLOCK, d), lambda i, be, nu: (i, 0)),
    )
    return pl.pallas_call(
        _experts_kernel,
        grid_spec=grid_spec,
        out_shape=jax.ShapeDtypeStruct((rows, d), BF16),
        compiler_params=_params(1),
        name="experts",
    )(block_e, n_used, x_rows, w_glu, w_lin, b_glu, b_lin, w_down, b_down)


def _combine_ln_kernel(x1_ref, yk_ref, gate_ref, g_ref, b_ref, x2_ref, *, alpha):
    gates = gate_ref[...]
    ffn = gates[:, 0:1] * yk_ref[0].astype(F32)
    for k in range(1, yk_ref.shape[0]):
        ffn = ffn + gates[:, k:k + 1] * yk_ref[k].astype(F32)
    x2_ref[...] = _layer_norm(alpha * x1_ref[...] + ffn, g_ref[...], b_ref[...])


def _combine_ln(x1, yk, gates, g, b, alpha):
    t, d = x1.shape
    k = yk.shape[0]
    return pl.pallas_call(
        functools.partial(_combine_ln_kernel, alpha=alpha),
        grid=(t // ROW_TILE,),
        in_specs=[pl.BlockSpec((ROW_TILE, d), lambda i: (i, 0)),
                  pl.BlockSpec((k, ROW_TILE, d), lambda i: (0, i, 0)),
                  pl.BlockSpec((ROW_TILE, k), lambda i: (i, 0)),
                  pl.BlockSpec((1, d), lambda i: (0, 0)),
                  pl.BlockSpec((1, d), lambda i: (0, 0))],
        out_specs=pl.BlockSpec((ROW_TILE, d), lambda i: (i, 0)),
        out_shape=jax.ShapeDtypeStruct((t, d), F32),
        compiler_params=_params(1),
        name="combine_ln",
    )(x1, yk, gates, g.reshape(1, d), b.reshape(1, d))


def _route(logits):
    t, n_exp = logits.shape
    top_logits, top_idx = lax.top_k(logits, TOP_K)
    gates = jax.nn.softmax(top_logits, axis=-1)
    multihot = jnp.sum((top_idx[:, :, None] == jnp.arange(n_exp, dtype=top_idx.dtype)).astype(jnp.int32), axis=1)
    before = jnp.cumsum(multihot, axis=0) - multihot
    rank = jnp.take_along_axis(before, top_idx, axis=1)
    counts = before[-1] + multihot[-1]
    padded = (counts + MOE_BLOCK - 1) // MOE_BLOCK * MOE_BLOCK
    pad_end = jnp.cumsum(padded)
    pos = ((pad_end - padded)[top_idx] + rank).astype(jnp.int32)
    n_blocks = -(-(t * TOP_K) // MOE_BLOCK) + n_exp
    tok = jnp.broadcast_to(jnp.arange(t, dtype=jnp.int32)[:, None], (t, TOP_K))
    slot_tok = jnp.zeros((n_blocks * MOE_BLOCK,), jnp.int32).at[pos.reshape(-1)].set(tok.reshape(-1))
    block_e = jnp.minimum(jnp.searchsorted(pad_end, jnp.arange(n_blocks) * MOE_BLOCK, side="right"),
                          n_exp - 1).astype(jnp.int32)
    n_used = (pad_end[-1:] // MOE_BLOCK).astype(jnp.int32)
    return gates, pos, slot_tok, block_e, n_used


def _moe(x1, x1b, logits, w_glu, w_lin, b_glu, b_lin, w_down, b_down, g, b, alpha):
    t, d = x1.shape
    gates, pos, slot_tok, block_e, n_used = _route(logits)
    x_rows = jnp.take(x1b, slot_tok, axis=0)
    y = _experts(x_rows, block_e, n_used, w_glu, w_lin, b_glu, b_lin, w_down, b_down)
    yk = jnp.take(y, pos.T.reshape(-1), axis=0).reshape(TOP_K, t, d)
    return _combine_ln(x1, yk, gates, g, b, alpha)


def kernel(x, ev_w_in, ev_pool_w, ev_pool_scale, ev_conv_w, ev_w_out, od_w_in, od_sink, od_w_out,
           router_w, router_b, exp_w_gu, exp_b_gu, exp_w_down, exp_b_down, ln_g, ln_b):
    bsz, seq, d = x.shape
    t = bsz * seq
    depth = ln_g.shape[0]
    alpha = (2 * depth) ** 0.25
    n_exp = router_w.shape[2]
    c_width = d // 2
    dq_width = d // 2
    dkv_width = D_KV_HEADS * HEAD_DIM
    assert t % ROW_TILE == 0

    w_glu = exp_w_gu[..., 0::2].astype(BF16)
    w_lin = exp_w_gu[..., 1::2].astype(BF16)
    b_glu = exp_b_gu[..., 0::2].reshape(depth, n_exp, 1, -1)
    b_lin = exp_b_gu[..., 1::2].reshape(depth, n_exp, 1, -1)
    w_down = exp_w_down.astype(BF16)
    b_down = exp_b_down.reshape(depth, n_exp, 1, -1)

    xt = x.reshape(t, d)
    for layer in range(depth):
        i = layer // 2
        if layer % 2 == 0:
            cat = _even_mix(xt.reshape(bsz, seq, d), ev_w_in[i], ev_pool_w[i], ev_pool_scale[i], ev_conv_w[i])
            cat = cat.reshape(t, -1)
            w_out = ev_w_out[i]
        else:
            qc, kc, vc, qd, kd, vd = _odd_proj(xt, od_w_in[i], c_width, dq_width, dkv_width)
            seq3 = lambda a: a.reshape(bsz, seq, a.shape[1])
            os_, lses = [], []
            for window, dil in C_PATTERNS:
                o, lse = _attn_c(seq3(qc), seq3(kc), seq3(vc), window, dil)
                os_.append(o.reshape(t, c_width))
                lses.append(lse.reshape(t, c_width))
            yd = _attn_d(seq3(qd), seq3(kd), seq3(vd), od_sink[i]).reshape(t, dq_width)
            cat = _merge(os_, lses, yd)
            w_out = od_w_out[i]
        x1, x1b, logits = _out_ln(cat, w_out, xt, ln_g[layer, 0], ln_b[layer, 0],
                                  router_w[layer], router_b[layer], alpha)
        xt = _moe(x1, x1b, logits, w_glu[layer], w_lin[layer], b_glu[layer], b_lin[layer],
                  w_down[layer], b_down[layer], ln_g[layer, 1], ln_b[layer, 1], alpha)
    return xt.reshape(bsz, seq, d)
```

```python
import functools

import jax
import jax.numpy as jnp
from jax import lax
from jax.experimental import pallas as pl
from jax.experimental.pallas import tpu as pltpu

HEAD_DIM = 64
POOL_WINDOWS = (2, 4, 8, 16)
C_PATTERNS = ((128, 1), (512, 4), (2048, 16))
D_KV_HEADS = 2
D_RADIUS = 128
TOP_K = 4
SWIGLU_LIMIT = 7.0
SWIGLU_ALPHA = 1.702
MOE_BLOCK = 256
LN_EPS = 1e-5
NEG_INF = -1e30

ROW_TILE = 512
ATTN_Q_BLOCK = 128
LANES = 128
GU_BLOCK = 256
VMEM_LIMIT_BYTES = 56 * 1024 * 1024

F32 = jnp.float32
BF16 = jnp.bfloat16


def _params(n_axes=1):
    return pltpu.CompilerParams(dimension_semantics=("arbitrary",) * n_axes,
                                vmem_limit_bytes=VMEM_LIMIT_BYTES)


def _dot(a, b):
    return jnp.dot(a, b, preferred_element_type=F32)


def _dot_nt(a, b):
    return lax.dot_general(a, b, (((1,), (1,)), ((), ())), preferred_element_type=F32)


def _shift_down(a, k, row):
    return jnp.where(row >= k, pltpu.roll(a, k, axis=0), 0.0)


def _shift_up(a, k, row):
    n = a.shape[0]
    return jnp.where(row < n - k, pltpu.roll(a, n - k, axis=0), 0.0)


def _even_mix_kernel(x_ref, w_in_ref, pool_w_ref, pool_scale_ref, conv_w_ref, cat_ref, xb_ref):
    s = x_ref.shape[1]
    pool_width = pool_scale_ref.shape[1]
    gd = pool_width // len(POOL_WINDOWS)
    conv_width = conv_w_ref.shape[1]
    xb_ref[...] = x_ref[0].astype(BF16)
    xb = xb_ref[...]

    row = lax.broadcasted_iota(jnp.int32, (s, gd), 0)
    for g, w in enumerate(POOL_WINDOWS):
        lo = g * gd
        u = _dot(xb, w_in_ref[:, lo:lo + gd])
        half = w // 2
        back, fwd, span = u, u, 1
        while span < half:
            back = back + _shift_down(back, span, row)
            fwd = fwd + _shift_up(fwd, span, row)
            span *= 2
        win = _shift_down(back, 1, row) + fwd
        cnt = (jnp.minimum(row + (w - half), s) - jnp.maximum(row - half, 0)).astype(F32)
        pooled = win / cnt - u
        mixed = _dot(pooled.astype(BF16), pool_w_ref[g])
        cat_ref[0, :, lo:lo + gd] = (mixed * pool_scale_ref[:, lo:lo + gd]).astype(BF16)

    cw = 256
    rowc = lax.broadcasted_iota(jnp.int32, (s, cw), 0)
    for j in range(conv_width // cw):
        c0 = j * cw
        b_gate = _dot(xb, w_in_ref[:, pool_width + c0:pool_width + c0 + cw])
        c_gate = _dot(xb, w_in_ref[:, pool_width + conv_width + c0:pool_width + conv_width + c0 + cw])
        v = _dot(xb, w_in_ref[:, pool_width + 2 * conv_width + c0:pool_width + 2 * conv_width + c0 + cw])
        u = c_gate * v
        conv = (_shift_down(u, 1, rowc) * conv_w_ref[0:1, c0:c0 + cw] + u * conv_w_ref[1:2, c0:c0 + cw]
                + _shift_up(u, 1, rowc) * conv_w_ref[2:3, c0:c0 + cw])
        cat_ref[0, :, pool_width + c0:pool_width + c0 + cw] = (b_gate * conv).astype(BF16)


def _even_mix(x, w_in, pool_w, pool_scale, conv_w):
    b, s, d = x.shape
    pool_width = pool_scale.shape[0]
    conv_width = conv_w.shape[1]
    assert conv_width % 256 == 0 and w_in.shape[1] == pool_width + 3 * conv_width
    cat_width = pool_width + conv_width
    return pl.pallas_call(
        _even_mix_kernel,
        grid=(b,),
        in_specs=[
            pl.BlockSpec((1, s, d), lambda i: (i, 0, 0)),
            pl.BlockSpec(w_in.shape, lambda i: (0, 0)),
            pl.BlockSpec(pool_w.shape, lambda i: (0, 0, 0)),
            pl.BlockSpec((1, pool_width), lambda i: (0, 0)),
            pl.BlockSpec(conv_w.shape, lambda i: (0, 0)),
        ],
        out_specs=pl.BlockSpec((1, s, cat_width), lambda i: (i, 0, 0)),
        out_shape=jax.ShapeDtypeStruct((b, s, cat_width), BF16),
        scratch_shapes=[pltpu.VMEM((s, d), BF16)],
        compiler_params=_params(1),
        name="even_mix",
    )(x, w_in.astype(BF16), pool_w.astype(BF16), pool_scale.reshape(1, pool_width), conv_w)


def _odd_proj_kernel(x_ref, w_ref, *refs, c_width, dils):
    outs, h_ref = refs[:-1], refs[-1]
    h = _dot(x_ref[...].astype(BF16), w_ref[...])
    for c in range(h_ref.shape[0]):
        h_ref[c] = h[:, c * LANES:(c + 1) * LANES]
    tm = x_ref.shape[0]
    chunks = c_width // LANES
    for pi, dil in enumerate(dils):
        n = tm // dil
        for r in range(dil):
            for j in range(3):
                for c in range(chunks):
                    src = h_ref[j * chunks + c] if dil == 1 else h_ref[j * chunks + c, pl.ds(r, n, stride=dil), :]
                    lo = r * c_width + c * LANES
                    outs[3 * pi + j][:, lo:lo + LANES] = src.astype(BF16)
    c0 = 3 * c_width
    for ref in outs[3 * len(dils):]:
        wd = ref.shape[1]
        ref[...] = h[:, c0:c0 + wd].astype(BF16)
        c0 += wd


def _odd_proj(xt, w_in, c_width, dq_width, dkv_width):
    t, d = xt.shape
    dils = tuple(dil for _, dil in C_PATTERNS)
    assert 3 * c_width + dq_width + 2 * dkv_width == w_in.shape[1] and c_width % LANES == 0
    col = jnp.arange(w_in.shape[1])
    is_q = (col < c_width) | ((col >= 3 * c_width) & (col < 3 * c_width + dq_width))
    w_scaled = (w_in * jnp.where(is_q, HEAD_DIM ** -0.5, 1.0)).astype(BF16)
    shapes = [(t // dil, dil * c_width) for dil in dils for _ in range(3)]
    blocks = [(ROW_TILE // dil, dil * c_width) for dil in dils for _ in range(3)]
    for wd in (dq_width, dkv_width, dkv_width):
        shapes.append((t, wd))
        blocks.append((ROW_TILE, wd))
    outs = pl.pallas_call(
        functools.partial(_odd_proj_kernel, c_width=c_width, dils=dils),
        grid=(t // ROW_TILE,),
        in_specs=[pl.BlockSpec((ROW_TILE, d), lambda i: (i, 0)),
                  pl.BlockSpec(w_in.shape, lambda i: (0, 0))],
        out_specs=[pl.BlockSpec(blk, lambda i: (i, 0)) for blk in blocks],
        out_shape=[jax.ShapeDtypeStruct(shp, BF16) for shp in shapes],
        scratch_shapes=[pltpu.VMEM((w_in.shape[1] // LANES, ROW_TILE, LANES), F32)],
        compiler_params=_params(1),
        name="odd_proj",
    )(xt, w_scaled)
    qkv = [outs[3 * pi:3 * pi + 3] for pi in range(len(dils))]
    return qkv, outs[3 * len(dils):]


def _band_geometry(i, length, radius):
    qb = min(ATTN_Q_BLOCK, length)
    span = min(length, qb + 2 * radius)
    q0 = pl.multiple_of(i * qb, qb)
    align = 8
    for cand in (128, 64, 32, 16):
        if qb % cand == 0 and radius % cand == 0 and (length - span) % cand == 0:
            align = cand
            break
    start = pl.multiple_of(jnp.clip(q0 - radius, 0, length - span), align)
    qpos = q0 + lax.broadcasted_iota(jnp.int32, (qb, span), 0)
    kpos = start + lax.broadcasted_iota(jnp.int32, (qb, span), 1)
    dist = jnp.abs(kpos - qpos)
    mask = jnp.where(dist <= radius, 0.0, NEG_INF).astype(F32)
    return q0, qb, start, span, dist.astype(F32), mask


def _softmax_av(s, vh):
    m = jnp.max(s, axis=-1, keepdims=True)
    p = jnp.exp(s - m)
    l = jnp.sum(p, axis=-1, keepdims=True)
    o = _dot(p.astype(BF16), vh) / l
    return o, m + jnp.log(l)


def _attn_c_kernel(q_ref, k_ref, v_ref, o_ref, lse_ref, *, dil, radius, n_heads):
    length = q_ref.shape[1]
    qb = min(ATTN_Q_BLOCK, length)

    def block(i, carry):
        q0, _, start, span, dist, mask = _band_geometry(i, length, radius)
        dist = dist * float(dil)
        for hh in range(n_heads):
            lo = hh * HEAD_DIM
            qh = q_ref[0, pl.ds(q0, qb), lo:lo + HEAD_DIM]
            kh = k_ref[0, pl.ds(start, span), lo:lo + HEAD_DIM]
            vh = v_ref[0, pl.ds(start, span), lo:lo + HEAD_DIM]
            slope = 2.0 ** (-8.0 * (hh + 1) / n_heads)
            s = _dot_nt(qh, kh) - slope * dist + mask
            o, lse = _softmax_av(s, vh)
            o_ref[0, pl.ds(q0, qb), lo:lo + HEAD_DIM] = o.astype(o_ref.dtype)
            lse_ref[0, pl.ds(q0, qb), lo:lo + HEAD_DIM] = jnp.broadcast_to(lse, (qb, HEAD_DIM))
        return carry

    lax.fori_loop(0, length // qb, block, 0)


def _attn_c(q, k, v, bsz, window, dil):
    rows, dw = q.shape
    w = dw // dil
    length = rows // bsz
    radius = window // 2 // dil
    assert length % min(ATTN_Q_BLOCK, length) == 0
    view = lambda a: a.reshape(bsz, length, dw)
    spec = pl.BlockSpec((1, length, w), lambda i, r: (i, 0, r))
    o, lse = pl.pallas_call(
        functools.partial(_attn_c_kernel, dil=dil, radius=radius, n_heads=w // HEAD_DIM),
        grid=(bsz, dil),
        in_specs=[spec, spec, spec],
        out_specs=[spec, spec],
        out_shape=[jax.ShapeDtypeStruct((bsz, length, dw), BF16),
                   jax.ShapeDtypeStruct((bsz, length, dw), F32)],
        compiler_params=_params(2),
        name=f"attn_c_d{dil}",
    )(view(q), view(k), view(v))
    return o.reshape(rows, dw), lse.reshape(rows, dw)


def _attn_d_kernel(sink_ref, q_ref, k_ref, v_ref, y_ref, *, radius, n_heads, group):
    length = q_ref.shape[1]
    qb = min(ATTN_Q_BLOCK, length)

    def block(i, carry):
        q0, _, start, span, dist, mask = _band_geometry(i, length, radius)
        for hh in range(n_heads):
            lo = hh * HEAD_DIM
            kv = (hh // group) * HEAD_DIM
            qh = q_ref[0, pl.ds(q0, qb), lo:lo + HEAD_DIM]
            kh = k_ref[0, pl.ds(start, span), kv:kv + HEAD_DIM]
            vh = v_ref[0, pl.ds(start, span), kv:kv + HEAD_DIM]
            slope = 2.0 ** (-8.0 * (hh + 1) / n_heads)
            s = _dot_nt(qh, kh) - slope * dist + mask
            o, lse = _softmax_av(s, vh)
            y_ref[0, pl.ds(q0, qb), lo:lo + HEAD_DIM] = (o * jax.nn.sigmoid(lse - sink_ref[hh])).astype(y_ref.dtype)
        return carry

    lax.fori_loop(0, length // qb, block, 0)


def _attn_d(q, k, v, sink):
    b, s, w = q.shape
    kvw = k.shape[2]
    n_heads = w // HEAD_DIM
    return pl.pallas_call(
        functools.partial(_attn_d_kernel, radius=D_RADIUS, n_heads=n_heads, group=n_heads // D_KV_HEADS),
        grid=(b,),
        in_specs=[pl.BlockSpec(memory_space=pltpu.SMEM),
                  pl.BlockSpec((1, s, w), lambda i: (i, 0, 0)),
                  pl.BlockSpec((1, s, kvw), lambda i: (i, 0, 0)),
                  pl.BlockSpec((1, s, kvw), lambda i: (i, 0, 0))],
        out_specs=pl.BlockSpec((1, s, w), lambda i: (i, 0, 0)),
        out_shape=jax.ShapeDtypeStruct((b, s, w), BF16),
        compiler_params=_params(1),
        name="attn_d",
    )(sink.astype(F32), q, k, v)


def _merge_kernel(*refs, dils, cw):
    n_pat = len(dils)
    o_refs, l_refs = refs[:n_pat], refs[n_pat:2 * n_pat]
    yd_ref, cat_ref = refs[2 * n_pat], refs[2 * n_pat + 1]
    scratch = list(refs[2 * n_pat + 2:])
    tm = cat_ref.shape[0]
    chunks = cw // LANES
    staged = []
    for p, dil in enumerate(dils):
        if dil == 1:
            staged.append(None)
            continue
        o_s, l_s = scratch.pop(0), scratch.pop(0)
        n = tm // dil
        for r in range(dil):
            for c in range(chunks):
                lo = r * cw + c * LANES
                o_s[c, pl.ds(r, n, stride=dil), :] = o_refs[p][:, lo:lo + LANES].astype(F32)
                l_s[c, pl.ds(r, n, stride=dil), :] = l_refs[p][:, lo:lo + LANES]
        staged.append((o_s, l_s))
    for c in range(chunks):
        cols = slice(c * LANES, (c + 1) * LANES)
        os_ = [o_refs[p][:, cols].astype(F32) if st is None else st[0][c] for p, st in enumerate(staged)]
        ls_ = [l_refs[p][:, cols] if st is None else st[1][c] for p, st in enumerate(staged)]
        m = functools.reduce(jnp.maximum, ls_)
        es = [jnp.exp(l - m) for l in ls_]
        num = functools.reduce(lambda a, b: a + b, [e * o for e, o in zip(es, os_)])
        den = functools.reduce(lambda a, b: a + b, es)
        cat_ref[:, cols] = (num / den).astype(BF16)
    cat_ref[:, cw:] = yd_ref[...]


def _merge(os_, lses, yd, dils):
    t, dw = yd.shape
    cw = os_[0].shape[1] // dils[0]
    vspecs = [pl.BlockSpec((ROW_TILE // dil, dil * cw), lambda i: (i, 0)) for dil in dils]
    n_scratch = 2 * sum(1 for dil in dils if dil != 1)
    return pl.pallas_call(
        functools.partial(_merge_kernel, dils=dils, cw=cw),
        grid=(t // ROW_TILE,),
        in_specs=vspecs + vspecs + [pl.BlockSpec((ROW_TILE, dw), lambda i: (i, 0))],
        out_specs=pl.BlockSpec((ROW_TILE, cw + dw), lambda i: (i, 0)),
        out_shape=jax.ShapeDtypeStruct((t, cw + dw), BF16),
        scratch_shapes=[pltpu.VMEM((cw // LANES, ROW_TILE, LANES), F32)] * n_scratch,
        compiler_params=_params(1),
        name="merge",
    )(*os_, *lses, yd)


def _layer_norm(z, g, b):
    mu = jnp.mean(z, axis=-1, keepdims=True)
    zc = z - mu
    var = jnp.mean(zc * zc, axis=-1, keepdims=True)
    return zc * lax.rsqrt(var + LN_EPS) * g + b


def _route_tile(logits, tri, count_ref):
    tm, n_exp = logits.shape
    lane = lax.broadcasted_iota(jnp.int32, (tm, n_exp), 1)
    col = lax.broadcasted_iota(jnp.int32, (tm, TOP_K), 1)
    work = logits
    vals, onehots = [], []
    idx_out = jnp.zeros((tm, TOP_K), jnp.int32)
    for k in range(TOP_K):
        m = jnp.max(work, axis=-1, keepdims=True)
        idx = jnp.min(jnp.where(work == m, lane, n_exp), axis=-1, keepdims=True)
        hot = lane == idx
        work = jnp.where(hot, -jnp.inf, work)
        vals.append(m)
        onehots.append(hot)
        idx_out = jnp.where(col == k, idx, idx_out)
    exps = [jnp.exp(v - vals[0]) for v in vals]
    den = functools.reduce(lambda a, b: a + b, exps)
    gate_out = jnp.zeros((tm, TOP_K), F32)
    for k in range(TOP_K):
        gate_out = jnp.where(col == k, exps[k] / den, gate_out)
    multihot = functools.reduce(lambda a, b: a + b, [h.astype(F32) for h in onehots])
    before = _dot(tri, multihot.astype(BF16)) + count_ref[...]
    rank_out = jnp.zeros((tm, TOP_K), jnp.int32)
    for k in range(TOP_K):
        rank = jnp.sum(jnp.where(onehots[k], before, 0.0), axis=-1, keepdims=True)
        rank_out = jnp.where(col == k, rank.astype(jnp.int32), rank_out)
    count_ref[...] += jnp.sum(multihot, axis=0, keepdims=True)
    return idx_out, gate_out, rank_out


def _out_ln_kernel(cat_ref, w_ref, x_ref, g_ref, b_ref, rw_hi_ref, rw_lo_ref, rb_ref, tri_ref,
                   x1_ref, x1b_ref, idx_ref, gate_ref, rank_ref, count_ref, *, alpha):
    @pl.when(pl.program_id(0) == 0)
    def _():
        count_ref[...] = jnp.zeros_like(count_ref)

    mix = _dot(cat_ref[...], w_ref[...])
    x1 = _layer_norm(alpha * x_ref[...] + mix, g_ref[...], b_ref[...])
    x1_ref[...] = x1
    hi = x1.astype(BF16)
    x1b_ref[...] = hi
    lo = (x1 - hi.astype(F32)).astype(BF16)
    logits = (_dot(hi, rw_hi_ref[...]) + _dot(lo, rw_hi_ref[...]) + _dot(hi, rw_lo_ref[...]) + rb_ref[...])
    idx_ref[...], gate_ref[...], rank_ref[...] = _route_tile(logits, tri_ref[...], count_ref)


def _out_ln(cat, w_out, xt, g, b, router_w, router_b, alpha):
    t, d = xt.shape
    n_exp = router_w.shape[1]
    rw_hi = router_w.astype(BF16)
    rw_lo = (router_w - rw_hi.astype(F32)).astype(BF16)
    tri = jnp.tril(jnp.ones((ROW_TILE, ROW_TILE), BF16), -1)
    row = lambda wd: pl.BlockSpec((ROW_TILE, wd), lambda i: (i, 0))
    full = lambda a: pl.BlockSpec(a.shape, lambda i: (0,) * a.ndim)
    args = (cat, w_out.astype(BF16), xt, g.reshape(1, d), b.reshape(1, d), rw_hi, rw_lo,
            router_b.reshape(1, n_exp), tri)
    return pl.pallas_call(
        functools.partial(_out_ln_kernel, alpha=alpha),
        grid=(t // ROW_TILE,),
        in_specs=[row(cat.shape[1]), full(args[1]), row(d)] + [full(a) for a in args[3:]],
        out_specs=[row(d), row(d), row(TOP_K), row(TOP_K), row(TOP_K),
                   pl.BlockSpec((1, n_exp), lambda i: (0, 0))],
        out_shape=[jax.ShapeDtypeStruct((t, d), F32), jax.ShapeDtypeStruct((t, d), BF16),
                   jax.ShapeDtypeStruct((t, TOP_K), jnp.int32), jax.ShapeDtypeStruct((t, TOP_K), F32),
                   jax.ShapeDtypeStruct((t, TOP_K), jnp.int32), jax.ShapeDtypeStruct((1, n_exp), F32)],
        compiler_params=_params(1),
        name="out_ln",
    )(*args)


def _experts_kernel(block_e_ref, n_used_ref, x_ref, wgu_ref, bgu_ref, wd_ref, bd_ref, perm_ref, y_ref,
                    wgu_s, wd_s):
    i = pl.program_id(0)
    half = GU_BLOCK // 2

    @pl.when((i == 0) | (block_e_ref[i] != block_e_ref[jnp.maximum(i - 1, 0)]))
    def _():
        for c in range(wgu_ref.shape[2] // GU_BLOCK):
            cols = slice(c * GU_BLOCK, (c + 1) * GU_BLOCK)
            wgu_s[:, cols] = _dot(wgu_ref[0, :, cols].astype(BF16), perm_ref[...]).astype(BF16)
        wd_s[...] = wd_ref[0].astype(BF16)

    @pl.when(i < n_used_ref[0])
    def _():
        h = _dot(x_ref[...], wgu_s[...]) + bgu_ref[0]
        acts = []
        for c in range(h.shape[1] // GU_BLOCK):
            glu = jnp.minimum(h[:, c * GU_BLOCK:c * GU_BLOCK + half], SWIGLU_LIMIT)
            lin = jnp.clip(h[:, c * GU_BLOCK + half:(c + 1) * GU_BLOCK], -SWIGLU_LIMIT, SWIGLU_LIMIT)
            acts.append((glu * jax.nn.sigmoid(SWIGLU_ALPHA * glu) * (lin + 1.0)).astype(BF16))
        act = jnp.concatenate(acts, axis=1)
        y_ref[...] = (_dot(act, wd_s[...]) + bd_ref[0]).astype(y_ref.dtype)

    @pl.when(i >= n_used_ref[0])
    def _():
        y_ref[...] = jnp.zeros_like(y_ref)


def _experts(x_rows, block_e, n_used, w_gu, b_gu, w_down, b_down):
    rows, d = x_rows.shape
    n_exp, _, de2 = w_gu.shape
    assert de2 % GU_BLOCK == 0
    n_blocks = rows // MOE_BLOCK
    half = GU_BLOCK // 2
    j = jnp.arange(GU_BLOCK)
    src = jnp.where(j < half, 2 * j, 2 * (j - half) + 1)
    perm = (jnp.arange(GU_BLOCK)[:, None] == src[None, :]).astype(BF16)
    wspec = lambda a: pl.BlockSpec((1,) + a.shape[1:], lambda i, be, nu: (be[i], 0, 0))
    grid_spec = pltpu.PrefetchScalarGridSpec(
        num_scalar_prefetch=2,
        grid=(n_blocks,),
        in_specs=[pl.BlockSpec((MOE_BLOCK, d), lambda i, be, nu: (i, 0)),
                  wspec(w_gu), wspec(b_gu), wspec(w_down), wspec(b_down),
                  pl.BlockSpec((GU_BLOCK, GU_BLOCK), lambda i, be, nu: (0, 0))],
        out_specs=pl.BlockSpec((MOE_BLOCK, d), lambda i, be, nu: (i, 0)),
        scratch_shapes=[pltpu.VMEM(w_gu.shape[1:], BF16), pltpu.VMEM(w_down.shape[1:], BF16)],
    )
    return pl.pallas_call(
        _experts_kernel,
        grid_spec=grid_spec,
        out_shape=jax.ShapeDtypeStruct((rows, d), BF16),
        compiler_params=_params(1),
        name="experts",
    )(block_e, n_used, x_rows, w_gu, b_gu, w_down, b_down, perm)


def _combine_ln_kernel(x1_ref, yk_ref, gate_ref, g_ref, b_ref, x2_ref, *, alpha):
    gates = gate_ref[...]
    ffn = gates[:, 0:1] * yk_ref[0].astype(F32)
    for k in range(1, yk_ref.shape[0]):
        ffn = ffn + gates[:, k:k + 1] * yk_ref[k].astype(F32)
    x2_ref[...] = _layer_norm(alpha * x1_ref[...] + ffn, g_ref[...], b_ref[...])


def _combine_ln(x1, yk, gates, g, b, alpha):
    t, d = x1.shape
    k = yk.shape[0]
    return pl.pallas_call(
        functools.partial(_combine_ln_kernel, alpha=alpha),
        grid=(t // ROW_TILE,),
        in_specs=[pl.BlockSpec((ROW_TILE, d), lambda i: (i, 0)),
                  pl.BlockSpec((k, ROW_TILE, d), lambda i: (0, i, 0)),
                  pl.BlockSpec((ROW_TILE, k), lambda i: (i, 0)),
                  pl.BlockSpec((1, d), lambda i: (0, 0)),
                  pl.BlockSpec((1, d), lambda i: (0, 0))],
        out_specs=pl.BlockSpec((ROW_TILE, d), lambda i: (i, 0)),
        out_shape=jax.ShapeDtypeStruct((t, d), F32),
        compiler_params=_params(1),
        name="combine_ln",
    )(x1, yk, gates, g.reshape(1, d), b.reshape(1, d))


def _route(top_idx, rank, counts):
    t = top_idx.shape[0]
    n_exp = counts.shape[1]
    counts = counts.reshape(n_exp).astype(jnp.int32)
    padded = (counts + MOE_BLOCK - 1) // MOE_BLOCK * MOE_BLOCK
    pad_end = jnp.cumsum(padded)
    pad_start = pad_end - padded
    onehot = top_idx[:, :, None] == jnp.arange(n_exp, dtype=jnp.int32)
    pos = rank + jnp.sum(jnp.where(onehot, pad_start, 0), axis=-1)
    n_blocks = -(-(t * TOP_K) // MOE_BLOCK) + n_exp
    tok = jnp.broadcast_to(jnp.arange(t, dtype=jnp.int32)[:, None], (t, TOP_K))
    slot_tok = jnp.zeros((n_blocks * MOE_BLOCK,), jnp.int32).at[pos.reshape(-1)].set(tok.reshape(-1))
    block_e = jnp.minimum(jnp.searchsorted(pad_end, jnp.arange(n_blocks) * MOE_BLOCK, side="right"),
                          n_exp - 1).astype(jnp.int32)
    n_used = (pad_end[-1:] // MOE_BLOCK).astype(jnp.int32)
    return pos, slot_tok, block_e, n_used


def _moe(x1, x1b, top_idx, gates, rank, counts, w_gu, b_gu, w_down, b_down, g, b, alpha):
    t, d = x1.shape
    pos, slot_tok, block_e, n_used = _route(top_idx, rank, counts)
    x_rows = jnp.take(x1b, slot_tok, axis=0)
    y = _experts(x_rows, block_e, n_used, w_gu, b_gu, w_down, b_down)
    yk = jnp.take(y, pos.T.reshape(-1), axis=0).reshape(TOP_K, t, d)
    return _combine_ln(x1, yk, gates, g, b, alpha)


def kernel(x, ev_w_in, ev_pool_w, ev_pool_scale, ev_conv_w, ev_w_out, od_w_in, od_sink, od_w_out,
           router_w, router_b, exp_w_gu, exp_b_gu, exp_w_down, exp_b_down, ln_g, ln_b):
    bsz, seq, d = x.shape
    t = bsz * seq
    depth = ln_g.shape[0]
    alpha = (2 * depth) ** 0.25
    n_exp = router_w.shape[2]
    c_width = d // 2
    dq_width = d // 2
    dkv_width = D_KV_HEADS * HEAD_DIM
    assert t % ROW_TILE == 0

    de2 = exp_b_gu.shape[-1]
    b_gu = exp_b_gu.reshape(depth, n_exp, de2 // GU_BLOCK, GU_BLOCK // 2, 2)
    b_gu = jnp.swapaxes(b_gu, -1, -2).reshape(depth, n_exp, 1, de2)
    b_down = exp_b_down.reshape(depth, n_exp, 1, -1)

    xt = x.reshape(t, d)
    for layer in range(depth):
        i = layer // 2
        if layer % 2 == 0:
            cat = _even_mix(xt.reshape(bsz, seq, d), ev_w_in[i], ev_pool_w[i], ev_pool_scale[i], ev_conv_w[i])
            cat = cat.reshape(t, -1)
            w_out = ev_w_out[i]
        else:
            qkv, (qd, kd, vd) = _odd_proj(xt, od_w_in[i], c_width, dq_width, dkv_width)
            dils = tuple(dil for _, dil in C_PATTERNS)
            os_, lses = [], []
            for (window, dil), (qc, kc, vc) in zip(C_PATTERNS, qkv):
                o, lse = _attn_c(qc, kc, vc, bsz, window, dil)
                os_.append(o)
                lses.append(lse)
            seq3 = lambda a: a.reshape(bsz, seq, a.shape[1])
            yd = _attn_d(seq3(qd), seq3(kd), seq3(vd), od_sink[i]).reshape(t, dq_width)
            cat = _merge(os_, lses, yd, dils)
            w_out = od_w_out[i]
        x1, x1b, top_idx, gates, rank, counts = _out_ln(cat, w_out, xt, ln_g[layer, 0], ln_b[layer, 0],
                                                        router_w[layer], router_b[layer], alpha)
        xt = _moe(x1, x1b, top_idx, gates, rank, counts, exp_w_gu[layer], b_gu[layer], exp_w_down[layer],
                  b_down[layer], ln_g[layer, 1], ln_b[layer, 1], alpha)
    return xt.reshape(bsz, seq, d)
```

```python
import functools

import jax
import jax.numpy as jnp
from jax import lax
from jax.experimental import pallas as pl
from jax.experimental.pallas import tpu as pltpu
from jax.experimental.pallas import tpu_sc as plsc

HEAD_DIM = 64
POOL_WINDOWS = (2, 4, 8, 16)
C_PATTERNS = ((128, 1), (512, 4), (2048, 16))
D_KV_HEADS = 2
D_RADIUS = 128
TOP_K = 4
SWIGLU_LIMIT = 7.0
SWIGLU_ALPHA = 1.702
MOE_BLOCK = 256
LN_EPS = 1e-5
NEG_INF = -1e30

ROW_TILE = 512
ATTN_Q_BLOCK = 128
LANES = 128
GU_BLOCK = 256
VMEM_LIMIT_BYTES = 56 * 1024 * 1024

F32 = jnp.float32
BF16 = jnp.bfloat16


def _params(n_axes=1):
    return pltpu.CompilerParams(dimension_semantics=("arbitrary",) * n_axes,
                                vmem_limit_bytes=VMEM_LIMIT_BYTES)


def _dot(a, b):
    return jnp.dot(a, b, preferred_element_type=F32)


def _dot_nt(a, b):
    return lax.dot_general(a, b, (((1,), (1,)), ((), ())), preferred_element_type=F32)


def _pack_pairs(a):
    n = a.shape[1] // 2
    bits = lax.bitcast_convert_type(a.astype(BF16).astype(F32), jnp.int32)
    return bits[:, :n] | lax.shift_right_logical(bits[:, n:], 16)


def _unpack_pairs(w):
    hi = lax.bitcast_convert_type(w & jnp.int32(-65536), F32)
    lo = lax.bitcast_convert_type(lax.shift_left(w, 16), F32)
    return jnp.concatenate([hi, lo], axis=1).astype(BF16)


SC_CORES = 2
SC_SUBCORES = 16
SC_WORKERS = SC_CORES * SC_SUBCORES
SC_CHUNK = 64


def _sc_mesh():
    return plsc.VectorSubcoreMesh(core_axis_name="c", subcore_axis_name="s")


def _sc_worker():
    return lax.axis_index("s") * SC_CORES + lax.axis_index("c")


def _sc_gather_rows(table, idx):
    n, w = idx.shape[0], table.shape[1]
    per_w = n // SC_WORKERS
    assert n % SC_WORKERS == 0 and per_w % (2 * SC_CHUNK) == 0

    def body(table_hbm, idx_hbm, out_hbm, i0, i1, r0, r1, s0, s1):
        base = _sc_worker() * per_w

        def start(off, iv, rv, sem):
            pltpu.sync_copy(idx_hbm.at[pl.ds(off, SC_CHUNK)], iv)
            pltpu.async_copy(table_hbm.at[iv], rv, sem)

        def finish(off, iv, rv, sem):
            pltpu.make_async_copy(table_hbm.at[iv], rv, sem).wait()
            pltpu.sync_copy(rv, out_hbm.at[pl.ds(off, SC_CHUNK)])

        start(base, i0, r0, s0)

        @pl.loop(0, per_w, step=2 * SC_CHUNK)
        def _(o):
            off = base + o
            start(off + SC_CHUNK, i1, r1, s1)
            finish(off, i0, r0, s0)

            @pl.when(o + 2 * SC_CHUNK < per_w)
            def _():
                start(off + 2 * SC_CHUNK, i0, r0, s0)

            finish(off + SC_CHUNK, i1, r1, s1)

    return pl.kernel(
        body, mesh=_sc_mesh(),
        out_type=jax.ShapeDtypeStruct((n, w), table.dtype),
        scratch_types=[pltpu.VMEM((SC_CHUNK,), jnp.int32), pltpu.VMEM((SC_CHUNK,), jnp.int32),
                       pltpu.VMEM((SC_CHUNK, w), table.dtype), pltpu.VMEM((SC_CHUNK, w), table.dtype),
                       pltpu.SemaphoreType.DMA, pltpu.SemaphoreType.DMA],
    )(table, idx)


def _sc_scatter_rows(src, pos3, n_out):
    t, w = src.shape
    kk = pos3.shape[1]
    per_w = t // SC_WORKERS
    assert t % SC_WORKERS == 0 and per_w % SC_CHUNK == 0 and pos3.shape == (t // SC_CHUNK, kk, SC_CHUNK)

    def body(src_hbm, pos_hbm, out_hbm, iv, rv, sem):
        base = _sc_worker() * per_w

        @pl.loop(0, per_w, step=SC_CHUNK)
        def _(o):
            off = base + o
            pltpu.sync_copy(pos_hbm.at[off // SC_CHUNK], iv)
            pltpu.sync_copy(src_hbm.at[pl.ds(off, SC_CHUNK)], rv)
            copies = [pltpu.async_copy(rv, out_hbm.at[iv.at[j]], sem) for j in range(kk)]
            for cp in copies:
                cp.wait()

    return pl.kernel(
        body, mesh=_sc_mesh(),
        out_type=jax.ShapeDtypeStruct((n_out, w), src.dtype),
        scratch_types=[pltpu.VMEM((kk, SC_CHUNK), jnp.int32), pltpu.VMEM((SC_CHUNK, w), src.dtype),
                       pltpu.SemaphoreType.DMA],
    )(src, pos3)


def _shift_down(a, k, row):
    return jnp.where(row >= k, pltpu.roll(a, k, axis=0), 0.0)


def _shift_up(a, k, row):
    n = a.shape[0]
    return jnp.where(row < n - k, pltpu.roll(a, n - k, axis=0), 0.0)


def _even_mix_kernel(x_ref, w_in_ref, pool_w_ref, pool_scale_ref, conv_w_ref, cat_ref, xb_ref):
    s = x_ref.shape[1]
    pool_width = pool_scale_ref.shape[1]
    gd = pool_width // len(POOL_WINDOWS)
    conv_width = conv_w_ref.shape[1]
    xb_ref[...] = x_ref[0].astype(BF16)
    xb = xb_ref[...]

    row = lax.broadcasted_iota(jnp.int32, (s, gd), 0)
    for g, w in enumerate(POOL_WINDOWS):
        lo = g * gd
        u = _dot(xb, w_in_ref[:, lo:lo + gd])
        half = w // 2
        back, fwd, span = u, u, 1
        while span < half:
            back = back + _shift_down(back, span, row)
            fwd = fwd + _shift_up(fwd, span, row)
            span *= 2
        win = _shift_down(back, 1, row) + fwd
        cnt = (jnp.minimum(row + (w - half), s) - jnp.maximum(row - half, 0)).astype(F32)
        pooled = win / cnt - u
        mixed = _dot(pooled.astype(BF16), pool_w_ref[g])
        cat_ref[0, :, lo:lo + gd] = (mixed * pool_scale_ref[:, lo:lo + gd]).astype(BF16)

    cw = 256
    rowc = lax.broadcasted_iota(jnp.int32, (s, cw), 0)
    for j in range(conv_width // cw):
        c0 = j * cw
        b_gate = _dot(xb, w_in_ref[:, pool_width + c0:pool_width + c0 + cw])
        c_gate = _dot(xb, w_in_ref[:, pool_width + conv_width + c0:pool_width + conv_width + c0 + cw])
        v = _dot(xb, w_in_ref[:, pool_width + 2 * conv_width + c0:pool_width + 2 * conv_width + c0 + cw])
        u = c_gate * v
        conv = (_shift_down(u, 1, rowc) * conv_w_ref[0:1, c0:c0 + cw] + u * conv_w_ref[1:2, c0:c0 + cw]
                + _shift_up(u, 1, rowc) * conv_w_ref[2:3, c0:c0 + cw])
        cat_ref[0, :, pool_width + c0:pool_width + c0 + cw] = (b_gate * conv).astype(BF16)


def _even_mix(x, w_in, pool_w, pool_scale, conv_w):
    b, s, d = x.shape
    pool_width = pool_scale.shape[0]
    conv_width = conv_w.shape[1]
    assert conv_width % 256 == 0 and w_in.shape[1] == pool_width + 3 * conv_width
    cat_width = pool_width + conv_width
    return pl.pallas_call(
        _even_mix_kernel,
        grid=(b,),
        in_specs=[
            pl.BlockSpec((1, s, d), lambda i: (i, 0, 0)),
            pl.BlockSpec(w_in.shape, lambda i: (0, 0)),
            pl.BlockSpec(pool_w.shape, lambda i: (0, 0, 0)),
            pl.BlockSpec((1, pool_width), lambda i: (0, 0)),
            pl.BlockSpec(conv_w.shape, lambda i: (0, 0)),
        ],
        out_specs=pl.BlockSpec((1, s, cat_width), lambda i: (i, 0, 0)),
        out_shape=jax.ShapeDtypeStruct((b, s, cat_width), BF16),
        scratch_shapes=[pltpu.VMEM((s, d), BF16)],
        compiler_params=_params(1),
        name="even_mix",
    )(x, w_in.astype(BF16), pool_w.astype(BF16), pool_scale.reshape(1, pool_width), conv_w)


def _odd_proj_kernel(x_ref, w_ref, *refs, c_width, dils):
    outs, h_ref = refs[:-1], refs[-1]
    h = _dot(x_ref[...].astype(BF16), w_ref[...])
    for c in range(h_ref.shape[0]):
        h_ref[c] = h[:, c * LANES:(c + 1) * LANES]
    tm = x_ref.shape[0]
    chunks = c_width // LANES
    for pi, dil in enumerate(dils):
        n = tm // dil
        for r in range(dil):
            for j in range(3):
                for c in range(chunks):
                    src = h_ref[j * chunks + c] if dil == 1 else h_ref[j * chunks + c, pl.ds(r, n, stride=dil), :]
                    lo = r * c_width + c * LANES
                    outs[3 * pi + j][:, lo:lo + LANES] = src.astype(BF16)
    c0 = 3 * c_width
    for ref in outs[3 * len(dils):]:
        wd = ref.shape[1]
        ref[...] = h[:, c0:c0 + wd].astype(BF16)
        c0 += wd


def _odd_proj(xt, w_in, c_width, dq_width, dkv_width):
    t, d = xt.shape
    dils = tuple(dil for _, dil in C_PATTERNS)
    assert 3 * c_width + dq_width + 2 * dkv_width == w_in.shape[1] and c_width % LANES == 0
    col = jnp.arange(w_in.shape[1])
    is_q = (col < c_width) | ((col >= 3 * c_width) & (col < 3 * c_width + dq_width))
    w_scaled = (w_in * jnp.where(is_q, HEAD_DIM ** -0.5, 1.0)).astype(BF16)
    shapes = [(t // dil, dil * c_width) for dil in dils for _ in range(3)]
    blocks = [(ROW_TILE // dil, dil * c_width) for dil in dils for _ in range(3)]
    for wd in (dq_width, dkv_width, dkv_width):
        shapes.append((t, wd))
        blocks.append((ROW_TILE, wd))
    outs = pl.pallas_call(
        functools.partial(_odd_proj_kernel, c_width=c_width, dils=dils),
        grid=(t // ROW_TILE,),
        in_specs=[pl.BlockSpec((ROW_TILE, d), lambda i: (i, 0)),
                  pl.BlockSpec(w_in.shape, lambda i: (0, 0))],
        out_specs=[pl.BlockSpec(blk, lambda i: (i, 0)) for blk in blocks],
        out_shape=[jax.ShapeDtypeStruct(shp, BF16) for shp in shapes],
        scratch_shapes=[pltpu.VMEM((w_in.shape[1] // LANES, ROW_TILE, LANES), F32)],
        compiler_params=_params(1),
        name="odd_proj",
    )(xt, w_scaled)
    qkv = [outs[3 * pi:3 * pi + 3] for pi in range(len(dils))]
    return qkv, outs[3 * len(dils):]


def _band_geometry(i, length, radius):
    qb = min(ATTN_Q_BLOCK, length)
    span = min(length, qb + 2 * radius)
    q0 = pl.multiple_of(i * qb, qb)
    align = 8
    for cand in (128, 64, 32, 16):
        if qb % cand == 0 and radius % cand == 0 and (length - span) % cand == 0:
            align = cand
            break
    start = pl.multiple_of(jnp.clip(q0 - radius, 0, length - span), align)
    qpos = q0 + lax.broadcasted_iota(jnp.int32, (qb, span), 0)
    kpos = start + lax.broadcasted_iota(jnp.int32, (qb, span), 1)
    dist = jnp.abs(kpos - qpos)
    mask = jnp.where(dist <= radius, 0.0, NEG_INF).astype(F32)
    return q0, qb, start, span, dist.astype(F32), mask


def _softmax_av(s, vh):
    m = jnp.max(s, axis=-1, keepdims=True)
    p = jnp.exp(s - m)
    l = jnp.sum(p, axis=-1, keepdims=True)
    o = _dot(p.astype(BF16), vh) / l
    return o, m + jnp.log(l)


def _attn_c_kernel(q_ref, k_ref, v_ref, o_ref, lse_ref, *, dil, radius, n_heads):
    length = q_ref.shape[1]
    qb = min(ATTN_Q_BLOCK, length)

    def block(i, carry):
        q0, _, start, span, dist, mask = _band_geometry(i, length, radius)
        dist = dist * float(dil)
        for hh in range(n_heads):
            lo = hh * HEAD_DIM
            qh = q_ref[0, pl.ds(q0, qb), lo:lo + HEAD_DIM]
            kh = k_ref[0, pl.ds(start, span), lo:lo + HEAD_DIM]
            vh = v_ref[0, pl.ds(start, span), lo:lo + HEAD_DIM]
            slope = 2.0 ** (-8.0 * (hh + 1) / n_heads)
            s = _dot_nt(qh, kh) - slope * dist + mask
            o, lse = _softmax_av(s, vh)
            o_ref[0, pl.ds(q0, qb), lo:lo + HEAD_DIM] = o.astype(o_ref.dtype)
            lse_ref[0, pl.ds(q0, qb), lo:lo + HEAD_DIM] = jnp.broadcast_to(lse, (qb, HEAD_DIM))
        return carry

    lax.fori_loop(0, length // qb, block, 0)


def _attn_c(q, k, v, bsz, window, dil):
    rows, dw = q.shape
    w = dw // dil
    length = rows // bsz
    radius = window // 2 // dil
    assert length % min(ATTN_Q_BLOCK, length) == 0
    view = lambda a: a.reshape(bsz, length, dw)
    spec = pl.BlockSpec((1, length, w), lambda i, r: (i, 0, r))
    o, lse = pl.pallas_call(
        functools.partial(_attn_c_kernel, dil=dil, radius=radius, n_heads=w // HEAD_DIM),
        grid=(bsz, dil),
        in_specs=[spec, spec, spec],
        out_specs=[spec, spec],
        out_shape=[jax.ShapeDtypeStruct((bsz, length, dw), BF16),
                   jax.ShapeDtypeStruct((bsz, length, dw), F32)],
        compiler_params=_params(2),
        name=f"attn_c_d{dil}",
    )(view(q), view(k), view(v))
    return o.reshape(rows, dw), lse.reshape(rows, dw)


def _attn_d_kernel(sink_ref, q_ref, k_ref, v_ref, y_ref, *, radius, n_heads, group):
    length = q_ref.shape[1]
    qb = min(ATTN_Q_BLOCK, length)

    def block(i, carry):
        q0, _, start, span, dist, mask = _band_geometry(i, length, radius)
        for hh in range(n_heads):
            lo = hh * HEAD_DIM
            kv = (hh // group) * HEAD_DIM
            qh = q_ref[0, pl.ds(q0, qb), lo:lo + HEAD_DIM]
            kh = k_ref[0, pl.ds(start, span), kv:kv + HEAD_DIM]
            vh = v_ref[0, pl.ds(start, span), kv:kv + HEAD_DIM]
            slope = 2.0 ** (-8.0 * (hh + 1) / n_heads)
            s = _dot_nt(qh, kh) - slope * dist + mask
            o, lse = _softmax_av(s, vh)
            y_ref[0, pl.ds(q0, qb), lo:lo + HEAD_DIM] = (o * jax.nn.sigmoid(lse - sink_ref[hh])).astype(y_ref.dtype)
        return carry

    lax.fori_loop(0, length // qb, block, 0)


def _attn_d(q, k, v, sink):
    b, s, w = q.shape
    kvw = k.shape[2]
    n_heads = w // HEAD_DIM
    return pl.pallas_call(
        functools.partial(_attn_d_kernel, radius=D_RADIUS, n_heads=n_heads, group=n_heads // D_KV_HEADS),
        grid=(b,),
        in_specs=[pl.BlockSpec(memory_space=pltpu.SMEM),
                  pl.BlockSpec((1, s, w), lambda i: (i, 0, 0)),
                  pl.BlockSpec((1, s, kvw), lambda i: (i, 0, 0)),
                  pl.BlockSpec((1, s, kvw), lambda i: (i, 0, 0))],
        out_specs=pl.BlockSpec((1, s, w), lambda i: (i, 0, 0)),
        out_shape=jax.ShapeDtypeStruct((b, s, w), BF16),
        compiler_params=_params(1),
        name="attn_d",
    )(sink.astype(F32), q, k, v)


def _merge_kernel(*refs, dils, cw):
    n_pat = len(dils)
    o_refs, l_refs = refs[:n_pat], refs[n_pat:2 * n_pat]
    yd_ref, cat_ref = refs[2 * n_pat], refs[2 * n_pat + 1]
    scratch = list(refs[2 * n_pat + 2:])
    tm = cat_ref.shape[0]
    chunks = cw // LANES
    staged = []
    for p, dil in enumerate(dils):
        if dil == 1:
            staged.append(None)
            continue
        o_s, l_s = scratch.pop(0), scratch.pop(0)
        n = tm // dil
        for r in range(dil):
            for c in range(chunks):
                lo = r * cw + c * LANES
                o_s[c, pl.ds(r, n, stride=dil), :] = o_refs[p][:, lo:lo + LANES].astype(F32)
                l_s[c, pl.ds(r, n, stride=dil), :] = l_refs[p][:, lo:lo + LANES]
        staged.append((o_s, l_s))
    for c in range(chunks):
        cols = slice(c * LANES, (c + 1) * LANES)
        os_ = [o_refs[p][:, cols].astype(F32) if st is None else st[0][c] for p, st in enumerate(staged)]
        ls_ = [l_refs[p][:, cols] if st is None else st[1][c] for p, st in enumerate(staged)]
        m = functools.reduce(jnp.maximum, ls_)
        es = [jnp.exp(l - m) for l in ls_]
        num = functools.reduce(lambda a, b: a + b, [e * o for e, o in zip(es, os_)])
        den = functools.reduce(lambda a, b: a + b, es)
        cat_ref[:, cols] = (num / den).astype(BF16)
    cat_ref[:, cw:] = yd_ref[...]


def _merge(os_, lses, yd, dils):
    t, dw = yd.shape
    cw = os_[0].shape[1] // dils[0]
    vspecs = [pl.BlockSpec((ROW_TILE // dil, dil * cw), lambda i: (i, 0)) for dil in dils]
    n_scratch = 2 * sum(1 for dil in dils if dil != 1)
    return pl.pallas_call(
        functools.partial(_merge_kernel, dils=dils, cw=cw),
        grid=(t // ROW_TILE,),
        in_specs=vspecs + vspecs + [pl.BlockSpec((ROW_TILE, dw), lambda i: (i, 0))],
        out_specs=pl.BlockSpec((ROW_TILE, cw + dw), lambda i: (i, 0)),
        out_shape=jax.ShapeDtypeStruct((t, cw + dw), BF16),
        scratch_shapes=[pltpu.VMEM((cw // LANES, ROW_TILE, LANES), F32)] * n_scratch,
        compiler_params=_params(1),
        name="merge",
    )(*os_, *lses, yd)


def _layer_norm(z, g, b):
    mu = jnp.mean(z, axis=-1, keepdims=True)
    zc = z - mu
    var = jnp.mean(zc * zc, axis=-1, keepdims=True)
    return zc * lax.rsqrt(var + LN_EPS) * g + b


def _route_tile(logits, tri, count_ref):
    tm, n_exp = logits.shape
    lane = lax.broadcasted_iota(jnp.int32, (tm, n_exp), 1)
    col = lax.broadcasted_iota(jnp.int32, (tm, TOP_K), 1)
    work = logits
    vals, onehots = [], []
    idx_out = jnp.zeros((tm, TOP_K), jnp.int32)
    for k in range(TOP_K):
        m = jnp.max(work, axis=-1, keepdims=True)
        idx = jnp.min(jnp.where(work == m, lane, n_exp), axis=-1, keepdims=True)
        hot = lane == idx
        work = jnp.where(hot, -jnp.inf, work)
        vals.append(m)
        onehots.append(hot)
        idx_out = jnp.where(col == k, idx, idx_out)
    exps = [jnp.exp(v - vals[0]) for v in vals]
    den = functools.reduce(lambda a, b: a + b, exps)
    gate_out = jnp.zeros((tm, TOP_K), F32)
    for k in range(TOP_K):
        gate_out = jnp.where(col == k, exps[k] / den, gate_out)
    multihot = functools.reduce(lambda a, b: a + b, [h.astype(F32) for h in onehots])
    before = _dot(tri, multihot.astype(BF16)) + count_ref[...]
    rank_out = jnp.zeros((tm, TOP_K), jnp.int32)
    for k in range(TOP_K):
        rank = jnp.sum(jnp.where(onehots[k], before, 0.0), axis=-1, keepdims=True)
        rank_out = jnp.where(col == k, rank.astype(jnp.int32), rank_out)
    count_ref[...] += jnp.sum(multihot, axis=0, keepdims=True)
    return idx_out, gate_out, rank_out


def _out_ln_kernel(cat_ref, w_ref, x_ref, g_ref, b_ref, rw_hi_ref, rw_lo_ref, rb_ref, tri_ref,
                   x1_ref, x1p_ref, idx_ref, gate_ref, rank_ref, count_ref, *, alpha):
    @pl.when(pl.program_id(0) == 0)
    def _():
        count_ref[...] = jnp.zeros_like(count_ref)

    mix = _dot(cat_ref[...], w_ref[...])
    x1 = _layer_norm(alpha * x_ref[...] + mix, g_ref[...], b_ref[...])
    x1_ref[...] = x1
    x1p_ref[...] = _pack_pairs(x1)
    hi = x1.astype(BF16)
    lo = (x1 - hi.astype(F32)).astype(BF16)
    logits = (_dot(hi, rw_hi_ref[...]) + _dot(lo, rw_hi_ref[...]) + _dot(hi, rw_lo_ref[...]) + rb_ref[...])
    idx_ref[...], gate_ref[...], rank_ref[...] = _route_tile(logits, tri_ref[...], count_ref)


def _out_ln(cat, w_out, xt, g, b, router_w, router_b, alpha):
    t, d = xt.shape
    n_exp = router_w.shape[1]
    rw_hi = router_w.astype(BF16)
    rw_lo = (router_w - rw_hi.astype(F32)).astype(BF16)
    tri = jnp.tril(jnp.ones((ROW_TILE, ROW_TILE), BF16), -1)
    row = lambda wd: pl.BlockSpec((ROW_TILE, wd), lambda i: (i, 0))
    full = lambda a: pl.BlockSpec(a.shape, lambda i: (0,) * a.ndim)
    args = (cat, w_out.astype(BF16), xt, g.reshape(1, d), b.reshape(1, d), rw_hi, rw_lo,
            router_b.reshape(1, n_exp), tri)
    return pl.pallas_call(
        functools.partial(_out_ln_kernel, alpha=alpha),
        grid=(t // ROW_TILE,),
        in_specs=[row(cat.shape[1]), full(args[1]), row(d)] + [full(a) for a in args[3:]],
        out_specs=[row(d), row(d // 2), row(TOP_K), row(TOP_K), row(TOP_K),
                   pl.BlockSpec((1, n_exp), lambda i: (0, 0))],
        out_shape=[jax.ShapeDtypeStruct((t, d), F32), jax.ShapeDtypeStruct((t, d // 2), jnp.int32),
                   jax.ShapeDtypeStruct((t, TOP_K), jnp.int32), jax.ShapeDtypeStruct((t, TOP_K), F32),
                   jax.ShapeDtypeStruct((t, TOP_K), jnp.int32), jax.ShapeDtypeStruct((1, n_exp), F32)],
        compiler_params=_params(1),
        name="out_ln",
    )(*args)


def _experts_kernel(block_e_ref, valid_ref, x_ref, wgu_ref, bgu_ref, wd_ref, bd_ref, perm_ref, y_ref,
                    wgu_s, wd_s):
    i = pl.program_id(0)
    valid = valid_ref[i]
    half = GU_BLOCK // 2

    @pl.when((i == 0) | (block_e_ref[i] != block_e_ref[jnp.maximum(i - 1, 0)]))
    def _():
        for c in range(wgu_ref.shape[1] // GU_BLOCK):
            cols = slice(c * GU_BLOCK, (c + 1) * GU_BLOCK)
            wgu_s[:, cols] = _dot(wgu_ref[:, cols].astype(BF16), perm_ref[...]).astype(BF16)
        wd_s[...] = wd_ref[...].astype(BF16)

    @pl.when(valid > 0)
    def _():
        xb = _unpack_pairs(x_ref[...])
        row = lax.broadcasted_iota(jnp.int32, xb.shape, 0)
        xb = jnp.where(row < valid, xb, jnp.zeros_like(xb))
        h = _dot(xb, wgu_s[...]) + bgu_ref[...]
        acts = []
        for c in range(h.shape[1] // GU_BLOCK):
            glu = jnp.minimum(h[:, c * GU_BLOCK:c * GU_BLOCK + half], SWIGLU_LIMIT)
            lin = jnp.clip(h[:, c * GU_BLOCK + half:(c + 1) * GU_BLOCK], -SWIGLU_LIMIT, SWIGLU_LIMIT)
            acts.append((glu * jax.nn.sigmoid(SWIGLU_ALPHA * glu) * (lin + 1.0)).astype(BF16))
        act = jnp.concatenate(acts, axis=1)
        y_ref[...] = _pack_pairs(_dot(act, wd_s[...]) + bd_ref[...])

    @pl.when(valid == 0)
    def _():
        y_ref[...] = jnp.zeros_like(y_ref)


def _experts(x_rows, block_e, block_valid, layer, w_gu, b_gu, w_down, b_down):
    rows, dp = x_rows.shape
    _, n_exp, d, de2 = w_gu.shape
    assert de2 % GU_BLOCK == 0 and dp * 2 == d
    n_blocks = rows // MOE_BLOCK
    half = GU_BLOCK // 2
    j = jnp.arange(GU_BLOCK)
    src = jnp.where(j < half, 2 * j, 2 * (j - half) + 1)
    perm = (jnp.arange(GU_BLOCK)[:, None] == src[None, :]).astype(BF16)
    wspec = lambda a: pl.BlockSpec((None, None) + a.shape[2:], lambda i, be, bv: (layer, be[i], 0, 0))
    grid_spec = pltpu.PrefetchScalarGridSpec(
        num_scalar_prefetch=2,
        grid=(n_blocks,),
        in_specs=[pl.BlockSpec((MOE_BLOCK, dp), lambda i, be, bv: (i, 0)),
                  wspec(w_gu), wspec(b_gu), wspec(w_down), wspec(b_down),
                  pl.BlockSpec((GU_BLOCK, GU_BLOCK), lambda i, be, bv: (0, 0))],
        out_specs=pl.BlockSpec((MOE_BLOCK, dp), lambda i, be, bv: (i, 0)),
        scratch_shapes=[pltpu.VMEM(w_gu.shape[2:], BF16), pltpu.VMEM(w_down.shape[2:], BF16)],
    )
    return pl.pallas_call(
        _experts_kernel,
        grid_spec=grid_spec,
        out_shape=jax.ShapeDtypeStruct((rows, dp), jnp.int32),
        compiler_params=_params(1),
        name="experts",
    )(block_e, block_valid, x_rows, w_gu, b_gu, w_down, b_down, perm)


def _combine_ln_kernel(x1_ref, yk_ref, gate_ref, g_ref, b_ref, x2_ref, *, alpha):
    gates = gate_ref[...]
    ffn = gates[:, 0:1] * _unpack_pairs(yk_ref[0]).astype(F32)
    for k in range(1, yk_ref.shape[0]):
        ffn = ffn + gates[:, k:k + 1] * _unpack_pairs(yk_ref[k]).astype(F32)
    x2_ref[...] = _layer_norm(alpha * x1_ref[...] + ffn, g_ref[...], b_ref[...])


def _combine_ln(x1, yk, gates, g, b, alpha):
    t, d = x1.shape
    k = yk.shape[0]
    return pl.pallas_call(
        functools.partial(_combine_ln_kernel, alpha=alpha),
        grid=(t // ROW_TILE,),
        in_specs=[pl.BlockSpec((ROW_TILE, d), lambda i: (i, 0)),
                  pl.BlockSpec((k, ROW_TILE, d // 2), lambda i: (0, i, 0)),
                  pl.BlockSpec((ROW_TILE, k), lambda i: (i, 0)),
                  pl.BlockSpec((1, d), lambda i: (0, 0)),
                  pl.BlockSpec((1, d), lambda i: (0, 0))],
        out_specs=pl.BlockSpec((ROW_TILE, d), lambda i: (i, 0)),
        out_shape=jax.ShapeDtypeStruct((t, d), F32),
        compiler_params=_params(1),
        name="combine_ln",
    )(x1, yk, gates, g.reshape(1, d), b.reshape(1, d))


def _route(top_idx, rank, counts):
    t = top_idx.shape[0]
    n_exp = counts.shape[1]
    experts = jnp.arange(n_exp, dtype=jnp.int32)
    counts = counts.reshape(n_exp).astype(jnp.int32)
    padded = (counts + MOE_BLOCK - 1) // MOE_BLOCK * MOE_BLOCK
    pad_end = jnp.cumsum(padded)
    pad_start = pad_end - padded
    idx_flat = top_idx.reshape(-1)
    start_flat = jnp.sum(jnp.where(idx_flat[None, :] == experts[:, None], pad_start[:, None], 0), axis=0)
    pos = (rank.reshape(-1) + start_flat).reshape(t, TOP_K)
    n_blocks = -(-(t * TOP_K) // MOE_BLOCK) + n_exp
    blk = jnp.arange(n_blocks, dtype=jnp.int32) * MOE_BLOCK
    block_e = jnp.minimum(jnp.sum((pad_end[None, :] <= blk[:, None]).astype(jnp.int32), axis=1), n_exp - 1)
    mine = block_e[:, None] == experts[None, :]
    pick = lambda v: jnp.sum(jnp.where(mine, v[None, :], 0), axis=1)
    block_valid = jnp.clip(pick(counts) - (blk - pick(pad_start)), 0, MOE_BLOCK).astype(jnp.int32)
    return pos, block_e, block_valid


def _moe(x1, x1p, top_idx, gates, rank, counts, layer, w_gu, b_gu, w_down, b_down, g, b, alpha):
    t, d = x1.shape
    n_exp = counts.shape[1]
    pos, block_e, block_valid = _route(top_idx, rank, counts)
    rows = (-(-(t * TOP_K) // MOE_BLOCK) + n_exp) * MOE_BLOCK
    pos3 = pos.reshape(t // SC_CHUNK, SC_CHUNK, TOP_K).transpose(0, 2, 1)
    x_rows = _sc_scatter_rows(x1p, pos3, rows)
    y = _experts(x_rows, block_e, block_valid, layer, w_gu, b_gu, w_down, b_down)
    yk = _sc_gather_rows(y, pos.T.reshape(-1)).reshape(TOP_K, t, d // 2)
    return _combine_ln(x1, yk, gates, g, b, alpha)


def kernel(x, ev_w_in, ev_pool_w, ev_pool_scale, ev_conv_w, ev_w_out, od_w_in, od_sink, od_w_out,
           router_w, router_b, exp_w_gu, exp_b_gu, exp_w_down, exp_b_down, ln_g, ln_b):
    bsz, seq, d = x.shape
    t = bsz * seq
    depth = ln_g.shape[0]
    alpha = (2 * depth) ** 0.25
    n_exp = router_w.shape[2]
    c_width = d // 2
    dq_width = d // 2
    dkv_width = D_KV_HEADS * HEAD_DIM
    assert t % ROW_TILE == 0

    de2 = exp_b_gu.shape[-1]
    b_gu = exp_b_gu.reshape(depth, n_exp, de2 // GU_BLOCK, GU_BLOCK // 2, 2)
    b_gu = jnp.swapaxes(b_gu, -1, -2).reshape(depth, n_exp, 1, de2)
    b_down = exp_b_down.reshape(depth, n_exp, 1, -1)

    xt = x.reshape(t, d)
    for layer in range(depth):
        i = layer // 2
        if layer % 2 == 0:
            cat = _even_mix(xt.reshape(bsz, seq, d), ev_w_in[i], ev_pool_w[i], ev_pool_scale[i], ev_conv_w[i])
            cat = cat.reshape(t, -1)
            w_out = ev_w_out[i]
        else:
            qkv, (qd, kd, vd) = _odd_proj(xt, od_w_in[i], c_width, dq_width, dkv_width)
            dils = tuple(dil for _, dil in C_PATTERNS)
            os_, lses = [], []
            for (window, dil), (qc, kc, vc) in zip(C_PATTERNS, qkv):
                o, lse = _attn_c(qc, kc, vc, bsz, window, dil)
                os_.append(o)
                lses.append(lse)
            seq3 = lambda a: a.reshape(bsz, seq, a.shape[1])
            yd = _attn_d(seq3(qd), seq3(kd), seq3(vd), od_sink[i]).reshape(t, dq_width)
            cat = _merge(os_, lses, yd, dils)
            w_out = od_w_out[i]
        x1, x1p, top_idx, gates, rank, counts = _out_ln(cat, w_out, xt, ln_g[layer, 0], ln_b[layer, 0],
                                                        router_w[layer], router_b[layer], alpha)
        xt = _moe(x1, x1p, top_idx, gates, rank, counts, layer, exp_w_gu, b_gu, exp_w_down, b_down,
                  ln_g[layer, 1], ln_b[layer, 1], alpha)
    return xt.reshape(bsz, seq, d)
```

```python
import functools

import jax
import jax.numpy as jnp
from jax import lax
from jax.experimental import pallas as pl
from jax.experimental.pallas import tpu as pltpu
from jax.experimental.pallas import tpu_sc as plsc

HEAD_DIM = 64
POOL_WINDOWS = (2, 4, 8, 16)
C_PATTERNS = ((128, 1), (512, 4), (2048, 16))
D_KV_HEADS = 2
D_RADIUS = 128
TOP_K = 4
SWIGLU_LIMIT = 7.0
SWIGLU_ALPHA = 1.702
MOE_BLOCK = 512
LN_EPS = 1e-5
NEG_INF = -1e30

ROW_TILE = 512
ATTN_Q_BLOCK = 128
ATTN_STEP_ROWS = 512
LANES = 128
GU_BLOCK = 256
VMEM_LIMIT_BYTES = 56 * 1024 * 1024

F32 = jnp.float32
BF16 = jnp.bfloat16


def _params(n_axes=1):
    return pltpu.CompilerParams(dimension_semantics=("arbitrary",) * n_axes,
                                vmem_limit_bytes=VMEM_LIMIT_BYTES)


def _dot(a, b):
    return jnp.dot(a, b, preferred_element_type=F32)


def _dot_nt(a, b):
    return lax.dot_general(a, b, (((1,), (1,)), ((), ())), preferred_element_type=F32)


def _pack_pairs(a):
    n = a.shape[1] // 2
    bits = lax.bitcast_convert_type(a.astype(BF16).astype(F32), jnp.int32)
    return bits[:, :n] | lax.shift_right_logical(bits[:, n:], 16)


def _unpack_pairs(w):
    hi = lax.bitcast_convert_type(w & jnp.int32(-65536), F32)
    lo = lax.bitcast_convert_type(lax.shift_left(w, 16), F32)
    return jnp.concatenate([hi, lo], axis=1).astype(BF16)


SC_CORES = 2
SC_SUBCORES = 16
SC_WORKERS = SC_CORES * SC_SUBCORES
SC_CHUNK = 64


def _sc_mesh():
    return plsc.VectorSubcoreMesh(core_axis_name="c", subcore_axis_name="s")


def _sc_worker():
    return lax.axis_index("s") * SC_CORES + lax.axis_index("c")


def _sc_gather_rows(table, idx):
    n, w = idx.shape[0], table.shape[1]
    per_w = n // SC_WORKERS
    assert n % SC_WORKERS == 0 and per_w % (2 * SC_CHUNK) == 0

    def body(table_hbm, idx_hbm, out_hbm, i0, i1, r0, r1, s0, s1):
        base = _sc_worker() * per_w

        def start(off, iv, rv, sem):
            pltpu.sync_copy(idx_hbm.at[pl.ds(off, SC_CHUNK)], iv)
            pltpu.async_copy(table_hbm.at[iv], rv, sem)

        def finish(off, iv, rv, sem):
            pltpu.make_async_copy(table_hbm.at[iv], rv, sem).wait()
            pltpu.sync_copy(rv, out_hbm.at[pl.ds(off, SC_CHUNK)])

        start(base, i0, r0, s0)

        @pl.loop(0, per_w, step=2 * SC_CHUNK)
        def _(o):
            off = base + o
            start(off + SC_CHUNK, i1, r1, s1)
            finish(off, i0, r0, s0)

            @pl.when(o + 2 * SC_CHUNK < per_w)
            def _():
                start(off + 2 * SC_CHUNK, i0, r0, s0)

            finish(off + SC_CHUNK, i1, r1, s1)

    return pl.kernel(
        body, mesh=_sc_mesh(),
        out_type=jax.ShapeDtypeStruct((n, w), table.dtype),
        scratch_types=[pltpu.VMEM((SC_CHUNK,), jnp.int32), pltpu.VMEM((SC_CHUNK,), jnp.int32),
                       pltpu.VMEM((SC_CHUNK, w), table.dtype), pltpu.VMEM((SC_CHUNK, w), table.dtype),
                       pltpu.SemaphoreType.DMA, pltpu.SemaphoreType.DMA],
    )(table, idx)


def _sc_scatter_rows(src, pos3, n_out):
    t, w = src.shape
    kk = pos3.shape[1]
    per_w = t // SC_WORKERS
    assert t % SC_WORKERS == 0 and per_w % SC_CHUNK == 0 and pos3.shape == (t // SC_CHUNK, kk, SC_CHUNK)

    def body(src_hbm, pos_hbm, out_hbm, iv, rv, sem):
        base = _sc_worker() * per_w

        @pl.loop(0, per_w, step=SC_CHUNK)
        def _(o):
            off = base + o
            pltpu.sync_copy(pos_hbm.at[off // SC_CHUNK], iv)
            pltpu.sync_copy(src_hbm.at[pl.ds(off, SC_CHUNK)], rv)
            copies = [pltpu.async_copy(rv, out_hbm.at[iv.at[j]], sem) for j in range(kk)]
            for cp in copies:
                cp.wait()

    return pl.kernel(
        body, mesh=_sc_mesh(),
        out_type=jax.ShapeDtypeStruct((n_out, w), src.dtype),
        scratch_types=[pltpu.VMEM((kk, SC_CHUNK), jnp.int32), pltpu.VMEM((SC_CHUNK, w), src.dtype),
                       pltpu.SemaphoreType.DMA],
    )(src, pos3)


def _shift_down(a, k, row):
    return jnp.where(row >= k, pltpu.roll(a, k, axis=0), 0.0)


def _shift_up(a, k, row):
    n = a.shape[0]
    return jnp.where(row < n - k, pltpu.roll(a, n - k, axis=0), 0.0)


def _even_mix_kernel(x_ref, w_in_ref, pool_w_ref, pool_scale_ref, conv_w_ref, cat_ref, xb_ref):
    s = x_ref.shape[1]
    pool_width = pool_scale_ref.shape[1]
    gd = pool_width // len(POOL_WINDOWS)
    conv_width = conv_w_ref.shape[1]
    xb_ref[...] = x_ref[0].astype(BF16)
    xb = xb_ref[...]

    row = lax.broadcasted_iota(jnp.int32, (s, gd), 0)
    for g, w in enumerate(POOL_WINDOWS):
        lo = g * gd
        u = _dot(xb, w_in_ref[:, lo:lo + gd])
        half = w // 2
        back, fwd, span = u, u, 1
        while span < half:
            back = back + _shift_down(back, span, row)
            fwd = fwd + _shift_up(fwd, span, row)
            span *= 2
        win = _shift_down(back, 1, row) + fwd
        cnt = (jnp.minimum(row + (w - half), s) - jnp.maximum(row - half, 0)).astype(F32)
        pooled = win / cnt - u
        mixed = _dot(pooled.astype(BF16), pool_w_ref[g])
        cat_ref[0, :, lo:lo + gd] = (mixed * pool_scale_ref[:, lo:lo + gd]).astype(BF16)

    cw = 256
    rowc = lax.broadcasted_iota(jnp.int32, (s, cw), 0)
    for j in range(conv_width // cw):
        c0 = j * cw
        b_gate = _dot(xb, w_in_ref[:, pool_width + c0:pool_width + c0 + cw])
        c_gate = _dot(xb, w_in_ref[:, pool_width + conv_width + c0:pool_width + conv_width + c0 + cw])
        v = _dot(xb, w_in_ref[:, pool_width + 2 * conv_width + c0:pool_width + 2 * conv_width + c0 + cw])
        u = c_gate * v
        conv = (_shift_down(u, 1, rowc) * conv_w_ref[0:1, c0:c0 + cw] + u * conv_w_ref[1:2, c0:c0 + cw]
                + _shift_up(u, 1, rowc) * conv_w_ref[2:3, c0:c0 + cw])
        cat_ref[0, :, pool_width + c0:pool_width + c0 + cw] = (b_gate * conv).astype(BF16)


def _even_mix(x, w_in, pool_w, pool_scale, conv_w):
    b, s, d = x.shape
    pool_width = pool_scale.shape[0]
    conv_width = conv_w.shape[1]
    assert conv_width % 256 == 0 and w_in.shape[1] == pool_width + 3 * conv_width
    cat_width = pool_width + conv_width
    return pl.pallas_call(
        _even_mix_kernel,
        grid=(b,),
        in_specs=[
            pl.BlockSpec((1, s, d), lambda i: (i, 0, 0)),
            pl.BlockSpec(w_in.shape, lambda i: (0, 0)),
            pl.BlockSpec(pool_w.shape, lambda i: (0, 0, 0)),
            pl.BlockSpec((1, pool_width), lambda i: (0, 0)),
            pl.BlockSpec(conv_w.shape, lambda i: (0, 0)),
        ],
        out_specs=pl.BlockSpec((1, s, cat_width), lambda i: (i, 0, 0)),
        out_shape=jax.ShapeDtypeStruct((b, s, cat_width), BF16),
        scratch_shapes=[pltpu.VMEM((s, d), BF16)],
        compiler_params=_params(1),
        name="even_mix",
    )(x, w_in.astype(BF16), pool_w.astype(BF16), pool_scale.reshape(1, pool_width), conv_w)


def _odd_proj_kernel(x_ref, w_ref, *refs, c_width, dils):
    outs, h_ref = refs[:-1], refs[-1]
    h = _dot(x_ref[...].astype(BF16), w_ref[...])
    for c in range(h_ref.shape[0]):
        h_ref[c] = h[:, c * LANES:(c + 1) * LANES]
    tm = x_ref.shape[0]
    chunks = c_width // LANES
    for pi, dil in enumerate(dils):
        n = tm // dil
        for r in range(dil):
            for j in range(3):
                for c in range(chunks):
                    src = h_ref[j * chunks + c] if dil == 1 else h_ref[j * chunks + c, pl.ds(r, n, stride=dil), :]
                    lo = r * c_width + c * LANES
                    outs[3 * pi + j][:, lo:lo + LANES] = src.astype(BF16)
    c0 = 3 * c_width
    for ref in outs[3 * len(dils):]:
        wd = ref.shape[1]
        ref[...] = h[:, c0:c0 + wd].astype(BF16)
        c0 += wd


def _odd_proj(xt, w_in, c_width, dq_width, dkv_width):
    t, d = xt.shape
    dils = tuple(dil for _, dil in C_PATTERNS)
    assert 3 * c_width + dq_width + 2 * dkv_width == w_in.shape[1] and c_width % LANES == 0
    col = jnp.arange(w_in.shape[1])
    is_q = (col < c_width) | ((col >= 3 * c_width) & (col < 3 * c_width + dq_width))
    w_scaled = (w_in * jnp.where(is_q, HEAD_DIM ** -0.5, 1.0)).astype(BF16)
    shapes = [(t // dil, dil * c_width) for dil in dils for _ in range(3)]
    blocks = [(ROW_TILE // dil, dil * c_width) for dil in dils for _ in range(3)]
    for wd in (dq_width, dkv_width, dkv_width):
        shapes.append((t, wd))
        blocks.append((ROW_TILE, wd))
    outs = pl.pallas_call(
        functools.partial(_odd_proj_kernel, c_width=c_width, dils=dils),
        grid=(t // ROW_TILE,),
        in_specs=[pl.BlockSpec((ROW_TILE, d), lambda i: (i, 0)),
                  pl.BlockSpec(w_in.shape, lambda i: (0, 0))],
        out_specs=[pl.BlockSpec(blk, lambda i: (i, 0)) for blk in blocks],
        out_shape=[jax.ShapeDtypeStruct(shp, BF16) for shp in shapes],
        scratch_shapes=[pltpu.VMEM((w_in.shape[1] // LANES, ROW_TILE, LANES), F32)],
        compiler_params=_params(1),
        name="odd_proj",
    )(xt, w_scaled)
    qkv = [outs[3 * pi:3 * pi + 3] for pi in range(len(dils))]
    return qkv, outs[3 * len(dils):]


def _band_geometry(i, length, radius):
    qb = min(ATTN_Q_BLOCK, length)
    span = min(length, qb + 2 * radius)
    q0 = pl.multiple_of(i * qb, qb)
    align = 8
    for cand in (128, 64, 32, 16):
        if qb % cand == 0 and radius % cand == 0 and (length - span) % cand == 0:
            align = cand
            break
    start = pl.multiple_of(jnp.clip(q0 - radius, 0, length - span), align)
    qpos = q0 + lax.broadcasted_iota(jnp.int32, (qb, span), 0)
    kpos = start + lax.broadcasted_iota(jnp.int32, (qb, span), 1)
    dist = jnp.abs(kpos - qpos)
    mask = jnp.where(dist <= radius, 0.0, NEG_INF).astype(F32)
    return q0, qb, start, span, dist.astype(F32), mask


def _attend_heads(scores, values):
    probs = []
    for s in scores:
        m = jnp.max(s, axis=-1, keepdims=True)
        p = jnp.exp(s - m)
        l = jnp.sum(p, axis=-1, keepdims=True)
        probs.append((p.astype(BF16), l, m + jnp.log(l)))
    return [(_dot(p, vh) / l, lse) for (p, l, lse), vh in zip(probs, values)]


def _attn_c_kernel(q_ref, k_ref, v_ref, o_ref, lse_ref, *, dil, radius, n_heads):
    length = q_ref.shape[1]
    qb = min(ATTN_Q_BLOCK, length)
    width = n_heads * HEAD_DIM

    def block(i, carry):
        q0, _, start, span, dist, mask = _band_geometry(i, length, radius)
        dist = dist * float(dil)
        for res in range(q_ref.shape[2] // width):
            cols = [res * width + hh * HEAD_DIM for hh in range(n_heads)]
            scores = []
            for hh, lo in enumerate(cols):
                qh = q_ref[0, pl.ds(q0, qb), lo:lo + HEAD_DIM]
                kh = k_ref[0, pl.ds(start, span), lo:lo + HEAD_DIM]
                slope = 2.0 ** (-8.0 * (hh + 1) / n_heads)
                scores.append(_dot_nt(qh, kh) - slope * dist + mask)
            values = [v_ref[0, pl.ds(start, span), lo:lo + HEAD_DIM] for lo in cols]
            for lo, (o, lse) in zip(cols, _attend_heads(scores, values)):
                o_ref[0, pl.ds(q0, qb), lo:lo + HEAD_DIM] = o.astype(o_ref.dtype)
                lse_ref[0, pl.ds(q0, qb), lo:lo + HEAD_DIM] = jnp.broadcast_to(lse, (qb, HEAD_DIM))
        return carry

    lax.fori_loop(0, length // qb, block, 0)


def _attn_c(q, k, v, bsz, window, dil):
    rows, dw = q.shape
    w = dw // dil
    length = rows // bsz
    radius = window // 2 // dil
    assert length % min(ATTN_Q_BLOCK, length) == 0
    per_step = max(1, min(dil, ATTN_STEP_ROWS // length))
    assert dil % per_step == 0
    view = lambda a: a.reshape(bsz, length, dw)
    spec = pl.BlockSpec((1, length, per_step * w), lambda i, r: (i, 0, r))
    o, lse = pl.pallas_call(
        functools.partial(_attn_c_kernel, dil=dil, radius=radius, n_heads=w // HEAD_DIM),
        grid=(bsz, dil // per_step),
        in_specs=[spec, spec, spec],
        out_specs=[spec, spec],
        out_shape=[jax.ShapeDtypeStruct((bsz, length, dw), BF16),
                   jax.ShapeDtypeStruct((bsz, length, dw), F32)],
        compiler_params=_params(2),
        name=f"attn_c_d{dil}",
    )(view(q), view(k), view(v))
    return o.reshape(rows, dw), lse.reshape(rows, dw)


def _attn_d_kernel(sink_ref, q_ref, k_ref, v_ref, y_ref, *, radius, n_heads, group):
    length = q_ref.shape[1]
    qb = min(ATTN_Q_BLOCK, length)

    def block(i, carry):
        q0, _, start, span, dist, mask = _band_geometry(i, length, radius)
        scores = []
        for hh in range(n_heads):
            lo, kv = hh * HEAD_DIM, (hh // group) * HEAD_DIM
            qh = q_ref[0, pl.ds(q0, qb), lo:lo + HEAD_DIM]
            kh = k_ref[0, pl.ds(start, span), kv:kv + HEAD_DIM]
            slope = 2.0 ** (-8.0 * (hh + 1) / n_heads)
            scores.append(_dot_nt(qh, kh) - slope * dist + mask)
        values = [v_ref[0, pl.ds(start, span), (hh // group) * HEAD_DIM:(hh // group + 1) * HEAD_DIM]
                  for hh in range(n_heads)]
        for hh, (o, lse) in enumerate(_attend_heads(scores, values)):
            gated = o * jax.nn.sigmoid(lse - sink_ref[hh])
            y_ref[0, pl.ds(q0, qb), hh * HEAD_DIM:(hh + 1) * HEAD_DIM] = gated.astype(y_ref.dtype)
        return carry

    lax.fori_loop(0, length // qb, block, 0)


def _attn_d(q, k, v, sink):
    b, s, w = q.shape
    kvw = k.shape[2]
    n_heads = w // HEAD_DIM
    return pl.pallas_call(
        functools.partial(_attn_d_kernel, radius=D_RADIUS, n_heads=n_heads, group=n_heads // D_KV_HEADS),
        grid=(b,),
        in_specs=[pl.BlockSpec(memory_space=pltpu.SMEM),
                  pl.BlockSpec((1, s, w), lambda i: (i, 0, 0)),
                  pl.BlockSpec((1, s, kvw), lambda i: (i, 0, 0)),
                  pl.BlockSpec((1, s, kvw), lambda i: (i, 0, 0))],
        out_specs=pl.BlockSpec((1, s, w), lambda i: (i, 0, 0)),
        out_shape=jax.ShapeDtypeStruct((b, s, w), BF16),
        compiler_params=_params(1),
        name="attn_d",
    )(sink.astype(F32), q, k, v)


def _merge_kernel(*refs, dils, cw):
    n_pat = len(dils)
    o_refs, l_refs = refs[:n_pat], refs[n_pat:2 * n_pat]
    yd_ref, cat_ref = refs[2 * n_pat], refs[2 * n_pat + 1]
    scratch = list(refs[2 * n_pat + 2:])
    tm = cat_ref.shape[0]
    chunks = cw // LANES
    staged = []
    for p, dil in enumerate(dils):
        if dil == 1:
            staged.append(None)
            continue
        o_s, l_s = scratch.pop(0), scratch.pop(0)
        n = tm // dil
        for r in range(dil):
            for c in range(chunks):
                lo = r * cw + c * LANES
                o_s[c, pl.ds(r, n, stride=dil), :] = o_refs[p][:, lo:lo + LANES].astype(F32)
                l_s[c, pl.ds(r, n, stride=dil), :] = l_refs[p][:, lo:lo + LANES]
        staged.append((o_s, l_s))
    for c in range(chunks):
        cols = slice(c * LANES, (c + 1) * LANES)
        os_ = [o_refs[p][:, cols].astype(F32) if st is None else st[0][c] for p, st in enumerate(staged)]
        ls_ = [l_refs[p][:, cols] if st is None else st[1][c] for p, st in enumerate(staged)]
        m = functools.reduce(jnp.maximum, ls_)
        es = [jnp.exp(l - m) for l in ls_]
        num = functools.reduce(lambda a, b: a + b, [e * o for e, o in zip(es, os_)])
        den = functools.reduce(lambda a, b: a + b, es)
        cat_ref[:, cols] = (num / den).astype(BF16)
    cat_ref[:, cw:] = yd_ref[...]


def _merge(os_, lses, yd, dils):
    t, dw = yd.shape
    cw = os_[0].shape[1] // dils[0]
    vspecs = [pl.BlockSpec((ROW_TILE // dil, dil * cw), lambda i: (i, 0)) for dil in dils]
    n_scratch = 2 * sum(1 for dil in dils if dil != 1)
    return pl.pallas_call(
        functools.partial(_merge_kernel, dils=dils, cw=cw),
        grid=(t // ROW_TILE,),
        in_specs=vspecs + vspecs + [pl.BlockSpec((ROW_TILE, dw), lambda i: (i, 0))],
        out_specs=pl.BlockSpec((ROW_TILE, cw + dw), lambda i: (i, 0)),
        out_shape=jax.ShapeDtypeStruct((t, cw + dw), BF16),
        scratch_shapes=[pltpu.VMEM((cw // LANES, ROW_TILE, LANES), F32)] * n_scratch,
        compiler_params=_params(1),
        name="merge",
    )(*os_, *lses, yd)


def _layer_norm(z, g, b):
    mu = jnp.mean(z, axis=-1, keepdims=True)
    zc = z - mu
    var = jnp.mean(zc * zc, axis=-1, keepdims=True)
    return zc * lax.rsqrt(var + LN_EPS) * g + b


def _route_tile(logits, tri, count_ref):
    tm, n_exp = logits.shape
    lane = lax.broadcasted_iota(jnp.int32, (tm, n_exp), 1)
    col = lax.broadcasted_iota(jnp.int32, (tm, TOP_K), 1)
    work = logits
    vals, onehots = [], []
    idx_out = jnp.zeros((tm, TOP_K), jnp.int32)
    for k in range(TOP_K):
        m = jnp.max(work, axis=-1, keepdims=True)
        idx = jnp.min(jnp.where(work == m, lane, n_exp), axis=-1, keepdims=True)
        hot = lane == idx
        work = jnp.where(hot, -jnp.inf, work)
        vals.append(m)
        onehots.append(hot)
        idx_out = jnp.where(col == k, idx, idx_out)
    exps = [jnp.exp(v - vals[0]) for v in vals]
    den = functools.reduce(lambda a, b: a + b, exps)
    gate_out = jnp.zeros((tm, TOP_K), F32)
    for k in range(TOP_K):
        gate_out = jnp.where(col == k, exps[k] / den, gate_out)
    multihot = functools.reduce(lambda a, b: a + b, [h.astype(F32) for h in onehots])
    before = _dot(tri, multihot.astype(BF16)) + count_ref[...]
    rank_out = jnp.zeros((tm, TOP_K), jnp.int32)
    for k in range(TOP_K):
        rank = jnp.sum(jnp.where(onehots[k], before, 0.0), axis=-1, keepdims=True)
        rank_out = jnp.where(col == k, rank.astype(jnp.int32), rank_out)
    count_ref[...] += jnp.sum(multihot, axis=0, keepdims=True)
    return idx_out, gate_out, rank_out


def _out_ln_kernel(cat_ref, w_ref, x_ref, g_ref, b_ref, rw_hi_ref, rw_lo_ref, rb_ref, tri_ref,
                   x1_ref, x1p_ref, idx_ref, gate_ref, rank_ref, count_ref, *, alpha):
    @pl.when(pl.program_id(0) == 0)
    def _():
        count_ref[...] = jnp.zeros_like(count_ref)

    mix = _dot(cat_ref[...], w_ref[...])
    x1 = _layer_norm(alpha * x_ref[...] + mix, g_ref[...], b_ref[...])
    x1_ref[...] = x1
    x1p_ref[...] = _pack_pairs(x1)
    hi = x1.astype(BF16)
    lo = (x1 - hi.astype(F32)).astype(BF16)
    logits = (_dot(hi, rw_hi_ref[...]) + _dot(lo, rw_hi_ref[...]) + _dot(hi, rw_lo_ref[...]) + rb_ref[...])
    idx_ref[...], gate_ref[...], rank_ref[...] = _route_tile(logits, tri_ref[...], count_ref)


def _out_ln(cat, w_out, xt, g, b, router_w, router_b, alpha):
    t, d = xt.shape
    n_exp = router_w.shape[1]
    rw_hi = router_w.astype(BF16)
    rw_lo = (router_w - rw_hi.astype(F32)).astype(BF16)
    tri = jnp.tril(jnp.ones((ROW_TILE, ROW_TILE), BF16), -1)
    row = lambda wd: pl.BlockSpec((ROW_TILE, wd), lambda i: (i, 0))
    full = lambda a: pl.BlockSpec(a.shape, lambda i: (0,) * a.ndim)
    args = (cat, w_out.astype(BF16), xt, g.reshape(1, d), b.reshape(1, d), rw_hi, rw_lo,
            router_b.reshape(1, n_exp), tri)
    return pl.pallas_call(
        functools.partial(_out_ln_kernel, alpha=alpha),
        grid=(t // ROW_TILE,),
        in_specs=[row(cat.shape[1]), full(args[1]), row(d)] + [full(a) for a in args[3:]],
        out_specs=[row(d), row(d // 2), row(TOP_K), row(TOP_K), row(TOP_K),
                   pl.BlockSpec((1, n_exp), lambda i: (0, 0))],
        out_shape=[jax.ShapeDtypeStruct((t, d), F32), jax.ShapeDtypeStruct((t, d // 2), jnp.int32),
                   jax.ShapeDtypeStruct((t, TOP_K), jnp.int32), jax.ShapeDtypeStruct((t, TOP_K), F32),
                   jax.ShapeDtypeStruct((t, TOP_K), jnp.int32), jax.ShapeDtypeStruct((1, n_exp), F32)],
        compiler_params=_params(1),
        name="out_ln",
    )(*args)


def _experts_kernel(block_e_ref, valid_ref, x_ref, wgu_ref, bgu_ref, wd_ref, bd_ref, perm_ref, y_ref,
                    wgu_s, wd_s):
    i = pl.program_id(0)
    valid = valid_ref[i]
    half = GU_BLOCK // 2

    @pl.when((i == 0) | (block_e_ref[i] != block_e_ref[jnp.maximum(i - 1, 0)]))
    def _():
        for c in range(wgu_ref.shape[1] // GU_BLOCK):
            cols = slice(c * GU_BLOCK, (c + 1) * GU_BLOCK)
            wgu_s[:, cols] = _dot(wgu_ref[:, cols].astype(BF16), perm_ref[...]).astype(BF16)
        wd_s[...] = wd_ref[...].astype(BF16)

    @pl.when(valid > 0)
    def _():
        xb = _unpack_pairs(x_ref[...])
        row = lax.broadcasted_iota(jnp.int32, xb.shape, 0)
        xb = jnp.where(row < valid, xb, jnp.zeros_like(xb))
        h = _dot(xb, wgu_s[...]) + bgu_ref[...]
        acts = []
        for c in range(h.shape[1] // GU_BLOCK):
            glu = jnp.minimum(h[:, c * GU_BLOCK:c * GU_BLOCK + half], SWIGLU_LIMIT)
            lin = jnp.clip(h[:, c * GU_BLOCK + half:(c + 1) * GU_BLOCK], -SWIGLU_LIMIT, SWIGLU_LIMIT)
            acts.append((glu * jax.nn.sigmoid(SWIGLU_ALPHA * glu) * (lin + 1.0)).astype(BF16))
        act = jnp.concatenate(acts, axis=1)
        y_ref[...] = _pack_pairs(_dot(act, wd_s[...]) + bd_ref[...])

    @pl.when(valid == 0)
    def _():
        y_ref[...] = jnp.zeros_like(y_ref)


def _experts(x_rows, block_e, block_valid, layer, w_gu, b_gu, w_down, b_down):
    rows, dp = x_rows.shape
    _, n_exp, d, de2 = w_gu.shape
    assert de2 % GU_BLOCK == 0 and dp * 2 == d
    n_blocks = rows // MOE_BLOCK
    half = GU_BLOCK // 2
    j = jnp.arange(GU_BLOCK)
    src = jnp.where(j < half, 2 * j, 2 * (j - half) + 1)
    perm = (jnp.arange(GU_BLOCK)[:, None] == src[None, :]).astype(BF16)
    wspec = lambda a: pl.BlockSpec((None, None) + a.shape[2:], lambda i, be, bv: (layer, be[i], 0, 0))
    grid_spec = pltpu.PrefetchScalarGridSpec(
        num_scalar_prefetch=2,
        grid=(n_blocks,),
        in_specs=[pl.BlockSpec((MOE_BLOCK, dp), lambda i, be, bv: (i, 0)),
                  wspec(w_gu), wspec(b_gu), wspec(w_down), wspec(b_down),
                  pl.BlockSpec((GU_BLOCK, GU_BLOCK), lambda i, be, bv: (0, 0))],
        out_specs=pl.BlockSpec((MOE_BLOCK, dp), lambda i, be, bv: (i, 0)),
        scratch_shapes=[pltpu.VMEM(w_gu.shape[2:], BF16), pltpu.VMEM(w_down.shape[2:], BF16)],
    )
    return pl.pallas_call(
        _experts_kernel,
        grid_spec=grid_spec,
        out_shape=jax.ShapeDtypeStruct((rows, dp), jnp.int32),
        compiler_params=_params(1),
        name="experts",
    )(block_e, block_valid, x_rows, w_gu, b_gu, w_down, b_down, perm)


def _combine_ln_kernel(x1_ref, yk_ref, gate_ref, g_ref, b_ref, x2_ref, *, alpha):
    gates = gate_ref[...]
    ffn = gates[:, 0:1] * _unpack_pairs(yk_ref[0]).astype(F32)
    for k in range(1, yk_ref.shape[0]):
        ffn = ffn + gates[:, k:k + 1] * _unpack_pairs(yk_ref[k]).astype(F32)
    x2_ref[...] = _layer_norm(alpha * x1_ref[...] + ffn, g_ref[...], b_ref[...])


def _combine_ln(x1, yk, gates, g, b, alpha):
    t, d = x1.shape
    k = yk.shape[0]
    return pl.pallas_call(
        functools.partial(_combine_ln_kernel, alpha=alpha),
        grid=(t // ROW_TILE,),
        in_specs=[pl.BlockSpec((ROW_TILE, d), lambda i: (i, 0)),
                  pl.BlockSpec((k, ROW_TILE, d // 2), lambda i: (0, i, 0)),
                  pl.BlockSpec((ROW_TILE, k), lambda i: (i, 0)),
                  pl.BlockSpec((1, d), lambda i: (0, 0)),
                  pl.BlockSpec((1, d), lambda i: (0, 0))],
        out_specs=pl.BlockSpec((ROW_TILE, d), lambda i: (i, 0)),
        out_shape=jax.ShapeDtypeStruct((t, d), F32),
        compiler_params=_params(1),
        name="combine_ln",
    )(x1, yk, gates, g.reshape(1, d), b.reshape(1, d))


def _route(top_idx, rank, counts):
    t = top_idx.shape[0]
    n_exp = counts.shape[1]
    experts = jnp.arange(n_exp, dtype=jnp.int32)
    counts = counts.reshape(n_exp).astype(jnp.int32)
    padded = (counts + MOE_BLOCK - 1) // MOE_BLOCK * MOE_BLOCK
    pad_end = jnp.cumsum(padded)
    pad_start = pad_end - padded
    idx_flat = top_idx.reshape(-1)
    start_flat = jnp.sum(jnp.where(idx_flat[None, :] == experts[:, None], pad_start[:, None], 0), axis=0)
    pos = (rank.reshape(-1) + start_flat).reshape(t, TOP_K)
    n_blocks = -(-(t * TOP_K) // MOE_BLOCK) + n_exp
    blk = jnp.arange(n_blocks, dtype=jnp.int32) * MOE_BLOCK
    block_e = jnp.minimum(jnp.sum((pad_end[None, :] <= blk[:, None]).astype(jnp.int32), axis=1), n_exp - 1)
    mine = block_e[:, None] == experts[None, :]
    pick = lambda v: jnp.sum(jnp.where(mine, v[None, :], 0), axis=1)
    block_valid = jnp.clip(pick(counts) - (blk - pick(pad_start)), 0, MOE_BLOCK).astype(jnp.int32)
    return pos, block_e, block_valid


def _moe(x1, x1p, top_idx, gates, rank, counts, layer, w_gu, b_gu, w_down, b_down, g, b, alpha):
    t, d = x1.shape
    n_exp = counts.shape[1]
    pos, block_e, block_valid = _route(top_idx, rank, counts)
    rows = (-(-(t * TOP_K) // MOE_BLOCK) + n_exp) * MOE_BLOCK
    pos3 = pos.reshape(t // SC_CHUNK, SC_CHUNK, TOP_K).transpose(0, 2, 1)
    x_rows = _sc_scatter_rows(x1p, pos3, rows)
    y = _experts(x_rows, block_e, block_valid, layer, w_gu, b_gu, w_down, b_down)
    yk = _sc_gather_rows(y, pos.T.reshape(-1)).reshape(TOP_K, t, d // 2)
    return _combine_ln(x1, yk, gates, g, b, alpha)


def kernel(x, ev_w_in, ev_pool_w, ev_pool_scale, ev_conv_w, ev_w_out, od_w_in, od_sink, od_w_out,
           router_w, router_b, exp_w_gu, exp_b_gu, exp_w_down, exp_b_down, ln_g, ln_b):
    bsz, seq, d = x.shape
    t = bsz * seq
    depth = ln_g.shape[0]
    alpha = (2 * depth) ** 0.25
    n_exp = router_w.shape[2]
    c_width = d // 2
    dq_width = d // 2
    dkv_width = D_KV_HEADS * HEAD_DIM
    assert t % ROW_TILE == 0

    de2 = exp_b_gu.shape[-1]
    b_gu = exp_b_gu.reshape(depth, n_exp, de2 // GU_BLOCK, GU_BLOCK // 2, 2)
    b_gu = jnp.swapaxes(b_gu, -1, -2).reshape(depth, n_exp, 1, de2)
    b_down = exp_b_down.reshape(depth, n_exp, 1, -1)

    xt = x.reshape(t, d)
    for layer in range(depth):
        i = layer // 2
        if layer % 2 == 0:
            cat = _even_mix(xt.reshape(bsz, seq, d), ev_w_in[i], ev_pool_w[i], ev_pool_scale[i], ev_conv_w[i])
            cat = cat.reshape(t, -1)
            w_out = ev_w_out[i]
        else:
            qkv, (qd, kd, vd) = _odd_proj(xt, od_w_in[i], c_width, dq_width, dkv_width)
            dils = tuple(dil for _, dil in C_PATTERNS)
            os_, lses = [], []
            for (window, dil), (qc, kc, vc) in zip(C_PATTERNS, qkv):
                o, lse = _attn_c(qc, kc, vc, bsz, window, dil)
                os_.append(o)
                lses.append(lse)
            seq3 = lambda a: a.reshape(bsz, seq, a.shape[1])
            yd = _attn_d(seq3(qd), seq3(kd), seq3(vd), od_sink[i]).reshape(t, dq_width)
            cat = _merge(os_, lses, yd, dils)
            w_out = od_w_out[i]
        x1, x1p, top_idx, gates, rank, counts = _out_ln(cat, w_out, xt, ln_g[layer, 0], ln_b[layer, 0],
                                                        router_w[layer], router_b[layer], alpha)
        xt = _moe(x1, x1p, top_idx, gates, rank, counts, layer, exp_w_gu, b_gu, exp_w_down, b_down,
                  ln_g[layer, 1], ln_b[layer, 1], alpha)
    return xt.reshape(bsz, seq, d)
```

```python
import functools

import jax
import jax.numpy as jnp
from jax import lax
from jax.experimental import pallas as pl
from jax.experimental.pallas import tpu as pltpu
from jax.experimental.pallas import tpu_sc as plsc

HEAD_DIM = 64
POOL_WINDOWS = (2, 4, 8, 16)
C_PATTERNS = ((128, 1), (512, 4), (2048, 16))
D_KV_HEADS = 2
D_RADIUS = 128
TOP_K = 4
SWIGLU_LIMIT = 7.0
SWIGLU_ALPHA = 1.702
MOE_BLOCK = 512
LN_EPS = 1e-5
NEG_INF = -1e30

ROW_TILE = 512
ATTN_Q_BLOCK = 128
ATTN_STEP_ROWS = 512
LANES = 128
GU_BLOCK = 256
VMEM_LIMIT_BYTES = 56 * 1024 * 1024

F32 = jnp.float32
BF16 = jnp.bfloat16


def _params(n_axes=1):
    return pltpu.CompilerParams(dimension_semantics=("arbitrary",) * n_axes,
                                vmem_limit_bytes=VMEM_LIMIT_BYTES)


def _dot(a, b):
    return jnp.dot(a, b, preferred_element_type=F32)


def _dot_nt(a, b):
    return lax.dot_general(a, b, (((1,), (1,)), ((), ())), preferred_element_type=F32)


def _pack_pairs(a):
    n = a.shape[1] // 2
    bits = lax.bitcast_convert_type(a.astype(BF16).astype(F32), jnp.int32)
    return bits[:, :n] | lax.shift_right_logical(bits[:, n:], 16)


def _unpack_pairs(w):
    hi = lax.bitcast_convert_type(w & jnp.int32(-65536), F32)
    lo = lax.bitcast_convert_type(lax.shift_left(w, 16), F32)
    return jnp.concatenate([hi, lo], axis=1).astype(BF16)


SC_CORES = 2
SC_SUBCORES = 16
SC_WORKERS = SC_CORES * SC_SUBCORES
SC_CHUNK = 64


def _sc_mesh():
    return plsc.VectorSubcoreMesh(core_axis_name="c", subcore_axis_name="s")


def _sc_worker():
    return lax.axis_index("s") * SC_CORES + lax.axis_index("c")


def _sc_gather_rows(table, idx):
    n, w = idx.shape[0], table.shape[1]
    per_w = n // SC_WORKERS
    assert n % SC_WORKERS == 0 and per_w % (2 * SC_CHUNK) == 0

    def body(table_hbm, idx_hbm, out_hbm, i0, i1, r0, r1, s0, s1):
        base = _sc_worker() * per_w

        def start(off, iv, rv, sem):
            pltpu.sync_copy(idx_hbm.at[pl.ds(off, SC_CHUNK)], iv)
            pltpu.async_copy(table_hbm.at[iv], rv, sem)

        def finish(off, iv, rv, sem):
            pltpu.make_async_copy(table_hbm.at[iv], rv, sem).wait()
            pltpu.sync_copy(rv, out_hbm.at[pl.ds(off, SC_CHUNK)])

        start(base, i0, r0, s0)

        @pl.loop(0, per_w, step=2 * SC_CHUNK)
        def _(o):
            off = base + o
            start(off + SC_CHUNK, i1, r1, s1)
            finish(off, i0, r0, s0)

            @pl.when(o + 2 * SC_CHUNK < per_w)
            def _():
                start(off + 2 * SC_CHUNK, i0, r0, s0)

            finish(off + SC_CHUNK, i1, r1, s1)

    return pl.kernel(
        body, mesh=_sc_mesh(),
        out_type=jax.ShapeDtypeStruct((n, w), table.dtype),
        scratch_types=[pltpu.VMEM((SC_CHUNK,), jnp.int32), pltpu.VMEM((SC_CHUNK,), jnp.int32),
                       pltpu.VMEM((SC_CHUNK, w), table.dtype), pltpu.VMEM((SC_CHUNK, w), table.dtype),
                       pltpu.SemaphoreType.DMA, pltpu.SemaphoreType.DMA],
    )(table, idx)


def _sc_scatter_rows(src, pos3, n_out):
    t, w = src.shape
    kk = pos3.shape[1]
    per_w = t // SC_WORKERS
    assert t % SC_WORKERS == 0 and per_w % SC_CHUNK == 0 and pos3.shape == (t // SC_CHUNK, kk, SC_CHUNK)

    def body(src_hbm, pos_hbm, out_hbm, iv, rv, sem):
        base = _sc_worker() * per_w

        @pl.loop(0, per_w, step=SC_CHUNK)
        def _(o):
            off = base + o
            pltpu.sync_copy(pos_hbm.at[off // SC_CHUNK], iv)
            pltpu.sync_copy(src_hbm.at[pl.ds(off, SC_CHUNK)], rv)
            copies = [pltpu.async_copy(rv, out_hbm.at[iv.at[j]], sem) for j in range(kk)]
            for cp in copies:
                cp.wait()

    return pl.kernel(
        body, mesh=_sc_mesh(),
        out_type=jax.ShapeDtypeStruct((n_out, w), src.dtype),
        scratch_types=[pltpu.VMEM((kk, SC_CHUNK), jnp.int32), pltpu.VMEM((SC_CHUNK, w), src.dtype),
                       pltpu.SemaphoreType.DMA],
    )(src, pos3)


def _shift_down(a, k, row):
    return jnp.where(row >= k, pltpu.roll(a, k, axis=0), 0.0)


def _shift_up(a, k, row):
    n = a.shape[0]
    return jnp.where(row < n - k, pltpu.roll(a, n - k, axis=0), 0.0)


def _even_mix_kernel(x_ref, w_in_ref, pool_w_ref, pool_scale_ref, conv_w_ref, cat_ref, xb_ref):
    s = x_ref.shape[1]
    pool_width = pool_scale_ref.shape[1]
    gd = pool_width // len(POOL_WINDOWS)
    conv_width = conv_w_ref.shape[1]
    xb_ref[...] = x_ref[0].astype(BF16)
    xb = xb_ref[...]

    row = lax.broadcasted_iota(jnp.int32, (s, gd), 0)
    for g, w in enumerate(POOL_WINDOWS):
        lo = g * gd
        if g % 2 == 0:
            u2 = _dot(xb, w_in_ref[:, lo:lo + 2 * gd])
        u = u2[:, (g % 2) * gd:(g % 2 + 1) * gd]
        half = w // 2
        back, fwd, span = u, u, 1
        while span < half:
            back = back + _shift_down(back, span, row)
            fwd = fwd + _shift_up(fwd, span, row)
            span *= 2
        win = _shift_down(back, 1, row) + fwd
        cnt = (jnp.minimum(row + (w - half), s) - jnp.maximum(row - half, 0)).astype(F32)
        pooled = win / cnt - u
        mixed = _dot(pooled.astype(BF16), pool_w_ref[g])
        cat_ref[0, :, lo:lo + gd] = (mixed * pool_scale_ref[:, lo:lo + gd]).astype(BF16)

    cw = 256
    rowc = lax.broadcasted_iota(jnp.int32, (s, cw), 0)
    for j in range(conv_width // cw):
        c0 = j * cw
        b_gate = _dot(xb, w_in_ref[:, pool_width + c0:pool_width + c0 + cw])
        c_gate = _dot(xb, w_in_ref[:, pool_width + conv_width + c0:pool_width + conv_width + c0 + cw])
        v = _dot(xb, w_in_ref[:, pool_width + 2 * conv_width + c0:pool_width + 2 * conv_width + c0 + cw])
        u = c_gate * v
        conv = (_shift_down(u, 1, rowc) * conv_w_ref[0:1, c0:c0 + cw] + u * conv_w_ref[1:2, c0:c0 + cw]
                + _shift_up(u, 1, rowc) * conv_w_ref[2:3, c0:c0 + cw])
        cat_ref[0, :, pool_width + c0:pool_width + c0 + cw] = (b_gate * conv).astype(BF16)


def _even_mix(x, w_in, pool_w, pool_scale, conv_w):
    b, s, d = x.shape
    pool_width = pool_scale.shape[0]
    conv_width = conv_w.shape[1]
    assert conv_width % 256 == 0 and w_in.shape[1] == pool_width + 3 * conv_width
    cat_width = pool_width + conv_width
    return pl.pallas_call(
        _even_mix_kernel,
        grid=(b,),
        in_specs=[
            pl.BlockSpec((1, s, d), lambda i: (i, 0, 0)),
            pl.BlockSpec(w_in.shape, lambda i: (0, 0)),
            pl.BlockSpec(pool_w.shape, lambda i: (0, 0, 0)),
            pl.BlockSpec((1, pool_width), lambda i: (0, 0)),
            pl.BlockSpec(conv_w.shape, lambda i: (0, 0)),
        ],
        out_specs=pl.BlockSpec((1, s, cat_width), lambda i: (i, 0, 0)),
        out_shape=jax.ShapeDtypeStruct((b, s, cat_width), BF16),
        scratch_shapes=[pltpu.VMEM((s, d), BF16)],
        compiler_params=_params(1),
        name="even_mix",
    )(x, w_in.astype(BF16), pool_w.astype(BF16), pool_scale.reshape(1, pool_width), conv_w)


def _odd_proj_kernel(x_ref, w_ref, *refs, c_width, dils):
    outs, h_ref = refs[:-1], refs[-1]
    h = _dot(x_ref[...].astype(BF16), w_ref[...])
    for c in range(h_ref.shape[0]):
        h_ref[c] = h[:, c * LANES:(c + 1) * LANES]
    tm = x_ref.shape[0]
    chunks = c_width // LANES
    for pi, dil in enumerate(dils):
        n = tm // dil
        for r in range(dil):
            for j in range(3):
                for c in range(chunks):
                    src = h_ref[j * chunks + c] if dil == 1 else h_ref[j * chunks + c, pl.ds(r, n, stride=dil), :]
                    lo = r * c_width + c * LANES
                    outs[3 * pi + j][:, lo:lo + LANES] = src.astype(BF16)
    c0 = 3 * c_width
    for ref in outs[3 * len(dils):]:
        wd = ref.shape[1]
        ref[...] = h[:, c0:c0 + wd].astype(BF16)
        c0 += wd


def _odd_proj(xt, w_in, c_width, dq_width, dkv_width):
    t, d = xt.shape
    dils = tuple(dil for _, dil in C_PATTERNS)
    assert 3 * c_width + dq_width + 2 * dkv_width == w_in.shape[1] and c_width % LANES == 0
    col = jnp.arange(w_in.shape[1])
    is_q = (col < c_width) | ((col >= 3 * c_width) & (col < 3 * c_width + dq_width))
    w_in = w_in * jnp.where(is_q, HEAD_DIM ** -0.5, 1.0)
    kv0 = 3 * c_width + dq_width
    twice = lambda w: jnp.repeat(w.reshape(d, -1, HEAD_DIM), LANES // HEAD_DIM, axis=1).reshape(d, -1)
    w_in = jnp.concatenate([w_in[:, :kv0], twice(w_in[:, kv0:kv0 + dkv_width]), twice(w_in[:, kv0 + dkv_width:])], axis=1)
    dkv_width = dkv_width * (LANES // HEAD_DIM)
    w_scaled = w_in.astype(BF16)
    shapes = [(t // dil, dil * c_width) for dil in dils for _ in range(3)]
    blocks = [(ROW_TILE // dil, dil * c_width) for dil in dils for _ in range(3)]
    for wd in (dq_width, dkv_width, dkv_width):
        shapes.append((t, wd))
        blocks.append((ROW_TILE, wd))
    outs = pl.pallas_call(
        functools.partial(_odd_proj_kernel, c_width=c_width, dils=dils),
        grid=(t // ROW_TILE,),
        in_specs=[pl.BlockSpec((ROW_TILE, d), lambda i: (i, 0)),
                  pl.BlockSpec(w_in.shape, lambda i: (0, 0))],
        out_specs=[pl.BlockSpec(blk, lambda i: (i, 0)) for blk in blocks],
        out_shape=[jax.ShapeDtypeStruct(shp, BF16) for shp in shapes],
        scratch_shapes=[pltpu.VMEM((w_in.shape[1] // LANES, ROW_TILE, LANES), F32)],
        compiler_params=_params(1),
        name="odd_proj",
    )(xt, w_scaled)
    qkv = [outs[3 * pi:3 * pi + 3] for pi in range(len(dils))]
    return qkv, outs[3 * len(dils):]


def _band_geometry(i, length, radius):
    qb = min(ATTN_Q_BLOCK, length)
    span = min(length, qb + 2 * radius)
    q0 = pl.multiple_of(i * qb, qb)
    align = 8
    for cand in (128, 64, 32, 16):
        if qb % cand == 0 and radius % cand == 0 and (length - span) % cand == 0:
            align = cand
            break
    start = pl.multiple_of(jnp.clip(q0 - radius, 0, length - span), align)
    qpos = q0 + lax.broadcasted_iota(jnp.int32, (qb, span), 0)
    kpos = start + lax.broadcasted_iota(jnp.int32, (qb, span), 1)
    dist = jnp.abs(kpos - qpos)
    mask = jnp.where(dist <= radius, 0.0, NEG_INF).astype(F32)
    return q0, qb, start, span, dist.astype(F32), mask


def _attend_pairs(q_slabs, k_slabs, v_slabs, biases):
    lower = lax.broadcasted_iota(jnp.int32, (1, LANES), 1) < HEAD_DIM
    scores = []
    for j, (q2, k2) in enumerate(zip(q_slabs, k_slabs)):
        zero = jnp.zeros_like(q2)
        scores.append(_dot_nt(jnp.where(lower, q2, zero), k2) + biases[2 * j])
        scores.append(_dot_nt(jnp.where(lower, zero, q2), k2) + biases[2 * j + 1])
    probs = []
    for s in scores:
        m = jnp.max(s, axis=-1, keepdims=True)
        probs.append((jnp.exp(s - m).astype(BF16), m))
    ones = jnp.ones(v_slabs[0].shape, BF16)
    outs = []
    for j, v2 in enumerate(v_slabs):
        (p_lo, m_lo), (p_up, m_up) = probs[2 * j], probs[2 * j + 1]
        l = jnp.where(lower, _dot(p_lo, ones), _dot(p_up, ones))
        o = jnp.where(lower, _dot(p_lo, v2), _dot(p_up, v2)) / l
        outs.append((o, jnp.where(lower, m_lo, m_up) + jnp.log(l)))
    return outs


def _attn_c_kernel(q_ref, k_ref, v_ref, o_ref, lse_ref, *, dil, radius, n_heads):
    length = q_ref.shape[1]
    qb = min(ATTN_Q_BLOCK, length)
    width = n_heads * HEAD_DIM

    def block(i, carry):
        q0, _, start, span, dist, mask = _band_geometry(i, length, radius)
        dist = dist * float(dil)
        biases = [mask - 2.0 ** (-8.0 * (hh + 1) / n_heads) * dist for hh in range(n_heads)]
        for res in range(q_ref.shape[2] // width):
            slabs = [res * width + j * LANES for j in range(width // LANES)]
            outs = _attend_pairs([q_ref[0, pl.ds(q0, qb), lo:lo + LANES] for lo in slabs],
                                 [k_ref[0, pl.ds(start, span), lo:lo + LANES] for lo in slabs],
                                 [v_ref[0, pl.ds(start, span), lo:lo + LANES] for lo in slabs], biases)
            for lo, (o, lse) in zip(slabs, outs):
                o_ref[0, pl.ds(q0, qb), lo:lo + LANES] = o.astype(o_ref.dtype)
                lse_ref[0, pl.ds(q0, qb), lo:lo + LANES] = lse
        return carry

    lax.fori_loop(0, length // qb, block, 0)


def _attn_c(q, k, v, bsz, window, dil):
    rows, dw = q.shape
    w = dw // dil
    length = rows // bsz
    radius = window // 2 // dil
    assert length % min(ATTN_Q_BLOCK, length) == 0
    per_step = max(1, min(dil, ATTN_STEP_ROWS // length))
    assert dil % per_step == 0
    view = lambda a: a.reshape(bsz, length, dw)
    spec = pl.BlockSpec((1, length, per_step * w), lambda i, r: (i, 0, r))
    o, lse = pl.pallas_call(
        functools.partial(_attn_c_kernel, dil=dil, radius=radius, n_heads=w // HEAD_DIM),
        grid=(bsz, dil // per_step),
        in_specs=[spec, spec, spec],
        out_specs=[spec, spec],
        out_shape=[jax.ShapeDtypeStruct((bsz, length, dw), BF16),
                   jax.ShapeDtypeStruct((bsz, length, dw), F32)],
        compiler_params=_params(2),
        name=f"attn_c_d{dil}",
    )(view(q), view(k), view(v))
    return o.reshape(rows, dw), lse.reshape(rows, dw)


def _attn_d_kernel(sink_ref, q_ref, k_ref, v_ref, y_ref, *, radius, n_heads, group):
    length = q_ref.shape[1]
    qb = min(ATTN_Q_BLOCK, length)
    pairs = n_heads * HEAD_DIM // LANES
    lower = lax.broadcasted_iota(jnp.int32, (1, LANES), 1) < HEAD_DIM

    def block(i, carry):
        q0, _, start, span, dist, mask = _band_geometry(i, length, radius)
        biases = [mask - 2.0 ** (-8.0 * (hh + 1) / n_heads) * dist for hh in range(n_heads)]
        kv_lo = [(2 * j // group) * LANES for j in range(pairs)]
        outs = _attend_pairs([q_ref[0, pl.ds(q0, qb), j * LANES:(j + 1) * LANES] for j in range(pairs)],
                             [k_ref[0, pl.ds(start, span), lo:lo + LANES] for lo in kv_lo],
                             [v_ref[0, pl.ds(start, span), lo:lo + LANES] for lo in kv_lo], biases)
        for j, (o, lse) in enumerate(outs):
            sink = jnp.where(lower, sink_ref[2 * j], sink_ref[2 * j + 1])
            y_ref[0, pl.ds(q0, qb), j * LANES:(j + 1) * LANES] = (o * jax.nn.sigmoid(lse - sink)).astype(y_ref.dtype)
        return carry

    lax.fori_loop(0, length // qb, block, 0)


def _attn_d(q, k, v, sink):
    b, s, w = q.shape
    kvw = k.shape[2]
    n_heads = w // HEAD_DIM
    group = n_heads // D_KV_HEADS
    assert kvw == D_KV_HEADS * LANES and group % 2 == 0
    return pl.pallas_call(
        functools.partial(_attn_d_kernel, radius=D_RADIUS, n_heads=n_heads, group=group),
        grid=(b,),
        in_specs=[pl.BlockSpec(memory_space=pltpu.SMEM),
                  pl.BlockSpec((1, s, w), lambda i: (i, 0, 0)),
                  pl.BlockSpec((1, s, kvw), lambda i: (i, 0, 0)),
                  pl.BlockSpec((1, s, kvw), lambda i: (i, 0, 0))],
        out_specs=pl.BlockSpec((1, s, w), lambda i: (i, 0, 0)),
        out_shape=jax.ShapeDtypeStruct((b, s, w), BF16),
        compiler_params=_params(1),
        name="attn_d",
    )(sink.astype(F32), q, k, v)


def _merge_kernel(*refs, dils, cw):
    n_pat = len(dils)
    o_refs, l_refs = refs[:n_pat], refs[n_pat:2 * n_pat]
    yd_ref, cat_ref = refs[2 * n_pat], refs[2 * n_pat + 1]
    scratch = list(refs[2 * n_pat + 2:])
    tm = cat_ref.shape[0]
    chunks = cw // LANES
    staged = []
    for p, dil in enumerate(dils):
        if dil == 1:
            staged.append(None)
            continue
        o_s, l_s = scratch.pop(0), scratch.pop(0)
        n = tm // dil
        for r in range(dil):
            for c in range(chunks):
                lo = r * cw + c * LANES
                o_s[c, pl.ds(r, n, stride=dil), :] = o_refs[p][:, lo:lo + LANES].astype(F32)
                l_s[c, pl.ds(r, n, stride=dil), :] = l_refs[p][:, lo:lo + LANES]
        staged.append((o_s, l_s))
    for c in range(chunks):
        cols = slice(c * LANES, (c + 1) * LANES)
        os_ = [o_refs[p][:, cols].astype(F32) if st is None else st[0][c] for p, st in enumerate(staged)]
        ls_ = [l_refs[p][:, cols] if st is None else st[1][c] for p, st in enumerate(staged)]
        m = functools.reduce(jnp.maximum, ls_)
        es = [jnp.exp(l - m) for l in ls_]
        num = functools.reduce(lambda a, b: a + b, [e * o for e, o in zip(es, os_)])
        den = functools.reduce(lambda a, b: a + b, es)
        cat_ref[:, cols] = (num / den).astype(BF16)
    cat_ref[:, cw:] = yd_ref[...]


def _merge(os_, lses, yd, dils):
    t, dw = yd.shape
    cw = os_[0].shape[1] // dils[0]
    vspecs = [pl.BlockSpec((ROW_TILE // dil, dil * cw), lambda i: (i, 0)) for dil in dils]
    n_scratch = 2 * sum(1 for dil in dils if dil != 1)
    return pl.pallas_call(
        functools.partial(_merge_kernel, dils=dils, cw=cw),
        grid=(t // ROW_TILE,),
        in_specs=vspecs + vspecs + [pl.BlockSpec((ROW_TILE, dw), lambda i: (i, 0))],
        out_specs=pl.BlockSpec((ROW_TILE, cw + dw), lambda i: (i, 0)),
        out_shape=jax.ShapeDtypeStruct((t, cw + dw), BF16),
        scratch_shapes=[pltpu.VMEM((cw // LANES, ROW_TILE, LANES), F32)] * n_scratch,
        compiler_params=_params(1),
        name="merge",
    )(*os_, *lses, yd)


def _layer_norm(z, g, b):
    mu = jnp.mean(z, axis=-1, keepdims=True)
    zc = z - mu
    var = jnp.mean(zc * zc, axis=-1, keepdims=True)
    return zc * lax.rsqrt(var + LN_EPS) * g + b


ROUTE_ROWS = 8


def _route_tile(logits, triu, count_ref):
    n_exp, tm = logits.shape
    sub = lax.broadcasted_iota(jnp.int32, (n_exp, tm), 0)
    out_row = lax.broadcasted_iota(jnp.int32, (ROUTE_ROWS, tm), 0)
    work = logits
    vals, onehots = [], []
    idx_out = jnp.zeros((ROUTE_ROWS, tm), jnp.int32)
    for k in range(TOP_K):
        m = jnp.max(work, axis=0, keepdims=True)
        idx = jnp.min(jnp.where(work == m, sub, n_exp), axis=0, keepdims=True)
        hot = sub == idx
        work = jnp.where(hot, -jnp.inf, work)
        vals.append(m)
        onehots.append(hot)
        idx_out = jnp.where(out_row == k, idx, idx_out)
    exps = [jnp.exp(v - vals[0]) for v in vals]
    den = functools.reduce(lambda a, b: a + b, exps)
    gate_out = jnp.zeros((ROUTE_ROWS, tm), F32)
    for k in range(TOP_K):
        gate_out = jnp.where(out_row == k, exps[k] / den, gate_out)
    multihot = functools.reduce(lambda a, b: a + b, [h.astype(F32) for h in onehots])
    before = _dot(multihot.astype(BF16), triu) + count_ref[...]
    rank_out = jnp.zeros((ROUTE_ROWS, tm), jnp.int32)
    for k in range(TOP_K):
        rank = jnp.sum(jnp.where(onehots[k], before, 0.0), axis=0, keepdims=True)
        rank_out = jnp.where(out_row == k, rank.astype(jnp.int32), rank_out)
    count_ref[...] += jnp.sum(multihot, axis=1, keepdims=True)
    return idx_out, gate_out, rank_out


def _out_ln_kernel(cat_ref, w_ref, x_ref, g_ref, b_ref, rw_hi_ref, rw_lo_ref, rb_ref, triu_ref,
                   x1_ref, x1p_ref, idx_ref, gate_ref, rank_ref, count_ref, *, alpha):
    @pl.when(pl.program_id(0) == 0)
    def _():
        count_ref[...] = jnp.zeros_like(count_ref)

    mix = _dot(cat_ref[...], w_ref[...])
    x1 = _layer_norm(alpha * x_ref[...] + mix, g_ref[...], b_ref[...])
    x1_ref[...] = x1
    x1p_ref[...] = _pack_pairs(x1)
    hi = x1.astype(BF16)
    lo = (x1 - hi.astype(F32)).astype(BF16)
    logits = (_dot_nt(rw_hi_ref[...], hi) + _dot_nt(rw_hi_ref[...], lo) + _dot_nt(rw_lo_ref[...], hi)
              + rb_ref[...])
    idx_ref[...], gate_ref[...], rank_ref[...] = _route_tile(logits, triu_ref[...], count_ref)


def _out_ln(cat, w_out, xt, g, b, router_w, router_b, alpha):
    t, d = xt.shape
    n_exp = router_w.shape[1]
    rw_t = router_w.T
    rw_hi = rw_t.astype(BF16)
    rw_lo = (rw_t - rw_hi.astype(F32)).astype(BF16)
    triu = jnp.triu(jnp.ones((ROW_TILE, ROW_TILE), BF16), 1)
    row = lambda wd: pl.BlockSpec((ROW_TILE, wd), lambda i: (i, 0))
    col = pl.BlockSpec((ROUTE_ROWS, ROW_TILE), lambda i: (0, i))
    full = lambda a: pl.BlockSpec(a.shape, lambda i: (0,) * a.ndim)
    args = (cat, w_out.astype(BF16), xt, g.reshape(1, d), b.reshape(1, d), rw_hi, rw_lo,
            router_b.reshape(n_exp, 1), triu)
    return pl.pallas_call(
        functools.partial(_out_ln_kernel, alpha=alpha),
        grid=(t // ROW_TILE,),
        in_specs=[row(cat.shape[1]), full(args[1]), row(d)] + [full(a) for a in args[3:]],
        out_specs=[row(d), row(d // 2), col, col, col, pl.BlockSpec((n_exp, 1), lambda i: (0, 0))],
        out_shape=[jax.ShapeDtypeStruct((t, d), F32), jax.ShapeDtypeStruct((t, d // 2), jnp.int32),
                   jax.ShapeDtypeStruct((ROUTE_ROWS, t), jnp.int32), jax.ShapeDtypeStruct((ROUTE_ROWS, t), F32),
                   jax.ShapeDtypeStruct((ROUTE_ROWS, t), jnp.int32), jax.ShapeDtypeStruct((n_exp, 1), F32)],
        compiler_params=_params(1),
        name="out_ln",
    )(*args)


def _experts_kernel(block_e_ref, valid_ref, x_ref, wgu_ref, bgu_ref, wd_ref, bd_ref, perm_ref, y_ref,
                    wgu_s, wd_s):
    i = pl.program_id(0)
    valid = valid_ref[i]
    half = GU_BLOCK // 2

    @pl.when((i == 0) | (block_e_ref[i] != block_e_ref[jnp.maximum(i - 1, 0)]))
    def _():
        for c in range(wgu_ref.shape[1] // GU_BLOCK):
            cols = slice(c * GU_BLOCK, (c + 1) * GU_BLOCK)
            wgu_s[:, cols] = _dot(wgu_ref[:, cols].astype(BF16), perm_ref[...]).astype(BF16)
        wd_s[...] = wd_ref[...].astype(BF16)

    @pl.when(valid > 0)
    def _():
        xb = _unpack_pairs(x_ref[...])
        row = lax.broadcasted_iota(jnp.int32, xb.shape, 0)
        xb = jnp.where(row < valid, xb, jnp.zeros_like(xb))
        h = _dot(xb, wgu_s[...]) + bgu_ref[...]
        acts = []
        for c in range(h.shape[1] // GU_BLOCK):
            glu = jnp.minimum(h[:, c * GU_BLOCK:c * GU_BLOCK + half], SWIGLU_LIMIT)
            lin = jnp.clip(h[:, c * GU_BLOCK + half:(c + 1) * GU_BLOCK], -SWIGLU_LIMIT, SWIGLU_LIMIT)
            acts.append((glu * jax.nn.sigmoid(SWIGLU_ALPHA * glu) * (lin + 1.0)).astype(BF16))
        act = jnp.concatenate(acts, axis=1)
        y_ref[...] = _pack_pairs(_dot(act, wd_s[...]) + bd_ref[...])

    @pl.when(valid == 0)
    def _():
        y_ref[...] = jnp.zeros_like(y_ref)


def _experts(x_rows, block_e, block_valid, layer, w_gu, b_gu, w_down, b_down):
    rows, dp = x_rows.shape
    _, n_exp, d, de2 = w_gu.shape
    assert de2 % GU_BLOCK == 0 and dp * 2 == d
    n_blocks = rows // MOE_BLOCK
    half = GU_BLOCK // 2
    j = jnp.arange(GU_BLOCK)
    src = jnp.where(j < half, 2 * j, 2 * (j - half) + 1)
    perm = (jnp.arange(GU_BLOCK)[:, None] == src[None, :]).astype(BF16)
    wspec = lambda a: pl.BlockSpec((None, None) + a.shape[2:], lambda i, be, bv: (layer, be[i], 0, 0))
    grid_spec = pltpu.PrefetchScalarGridSpec(
        num_scalar_prefetch=2,
        grid=(n_blocks,),
        in_specs=[pl.BlockSpec((MOE_BLOCK, dp), lambda i, be, bv: (i, 0)),
                  wspec(w_gu), wspec(b_gu), wspec(w_down), wspec(b_down),
                  pl.BlockSpec((GU_BLOCK, GU_BLOCK), lambda i, be, bv: (0, 0))],
        out_specs=pl.BlockSpec((MOE_BLOCK, dp), lambda i, be, bv: (i, 0)),
        scratch_shapes=[pltpu.VMEM(w_gu.shape[2:], BF16), pltpu.VMEM(w_down.shape[2:], BF16)],
    )
    return pl.pallas_call(
        _experts_kernel,
        grid_spec=grid_spec,
        out_shape=jax.ShapeDtypeStruct((rows, dp), jnp.int32),
        compiler_params=_params(1),
        name="experts",
    )(block_e, block_valid, x_rows, w_gu, b_gu, w_down, b_down, perm)


def _combine_ln_kernel(x1_ref, yk_ref, gate_ref, g_ref, b_ref, x2_ref, *, alpha):
    gates = gate_ref[...]
    ffn = gates[:, 0:1] * _unpack_pairs(yk_ref[0]).astype(F32)
    for k in range(1, yk_ref.shape[0]):
        ffn = ffn + gates[:, k:k + 1] * _unpack_pairs(yk_ref[k]).astype(F32)
    x2_ref[...] = _layer_norm(alpha * x1_ref[...] + ffn, g_ref[...], b_ref[...])


def _combine_ln(x1, yk, gates, g, b, alpha):
    t, d = x1.shape
    k = yk.shape[0]
    return pl.pallas_call(
        functools.partial(_combine_ln_kernel, alpha=alpha),
        grid=(t // ROW_TILE,),
        in_specs=[pl.BlockSpec((ROW_TILE, d), lambda i: (i, 0)),
                  pl.BlockSpec((k, ROW_TILE, d // 2), lambda i: (0, i, 0)),
                  pl.BlockSpec((ROW_TILE, k), lambda i: (i, 0)),
                  pl.BlockSpec((1, d), lambda i: (0, 0)),
                  pl.BlockSpec((1, d), lambda i: (0, 0))],
        out_specs=pl.BlockSpec((ROW_TILE, d), lambda i: (i, 0)),
        out_shape=jax.ShapeDtypeStruct((t, d), F32),
        compiler_params=_params(1),
        name="combine_ln",
    )(x1, yk, gates, g.reshape(1, d), b.reshape(1, d))


def _route(top_idx, rank, counts):
    n_exp = counts.shape[0]
    t = top_idx.shape[1]
    experts = jnp.arange(n_exp, dtype=jnp.int32)
    counts = counts.reshape(n_exp).astype(jnp.int32)
    padded = (counts + MOE_BLOCK - 1) // MOE_BLOCK * MOE_BLOCK
    pad_end = jnp.cumsum(padded)
    pad_start = pad_end - padded
    start = jnp.sum(jnp.where(top_idx[None] == experts[:, None, None], pad_start[:, None, None], 0), axis=0)
    pos = rank + start
    n_blocks = -(-(t * TOP_K) // MOE_BLOCK) + n_exp
    blk = jnp.arange(n_blocks, dtype=jnp.int32) * MOE_BLOCK
    block_e = jnp.minimum(jnp.sum((pad_end[None, :] <= blk[:, None]).astype(jnp.int32), axis=1), n_exp - 1)
    mine = block_e[:, None] == experts[None, :]
    pick = lambda v: jnp.sum(jnp.where(mine, v[None, :], 0), axis=1)
    block_valid = jnp.clip(pick(counts) - (blk - pick(pad_start)), 0, MOE_BLOCK).astype(jnp.int32)
    return pos, block_e, block_valid


def _moe(x1, x1p, top_idx, gates, rank, counts, layer, w_gu, b_gu, w_down, b_down, g, b, alpha):
    t, d = x1.shape
    n_exp = counts.shape[0]
    pos, block_e, block_valid = _route(top_idx[:TOP_K], rank[:TOP_K], counts)
    rows = (-(-(t * TOP_K) // MOE_BLOCK) + n_exp) * MOE_BLOCK
    pos3 = pos.reshape(TOP_K, t // SC_CHUNK, SC_CHUNK).transpose(1, 0, 2)
    x_rows = _sc_scatter_rows(x1p, pos3, rows)
    y = _experts(x_rows, block_e, block_valid, layer, w_gu, b_gu, w_down, b_down)
    yk = _sc_gather_rows(y, pos.reshape(-1)).reshape(TOP_K, t, d // 2)
    return _combine_ln(x1, yk, gates[:TOP_K].T, g, b, alpha)


def kernel(x, ev_w_in, ev_pool_w, ev_pool_scale, ev_conv_w, ev_w_out, od_w_in, od_sink, od_w_out,
           router_w, router_b, exp_w_gu, exp_b_gu, exp_w_down, exp_b_down, ln_g, ln_b):
    bsz, seq, d = x.shape
    t = bsz * seq
    depth = ln_g.shape[0]
    alpha = (2 * depth) ** 0.25
    n_exp = router_w.shape[2]
    c_width = d // 2
    dq_width = d // 2
    dkv_width = D_KV_HEADS * HEAD_DIM
    assert t % ROW_TILE == 0

    de2 = exp_b_gu.shape[-1]
    b_gu = exp_b_gu.reshape(depth, n_exp, de2 // GU_BLOCK, GU_BLOCK // 2, 2)
    b_gu = jnp.swapaxes(b_gu, -1, -2).reshape(depth, n_exp, 1, de2)
    b_down = exp_b_down.reshape(depth, n_exp, 1, -1)

    xt = x.reshape(t, d)
    for layer in range(depth):
        i = layer // 2
        if layer % 2 == 0:
            cat = _even_mix(xt.reshape(bsz, seq, d), ev_w_in[i], ev_pool_w[i], ev_pool_scale[i], ev_conv_w[i])
            cat = cat.reshape(t, -1)
            w_out = ev_w_out[i]
        else:
            qkv, (qd, kd, vd) = _odd_proj(xt, od_w_in[i], c_width, dq_width, dkv_width)
            dils = tuple(dil for _, dil in C_PATTERNS)
            os_, lses = [], []
            for (window, dil), (qc, kc, vc) in zip(C_PATTERNS, qkv):
                o, lse = _attn_c(qc, kc, vc, bsz, window, dil)
                os_.append(o)
                lses.append(lse)
            seq3 = lambda a: a.reshape(bsz, seq, a.shape[1])
            yd = _attn_d(seq3(qd), seq3(kd), seq3(vd), od_sink[i]).reshape(t, dq_width)
            cat = _merge(os_, lses, yd, dils)
            w_out = od_w_out[i]
        x1, x1p, top_idx, gates, rank, counts = _out_ln(cat, w_out, xt, ln_g[layer, 0], ln_b[layer, 0],
                                                        router_w[layer], router_b[layer], alpha)
        xt = _moe(x1, x1p, top_idx, gates, rank, counts, layer, exp_w_gu, b_gu, exp_w_down, b_down,
                  ln_g[layer, 1], ln_b[layer, 1], alpha)
    return xt.reshape(bsz, seq, d)
```

```python
import functools

import jax
import jax.numpy as jnp
from jax import lax
from jax.experimental import pallas as pl
from jax.experimental.pallas import tpu as pltpu
from jax.experimental.pallas import tpu_sc as plsc

HEAD_DIM = 64
POOL_WINDOWS = (2, 4, 8, 16)
C_PATTERNS = ((128, 1), (512, 4), (2048, 16))
D_KV_HEADS = 2
D_RADIUS = 128
TOP_K = 4
SWIGLU_LIMIT = 7.0
SWIGLU_ALPHA = 1.702
MOE_BLOCK = 512
LN_EPS = 1e-5
NEG_INF = -1e30

ROW_TILE = 512
ATTN_Q_BLOCK = 128
ATTN_STEP_ROWS = 512
LANES = 128
GU_BLOCK = 256
VMEM_LIMIT_BYTES = 56 * 1024 * 1024

F32 = jnp.float32
BF16 = jnp.bfloat16


def _params(n_axes=1):
    return pltpu.CompilerParams(dimension_semantics=("arbitrary",) * n_axes,
                                vmem_limit_bytes=VMEM_LIMIT_BYTES)


def _dot(a, b):
    return jnp.dot(a, b, preferred_element_type=F32)


def _dot_nt(a, b):
    return lax.dot_general(a, b, (((1,), (1,)), ((), ())), preferred_element_type=F32)


def _pack_pairs(a):
    n = a.shape[1] // 2
    bits = lax.bitcast_convert_type(a.astype(BF16).astype(F32), jnp.int32)
    return bits[:, :n] | lax.shift_right_logical(bits[:, n:], 16)


def _unpack_pairs(w):
    hi = lax.bitcast_convert_type(w & jnp.int32(-65536), F32)
    lo = lax.bitcast_convert_type(lax.shift_left(w, 16), F32)
    return jnp.concatenate([hi, lo], axis=1).astype(BF16)


SC_CORES = 2
SC_SUBCORES = 16
SC_WORKERS = SC_CORES * SC_SUBCORES
SC_CHUNK = 64


def _sc_mesh():
    return plsc.VectorSubcoreMesh(core_axis_name="c", subcore_axis_name="s")


def _sc_worker():
    return lax.axis_index("s") * SC_CORES + lax.axis_index("c")


def _sc_gather_rows(table, idx):
    n, w = idx.shape[0], table.shape[1]
    per_w = n // SC_WORKERS
    assert n % SC_WORKERS == 0 and per_w % (2 * SC_CHUNK) == 0

    def body(table_hbm, idx_hbm, out_hbm, i0, i1, r0, r1, s0, s1):
        base = _sc_worker() * per_w

        def start(off, iv, rv, sem):
            pltpu.sync_copy(idx_hbm.at[pl.ds(off, SC_CHUNK)], iv)
            pltpu.async_copy(table_hbm.at[iv], rv, sem)

        def finish(off, iv, rv, sem):
            pltpu.make_async_copy(table_hbm.at[iv], rv, sem).wait()
            pltpu.sync_copy(rv, out_hbm.at[pl.ds(off, SC_CHUNK)])

        start(base, i0, r0, s0)

        @pl.loop(0, per_w, step=2 * SC_CHUNK)
        def _(o):
            off = base + o
            start(off + SC_CHUNK, i1, r1, s1)
            finish(off, i0, r0, s0)

            @pl.when(o + 2 * SC_CHUNK < per_w)
            def _():
                start(off + 2 * SC_CHUNK, i0, r0, s0)

            finish(off + SC_CHUNK, i1, r1, s1)

    return pl.kernel(
        body, mesh=_sc_mesh(),
        out_type=jax.ShapeDtypeStruct((n, w), table.dtype),
        scratch_types=[pltpu.VMEM((SC_CHUNK,), jnp.int32), pltpu.VMEM((SC_CHUNK,), jnp.int32),
                       pltpu.VMEM((SC_CHUNK, w), table.dtype), pltpu.VMEM((SC_CHUNK, w), table.dtype),
                       pltpu.SemaphoreType.DMA, pltpu.SemaphoreType.DMA],
    )(table, idx)


def _sc_scatter_rows(src, pos3, n_out):
    t, w = src.shape
    kk = pos3.shape[1]
    per_w = t // SC_WORKERS
    assert t % SC_WORKERS == 0 and per_w % SC_CHUNK == 0 and pos3.shape == (t // SC_CHUNK, kk, SC_CHUNK)

    def body(src_hbm, pos_hbm, out_hbm, iv, rv, sem):
        base = _sc_worker() * per_w

        @pl.loop(0, per_w, step=SC_CHUNK)
        def _(o):
            off = base + o
            pltpu.sync_copy(pos_hbm.at[off // SC_CHUNK], iv)
            pltpu.sync_copy(src_hbm.at[pl.ds(off, SC_CHUNK)], rv)
            copies = [pltpu.async_copy(rv, out_hbm.at[iv.at[j]], sem) for j in range(kk)]
            for cp in copies:
                cp.wait()

    return pl.kernel(
        body, mesh=_sc_mesh(),
        out_type=jax.ShapeDtypeStruct((n_out, w), src.dtype),
        scratch_types=[pltpu.VMEM((kk, SC_CHUNK), jnp.int32), pltpu.VMEM((SC_CHUNK, w), src.dtype),
                       pltpu.SemaphoreType.DMA],
    )(src, pos3)


def _shift_down(a, k, row):
    return jnp.where(row >= k, pltpu.roll(a, k, axis=0), 0.0)


def _shift_up(a, k, row):
    n = a.shape[0]
    return jnp.where(row < n - k, pltpu.roll(a, n - k, axis=0), 0.0)


def _even_mix_kernel(x_ref, w_in_ref, pool_w_ref, pool_scale_ref, conv_w_ref, cat_ref, xb_ref):
    s = x_ref.shape[1]
    pool_width = pool_scale_ref.shape[1]
    gd = pool_width // len(POOL_WINDOWS)
    conv_width = conv_w_ref.shape[1]
    xb_ref[...] = x_ref[0].astype(BF16)
    xb = xb_ref[...]

    row = lax.broadcasted_iota(jnp.int32, (s, gd), 0)
    for g, w in enumerate(POOL_WINDOWS):
        lo = g * gd
        if g % 2 == 0:
            u2 = _dot(xb, w_in_ref[:, lo:lo + 2 * gd])
        u = u2[:, (g % 2) * gd:(g % 2 + 1) * gd]
        half = w // 2
        back, fwd, span = u, u, 1
        while span < half:
            back = back + _shift_down(back, span, row)
            fwd = fwd + _shift_up(fwd, span, row)
            span *= 2
        win = _shift_down(back, 1, row) + fwd
        cnt = (jnp.minimum(row + (w - half), s) - jnp.maximum(row - half, 0)).astype(F32)
        pooled = win / cnt - u
        mixed = _dot(pooled.astype(BF16), pool_w_ref[g])
        cat_ref[0, :, lo:lo + gd] = (mixed * pool_scale_ref[:, lo:lo + gd]).astype(BF16)

    cw = 256
    rowc = lax.broadcasted_iota(jnp.int32, (s, cw), 0)
    for j in range(conv_width // cw):
        c0 = j * cw
        b_gate = _dot(xb, w_in_ref[:, pool_width + c0:pool_width + c0 + cw])
        c_gate = _dot(xb, w_in_ref[:, pool_width + conv_width + c0:pool_width + conv_width + c0 + cw])
        v = _dot(xb, w_in_ref[:, pool_width + 2 * conv_width + c0:pool_width + 2 * conv_width + c0 + cw])
        u = c_gate * v
        conv = (_shift_down(u, 1, rowc) * conv_w_ref[0:1, c0:c0 + cw] + u * conv_w_ref[1:2, c0:c0 + cw]
                + _shift_up(u, 1, rowc) * conv_w_ref[2:3, c0:c0 + cw])
        cat_ref[0, :, pool_width + c0:pool_width + c0 + cw] = (b_gate * conv).astype(BF16)


def _even_mix(x, w_in, pool_w, pool_scale, conv_w):
    b, s, d = x.shape
    pool_width = pool_scale.shape[0]
    conv_width = conv_w.shape[1]
    assert conv_width % 256 == 0 and w_in.shape[1] == pool_width + 3 * conv_width
    cat_width = pool_width + conv_width
    return pl.pallas_call(
        _even_mix_kernel,
        grid=(b,),
        in_specs=[
            pl.BlockSpec((1, s, d), lambda i: (i, 0, 0)),
            pl.BlockSpec(w_in.shape, lambda i: (0, 0)),
            pl.BlockSpec(pool_w.shape, lambda i: (0, 0, 0)),
            pl.BlockSpec((1, pool_width), lambda i: (0, 0)),
            pl.BlockSpec(conv_w.shape, lambda i: (0, 0)),
        ],
        out_specs=pl.BlockSpec((1, s, cat_width), lambda i: (i, 0, 0)),
        out_shape=jax.ShapeDtypeStruct((b, s, cat_width), BF16),
        scratch_shapes=[pltpu.VMEM((s, d), BF16)],
        compiler_params=_params(1),
        name="even_mix",
    )(x, w_in.astype(BF16), pool_w.astype(BF16), pool_scale.reshape(1, pool_width), conv_w)


def _odd_proj_kernel(x_ref, w_ref, *refs, c_width, dils):
    outs, h_ref = refs[:-1], refs[-1]
    h = _dot(x_ref[...].astype(BF16), w_ref[...])
    for c in range(h_ref.shape[0]):
        h_ref[c] = h[:, c * LANES:(c + 1) * LANES]
    tm = x_ref.shape[0]
    chunks = c_width // LANES
    for pi, dil in enumerate(dils):
        n = tm // dil
        for r in range(dil):
            for j in range(3):
                for c in range(chunks):
                    src = h_ref[j * chunks + c] if dil == 1 else h_ref[j * chunks + c, pl.ds(r, n, stride=dil), :]
                    lo = r * c_width + c * LANES
                    outs[3 * pi + j][:, lo:lo + LANES] = src.astype(BF16)
    c0 = 3 * c_width
    for ref in outs[3 * len(dils):]:
        wd = ref.shape[1]
        ref[...] = h[:, c0:c0 + wd].astype(BF16)
        c0 += wd


def _odd_proj(xt, w_in, c_width, dq_width, dkv_width):
    t, d = xt.shape
    dils = tuple(dil for _, dil in C_PATTERNS)
    assert 3 * c_width + dq_width + 2 * dkv_width == w_in.shape[1] and c_width % LANES == 0
    col = jnp.arange(w_in.shape[1])
    is_q = (col < c_width) | ((col >= 3 * c_width) & (col < 3 * c_width + dq_width))
    w_in = w_in * jnp.where(is_q, HEAD_DIM ** -0.5, 1.0)
    kv0 = 3 * c_width + dq_width
    twice = lambda w: jnp.repeat(w.reshape(d, -1, HEAD_DIM), LANES // HEAD_DIM, axis=1).reshape(d, -1)
    w_in = jnp.concatenate([w_in[:, :kv0], twice(w_in[:, kv0:kv0 + dkv_width]), twice(w_in[:, kv0 + dkv_width:])], axis=1)
    dkv_width = dkv_width * (LANES // HEAD_DIM)
    w_scaled = w_in.astype(BF16)
    shapes = [(t // dil, dil * c_width) for dil in dils for _ in range(3)]
    blocks = [(ROW_TILE // dil, dil * c_width) for dil in dils for _ in range(3)]
    for wd in (dq_width, dkv_width, dkv_width):
        shapes.append((t, wd))
        blocks.append((ROW_TILE, wd))
    outs = pl.pallas_call(
        functools.partial(_odd_proj_kernel, c_width=c_width, dils=dils),
        grid=(t // ROW_TILE,),
        in_specs=[pl.BlockSpec((ROW_TILE, d), lambda i: (i, 0)),
                  pl.BlockSpec(w_in.shape, lambda i: (0, 0))],
        out_specs=[pl.BlockSpec(blk, lambda i: (i, 0)) for blk in blocks],
        out_shape=[jax.ShapeDtypeStruct(shp, BF16) for shp in shapes],
        scratch_shapes=[pltpu.VMEM((w_in.shape[1] // LANES, ROW_TILE, LANES), F32)],
        compiler_params=_params(1),
        name="odd_proj",
    )(xt, w_scaled)
    qkv = [outs[3 * pi:3 * pi + 3] for pi in range(len(dils))]
    return qkv, outs[3 * len(dils):]


def _band_geometry(i, length, radius):
    qb = min(ATTN_Q_BLOCK, length)
    span = min(length, qb + 2 * radius)
    q0 = pl.multiple_of(i * qb, qb)
    align = 8
    for cand in (128, 64, 32, 16):
        if qb % cand == 0 and radius % cand == 0 and (length - span) % cand == 0:
            align = cand
            break
    start = pl.multiple_of(jnp.clip(q0 - radius, 0, length - span), align)
    qpos = q0 + lax.broadcasted_iota(jnp.int32, (qb, span), 0)
    kpos = start + lax.broadcasted_iota(jnp.int32, (qb, span), 1)
    dist = jnp.abs(kpos - qpos)
    mask = jnp.where(dist <= radius, 0.0, NEG_INF).astype(F32)
    return q0, qb, start, span, dist.astype(F32), mask


def _attend_pairs(q_slabs, k_slabs, v_slabs, biases):
    lower = lax.broadcasted_iota(jnp.int32, (1, LANES), 1) < HEAD_DIM
    scores = []
    for j, (q2, k2) in enumerate(zip(q_slabs, k_slabs)):
        zero = jnp.zeros_like(q2)
        scores.append(_dot_nt(jnp.where(lower, q2, zero), k2) + biases[2 * j])
        scores.append(_dot_nt(jnp.where(lower, zero, q2), k2) + biases[2 * j + 1])
    probs = []
    for s in scores:
        m = jnp.max(s, axis=-1, keepdims=True)
        probs.append((jnp.exp(s - m).astype(BF16), m))
    ones = jnp.ones(v_slabs[0].shape, BF16)
    outs = []
    for j, v2 in enumerate(v_slabs):
        (p_lo, m_lo), (p_up, m_up) = probs[2 * j], probs[2 * j + 1]
        l = jnp.where(lower, _dot(p_lo, ones), _dot(p_up, ones))
        o = jnp.where(lower, _dot(p_lo, v2), _dot(p_up, v2)) / l
        outs.append((o, jnp.where(lower, m_lo, m_up) + jnp.log(l)))
    return outs


def _attn_c_kernel(q_ref, k_ref, v_ref, o_ref, lse_ref, *, dil, radius, n_heads):
    length = q_ref.shape[1]
    qb = min(ATTN_Q_BLOCK, length)
    width = n_heads * HEAD_DIM

    def block(i, carry):
        q0, _, start, span, dist, mask = _band_geometry(i, length, radius)
        dist = dist * float(dil)
        biases = [mask - 2.0 ** (-8.0 * (hh + 1) / n_heads) * dist for hh in range(n_heads)]
        for res in range(q_ref.shape[2] // width):
            slabs = [res * width + j * LANES for j in range(width // LANES)]
            outs = _attend_pairs([q_ref[0, pl.ds(q0, qb), lo:lo + LANES] for lo in slabs],
                                 [k_ref[0, pl.ds(start, span), lo:lo + LANES] for lo in slabs],
                                 [v_ref[0, pl.ds(start, span), lo:lo + LANES] for lo in slabs], biases)
            for lo, (o, lse) in zip(slabs, outs):
                o_ref[0, pl.ds(q0, qb), lo:lo + LANES] = o.astype(o_ref.dtype)
                lse_ref[0, pl.ds(q0, qb), lo:lo + LANES] = lse
        return carry

    lax.fori_loop(0, length // qb, block, 0)


def _attn_c(q, k, v, bsz, window, dil):
    rows, dw = q.shape
    w = dw // dil
    length = rows // bsz
    radius = window // 2 // dil
    assert length % min(ATTN_Q_BLOCK, length) == 0
    per_step = max(1, min(dil, ATTN_STEP_ROWS // length))
    assert dil % per_step == 0
    view = lambda a: a.reshape(bsz, length, dw)
    spec = pl.BlockSpec((1, length, per_step * w), lambda i, r: (i, 0, r))
    o, lse = pl.pallas_call(
        functools.partial(_attn_c_kernel, dil=dil, radius=radius, n_heads=w // HEAD_DIM),
        grid=(bsz, dil // per_step),
        in_specs=[spec, spec, spec],
        out_specs=[spec, spec],
        out_shape=[jax.ShapeDtypeStruct((bsz, length, dw), BF16),
                   jax.ShapeDtypeStruct((bsz, length, dw), F32)],
        compiler_params=_params(2),
        name=f"attn_c_d{dil}",
    )(view(q), view(k), view(v))
    return o.reshape(rows, dw), lse.reshape(rows, dw)


def _attn_d_kernel(sink_ref, q_ref, k_ref, v_ref, y_ref, *, radius, n_heads, group):
    length = q_ref.shape[1]
    qb = min(ATTN_Q_BLOCK, length)
    pairs = n_heads * HEAD_DIM // LANES
    lower = lax.broadcasted_iota(jnp.int32, (1, LANES), 1) < HEAD_DIM

    def block(i, carry):
        q0, _, start, span, dist, mask = _band_geometry(i, length, radius)
        biases = [mask - 2.0 ** (-8.0 * (hh + 1) / n_heads) * dist for hh in range(n_heads)]
        kv_lo = [(2 * j // group) * LANES for j in range(pairs)]
        outs = _attend_pairs([q_ref[0, pl.ds(q0, qb), j * LANES:(j + 1) * LANES] for j in range(pairs)],
                             [k_ref[0, pl.ds(start, span), lo:lo + LANES] for lo in kv_lo],
                             [v_ref[0, pl.ds(start, span), lo:lo + LANES] for lo in kv_lo], biases)
        for j, (o, lse) in enumerate(outs):
            sink = jnp.where(lower, sink_ref[2 * j], sink_ref[2 * j + 1])
            y_ref[0, pl.ds(q0, qb), j * LANES:(j + 1) * LANES] = (o * jax.nn.sigmoid(lse - sink)).astype(y_ref.dtype)
        return carry

    lax.fori_loop(0, length // qb, block, 0)


def _attn_d(q, k, v, sink):
    b, s, w = q.shape
    kvw = k.shape[2]
    n_heads = w // HEAD_DIM
    group = n_heads // D_KV_HEADS
    assert kvw == D_KV_HEADS * LANES and group % 2 == 0
    return pl.pallas_call(
        functools.partial(_attn_d_kernel, radius=D_RADIUS, n_heads=n_heads, group=group),
        grid=(b,),
        in_specs=[pl.BlockSpec(memory_space=pltpu.SMEM),
                  pl.BlockSpec((1, s, w), lambda i: (i, 0, 0)),
                  pl.BlockSpec((1, s, kvw), lambda i: (i, 0, 0)),
                  pl.BlockSpec((1, s, kvw), lambda i: (i, 0, 0))],
        out_specs=pl.BlockSpec((1, s, w), lambda i: (i, 0, 0)),
        out_shape=jax.ShapeDtypeStruct((b, s, w), BF16),
        compiler_params=_params(1),
        name="attn_d",
    )(sink.astype(F32), q, k, v)


def _merge_kernel(*refs, dils, cw):
    n_pat = len(dils)
    o_refs, l_refs = refs[:n_pat], refs[n_pat:2 * n_pat]
    yd_ref, cat_ref = refs[2 * n_pat], refs[2 * n_pat + 1]
    scratch = list(refs[2 * n_pat + 2:])
    tm = cat_ref.shape[0]
    chunks = cw // LANES
    staged = []
    for p, dil in enumerate(dils):
        if dil == 1:
            staged.append(None)
            continue
        o_s, l_s = scratch.pop(0), scratch.pop(0)
        n = tm // dil
        for r in range(dil):
            for c in range(chunks):
                lo = r * cw + c * LANES
                o_s[c, pl.ds(r, n, stride=dil), :] = o_refs[p][:, lo:lo + LANES].astype(F32)
                l_s[c, pl.ds(r, n, stride=dil), :] = l_refs[p][:, lo:lo + LANES]
        staged.append((o_s, l_s))
    for c in range(chunks):
        cols = slice(c * LANES, (c + 1) * LANES)
        os_ = [o_refs[p][:, cols].astype(F32) if st is None else st[0][c] for p, st in enumerate(staged)]
        ls_ = [l_refs[p][:, cols] if st is None else st[1][c] for p, st in enumerate(staged)]
        m = functools.reduce(jnp.maximum, ls_)
        es = [jnp.exp(l - m) for l in ls_]
        num = functools.reduce(lambda a, b: a + b, [e * o for e, o in zip(es, os_)])
        den = functools.reduce(lambda a, b: a + b, es)
        cat_ref[:, cols] = (num / den).astype(BF16)
    cat_ref[:, cw:] = yd_ref[...]


def _merge(os_, lses, yd, dils):
    t, dw = yd.shape
    cw = os_[0].shape[1] // dils[0]
    vspecs = [pl.BlockSpec((ROW_TILE // dil, dil * cw), lambda i: (i, 0)) for dil in dils]
    n_scratch = 2 * sum(1 for dil in dils if dil != 1)
    return pl.pallas_call(
        functools.partial(_merge_kernel, dils=dils, cw=cw),
        grid=(t // ROW_TILE,),
        in_specs=vspecs + vspecs + [pl.BlockSpec((ROW_TILE, dw), lambda i: (i, 0))],
        out_specs=pl.BlockSpec((ROW_TILE, cw + dw), lambda i: (i, 0)),
        out_shape=jax.ShapeDtypeStruct((t, cw + dw), BF16),
        scratch_shapes=[pltpu.VMEM((cw // LANES, ROW_TILE, LANES), F32)] * n_scratch,
        compiler_params=_params(1),
        name="merge",
    )(*os_, *lses, yd)


def _layer_norm(z, g, b):
    mu = jnp.mean(z, axis=-1, keepdims=True)
    zc = z - mu
    var = jnp.mean(zc * zc, axis=-1, keepdims=True)
    return zc * lax.rsqrt(var + LN_EPS) * g + b


ROUTE_ROWS = 8


def _route_tile(logits, triu, count_ref):
    n_exp, tm = logits.shape
    sub = lax.broadcasted_iota(jnp.int32, (n_exp, tm), 0)
    out_row = lax.broadcasted_iota(jnp.int32, (ROUTE_ROWS, tm), 0)
    work = logits
    vals, onehots = [], []
    idx_out = jnp.zeros((ROUTE_ROWS, tm), jnp.int32)
    for k in range(TOP_K):
        m = jnp.max(work, axis=0, keepdims=True)
        idx = jnp.min(jnp.where(work == m, sub, n_exp), axis=0, keepdims=True)
        hot = sub == idx
        work = jnp.where(hot, -jnp.inf, work)
        vals.append(m)
        onehots.append(hot)
        idx_out = jnp.where(out_row == k, idx, idx_out)
    exps = [jnp.exp(v - vals[0]) for v in vals]
    den = functools.reduce(lambda a, b: a + b, exps)
    gate_out = jnp.zeros((ROUTE_ROWS, tm), F32)
    for k in range(TOP_K):
        gate_out = jnp.where(out_row == k, exps[k] / den, gate_out)
    multihot = functools.reduce(lambda a, b: a + b, [h.astype(F32) for h in onehots])
    before = _dot(multihot.astype(BF16), triu) + count_ref[...]
    rank_out = jnp.zeros((ROUTE_ROWS, tm), jnp.int32)
    for k in range(TOP_K):
        rank = jnp.sum(jnp.where(onehots[k], before, 0.0), axis=0, keepdims=True)
        rank_out = jnp.where(out_row == k, rank.astype(jnp.int32), rank_out)
    count_ref[...] += jnp.sum(multihot, axis=1, keepdims=True)
    return idx_out, gate_out, rank_out


def _out_ln_kernel(cat_ref, w_ref, x_ref, g_ref, b_ref, rw_hi_ref, rw_lo_ref, rb_ref, triu_ref,
                   x1_ref, x1p_ref, idx_ref, gate_ref, rank_ref, count_ref, *, alpha):
    @pl.when(pl.program_id(0) == 0)
    def _():
        count_ref[...] = jnp.zeros_like(count_ref)

    mix = _dot(cat_ref[...], w_ref[...])
    x1 = _layer_norm(alpha * x_ref[...] + mix, g_ref[...], b_ref[...])
    x1_ref[...] = x1
    x1p_ref[...] = _pack_pairs(x1)
    hi = x1.astype(BF16)
    lo = (x1 - hi.astype(F32)).astype(BF16)
    logits = (_dot_nt(rw_hi_ref[...], hi) + _dot_nt(rw_hi_ref[...], lo) + _dot_nt(rw_lo_ref[...], hi)
              + rb_ref[...])
    idx_ref[...], gate_ref[...], rank_ref[...] = _route_tile(logits, triu_ref[...], count_ref)


def _out_ln(cat, w_out, xt, g, b, router_w, router_b, alpha):
    t, d = xt.shape
    n_exp = router_w.shape[1]
    rw_t = router_w.T
    rw_hi = rw_t.astype(BF16)
    rw_lo = (rw_t - rw_hi.astype(F32)).astype(BF16)
    triu = jnp.triu(jnp.ones((ROW_TILE, ROW_TILE), BF16), 1)
    row = lambda wd: pl.BlockSpec((ROW_TILE, wd), lambda i: (i, 0))
    col = pl.BlockSpec((ROUTE_ROWS, ROW_TILE), lambda i: (0, i))
    full = lambda a: pl.BlockSpec(a.shape, lambda i: (0,) * a.ndim)
    args = (cat, w_out.astype(BF16), xt, g.reshape(1, d), b.reshape(1, d), rw_hi, rw_lo,
            router_b.reshape(n_exp, 1), triu)
    return pl.pallas_call(
        functools.partial(_out_ln_kernel, alpha=alpha),
        grid=(t // ROW_TILE,),
        in_specs=[row(cat.shape[1]), full(args[1]), row(d)] + [full(a) for a in args[3:]],
        out_specs=[row(d), row(d // 2), col, col, col, pl.BlockSpec((n_exp, 1), lambda i: (0, 0))],
        out_shape=[jax.ShapeDtypeStruct((t, d), F32), jax.ShapeDtypeStruct((t, d // 2), jnp.int32),
                   jax.ShapeDtypeStruct((ROUTE_ROWS, t), jnp.int32), jax.ShapeDtypeStruct((ROUTE_ROWS, t), F32),
                   jax.ShapeDtypeStruct((ROUTE_ROWS, t), jnp.int32), jax.ShapeDtypeStruct((n_exp, 1), F32)],
        compiler_params=_params(1),
        name="out_ln",
    )(*args)


def _experts_kernel(first_ref, blocks_ref, count_ref, x_hbm, wgu_ref, bgu_ref, wd_ref, bd_ref, perm_ref,
                    y_hbm, wgu_s, wd_s, x_buf, y_buf, x_sem, y_sem, zero_sem):
    e = pl.program_id(0)
    n_blk = blocks_ref[e]
    first = first_ref[e]
    count = count_ref[e]
    half = GU_BLOCK // 2

    def rows_of(b):
        return pl.ds(pl.multiple_of((first + b) * MOE_BLOCK, MOE_BLOCK), MOE_BLOCK)

    def x_copy(b, slot):
        return pltpu.make_async_copy(x_hbm.at[rows_of(b)], x_buf.at[slot], x_sem.at[slot])

    def y_copy(b, slot):
        return pltpu.make_async_copy(y_buf.at[slot], y_hbm.at[rows_of(b)], y_sem.at[slot])

    def compute(b, slot):
        xb = _unpack_pairs(x_buf[slot])
        row = lax.broadcasted_iota(jnp.int32, xb.shape, 0)
        xb = jnp.where(row < count - b * MOE_BLOCK, xb, jnp.zeros_like(xb))
        h = _dot(xb, wgu_s[...]) + bgu_ref[...]
        acts = []
        for c in range(h.shape[1] // GU_BLOCK):
            glu = jnp.minimum(h[:, c * GU_BLOCK:c * GU_BLOCK + half], SWIGLU_LIMIT)
            lin = jnp.clip(h[:, c * GU_BLOCK + half:(c + 1) * GU_BLOCK], -SWIGLU_LIMIT, SWIGLU_LIMIT)
            acts.append((glu * jax.nn.sigmoid(SWIGLU_ALPHA * glu) * (lin + 1.0)).astype(BF16))
        act = jnp.concatenate(acts, axis=1)
        y_buf[slot] = _pack_pairs(_dot(act, wd_s[...]) + bd_ref[...])

    def step(b, slot):
        x_copy(b, slot).wait()

        @pl.when(b + 1 < n_blk)
        def _():
            x_copy(b + 1, 1 - slot).start()

        @pl.when(b >= 2)
        def _():
            y_copy(b - 2, slot).wait()

        compute(b, slot)
        y_copy(b, slot).start()

    @pl.when(n_blk > 0)
    def _():
        x_copy(0, 0).start()
        for c in range(wgu_ref.shape[1] // GU_BLOCK):
            cols = slice(c * GU_BLOCK, (c + 1) * GU_BLOCK)
            wgu_s[:, cols] = _dot(wgu_ref[:, cols].astype(BF16), perm_ref[...]).astype(BF16)
        wd_s[...] = wd_ref[...].astype(BF16)

        def pair(i, carry):
            step(2 * i, 0)

            @pl.when(2 * i + 1 < n_blk)
            def _():
                step(2 * i + 1, 1)

            return carry

        lax.fori_loop(0, (n_blk + 1) // 2, pair, 0)
        y_copy(0, 0).wait()

        @pl.when(n_blk >= 2)
        def _():
            y_copy(0, 1).wait()

    @pl.when(e == pl.num_programs(0) - 1)
    def _():
        total = y_hbm.shape[0] // MOE_BLOCK
        used = first + n_blk
        y_buf[0] = jnp.zeros(y_buf.shape[1:], y_buf.dtype)

        def tail_copy(b):
            rows = pl.ds(pl.multiple_of(b * MOE_BLOCK, MOE_BLOCK), MOE_BLOCK)
            return pltpu.make_async_copy(y_buf.at[0], y_hbm.at[rows], zero_sem)

        def start_all(b, carry):
            tail_copy(b).start()
            return carry

        def wait_all(b, carry):
            tail_copy(b).wait()
            return carry

        lax.fori_loop(used, total, start_all, 0)
        lax.fori_loop(used, total, wait_all, 0)


def _experts(x_rows, first_blk, n_blk, counts, layer, w_gu, b_gu, w_down, b_down):
    rows, dp = x_rows.shape
    _, n_exp, d, de2 = w_gu.shape
    assert de2 % GU_BLOCK == 0 and dp * 2 == d and rows % MOE_BLOCK == 0
    half = GU_BLOCK // 2
    j = jnp.arange(GU_BLOCK)
    src = jnp.where(j < half, 2 * j, 2 * (j - half) + 1)
    perm = (jnp.arange(GU_BLOCK)[:, None] == src[None, :]).astype(BF16)
    wspec = lambda a: pl.BlockSpec((None, None) + a.shape[2:], lambda e, *_: (layer, e, 0, 0))
    grid_spec = pltpu.PrefetchScalarGridSpec(
        num_scalar_prefetch=3,
        grid=(n_exp,),
        in_specs=[pl.BlockSpec(memory_space=pl.ANY),
                  wspec(w_gu), wspec(b_gu), wspec(w_down), wspec(b_down),
                  pl.BlockSpec((GU_BLOCK, GU_BLOCK), lambda e, *_: (0, 0))],
        out_specs=pl.BlockSpec(memory_space=pl.ANY),
        scratch_shapes=[pltpu.VMEM(w_gu.shape[2:], BF16), pltpu.VMEM(w_down.shape[2:], BF16),
                        pltpu.VMEM((2, MOE_BLOCK, dp), jnp.int32), pltpu.VMEM((2, MOE_BLOCK, dp), jnp.int32),
                        pltpu.SemaphoreType.DMA((2,)), pltpu.SemaphoreType.DMA((2,)),
                        pltpu.SemaphoreType.DMA(())],
    )
    return pl.pallas_call(
        _experts_kernel,
        grid_spec=grid_spec,
        out_shape=jax.ShapeDtypeStruct((rows, dp), jnp.int32),
        compiler_params=_params(1),
        name="experts",
    )(first_blk, n_blk, counts, x_rows, w_gu, b_gu, w_down, b_down, perm)


def _combine_ln_kernel(x1_ref, yk_ref, gate_ref, g_ref, b_ref, x2_ref, *, alpha):
    gates = gate_ref[...]
    ffn = gates[:, 0:1] * _unpack_pairs(yk_ref[0]).astype(F32)
    for k in range(1, yk_ref.shape[0]):
        ffn = ffn + gates[:, k:k + 1] * _unpack_pairs(yk_ref[k]).astype(F32)
    x2_ref[...] = _layer_norm(alpha * x1_ref[...] + ffn, g_ref[...], b_ref[...])


def _combine_ln(x1, yk, gates, g, b, alpha):
    t, d = x1.shape
    k = yk.shape[0]
    return pl.pallas_call(
        functools.partial(_combine_ln_kernel, alpha=alpha),
        grid=(t // ROW_TILE,),
        in_specs=[pl.BlockSpec((ROW_TILE, d), lambda i: (i, 0)),
                  pl.BlockSpec((k, ROW_TILE, d // 2), lambda i: (0, i, 0)),
                  pl.BlockSpec((ROW_TILE, k), lambda i: (i, 0)),
                  pl.BlockSpec((1, d), lambda i: (0, 0)),
                  pl.BlockSpec((1, d), lambda i: (0, 0))],
        out_specs=pl.BlockSpec((ROW_TILE, d), lambda i: (i, 0)),
        out_shape=jax.ShapeDtypeStruct((t, d), F32),
        compiler_params=_params(1),
        name="combine_ln",
    )(x1, yk, gates, g.reshape(1, d), b.reshape(1, d))


def _route(top_idx, rank, counts):
    n_exp = counts.shape[0]
    experts = jnp.arange(n_exp, dtype=jnp.int32)
    counts = counts.reshape(n_exp).astype(jnp.int32)
    padded = (counts + MOE_BLOCK - 1) // MOE_BLOCK * MOE_BLOCK
    pad_end = jnp.cumsum(padded)
    pad_start = pad_end - padded
    start = jnp.sum(jnp.where(top_idx[None] == experts[:, None, None], pad_start[:, None, None], 0), axis=0)
    pos = rank + start
    first_blk = (pad_start // MOE_BLOCK).astype(jnp.int32)
    n_blk = (padded // MOE_BLOCK).astype(jnp.int32)
    return pos, first_blk, n_blk, counts


def _moe(x1, x1p, top_idx, gates, rank, counts, layer, w_gu, b_gu, w_down, b_down, g, b, alpha):
    t, d = x1.shape
    n_exp = counts.shape[0]
    pos, first_blk, n_blk, counts = _route(top_idx[:TOP_K], rank[:TOP_K], counts)
    rows = (-(-(t * TOP_K) // MOE_BLOCK) + n_exp) * MOE_BLOCK
    pos3 = pos.reshape(TOP_K, t // SC_CHUNK, SC_CHUNK).transpose(1, 0, 2)
    x_rows = _sc_scatter_rows(x1p, pos3, rows)
    y = _experts(x_rows, first_blk, n_blk, counts, layer, w_gu, b_gu, w_down, b_down)
    yk = _sc_gather_rows(y, pos.reshape(-1)).reshape(TOP_K, t, d // 2)
    return _combine_ln(x1, yk, gates[:TOP_K].T, g, b, alpha)


def kernel(x, ev_w_in, ev_pool_w, ev_pool_scale, ev_conv_w, ev_w_out, od_w_in, od_sink, od_w_out,
           router_w, router_b, exp_w_gu, exp_b_gu, exp_w_down, exp_b_down, ln_g, ln_b):
    bsz, seq, d = x.shape
    t = bsz * seq
    depth = ln_g.shape[0]
    alpha = (2 * depth) ** 0.25
    n_exp = router_w.shape[2]
    c_width = d // 2
    dq_width = d // 2
    dkv_width = D_KV_HEADS * HEAD_DIM
    assert t % ROW_TILE == 0

    de2 = exp_b_gu.shape[-1]
    b_gu = exp_b_gu.reshape(depth, n_exp, de2 // GU_BLOCK, GU_BLOCK // 2, 2)
    b_gu = jnp.swapaxes(b_gu, -1, -2).reshape(depth, n_exp, 1, de2)
    b_down = exp_b_down.reshape(depth, n_exp, 1, -1)

    xt = x.reshape(t, d)
    for layer in range(depth):
        i = layer // 2
        if layer % 2 == 0:
            cat = _even_mix(xt.reshape(bsz, seq, d), ev_w_in[i], ev_pool_w[i], ev_pool_scale[i], ev_conv_w[i])
            cat = cat.reshape(t, -1)
            w_out = ev_w_out[i]
        else:
            qkv, (qd, kd, vd) = _odd_proj(xt, od_w_in[i], c_width, dq_width, dkv_width)
            dils = tuple(dil for _, dil in C_PATTERNS)
            os_, lses = [], []
            for (window, dil), (qc, kc, vc) in zip(C_PATTERNS, qkv):
                o, lse = _attn_c(qc, kc, vc, bsz, window, dil)
                os_.append(o)
                lses.append(lse)
            seq3 = lambda a: a.reshape(bsz, seq, a.shape[1])
            yd = _attn_d(seq3(qd), seq3(kd), seq3(vd), od_sink[i]).reshape(t, dq_width)
            cat = _merge(os_, lses, yd, dils)
            w_out = od_w_out[i]
        x1, x1p, top_idx, gates, rank, counts = _out_ln(cat, w_out, xt, ln_g[layer, 0], ln_b[layer, 0],
                                                        router_w[layer], router_b[layer], alpha)
        xt = _moe(x1, x1p, top_idx, gates, rank, counts, layer, exp_w_gu, b_gu, exp_w_down, b_down,
                  ln_g[layer, 1], ln_b[layer, 1], alpha)
    return xt.reshape(bsz, seq, d)
```

```python
import functools

import jax
import jax.numpy as jnp
from jax import lax
from jax.experimental import pallas as pl
from jax.experimental.pallas import tpu as pltpu
from jax.experimental.pallas import tpu_sc as plsc

HEAD_DIM = 64
POOL_WINDOWS = (2, 4, 8, 16)
C_PATTERNS = ((128, 1), (512, 4), (2048, 16))
D_KV_HEADS = 2
D_RADIUS = 128
TOP_K = 4
SWIGLU_LIMIT = 7.0
SWIGLU_ALPHA = 1.702
MOE_BLOCK = 512
LN_EPS = 1e-5
NEG_INF = -1e30

ROW_TILE = 512
ATTN_Q_BLOCK = 128
ATTN_STEP_ROWS = 512
LANES = 128
GU_BLOCK = 256
VMEM_LIMIT_BYTES = 56 * 1024 * 1024

F32 = jnp.float32
BF16 = jnp.bfloat16


def _params(n_axes=1):
    return pltpu.CompilerParams(dimension_semantics=("arbitrary",) * n_axes,
                                vmem_limit_bytes=VMEM_LIMIT_BYTES)


def _dot(a, b):
    return jnp.dot(a, b, preferred_element_type=F32)


def _dot_nt(a, b):
    return lax.dot_general(a, b, (((1,), (1,)), ((), ())), preferred_element_type=F32)


def _pack_pairs(a):
    n = a.shape[1] // 2
    bits = lax.bitcast_convert_type(a.astype(BF16).astype(F32), jnp.int32)
    return bits[:, :n] | lax.shift_right_logical(bits[:, n:], 16)


def _unpack_pairs(w):
    hi = lax.bitcast_convert_type(w & jnp.int32(-65536), F32)
    lo = lax.bitcast_convert_type(lax.shift_left(w, 16), F32)
    return jnp.concatenate([hi, lo], axis=1).astype(BF16)


SC_CORES = 2
SC_SUBCORES = 16
SC_WORKERS = SC_CORES * SC_SUBCORES
SC_CHUNK = 64


def _sc_mesh():
    return plsc.VectorSubcoreMesh(core_axis_name="c", subcore_axis_name="s")


def _sc_worker():
    return lax.axis_index("s") * SC_CORES + lax.axis_index("c")


def _sc_gather_rows(table, idx):
    n, w = idx.shape[0], table.shape[1]
    per_w = n // SC_WORKERS
    assert n % SC_WORKERS == 0 and per_w % (2 * SC_CHUNK) == 0

    def body(table_hbm, idx_hbm, out_hbm, i0, i1, r0, r1, s0, s1):
        base = _sc_worker() * per_w

        def start(off, iv, rv, sem):
            pltpu.sync_copy(idx_hbm.at[pl.ds(off, SC_CHUNK)], iv)
            pltpu.async_copy(table_hbm.at[iv], rv, sem)

        def finish(off, iv, rv, sem):
            pltpu.make_async_copy(table_hbm.at[iv], rv, sem).wait()
            pltpu.sync_copy(rv, out_hbm.at[pl.ds(off, SC_CHUNK)])

        start(base, i0, r0, s0)

        @pl.loop(0, per_w, step=2 * SC_CHUNK)
        def _(o):
            off = base + o
            start(off + SC_CHUNK, i1, r1, s1)
            finish(off, i0, r0, s0)

            @pl.when(o + 2 * SC_CHUNK < per_w)
            def _():
                start(off + 2 * SC_CHUNK, i0, r0, s0)

            finish(off + SC_CHUNK, i1, r1, s1)

    return pl.kernel(
        body, mesh=_sc_mesh(),
        out_type=jax.ShapeDtypeStruct((n, w), table.dtype),
        scratch_types=[pltpu.VMEM((SC_CHUNK,), jnp.int32), pltpu.VMEM((SC_CHUNK,), jnp.int32),
                       pltpu.VMEM((SC_CHUNK, w), table.dtype), pltpu.VMEM((SC_CHUNK, w), table.dtype),
                       pltpu.SemaphoreType.DMA, pltpu.SemaphoreType.DMA],
    )(table, idx)


def _sc_scatter_rows(src, pos3, n_out):
    t, w = src.shape
    kk = pos3.shape[1]
    per_w = t // SC_WORKERS
    assert t % SC_WORKERS == 0 and per_w % SC_CHUNK == 0 and pos3.shape == (t // SC_CHUNK, kk, SC_CHUNK)

    def body(src_hbm, pos_hbm, out_hbm, iv, rv, sem):
        base = _sc_worker() * per_w

        @pl.loop(0, per_w, step=SC_CHUNK)
        def _(o):
            off = base + o
            pltpu.sync_copy(pos_hbm.at[off // SC_CHUNK], iv)
            pltpu.sync_copy(src_hbm.at[pl.ds(off, SC_CHUNK)], rv)
            copies = [pltpu.async_copy(rv, out_hbm.at[iv.at[j]], sem) for j in range(kk)]
            for cp in copies:
                cp.wait()

    return pl.kernel(
        body, mesh=_sc_mesh(),
        out_type=jax.ShapeDtypeStruct((n_out, w), src.dtype),
        scratch_types=[pltpu.VMEM((kk, SC_CHUNK), jnp.int32), pltpu.VMEM((SC_CHUNK, w), src.dtype),
                       pltpu.SemaphoreType.DMA],
    )(src, pos3)


def _shift_down(a, k, row):
    return jnp.where(row >= k, pltpu.roll(a, k, axis=0), 0.0)


def _shift_up(a, k, row):
    n = a.shape[0]
    return jnp.where(row < n - k, pltpu.roll(a, n - k, axis=0), 0.0)


def _even_mix_kernel(x_ref, w_in_ref, pool_w_ref, pool_scale_ref, conv_w_ref, cat_ref, xb_ref):
    s = x_ref.shape[1]
    pool_width = pool_scale_ref.shape[1]
    gd = pool_width // len(POOL_WINDOWS)
    conv_width = conv_w_ref.shape[1]
    xb_ref[...] = x_ref[0].astype(BF16)
    xb = xb_ref[...]

    row = lax.broadcasted_iota(jnp.int32, (s, gd), 0)
    for g, w in enumerate(POOL_WINDOWS):
        lo = g * gd
        if g % 2 == 0:
            u2 = _dot(xb, w_in_ref[:, lo:lo + 2 * gd])
        u = u2[:, (g % 2) * gd:(g % 2 + 1) * gd]
        half = w // 2
        back, fwd, span = u, u, 1
        while span < half:
            back = back + _shift_down(back, span, row)
            fwd = fwd + _shift_up(fwd, span, row)
            span *= 2
        win = _shift_down(back, 1, row) + fwd
        cnt = (jnp.minimum(row + (w - half), s) - jnp.maximum(row - half, 0)).astype(F32)
        pooled = win / cnt - u
        mixed = _dot(pooled.astype(BF16), pool_w_ref[g])
        cat_ref[0, :, lo:lo + gd] = (mixed * pool_scale_ref[:, lo:lo + gd]).astype(BF16)

    cw = 256
    rowc = lax.broadcasted_iota(jnp.int32, (s, cw), 0)
    for j in range(conv_width // cw):
        c0 = j * cw
        b_gate = _dot(xb, w_in_ref[:, pool_width + c0:pool_width + c0 + cw])
        c_gate = _dot(xb, w_in_ref[:, pool_width + conv_width + c0:pool_width + conv_width + c0 + cw])
        v = _dot(xb, w_in_ref[:, pool_width + 2 * conv_width + c0:pool_width + 2 * conv_width + c0 + cw])
        u = c_gate * v
        conv = (_shift_down(u, 1, rowc) * conv_w_ref[0:1, c0:c0 + cw] + u * conv_w_ref[1:2, c0:c0 + cw]
                + _shift_up(u, 1, rowc) * conv_w_ref[2:3, c0:c0 + cw])
        cat_ref[0, :, pool_width + c0:pool_width + c0 + cw] = (b_gate * conv).astype(BF16)


def _even_mix(x, w_in, pool_w, pool_scale, conv_w):
    b, s, d = x.shape
    pool_width = pool_scale.shape[0]
    conv_width = conv_w.shape[1]
    assert conv_width % 256 == 0 and w_in.shape[1] == pool_width + 3 * conv_width
    cat_width = pool_width + conv_width
    return pl.pallas_call(
        _even_mix_kernel,
        grid=(b,),
        in_specs=[
            pl.BlockSpec((1, s, d), lambda i: (i, 0, 0)),
            pl.BlockSpec(w_in.shape, lambda i: (0, 0)),
            pl.BlockSpec(pool_w.shape, lambda i: (0, 0, 0)),
            pl.BlockSpec((1, pool_width), lambda i: (0, 0)),
            pl.BlockSpec(conv_w.shape, lambda i: (0, 0)),
        ],
        out_specs=pl.BlockSpec((1, s, cat_width), lambda i: (i, 0, 0)),
        out_shape=jax.ShapeDtypeStruct((b, s, cat_width), BF16),
        scratch_shapes=[pltpu.VMEM((s, d), BF16)],
        compiler_params=_params(1),
        name="even_mix",
    )(x, w_in.astype(BF16), pool_w.astype(BF16), pool_scale.reshape(1, pool_width), conv_w)


def _odd_proj_kernel(x_ref, w_ref, *refs, c_width, dils):
    outs, h_ref = refs[:-1], refs[-1]
    h = _dot(x_ref[...].astype(BF16), w_ref[...])
    for c in range(h_ref.shape[0]):
        h_ref[c] = h[:, c * LANES:(c + 1) * LANES]
    tm = x_ref.shape[0]
    chunks = c_width // LANES
    for pi, dil in enumerate(dils):
        n = tm // dil
        for r in range(dil):
            for j in range(3):
                for c in range(chunks):
                    src = h_ref[j * chunks + c] if dil == 1 else h_ref[j * chunks + c, pl.ds(r, n, stride=dil), :]
                    lo = r * c_width + c * LANES
                    outs[3 * pi + j][:, lo:lo + LANES] = src.astype(BF16)
    c0 = 3 * c_width
    for ref in outs[3 * len(dils):]:
        wd = ref.shape[1]
        ref[...] = h[:, c0:c0 + wd].astype(BF16)
        c0 += wd


def _odd_proj(xt, w_in, c_width, dq_width, dkv_width):
    t, d = xt.shape
    dils = tuple(dil for _, dil in C_PATTERNS)
    assert 3 * c_width + dq_width + 2 * dkv_width == w_in.shape[1] and c_width % LANES == 0
    col = jnp.arange(w_in.shape[1])
    is_q = (col < c_width) | ((col >= 3 * c_width) & (col < 3 * c_width + dq_width))
    w_in = w_in * jnp.where(is_q, HEAD_DIM ** -0.5, 1.0)
    kv0 = 3 * c_width + dq_width
    twice = lambda w: jnp.repeat(w.reshape(d, -1, HEAD_DIM), LANES // HEAD_DIM, axis=1).reshape(d, -1)
    w_in = jnp.concatenate([w_in[:, :kv0], twice(w_in[:, kv0:kv0 + dkv_width]), twice(w_in[:, kv0 + dkv_width:])], axis=1)
    dkv_width = dkv_width * (LANES // HEAD_DIM)
    w_scaled = w_in.astype(BF16)
    shapes = [(t // dil, dil * c_width) for dil in dils for _ in range(3)]
    blocks = [(ROW_TILE // dil, dil * c_width) for dil in dils for _ in range(3)]
    for wd in (dq_width, dkv_width, dkv_width):
        shapes.append((t, wd))
        blocks.append((ROW_TILE, wd))
    outs = pl.pallas_call(
        functools.partial(_odd_proj_kernel, c_width=c_width, dils=dils),
        grid=(t // ROW_TILE,),
        in_specs=[pl.BlockSpec((ROW_TILE, d), lambda i: (i, 0)),
                  pl.BlockSpec(w_in.shape, lambda i: (0, 0))],
        out_specs=[pl.BlockSpec(blk, lambda i: (i, 0)) for blk in blocks],
        out_shape=[jax.ShapeDtypeStruct(shp, BF16) for shp in shapes],
        scratch_shapes=[pltpu.VMEM((w_in.shape[1] // LANES, ROW_TILE, LANES), F32)],
        compiler_params=_params(1),
        name="odd_proj",
    )(xt, w_scaled)
    qkv = [outs[3 * pi:3 * pi + 3] for pi in range(len(dils))]
    return qkv, outs[3 * len(dils):]


def _band_geometry(i, length, radius):
    qb = min(ATTN_Q_BLOCK, length)
    span = min(length, qb + 2 * radius)
    q0 = pl.multiple_of(i * qb, qb)
    align = 8
    for cand in (128, 64, 32, 16):
        if qb % cand == 0 and radius % cand == 0 and (length - span) % cand == 0:
            align = cand
            break
    start = pl.multiple_of(jnp.clip(q0 - radius, 0, length - span), align)
    qpos = q0 + lax.broadcasted_iota(jnp.int32, (qb, span), 0)
    kpos = start + lax.broadcasted_iota(jnp.int32, (qb, span), 1)
    dist = jnp.abs(kpos - qpos)
    mask = jnp.where(dist <= radius, 0.0, NEG_INF).astype(F32)
    return q0, qb, start, span, dist.astype(F32), mask


def _attend_pairs(q_slabs, k_slabs, v_slabs, biases):
    lower = lax.broadcasted_iota(jnp.int32, (1, LANES), 1) < HEAD_DIM
    scores = []
    for j, (q2, k2) in enumerate(zip(q_slabs, k_slabs)):
        zero = jnp.zeros_like(q2)
        scores.append(_dot_nt(jnp.where(lower, q2, zero), k2) + biases[2 * j])
        scores.append(_dot_nt(jnp.where(lower, zero, q2), k2) + biases[2 * j + 1])
    probs = []
    for s in scores:
        m = jnp.max(s, axis=-1, keepdims=True)
        probs.append((jnp.exp(s - m).astype(BF16), m))
    ones = jnp.ones(v_slabs[0].shape, BF16)
    outs = []
    for j, v2 in enumerate(v_slabs):
        (p_lo, m_lo), (p_up, m_up) = probs[2 * j], probs[2 * j + 1]
        l = jnp.where(lower, _dot(p_lo, ones), _dot(p_up, ones))
        o = jnp.where(lower, _dot(p_lo, v2), _dot(p_up, v2)) / l
        outs.append((o, jnp.where(lower, m_lo, m_up) + jnp.log(l)))
    return outs


def _attn_c_kernel(q_ref, k_ref, v_ref, o_ref, lse_ref, *, dil, radius, n_heads):
    length = q_ref.shape[1]
    qb = min(ATTN_Q_BLOCK, length)
    width = n_heads * HEAD_DIM

    def block(i, carry):
        q0, _, start, span, dist, mask = _band_geometry(i, length, radius)
        dist = dist * float(dil)
        biases = [mask - 2.0 ** (-8.0 * (hh + 1) / n_heads) * dist for hh in range(n_heads)]
        for res in range(q_ref.shape[2] // width):
            slabs = [res * width + j * LANES for j in range(width // LANES)]
            outs = _attend_pairs([q_ref[0, pl.ds(q0, qb), lo:lo + LANES] for lo in slabs],
                                 [k_ref[0, pl.ds(start, span), lo:lo + LANES] for lo in slabs],
                                 [v_ref[0, pl.ds(start, span), lo:lo + LANES] for lo in slabs], biases)
            for lo, (o, lse) in zip(slabs, outs):
                o_ref[0, pl.ds(q0, qb), lo:lo + LANES] = o.astype(o_ref.dtype)
                lse_ref[0, pl.ds(q0, qb), lo:lo + LANES] = lse
        return carry

    lax.fori_loop(0, length // qb, block, 0)


def _attn_c(q, k, v, bsz, window, dil):
    rows, dw = q.shape
    w = dw // dil
    length = rows // bsz
    radius = window // 2 // dil
    assert length % min(ATTN_Q_BLOCK, length) == 0
    per_step = max(1, min(dil, ATTN_STEP_ROWS // length))
    assert dil % per_step == 0
    view = lambda a: a.reshape(bsz, length, dw)
    spec = pl.BlockSpec((1, length, per_step * w), lambda i, r: (i, 0, r))
    o, lse = pl.pallas_call(
        functools.partial(_attn_c_kernel, dil=dil, radius=radius, n_heads=w // HEAD_DIM),
        grid=(bsz, dil // per_step),
        in_specs=[spec, spec, spec],
        out_specs=[spec, spec],
        out_shape=[jax.ShapeDtypeStruct((bsz, length, dw), BF16),
                   jax.ShapeDtypeStruct((bsz, length, dw), F32)],
        compiler_params=_params(2),
        name=f"attn_c_d{dil}",
    )(view(q), view(k), view(v))
    return o.reshape(rows, dw), lse.reshape(rows, dw)


def _attn_d_kernel(sink_ref, q_ref, k_ref, v_ref, y_ref, *, radius, n_heads, group):
    length = q_ref.shape[1]
    qb = min(ATTN_Q_BLOCK, length)
    pairs = n_heads * HEAD_DIM // LANES
    lower = lax.broadcasted_iota(jnp.int32, (1, LANES), 1) < HEAD_DIM

    def block(i, carry):
        q0, _, start, span, dist, mask = _band_geometry(i, length, radius)
        biases = [mask - 2.0 ** (-8.0 * (hh + 1) / n_heads) * dist for hh in range(n_heads)]
        kv_lo = [(2 * j // group) * LANES for j in range(pairs)]
        outs = _attend_pairs([q_ref[0, pl.ds(q0, qb), j * LANES:(j + 1) * LANES] for j in range(pairs)],
                             [k_ref[0, pl.ds(start, span), lo:lo + LANES] for lo in kv_lo],
                             [v_ref[0, pl.ds(start, span), lo:lo + LANES] for lo in kv_lo], biases)
        for j, (o, lse) in enumerate(outs):
            sink = jnp.where(lower, sink_ref[2 * j], sink_ref[2 * j + 1])
            y_ref[0, pl.ds(q0, qb), j * LANES:(j + 1) * LANES] = (o * jax.nn.sigmoid(lse - sink)).astype(y_ref.dtype)
        return carry

    lax.fori_loop(0, length // qb, block, 0)


def _attn_d(q, k, v, sink):
    b, s, w = q.shape
    kvw = k.shape[2]
    n_heads = w // HEAD_DIM
    group = n_heads // D_KV_HEADS
    assert kvw == D_KV_HEADS * LANES and group % 2 == 0
    return pl.pallas_call(
        functools.partial(_attn_d_kernel, radius=D_RADIUS, n_heads=n_heads, group=group),
        grid=(b,),
        in_specs=[pl.BlockSpec(memory_space=pltpu.SMEM),
                  pl.BlockSpec((1, s, w), lambda i: (i, 0, 0)),
                  pl.BlockSpec((1, s, kvw), lambda i: (i, 0, 0)),
                  pl.BlockSpec((1, s, kvw), lambda i: (i, 0, 0))],
        out_specs=pl.BlockSpec((1, s, w), lambda i: (i, 0, 0)),
        out_shape=jax.ShapeDtypeStruct((b, s, w), BF16),
        compiler_params=_params(1),
        name="attn_d",
    )(sink.astype(F32), q, k, v)


def _merge_kernel(*refs, dils, cw):
    n_pat = len(dils)
    o_refs, l_refs = refs[:n_pat], refs[n_pat:2 * n_pat]
    yd_ref, cat_ref = refs[2 * n_pat], refs[2 * n_pat + 1]
    scratch = list(refs[2 * n_pat + 2:])
    tm = cat_ref.shape[0]
    chunks = cw // LANES
    staged = []
    for p, dil in enumerate(dils):
        if dil == 1:
            staged.append(None)
            continue
        o_s, l_s = scratch.pop(0), scratch.pop(0)
        n = tm // dil
        for r in range(dil):
            for c in range(chunks):
                lo = r * cw + c * LANES
                o_s[c, pl.ds(r, n, stride=dil), :] = o_refs[p][:, lo:lo + LANES].astype(F32)
                l_s[c, pl.ds(r, n, stride=dil), :] = l_refs[p][:, lo:lo + LANES]
        staged.append((o_s, l_s))
    for c in range(chunks):
        cols = slice(c * LANES, (c + 1) * LANES)
        os_ = [o_refs[p][:, cols].astype(F32) if st is None else st[0][c] for p, st in enumerate(staged)]
        ls_ = [l_refs[p][:, cols] if st is None else st[1][c] for p, st in enumerate(staged)]
        m = functools.reduce(jnp.maximum, ls_)
        es = [jnp.exp(l - m) for l in ls_]
        num = functools.reduce(lambda a, b: a + b, [e * o for e, o in zip(es, os_)])
        den = functools.reduce(lambda a, b: a + b, es)
        cat_ref[:, cols] = (num / den).astype(BF16)
    cat_ref[:, cw:] = yd_ref[...]


def _merge(os_, lses, yd, dils):
    t, dw = yd.shape
    cw = os_[0].shape[1] // dils[0]
    vspecs = [pl.BlockSpec((ROW_TILE // dil, dil * cw), lambda i: (i, 0)) for dil in dils]
    n_scratch = 2 * sum(1 for dil in dils if dil != 1)
    return pl.pallas_call(
        functools.partial(_merge_kernel, dils=dils, cw=cw),
        grid=(t // ROW_TILE,),
        in_specs=vspecs + vspecs + [pl.BlockSpec((ROW_TILE, dw), lambda i: (i, 0))],
        out_specs=pl.BlockSpec((ROW_TILE, cw + dw), lambda i: (i, 0)),
        out_shape=jax.ShapeDtypeStruct((t, cw + dw), BF16),
        scratch_shapes=[pltpu.VMEM((cw // LANES, ROW_TILE, LANES), F32)] * n_scratch,
        compiler_params=_params(1),
        name="merge",
    )(*os_, *lses, yd)


def _layer_norm(z, g, b):
    mu = jnp.mean(z, axis=-1, keepdims=True)
    zc = z - mu
    var = jnp.mean(zc * zc, axis=-1, keepdims=True)
    return zc * lax.rsqrt(var + LN_EPS) * g + b


ROUTE_ROWS = 8


def _route_tile(logits, triu, count_ref):
    n_exp, tm = logits.shape
    sub = lax.broadcasted_iota(jnp.int32, (n_exp, tm), 0)
    out_row = lax.broadcasted_iota(jnp.int32, (ROUTE_ROWS, tm), 0)
    work = logits
    vals, onehots = [], []
    idx_out = jnp.zeros((ROUTE_ROWS, tm), jnp.int32)
    for k in range(TOP_K):
        m = jnp.max(work, axis=0, keepdims=True)
        idx = jnp.min(jnp.where(work == m, sub, n_exp), axis=0, keepdims=True)
        hot = sub == idx
        work = jnp.where(hot, -jnp.inf, work)
        vals.append(m)
        onehots.append(hot)
        idx_out = jnp.where(out_row == k, idx, idx_out)
    exps = [jnp.exp(v - vals[0]) for v in vals]
    den = functools.reduce(lambda a, b: a + b, exps)
    gate_out = jnp.zeros((ROUTE_ROWS, tm), F32)
    for k in range(TOP_K):
        gate_out = jnp.where(out_row == k, exps[k] / den, gate_out)
    multihot = functools.reduce(lambda a, b: a + b, [h.astype(F32) for h in onehots])
    before = _dot(multihot.astype(BF16), triu) + count_ref[...]
    rank_out = jnp.zeros((ROUTE_ROWS, tm), jnp.int32)
    for k in range(TOP_K):
        rank = jnp.sum(jnp.where(onehots[k], before, 0.0), axis=0, keepdims=True)
        rank_out = jnp.where(out_row == k, rank.astype(jnp.int32), rank_out)
    count_ref[...] += jnp.sum(multihot, axis=1, keepdims=True)
    return idx_out, gate_out, rank_out


def _out_ln_kernel(cat_ref, w_ref, x_ref, g_ref, b_ref, rw_hi_ref, rw_lo_ref, rb_ref, triu_ref,
                   x1_ref, x1p_ref, idx_ref, gate_ref, rank_ref, count_ref, *, alpha):
    @pl.when(pl.program_id(0) == 0)
    def _():
        count_ref[...] = jnp.zeros_like(count_ref)

    mix = _dot(cat_ref[...], w_ref[...])
    x1 = _layer_norm(alpha * x_ref[...] + mix, g_ref[...], b_ref[...])
    x1_ref[...] = x1
    x1p_ref[...] = _pack_pairs(x1)
    hi = x1.astype(BF16)
    lo = (x1 - hi.astype(F32)).astype(BF16)
    logits = (_dot_nt(rw_hi_ref[...], hi) + _dot_nt(rw_hi_ref[...], lo) + _dot_nt(rw_lo_ref[...], hi)
              + rb_ref[...])
    idx_ref[...], gate_ref[...], rank_ref[...] = _route_tile(logits, triu_ref[...], count_ref)


def _out_ln(cat, w_out, xt, g, b, router_w, router_b, alpha):
    t, d = xt.shape
    n_exp = router_w.shape[1]
    rw_t = router_w.T
    rw_hi = rw_t.astype(BF16)
    rw_lo = (rw_t - rw_hi.astype(F32)).astype(BF16)
    triu = jnp.triu(jnp.ones((ROW_TILE, ROW_TILE), BF16), 1)
    row = lambda wd: pl.BlockSpec((ROW_TILE, wd), lambda i: (i, 0))
    col = pl.BlockSpec((ROUTE_ROWS, ROW_TILE), lambda i: (0, i))
    full = lambda a: pl.BlockSpec(a.shape, lambda i: (0,) * a.ndim)
    args = (cat, w_out.astype(BF16), xt, g.reshape(1, d), b.reshape(1, d), rw_hi, rw_lo,
            router_b.reshape(n_exp, 1), triu)
    return pl.pallas_call(
        functools.partial(_out_ln_kernel, alpha=alpha),
        grid=(t // ROW_TILE,),
        in_specs=[row(cat.shape[1]), full(args[1]), row(d)] + [full(a) for a in args[3:]],
        out_specs=[row(d), row(d // 2), col, col, col, pl.BlockSpec((n_exp, 1), lambda i: (0, 0))],
        out_shape=[jax.ShapeDtypeStruct((t, d), F32), jax.ShapeDtypeStruct((t, d // 2), jnp.int32),
                   jax.ShapeDtypeStruct((ROUTE_ROWS, t), jnp.int32), jax.ShapeDtypeStruct((ROUTE_ROWS, t), F32),
                   jax.ShapeDtypeStruct((ROUTE_ROWS, t), jnp.int32), jax.ShapeDtypeStruct((n_exp, 1), F32)],
        compiler_params=_params(1),
        name="out_ln",
    )(*args)


def _experts_kernel(first_ref, blocks_ref, count_ref, x_hbm, wgu_ref, bgu_ref, wd_ref, bd_ref, perm_ref,
                    y_hbm, wgu_s, wd_s, x_buf, y_buf, x_sem, y_sem, zero_sem):
    e = pl.program_id(0)
    last = pl.num_programs(0) - 1
    n_blk = blocks_ref[e]
    first = first_ref[e]
    count = count_ref[e]
    used = first_ref[last] + blocks_ref[last]
    half = GU_BLOCK // 2

    def rows_of(g):
        return pl.ds(pl.multiple_of(g * MOE_BLOCK, MOE_BLOCK), MOE_BLOCK)

    def x_copy(g, slot):
        return pltpu.make_async_copy(x_hbm.at[rows_of(g)], x_buf.at[slot], x_sem.at[slot])

    def y_copy(g, slot):
        return pltpu.make_async_copy(y_buf.at[slot], y_hbm.at[rows_of(g)], y_sem.at[slot])

    def compute(g, slot):
        xb = _unpack_pairs(x_buf[slot])
        row = lax.broadcasted_iota(jnp.int32, xb.shape, 0)
        xb = jnp.where(row < count - (g - first) * MOE_BLOCK, xb, jnp.zeros_like(xb))
        h = _dot(xb, wgu_s[...]) + bgu_ref[...]
        acts = []
        for c in range(h.shape[1] // GU_BLOCK):
            glu = jnp.minimum(h[:, c * GU_BLOCK:c * GU_BLOCK + half], SWIGLU_LIMIT)
            lin = jnp.clip(h[:, c * GU_BLOCK + half:(c + 1) * GU_BLOCK], -SWIGLU_LIMIT, SWIGLU_LIMIT)
            acts.append((glu * jax.nn.sigmoid(SWIGLU_ALPHA * glu) * (lin + 1.0)).astype(BF16))
        act = jnp.concatenate(acts, axis=1)
        y_buf[slot] = _pack_pairs(_dot(act, wd_s[...]) + bd_ref[...])

    @pl.when((e == 0) & (used > 0))
    def _():
        x_copy(0, 0).start()

    @pl.when(n_blk > 0)
    def _():
        for c in range(wgu_ref.shape[1] // GU_BLOCK):
            cols = slice(c * GU_BLOCK, (c + 1) * GU_BLOCK)
            wgu_s[:, cols] = _dot(wgu_ref[:, cols].astype(BF16), perm_ref[...]).astype(BF16)
        wd_s[...] = wd_ref[...].astype(BF16)

    def step(g, carry):
        slot = g % 2
        x_copy(g, slot).wait()

        @pl.when(g + 1 < used)
        def _():
            x_copy(g + 1, 1 - slot).start()

        @pl.when(g >= 2)
        def _():
            y_copy(g - 2, slot).wait()

        compute(g, slot)
        y_copy(g, slot).start()
        return carry

    lax.fori_loop(first, first + n_blk, step, 0)

    @pl.when(e == last)
    def _():
        @pl.when(used >= 1)
        def _():
            y_copy(0, (used - 1) % 2).wait()

        @pl.when(used >= 2)
        def _():
            y_copy(0, used % 2).wait()

        total = y_hbm.shape[0] // MOE_BLOCK
        y_buf[0] = jnp.zeros(y_buf.shape[1:], y_buf.dtype)

        def tail_copy(g):
            return pltpu.make_async_copy(y_buf.at[0], y_hbm.at[rows_of(g)], zero_sem)

        def start_all(g, carry):
            tail_copy(g).start()
            return carry

        def wait_all(g, carry):
            tail_copy(g).wait()
            return carry

        lax.fori_loop(used, total, start_all, 0)
        lax.fori_loop(used, total, wait_all, 0)


def _experts(x_rows, first_blk, n_blk, counts, layer, w_gu, b_gu, w_down, b_down):
    rows, dp = x_rows.shape
    _, n_exp, d, de2 = w_gu.shape
    assert de2 % GU_BLOCK == 0 and dp * 2 == d and rows % MOE_BLOCK == 0
    half = GU_BLOCK // 2
    j = jnp.arange(GU_BLOCK)
    src = jnp.where(j < half, 2 * j, 2 * (j - half) + 1)
    perm = (jnp.arange(GU_BLOCK)[:, None] == src[None, :]).astype(BF16)
    wspec = lambda a: pl.BlockSpec((None, None) + a.shape[2:], lambda e, *_: (layer, e, 0, 0))
    grid_spec = pltpu.PrefetchScalarGridSpec(
        num_scalar_prefetch=3,
        grid=(n_exp,),
        in_specs=[pl.BlockSpec(memory_space=pl.ANY),
                  wspec(w_gu), wspec(b_gu), wspec(w_down), wspec(b_down),
                  pl.BlockSpec((GU_BLOCK, GU_BLOCK), lambda e, *_: (0, 0))],
        out_specs=pl.BlockSpec(memory_space=pl.ANY),
        scratch_shapes=[pltpu.VMEM(w_gu.shape[2:], BF16), pltpu.VMEM(w_down.shape[2:], BF16),
                        pltpu.VMEM((2, MOE_BLOCK, dp), jnp.int32), pltpu.VMEM((2, MOE_BLOCK, dp), jnp.int32),
                        pltpu.SemaphoreType.DMA((2,)), pltpu.SemaphoreType.DMA((2,)),
                        pltpu.SemaphoreType.DMA(())],
    )
    return pl.pallas_call(
        _experts_kernel,
        grid_spec=grid_spec,
        out_shape=jax.ShapeDtypeStruct((rows, dp), jnp.int32),
        compiler_params=_params(1),
        name="experts",
    )(first_blk, n_blk, counts, x_rows, w_gu, b_gu, w_down, b_down, perm)


def _combine_ln_kernel(x1_ref, yk_ref, gate_ref, g_ref, b_ref, x2_ref, *, alpha):
    gates = gate_ref[...]
    ffn = gates[:, 0:1] * _unpack_pairs(yk_ref[0]).astype(F32)
    for k in range(1, yk_ref.shape[0]):
        ffn = ffn + gates[:, k:k + 1] * _unpack_pairs(yk_ref[k]).astype(F32)
    x2_ref[...] = _layer_norm(alpha * x1_ref[...] + ffn, g_ref[...], b_ref[...])


def _combine_ln(x1, yk, gates, g, b, alpha):
    t, d = x1.shape
    k = yk.shape[0]
    return pl.pallas_call(
        functools.partial(_combine_ln_kernel, alpha=alpha),
        grid=(t // ROW_TILE,),
        in_specs=[pl.BlockSpec((ROW_TILE, d), lambda i: (i, 0)),
                  pl.BlockSpec((k, ROW_TILE, d // 2), lambda i: (0, i, 0)),
                  pl.BlockSpec((ROW_TILE, k), lambda i: (i, 0)),
                  pl.BlockSpec((1, d), lambda i: (0, 0)),
                  pl.BlockSpec((1, d), lambda i: (0, 0))],
        out_specs=pl.BlockSpec((ROW_TILE, d), lambda i: (i, 0)),
        out_shape=jax.ShapeDtypeStruct((t, d), F32),
        compiler_params=_params(1),
        name="combine_ln",
    )(x1, yk, gates, g.reshape(1, d), b.reshape(1, d))


def _route(top_idx, rank, counts):
    n_exp = counts.shape[0]
    experts = jnp.arange(n_exp, dtype=jnp.int32)
    counts = counts.reshape(n_exp).astype(jnp.int32)
    padded = (counts + MOE_BLOCK - 1) // MOE_BLOCK * MOE_BLOCK
    pad_end = jnp.cumsum(padded)
    pad_start = pad_end - padded
    start = jnp.sum(jnp.where(top_idx[None] == experts[:, None, None], pad_start[:, None, None], 0), axis=0)
    pos = rank + start
    first_blk = (pad_start // MOE_BLOCK).astype(jnp.int32)
    n_blk = (padded // MOE_BLOCK).astype(jnp.int32)
    return pos, first_blk, n_blk, counts


def _moe(x1, x1p, top_idx, gates, rank, counts, layer, w_gu, b_gu, w_down, b_down, g, b, alpha):
    t, d = x1.shape
    n_exp = counts.shape[0]
    pos, first_blk, n_blk, counts = _route(top_idx[:TOP_K], rank[:TOP_K], counts)
    rows = (-(-(t * TOP_K) // MOE_BLOCK) + n_exp) * MOE_BLOCK
    pos3 = pos.reshape(TOP_K, t // SC_CHUNK, SC_CHUNK).transpose(1, 0, 2)
    x_rows = _sc_scatter_rows(x1p, pos3, rows)
    y = _experts(x_rows, first_blk, n_blk, counts, layer, w_gu, b_gu, w_down, b_down)
    yk = _sc_gather_rows(y, pos.reshape(-1)).reshape(TOP_K, t, d // 2)
    return _combine_ln(x1, yk, gates[:TOP_K].T, g, b, alpha)


def kernel(x, ev_w_in, ev_pool_w, ev_pool_scale, ev_conv_w, ev_w_out, od_w_in, od_sink, od_w_out,
           router_w, router_b, exp_w_gu, exp_b_gu, exp_w_down, exp_b_down, ln_g, ln_b):
    bsz, seq, d = x.shape
    t = bsz * seq
    depth = ln_g.shape[0]
    alpha = (2 * depth) ** 0.25
    n_exp = router_w.shape[2]
    c_width = d // 2
    dq_width = d // 2
    dkv_width = D_KV_HEADS * HEAD_DIM
    assert t % ROW_TILE == 0

    de2 = exp_b_gu.shape[-1]
    b_gu = exp_b_gu.reshape(depth, n_exp, de2 // GU_BLOCK, GU_BLOCK // 2, 2)
    b_gu = jnp.swapaxes(b_gu, -1, -2).reshape(depth, n_exp, 1, de2)
    b_down = exp_b_down.reshape(depth, n_exp, 1, -1)

    xt = x.reshape(t, d)
    for layer in range(depth):
        i = layer // 2
        if layer % 2 == 0:
            cat = _even_mix(xt.reshape(bsz, seq, d), ev_w_in[i], ev_pool_w[i], ev_pool_scale[i], ev_conv_w[i])
            cat = cat.reshape(t, -1)
            w_out = ev_w_out[i]
        else:
            qkv, (qd, kd, vd) = _odd_proj(xt, od_w_in[i], c_width, dq_width, dkv_width)
            dils = tuple(dil for _, dil in C_PATTERNS)
            os_, lses = [], []
            for (window, dil), (qc, kc, vc) in zip(C_PATTERNS, qkv):
                o, lse = _attn_c(qc, kc, vc, bsz, window, dil)
                os_.append(o)
                lses.append(lse)
            seq3 = lambda a: a.reshape(bsz, seq, a.shape[1])
            yd = _attn_d(seq3(qd), seq3(kd), seq3(vd), od_sink[i]).reshape(t, dq_width)
            cat = _merge(os_, lses, yd, dils)
            w_out = od_w_out[i]
        x1, x1p, top_idx, gates, rank, counts = _out_ln(cat, w_out, xt, ln_g[layer, 0], ln_b[layer, 0],
                                                        router_w[layer], router_b[layer], alpha)
        xt = _moe(x1, x1p, top_idx, gates, rank, counts, layer, exp_w_gu, b_gu, exp_w_down, b_down,
                  ln_g[layer, 1], ln_b[layer, 1], alpha)
    return xt.reshape(bsz, seq, d)
```

```python
import functools

import jax
import jax.numpy as jnp
from jax import lax
from jax.experimental import pallas as pl
from jax.experimental.pallas import tpu as pltpu
from jax.experimental.pallas import tpu_sc as plsc

HEAD_DIM = 64
POOL_WINDOWS = (2, 4, 8, 16)
C_PATTERNS = ((128, 1), (512, 4), (2048, 16))
D_KV_HEADS = 2
D_RADIUS = 128
TOP_K = 4
SWIGLU_LIMIT = 7.0
SWIGLU_ALPHA = 1.702
MOE_BLOCK = 512
LN_EPS = 1e-5
NEG_INF = -1e30
LOG2E = 1.4426950408889634

ROW_TILE = 512
ATTN_Q_BLOCK = 128
ATTN_STEP_ROWS = 512
LANES = 128
GU_BLOCK = 256
VMEM_LIMIT_BYTES = 56 * 1024 * 1024

F32 = jnp.float32
BF16 = jnp.bfloat16


def _params(n_axes=1):
    return pltpu.CompilerParams(dimension_semantics=("arbitrary",) * n_axes,
                                vmem_limit_bytes=VMEM_LIMIT_BYTES)


def _dot(a, b):
    return jnp.dot(a, b, preferred_element_type=F32)


def _dot_nt(a, b):
    return lax.dot_general(a, b, (((1,), (1,)), ((), ())), preferred_element_type=F32)


def _pack_pairs(a):
    n = a.shape[1] // 2
    bits = lax.bitcast_convert_type(a.astype(BF16).astype(F32), jnp.int32)
    return bits[:, :n] | lax.shift_right_logical(bits[:, n:], 16)


def _unpack_pairs(w):
    hi = lax.bitcast_convert_type(w & jnp.int32(-65536), F32)
    lo = lax.bitcast_convert_type(lax.shift_left(w, 16), F32)
    return jnp.concatenate([hi, lo], axis=1).astype(BF16)


SC_CORES = 2
SC_SUBCORES = 16
SC_WORKERS = SC_CORES * SC_SUBCORES
SC_CHUNK = 64


def _sc_mesh():
    return plsc.VectorSubcoreMesh(core_axis_name="c", subcore_axis_name="s")


def _sc_worker():
    return lax.axis_index("s") * SC_CORES + lax.axis_index("c")


def _sc_gather_rows(table, idx):
    n, w = idx.shape[0], table.shape[1]
    per_w = n // SC_WORKERS
    assert n % SC_WORKERS == 0 and per_w % (2 * SC_CHUNK) == 0

    def body(table_hbm, idx_hbm, out_hbm, i0, i1, r0, r1, s0, s1):
        base = _sc_worker() * per_w

        def start(off, iv, rv, sem):
            pltpu.sync_copy(idx_hbm.at[pl.ds(off, SC_CHUNK)], iv)
            pltpu.async_copy(table_hbm.at[iv], rv, sem)

        def finish(off, iv, rv, sem):
            pltpu.make_async_copy(table_hbm.at[iv], rv, sem).wait()
            pltpu.sync_copy(rv, out_hbm.at[pl.ds(off, SC_CHUNK)])

        start(base, i0, r0, s0)

        @pl.loop(0, per_w, step=2 * SC_CHUNK)
        def _(o):
            off = base + o
            start(off + SC_CHUNK, i1, r1, s1)
            finish(off, i0, r0, s0)

            @pl.when(o + 2 * SC_CHUNK < per_w)
            def _():
                start(off + 2 * SC_CHUNK, i0, r0, s0)

            finish(off + SC_CHUNK, i1, r1, s1)

    return pl.kernel(
        body, mesh=_sc_mesh(),
        out_type=jax.ShapeDtypeStruct((n, w), table.dtype),
        scratch_types=[pltpu.VMEM((SC_CHUNK,), jnp.int32), pltpu.VMEM((SC_CHUNK,), jnp.int32),
                       pltpu.VMEM((SC_CHUNK, w), table.dtype), pltpu.VMEM((SC_CHUNK, w), table.dtype),
                       pltpu.SemaphoreType.DMA, pltpu.SemaphoreType.DMA],
    )(table, idx)


def _sc_scatter_rows(src, pos3, n_out):
    t, w = src.shape
    kk = pos3.shape[1]
    per_w = t // SC_WORKERS
    assert t % SC_WORKERS == 0 and per_w % SC_CHUNK == 0 and pos3.shape == (t // SC_CHUNK, kk, SC_CHUNK)

    def body(src_hbm, pos_hbm, out_hbm, iv, rv, sem):
        base = _sc_worker() * per_w

        @pl.loop(0, per_w, step=SC_CHUNK)
        def _(o):
            off = base + o
            pltpu.sync_copy(pos_hbm.at[off // SC_CHUNK], iv)
            pltpu.sync_copy(src_hbm.at[pl.ds(off, SC_CHUNK)], rv)
            copies = [pltpu.async_copy(rv, out_hbm.at[iv.at[j]], sem) for j in range(kk)]
            for cp in copies:
                cp.wait()

    return pl.kernel(
        body, mesh=_sc_mesh(),
        out_type=jax.ShapeDtypeStruct((n_out, w), src.dtype),
        scratch_types=[pltpu.VMEM((kk, SC_CHUNK), jnp.int32), pltpu.VMEM((SC_CHUNK, w), src.dtype),
                       pltpu.SemaphoreType.DMA],
    )(src, pos3)


def _shift_down(a, k, row):
    return jnp.where(row >= k, pltpu.roll(a, k, axis=0), 0.0)


def _shift_up(a, k, row):
    n = a.shape[0]
    return jnp.where(row < n - k, pltpu.roll(a, n - k, axis=0), 0.0)


def _even_mix_kernel(x_ref, w_in_ref, pool_w_ref, pool_scale_ref, conv_w_ref, cat_ref, xb_ref):
    s = x_ref.shape[1]
    pool_width = pool_scale_ref.shape[1]
    gd = pool_width // len(POOL_WINDOWS)
    conv_width = conv_w_ref.shape[1]
    xb_ref[...] = x_ref[0].astype(BF16)
    xb = xb_ref[...]

    row = lax.broadcasted_iota(jnp.int32, (s, gd), 0)
    for g, w in enumerate(POOL_WINDOWS):
        lo = g * gd
        if g % 2 == 0:
            u2 = _dot(xb, w_in_ref[:, lo:lo + 2 * gd])
        u = u2[:, (g % 2) * gd:(g % 2 + 1) * gd]
        half = w // 2
        back, fwd, span = u, u, 1
        while span < half:
            back = back + _shift_down(back, span, row)
            fwd = fwd + _shift_up(fwd, span, row)
            span *= 2
        win = _shift_down(back, 1, row) + fwd
        cnt = (jnp.minimum(row + (w - half), s) - jnp.maximum(row - half, 0)).astype(F32)
        pooled = win / cnt - u
        mixed = _dot(pooled.astype(BF16), pool_w_ref[g])
        cat_ref[0, :, lo:lo + gd] = (mixed * pool_scale_ref[:, lo:lo + gd]).astype(BF16)

    cw = 256
    rowc = lax.broadcasted_iota(jnp.int32, (s, cw), 0)
    for j in range(conv_width // cw):
        c0 = j * cw
        b_gate = _dot(xb, w_in_ref[:, pool_width + c0:pool_width + c0 + cw])
        c_gate = _dot(xb, w_in_ref[:, pool_width + conv_width + c0:pool_width + conv_width + c0 + cw])
        v = _dot(xb, w_in_ref[:, pool_width + 2 * conv_width + c0:pool_width + 2 * conv_width + c0 + cw])
        u = c_gate * v
        conv = (_shift_down(u, 1, rowc) * conv_w_ref[0:1, c0:c0 + cw] + u * conv_w_ref[1:2, c0:c0 + cw]
                + _shift_up(u, 1, rowc) * conv_w_ref[2:3, c0:c0 + cw])
        cat_ref[0, :, pool_width + c0:pool_width + c0 + cw] = (b_gate * conv).astype(BF16)


def _even_mix(x, w_in, pool_w, pool_scale, conv_w):
    b, s, d = x.shape
    pool_width = pool_scale.shape[0]
    conv_width = conv_w.shape[1]
    assert conv_width % 256 == 0 and w_in.shape[1] == pool_width + 3 * conv_width
    cat_width = pool_width + conv_width
    return pl.pallas_call(
        _even_mix_kernel,
        grid=(b,),
        in_specs=[
            pl.BlockSpec((1, s, d), lambda i: (i, 0, 0)),
            pl.BlockSpec(w_in.shape, lambda i: (0, 0)),
            pl.BlockSpec(pool_w.shape, lambda i: (0, 0, 0)),
            pl.BlockSpec((1, pool_width), lambda i: (0, 0)),
            pl.BlockSpec(conv_w.shape, lambda i: (0, 0)),
        ],
        out_specs=pl.BlockSpec((1, s, cat_width), lambda i: (i, 0, 0)),
        out_shape=jax.ShapeDtypeStruct((b, s, cat_width), BF16),
        scratch_shapes=[pltpu.VMEM((s, d), BF16)],
        compiler_params=_params(1),
        name="even_mix",
    )(x, w_in.astype(BF16), pool_w.astype(BF16), pool_scale.reshape(1, pool_width), conv_w)


def _odd_proj_kernel(x_ref, w_ref, *refs, c_width, dils):
    outs, h_ref = refs[:-1], refs[-1]
    h = _dot(x_ref[...].astype(BF16), w_ref[...])
    for c in range(h_ref.shape[0]):
        h_ref[c] = h[:, c * LANES:(c + 1) * LANES]
    tm = x_ref.shape[0]
    chunks = c_width // LANES
    for pi, dil in enumerate(dils):
        n = tm // dil
        for r in range(dil):
            for j in range(3):
                for c in range(chunks):
                    src = h_ref[j * chunks + c] if dil == 1 else h_ref[j * chunks + c, pl.ds(r, n, stride=dil), :]
                    lo = r * c_width + c * LANES
                    outs[3 * pi + j][:, lo:lo + LANES] = src.astype(BF16)
    c0 = 3 * c_width
    for ref in outs[3 * len(dils):]:
        wd = ref.shape[1]
        ref[...] = h[:, c0:c0 + wd].astype(BF16)
        c0 += wd


def _odd_proj(xt, w_in, c_width, dq_width, dkv_width):
    t, d = xt.shape
    dils = tuple(dil for _, dil in C_PATTERNS)
    assert 3 * c_width + dq_width + 2 * dkv_width == w_in.shape[1] and c_width % LANES == 0
    col = jnp.arange(w_in.shape[1])
    is_q = (col < c_width) | ((col >= 3 * c_width) & (col < 3 * c_width + dq_width))
    w_in = w_in * jnp.where(is_q, HEAD_DIM ** -0.5 * LOG2E, 1.0)
    kv0 = 3 * c_width + dq_width
    twice = lambda w: jnp.repeat(w.reshape(d, -1, HEAD_DIM), LANES // HEAD_DIM, axis=1).reshape(d, -1)
    w_in = jnp.concatenate([w_in[:, :kv0], twice(w_in[:, kv0:kv0 + dkv_width]), twice(w_in[:, kv0 + dkv_width:])], axis=1)
    dkv_width = dkv_width * (LANES // HEAD_DIM)
    w_scaled = w_in.astype(BF16)
    shapes = [(t // dil, dil * c_width) for dil in dils for _ in range(3)]
    blocks = [(ROW_TILE // dil, dil * c_width) for dil in dils for _ in range(3)]
    for wd in (dq_width, dkv_width, dkv_width):
        shapes.append((t, wd))
        blocks.append((ROW_TILE, wd))
    outs = pl.pallas_call(
        functools.partial(_odd_proj_kernel, c_width=c_width, dils=dils),
        grid=(t // ROW_TILE,),
        in_specs=[pl.BlockSpec((ROW_TILE, d), lambda i: (i, 0)),
                  pl.BlockSpec(w_in.shape, lambda i: (0, 0))],
        out_specs=[pl.BlockSpec(blk, lambda i: (i, 0)) for blk in blocks],
        out_shape=[jax.ShapeDtypeStruct(shp, BF16) for shp in shapes],
        scratch_shapes=[pltpu.VMEM((w_in.shape[1] // LANES, ROW_TILE, LANES), F32)],
        compiler_params=_params(1),
        name="odd_proj",
    )(xt, w_scaled)
    qkv = [outs[3 * pi:3 * pi + 3] for pi in range(len(dils))]
    return qkv, outs[3 * len(dils):]


def _band_shape(length, radius):
    qb = min(ATTN_Q_BLOCK, length)
    span = min(length, qb + 2 * radius)
    n_blk = length // qb
    assert length % qb == 0 and (n_blk <= 2 or radius <= qb)
    return qb, span, n_blk, (0, -radius, qb - span)


def _fill_band_bias(bias_ref, length, radius, dist_unit, n_heads):
    qb, span, _, offsets = _band_shape(length, radius)
    rel = lax.broadcasted_iota(jnp.int32, (qb, span), 1) - lax.broadcasted_iota(jnp.int32, (qb, span), 0)
    for v, off in enumerate(offsets):
        dist = jnp.abs(rel + off)
        mask = jnp.where(dist <= radius, 0.0, NEG_INF).astype(F32)
        far = dist.astype(F32) * (dist_unit * LOG2E)
        for hh in range(n_heads):
            bias_ref[v, hh] = mask - 2.0 ** (-8.0 * (hh + 1) / n_heads) * far


def _band_block(i, length, radius):
    qb, span, n_blk, offsets = _band_shape(length, radius)
    q0 = pl.multiple_of(i * qb, qb)
    variant = jnp.where(i == 0, 0, jnp.where(i == n_blk - 1, 2, 1))
    off = jnp.where(i == 0, offsets[0], jnp.where(i == n_blk - 1, offsets[2], offsets[1]))
    align = 8
    for cand in (128, 64, 32, 16):
        if all(o % cand == 0 for o in offsets) and qb % cand == 0:
            align = cand
            break
    return q0, pl.multiple_of(q0 + off, align), variant


def _attend_pairs(q_slabs, k_slabs, v_slabs, biases):
    lower = lax.broadcasted_iota(jnp.int32, (1, LANES), 1) < HEAD_DIM
    scores = []
    for j, (q2, k2) in enumerate(zip(q_slabs, k_slabs)):
        zero = jnp.zeros_like(q2)
        scores.append(_dot_nt(jnp.where(lower, q2, zero), k2) + biases[2 * j])
        scores.append(_dot_nt(jnp.where(lower, zero, q2), k2) + biases[2 * j + 1])
    probs = []
    for s in scores:
        m = jnp.max(s, axis=-1, keepdims=True)
        probs.append((jnp.exp2(s - m).astype(BF16), m))
    ones = jnp.ones(v_slabs[0].shape, BF16)
    outs = []
    for j, v2 in enumerate(v_slabs):
        (p_lo, m_lo), (p_up, m_up) = probs[2 * j], probs[2 * j + 1]
        l = jnp.where(lower, _dot(p_lo, ones), _dot(p_up, ones))
        o = jnp.where(lower, _dot(p_lo, v2), _dot(p_up, v2)) / l
        outs.append((o, (jnp.where(lower, m_lo, m_up) + jnp.log2(l)) * (1.0 / LOG2E)))
    return outs


def _attn_c_kernel(q_ref, k_ref, v_ref, o_ref, lse_ref, bias_ref, *, dil, radius, n_heads):
    length = q_ref.shape[1]
    qb, span, n_blk, _ = _band_shape(length, radius)
    width = n_heads * HEAD_DIM

    @pl.when((pl.program_id(0) == 0) & (pl.program_id(1) == 0))
    def _():
        _fill_band_bias(bias_ref, length, radius, float(dil), n_heads)

    def block(i, carry):
        q0, start, variant = _band_block(i, length, radius)
        biases = [bias_ref[variant, hh] for hh in range(n_heads)]
        for res in range(q_ref.shape[2] // width):
            slabs = [res * width + j * LANES for j in range(width // LANES)]
            outs = _attend_pairs([q_ref[0, pl.ds(q0, qb), lo:lo + LANES] for lo in slabs],
                                 [k_ref[0, pl.ds(start, span), lo:lo + LANES] for lo in slabs],
                                 [v_ref[0, pl.ds(start, span), lo:lo + LANES] for lo in slabs], biases)
            for lo, (o, lse) in zip(slabs, outs):
                o_ref[0, pl.ds(q0, qb), lo:lo + LANES] = o.astype(o_ref.dtype)
                lse_ref[0, pl.ds(q0, qb), lo:lo + LANES] = lse
        return carry

    lax.fori_loop(0, n_blk, block, 0)


def _attn_c(q, k, v, bsz, window, dil):
    rows, dw = q.shape
    w = dw // dil
    length = rows // bsz
    radius = window // 2 // dil
    qb, span, _, _ = _band_shape(length, radius)
    per_step = max(1, min(dil, ATTN_STEP_ROWS // length))
    assert dil % per_step == 0
    view = lambda a: a.reshape(bsz, length, dw)
    spec = pl.BlockSpec((1, length, per_step * w), lambda i, r: (i, 0, r))
    o, lse = pl.pallas_call(
        functools.partial(_attn_c_kernel, dil=dil, radius=radius, n_heads=w // HEAD_DIM),
        grid=(bsz, dil // per_step),
        in_specs=[spec, spec, spec],
        out_specs=[spec, spec],
        out_shape=[jax.ShapeDtypeStruct((bsz, length, dw), BF16),
                   jax.ShapeDtypeStruct((bsz, length, dw), F32)],
        scratch_shapes=[pltpu.VMEM((3, w // HEAD_DIM, qb, span), F32)],
        compiler_params=_params(2),
        name=f"attn_c_d{dil}",
    )(view(q), view(k), view(v))
    return o.reshape(rows, dw), lse.reshape(rows, dw)


def _attn_d_kernel(sink_ref, q_ref, k_ref, v_ref, y_ref, bias_ref, *, radius, n_heads, group):
    length = q_ref.shape[1]
    qb, span, n_blk, _ = _band_shape(length, radius)
    pairs = n_heads * HEAD_DIM // LANES
    lower = lax.broadcasted_iota(jnp.int32, (1, LANES), 1) < HEAD_DIM

    @pl.when(pl.program_id(0) == 0)
    def _():
        _fill_band_bias(bias_ref, length, radius, 1.0, n_heads)

    def block(i, carry):
        q0, start, variant = _band_block(i, length, radius)
        biases = [bias_ref[variant, hh] for hh in range(n_heads)]
        kv_lo = [(2 * j // group) * LANES for j in range(pairs)]
        outs = _attend_pairs([q_ref[0, pl.ds(q0, qb), j * LANES:(j + 1) * LANES] for j in range(pairs)],
                             [k_ref[0, pl.ds(start, span), lo:lo + LANES] for lo in kv_lo],
                             [v_ref[0, pl.ds(start, span), lo:lo + LANES] for lo in kv_lo], biases)
        for j, (o, lse) in enumerate(outs):
            sink = jnp.where(lower, sink_ref[2 * j], sink_ref[2 * j + 1])
            y_ref[0, pl.ds(q0, qb), j * LANES:(j + 1) * LANES] = (o * jax.nn.sigmoid(lse - sink)).astype(y_ref.dtype)
        return carry

    lax.fori_loop(0, n_blk, block, 0)


def _attn_d(q, k, v, sink):
    b, s, w = q.shape
    kvw = k.shape[2]
    n_heads = w // HEAD_DIM
    group = n_heads // D_KV_HEADS
    assert kvw == D_KV_HEADS * LANES and group % 2 == 0
    qb, span, _, _ = _band_shape(s, D_RADIUS)
    return pl.pallas_call(
        functools.partial(_attn_d_kernel, radius=D_RADIUS, n_heads=n_heads, group=group),
        grid=(b,),
        in_specs=[pl.BlockSpec(memory_space=pltpu.SMEM),
                  pl.BlockSpec((1, s, w), lambda i: (i, 0, 0)),
                  pl.BlockSpec((1, s, kvw), lambda i: (i, 0, 0)),
                  pl.BlockSpec((1, s, kvw), lambda i: (i, 0, 0))],
        out_specs=pl.BlockSpec((1, s, w), lambda i: (i, 0, 0)),
        out_shape=jax.ShapeDtypeStruct((b, s, w), BF16),
        scratch_shapes=[pltpu.VMEM((3, n_heads, qb, span), F32)],
        compiler_params=_params(1),
        name="attn_d",
    )(sink.astype(F32), q, k, v)


def _merge_kernel(*refs, dils, cw):
    n_pat = len(dils)
    o_refs, l_refs = refs[:n_pat], refs[n_pat:2 * n_pat]
    yd_ref, cat_ref = refs[2 * n_pat], refs[2 * n_pat + 1]
    scratch = list(refs[2 * n_pat + 2:])
    tm = cat_ref.shape[0]
    chunks = cw // LANES
    staged = []
    for p, dil in enumerate(dils):
        if dil == 1:
            staged.append(None)
            continue
        o_s, l_s = scratch.pop(0), scratch.pop(0)
        n = tm // dil
        for r in range(dil):
            for c in range(chunks):
                lo = r * cw + c * LANES
                o_s[c, pl.ds(r, n, stride=dil), :] = o_refs[p][:, lo:lo + LANES].astype(F32)
                l_s[c, pl.ds(r, n, stride=dil), :] = l_refs[p][:, lo:lo + LANES]
        staged.append((o_s, l_s))
    for c in range(chunks):
        cols = slice(c * LANES, (c + 1) * LANES)
        os_ = [o_refs[p][:, cols].astype(F32) if st is None else st[0][c] for p, st in enumerate(staged)]
        ls_ = [l_refs[p][:, cols] if st is None else st[1][c] for p, st in enumerate(staged)]
        m = functools.reduce(jnp.maximum, ls_)
        es = [jnp.exp(l - m) for l in ls_]
        num = functools.reduce(lambda a, b: a + b, [e * o for e, o in zip(es, os_)])
        den = functools.reduce(lambda a, b: a + b, es)
        cat_ref[:, cols] = (num / den).astype(BF16)
    cat_ref[:, cw:] = yd_ref[...]


def _merge(os_, lses, yd, dils):
    t, dw = yd.shape
    cw = os_[0].shape[1] // dils[0]
    vspecs = [pl.BlockSpec((ROW_TILE // dil, dil * cw), lambda i: (i, 0)) for dil in dils]
    n_scratch = 2 * sum(1 for dil in dils if dil != 1)
    return pl.pallas_call(
        functools.partial(_merge_kernel, dils=dils, cw=cw),
        grid=(t // ROW_TILE,),
        in_specs=vspecs + vspecs + [pl.BlockSpec((ROW_TILE, dw), lambda i: (i, 0))],
        out_specs=pl.BlockSpec((ROW_TILE, cw + dw), lambda i: (i, 0)),
        out_shape=jax.ShapeDtypeStruct((t, cw + dw), BF16),
        scratch_shapes=[pltpu.VMEM((cw // LANES, ROW_TILE, LANES), F32)] * n_scratch,
        compiler_params=_params(1),
        name="merge",
    )(*os_, *lses, yd)


def _layer_norm(z, g, b):
    mu = jnp.mean(z, axis=-1, keepdims=True)
    zc = z - mu
    var = jnp.mean(zc * zc, axis=-1, keepdims=True)
    return zc * lax.rsqrt(var + LN_EPS) * g + b


ROUTE_ROWS = 8


def _route_tile(logits, triu, count_ref):
    n_exp, tm = logits.shape
    sub = lax.broadcasted_iota(jnp.int32, (n_exp, tm), 0)
    out_row = lax.broadcasted_iota(jnp.int32, (ROUTE_ROWS, tm), 0)
    work = logits
    vals, onehots = [], []
    idx_out = jnp.zeros((ROUTE_ROWS, tm), jnp.int32)
    for k in range(TOP_K):
        m = jnp.max(work, axis=0, keepdims=True)
        idx = jnp.min(jnp.where(work == m, sub, n_exp), axis=0, keepdims=True)
        hot = sub == idx
        work = jnp.where(hot, -jnp.inf, work)
        vals.append(m)
        onehots.append(hot)
        idx_out = jnp.where(out_row == k, idx, idx_out)
    exps = [jnp.exp(v - vals[0]) for v in vals]
    den = functools.reduce(lambda a, b: a + b, exps)
    gate_out = jnp.zeros((ROUTE_ROWS, tm), F32)
    for k in range(TOP_K):
        gate_out = jnp.where(out_row == k, exps[k] / den, gate_out)
    multihot = functools.reduce(lambda a, b: a + b, [h.astype(F32) for h in onehots])
    before = _dot(multihot.astype(BF16), triu) + count_ref[...]
    rank_out = jnp.zeros((ROUTE_ROWS, tm), jnp.int32)
    for k in range(TOP_K):
        rank = jnp.sum(jnp.where(onehots[k], before, 0.0), axis=0, keepdims=True)
        rank_out = jnp.where(out_row == k, rank.astype(jnp.int32), rank_out)
    count_ref[...] += jnp.sum(multihot, axis=1, keepdims=True)
    return idx_out, gate_out, rank_out


def _out_ln_kernel(cat_ref, w_ref, x_ref, g_ref, b_ref, rw_hi_ref, rw_lo_ref, rb_ref, triu_ref,
                   x1_ref, x1p_ref, idx_ref, gate_ref, rank_ref, count_ref, *, alpha):
    @pl.when(pl.program_id(0) == 0)
    def _():
        count_ref[...] = jnp.zeros_like(count_ref)

    mix = _dot(cat_ref[...], w_ref[...])
    x1 = _layer_norm(alpha * x_ref[...] + mix, g_ref[...], b_ref[...])
    x1_ref[...] = x1
    x1p_ref[...] = _pack_pairs(x1)
    hi = x1.astype(BF16)
    lo = (x1 - hi.astype(F32)).astype(BF16)
    logits = (_dot_nt(rw_hi_ref[...], hi) + _dot_nt(rw_hi_ref[...], lo) + _dot_nt(rw_lo_ref[...], hi)
              + rb_ref[...])
    idx_ref[...], gate_ref[...], rank_ref[...] = _route_tile(logits, triu_ref[...], count_ref)


def _out_ln(cat, w_out, xt, g, b, router_w, router_b, alpha):
    t, d = xt.shape
    n_exp = router_w.shape[1]
    rw_t = router_w.T
    rw_hi = rw_t.astype(BF16)
    rw_lo = (rw_t - rw_hi.astype(F32)).astype(BF16)
    triu = jnp.triu(jnp.ones((ROW_TILE, ROW_TILE), BF16), 1)
    row = lambda wd: pl.BlockSpec((ROW_TILE, wd), lambda i: (i, 0))
    col = pl.BlockSpec((ROUTE_ROWS, ROW_TILE), lambda i: (0, i))
    full = lambda a: pl.BlockSpec(a.shape, lambda i: (0,) * a.ndim)
    args = (cat, w_out.astype(BF16), xt, g.reshape(1, d), b.reshape(1, d), rw_hi, rw_lo,
            router_b.reshape(n_exp, 1), triu)
    return pl.pallas_call(
        functools.partial(_out_ln_kernel, alpha=alpha),
        grid=(t // ROW_TILE,),
        in_specs=[row(cat.shape[1]), full(args[1]), row(d)] + [full(a) for a in args[3:]],
        out_specs=[row(d), row(d // 2), col, col, col, pl.BlockSpec((n_exp, 1), lambda i: (0, 0))],
        out_shape=[jax.ShapeDtypeStruct((t, d), F32), jax.ShapeDtypeStruct((t, d // 2), jnp.int32),
                   jax.ShapeDtypeStruct((ROUTE_ROWS, t), jnp.int32), jax.ShapeDtypeStruct((ROUTE_ROWS, t), F32),
                   jax.ShapeDtypeStruct((ROUTE_ROWS, t), jnp.int32), jax.ShapeDtypeStruct((n_exp, 1), F32)],
        compiler_params=_params(1),
        name="out_ln",
    )(*args)


def _experts_kernel(first_ref, blocks_ref, count_ref, x_hbm, wgu_ref, bgu_ref, wd_ref, bd_ref, perm_ref,
                    y_hbm, wgu_s, wd_s, x_buf, y_buf, x_sem, y_sem, zero_sem):
    e = pl.program_id(0)
    last = pl.num_programs(0) - 1
    n_blk = blocks_ref[e]
    first = first_ref[e]
    count = count_ref[e]
    used = first_ref[last] + blocks_ref[last]
    half = GU_BLOCK // 2

    def rows_of(g):
        return pl.ds(pl.multiple_of(g * MOE_BLOCK, MOE_BLOCK), MOE_BLOCK)

    def x_copy(g, slot):
        return pltpu.make_async_copy(x_hbm.at[rows_of(g)], x_buf.at[slot], x_sem.at[slot])

    def y_copy(g, slot):
        return pltpu.make_async_copy(y_buf.at[slot], y_hbm.at[rows_of(g)], y_sem.at[slot])

    def compute(g, slot):
        xb = _unpack_pairs(x_buf[slot])
        row = lax.broadcasted_iota(jnp.int32, xb.shape, 0)
        xb = jnp.where(row < count - (g - first) * MOE_BLOCK, xb, jnp.zeros_like(xb))
        h = _dot(xb, wgu_s[...]) + bgu_ref[...]
        acts = []
        for c in range(h.shape[1] // GU_BLOCK):
            glu = jnp.minimum(h[:, c * GU_BLOCK:c * GU_BLOCK + half], SWIGLU_LIMIT)
            lin = jnp.clip(h[:, c * GU_BLOCK + half:(c + 1) * GU_BLOCK], -SWIGLU_LIMIT, SWIGLU_LIMIT)
            acts.append((glu * jax.nn.sigmoid(SWIGLU_ALPHA * glu) * (lin + 1.0)).astype(BF16))
        act = jnp.concatenate(acts, axis=1)
        y_buf[slot] = _pack_pairs(_dot(act, wd_s[...]) + bd_ref[...])

    @pl.when((e == 0) & (used > 0))
    def _():
        x_copy(0, 0).start()

    @pl.when(n_blk > 0)
    def _():
        for c in range(wgu_ref.shape[1] // GU_BLOCK):
            cols = slice(c * GU_BLOCK, (c + 1) * GU_BLOCK)
            wgu_s[:, cols] = _dot(wgu_ref[:, cols].astype(BF16), perm_ref[...]).astype(BF16)
        wd_s[...] = wd_ref[...].astype(BF16)

    def step(g, carry):
        slot = g % 2
        x_copy(g, slot).wait()

        @pl.when(g + 1 < used)
        def _():
            x_copy(g + 1, 1 - slot).start()

        @pl.when(g >= 2)
        def _():
            y_copy(g - 2, slot).wait()

        compute(g, slot)
        y_copy(g, slot).start()
        return carry

    lax.fori_loop(first, first + n_blk, step, 0)

    @pl.when(e == last)
    def _():
        @pl.when(used >= 1)
        def _():
            y_copy(0, (used - 1) % 2).wait()

        @pl.when(used >= 2)
        def _():
            y_copy(0, used % 2).wait()

        total = y_hbm.shape[0] // MOE_BLOCK
        y_buf[0] = jnp.zeros(y_buf.shape[1:], y_buf.dtype)

        def tail_copy(g):
            return pltpu.make_async_copy(y_buf.at[0], y_hbm.at[rows_of(g)], zero_sem)

        def start_all(g, carry):
            tail_copy(g).start()
            return carry

        def wait_all(g, carry):
            tail_copy(g).wait()
            return carry

        lax.fori_loop(used, total, start_all, 0)
        lax.fori_loop(used, total, wait_all, 0)


def _experts(x_rows, first_blk, n_blk, counts, layer, w_gu, b_gu, w_down, b_down):
    rows, dp = x_rows.shape
    _, n_exp, d, de2 = w_gu.shape
    assert de2 % GU_BLOCK == 0 and dp * 2 == d and rows % MOE_BLOCK == 0
    half = GU_BLOCK // 2
    j = jnp.arange(GU_BLOCK)
    src = jnp.where(j < half, 2 * j, 2 * (j - half) + 1)
    perm = (jnp.arange(GU_BLOCK)[:, None] == src[None, :]).astype(BF16)
    wspec = lambda a: pl.BlockSpec((None, None) + a.shape[2:], lambda e, *_: (layer, e, 0, 0))
    grid_spec = pltpu.PrefetchScalarGridSpec(
        num_scalar_prefetch=3,
        grid=(n_exp,),
        in_specs=[pl.BlockSpec(memory_space=pl.ANY),
                  wspec(w_gu), wspec(b_gu), wspec(w_down), wspec(b_down),
                  pl.BlockSpec((GU_BLOCK, GU_BLOCK), lambda e, *_: (0, 0))],
        out_specs=pl.BlockSpec(memory_space=pl.ANY),
        scratch_shapes=[pltpu.VMEM(w_gu.shape[2:], BF16), pltpu.VMEM(w_down.shape[2:], BF16),
                        pltpu.VMEM((2, MOE_BLOCK, dp), jnp.int32), pltpu.VMEM((2, MOE_BLOCK, dp), jnp.int32),
                        pltpu.SemaphoreType.DMA((2,)), pltpu.SemaphoreType.DMA((2,)),
                        pltpu.SemaphoreType.DMA(())],
    )
    return pl.pallas_call(
        _experts_kernel,
        grid_spec=grid_spec,
        out_shape=jax.ShapeDtypeStruct((rows, dp), jnp.int32),
        compiler_params=_params(1),
        name="experts",
    )(first_blk, n_blk, counts, x_rows, w_gu, b_gu, w_down, b_down, perm)


def _combine_ln_kernel(x1_ref, yk_ref, gate_ref, g_ref, b_ref, x2_ref, *, alpha):
    gates = gate_ref[...]
    ffn = gates[:, 0:1] * _unpack_pairs(yk_ref[0]).astype(F32)
    for k in range(1, yk_ref.shape[0]):
        ffn = ffn + gates[:, k:k + 1] * _unpack_pairs(yk_ref[k]).astype(F32)
    x2_ref[...] = _layer_norm(alpha * x1_ref[...] + ffn, g_ref[...], b_ref[...])


def _combine_ln(x1, yk, gates, g, b, alpha):
    t, d = x1.shape
    k = yk.shape[0]
    return pl.pallas_call(
        functools.partial(_combine_ln_kernel, alpha=alpha),
        grid=(t // ROW_TILE,),
        in_specs=[pl.BlockSpec((ROW_TILE, d), lambda i: (i, 0)),
                  pl.BlockSpec((k, ROW_TILE, d // 2), lambda i: (0, i, 0)),
                  pl.BlockSpec((ROW_TILE, k), lambda i: (i, 0)),
                  pl.BlockSpec((1, d), lambda i: (0, 0)),
                  pl.BlockSpec((1, d), lambda i: (0, 0))],
        out_specs=pl.BlockSpec((ROW_TILE, d), lambda i: (i, 0)),
        out_shape=jax.ShapeDtypeStruct((t, d), F32),
        compiler_params=_params(1),
        name="combine_ln",
    )(x1, yk, gates, g.reshape(1, d), b.reshape(1, d))


def _route(top_idx, rank, counts):
    n_exp = counts.shape[0]
    experts = jnp.arange(n_exp, dtype=jnp.int32)
    counts = counts.reshape(n_exp).astype(jnp.int32)
    padded = (counts + MOE_BLOCK - 1) // MOE_BLOCK * MOE_BLOCK
    pad_end = jnp.cumsum(padded)
    pad_start = pad_end - padded
    start = jnp.sum(jnp.where(top_idx[None] == experts[:, None, None], pad_start[:, None, None], 0), axis=0)
    pos = rank + start
    first_blk = (pad_start // MOE_BLOCK).astype(jnp.int32)
    n_blk = (padded // MOE_BLOCK).astype(jnp.int32)
    return pos, first_blk, n_blk, counts


def _moe(x1, x1p, top_idx, gates, rank, counts, layer, w_gu, b_gu, w_down, b_down, g, b, alpha):
    t, d = x1.shape
    n_exp = counts.shape[0]
    pos, first_blk, n_blk, counts = _route(top_idx[:TOP_K], rank[:TOP_K], counts)
    rows = (-(-(t * TOP_K) // MOE_BLOCK) + n_exp) * MOE_BLOCK
    pos3 = pos.reshape(TOP_K, t // SC_CHUNK, SC_CHUNK).transpose(1, 0, 2)
    x_rows = _sc_scatter_rows(x1p, pos3, rows)
    y = _experts(x_rows, first_blk, n_blk, counts, layer, w_gu, b_gu, w_down, b_down)
    yk = _sc_gather_rows(y, pos.reshape(-1)).reshape(TOP_K, t, d // 2)
    return _combine_ln(x1, yk, gates[:TOP_K].T, g, b, alpha)


def kernel(x, ev_w_in, ev_pool_w, ev_pool_scale, ev_conv_w, ev_w_out, od_w_in, od_sink, od_w_out,
           router_w, router_b, exp_w_gu, exp_b_gu, exp_w_down, exp_b_down, ln_g, ln_b):
    bsz, seq, d = x.shape
    t = bsz * seq
    depth = ln_g.shape[0]
    alpha = (2 * depth) ** 0.25
    n_exp = router_w.shape[2]
    c_width = d // 2
    dq_width = d // 2
    dkv_width = D_KV_HEADS * HEAD_DIM
    assert t % ROW_TILE == 0

    de2 = exp_b_gu.shape[-1]
    b_gu = exp_b_gu.reshape(depth, n_exp, de2 // GU_BLOCK, GU_BLOCK // 2, 2)
    b_gu = jnp.swapaxes(b_gu, -1, -2).reshape(depth, n_exp, 1, de2)
    b_down = exp_b_down.reshape(depth, n_exp, 1, -1)

    xt = x.reshape(t, d)
    for layer in range(depth):
        i = layer // 2
        if layer % 2 == 0:
            cat = _even_mix(xt.reshape(bsz, seq, d), ev_w_in[i], ev_pool_w[i], ev_pool_scale[i], ev_conv_w[i])
            cat = cat.reshape(t, -1)
            w_out = ev_w_out[i]
        else:
            qkv, (qd, kd, vd) = _odd_proj(xt, od_w_in[i], c_width, dq_width, dkv_width)
            dils = tuple(dil for _, dil in C_PATTERNS)
            os_, lses = [], []
            for (window, dil), (qc, kc, vc) in zip(C_PATTERNS, qkv):
                o, lse = _attn_c(qc, kc, vc, bsz, window, dil)
                os_.append(o)
                lses.append(lse)
            seq3 = lambda a: a.reshape(bsz, seq, a.shape[1])
            yd = _attn_d(seq3(qd), seq3(kd), seq3(vd), od_sink[i]).reshape(t, dq_width)
            cat = _merge(os_, lses, yd, dils)
            w_out = od_w_out[i]
        x1, x1p, top_idx, gates, rank, counts = _out_ln(cat, w_out, xt, ln_g[layer, 0], ln_b[layer, 0],
                                                        router_w[layer], router_b[layer], alpha)
        xt = _moe(x1, x1p, top_idx, gates, rank, counts, layer, exp_w_gu, b_gu, exp_w_down, b_down,
                  ln_g[layer, 1], ln_b[layer, 1], alpha)
    return xt.reshape(bsz, seq, d)
```

```python
import functools

import jax
import jax.numpy as jnp
from jax import lax
from jax.experimental import pallas as pl
from jax.experimental.pallas import tpu as pltpu
from jax.experimental.pallas import tpu_sc as plsc

HEAD_DIM = 64
POOL_WINDOWS = (2, 4, 8, 16)
C_PATTERNS = ((128, 1), (512, 4), (2048, 16))
D_KV_HEADS = 2
D_RADIUS = 128
TOP_K = 4
SWIGLU_LIMIT = 7.0
SWIGLU_ALPHA = 1.702
MOE_BLOCK = 512
LN_EPS = 1e-5
NEG_INF = -1e30
LOG2E = 1.4426950408889634

ROW_TILE = 512
ATTN_Q_BLOCK = 128
ATTN_STEP_ROWS = 512
LANES = 128
GU_BLOCK = 256
VMEM_LIMIT_BYTES = 56 * 1024 * 1024

F32 = jnp.float32
BF16 = jnp.bfloat16


def _params(n_axes=1):
    return pltpu.CompilerParams(dimension_semantics=("arbitrary",) * n_axes,
                                vmem_limit_bytes=VMEM_LIMIT_BYTES)


def _dot(a, b):
    return jnp.dot(a, b, preferred_element_type=F32)


def _dot_nt(a, b):
    return lax.dot_general(a, b, (((1,), (1,)), ((), ())), preferred_element_type=F32)


def _pack_pairs(a):
    n = a.shape[1] // 2
    bits = lax.bitcast_convert_type(a.astype(BF16).astype(F32), jnp.int32)
    return bits[:, :n] | lax.shift_right_logical(bits[:, n:], 16)


def _unpack_pairs(w):
    hi = lax.bitcast_convert_type(w & jnp.int32(-65536), F32)
    lo = lax.bitcast_convert_type(lax.shift_left(w, 16), F32)
    return jnp.concatenate([hi, lo], axis=1).astype(BF16)


SC_CORES = 2
SC_SUBCORES = 16
SC_WORKERS = SC_CORES * SC_SUBCORES
SC_CHUNK = 64


def _sc_mesh():
    return plsc.VectorSubcoreMesh(core_axis_name="c", subcore_axis_name="s")


def _sc_worker():
    return lax.axis_index("s") * SC_CORES + lax.axis_index("c")


def _sc_gather_rows(table, idx):
    n, w = idx.shape[0], table.shape[1]
    per_w = n // SC_WORKERS
    assert n % SC_WORKERS == 0 and per_w % (2 * SC_CHUNK) == 0

    def body(table_hbm, idx_hbm, out_hbm, i0, i1, r0, r1, s0, s1):
        base = _sc_worker() * per_w

        def start(off, iv, rv, sem):
            pltpu.sync_copy(idx_hbm.at[pl.ds(off, SC_CHUNK)], iv)
            pltpu.async_copy(table_hbm.at[iv], rv, sem)

        def finish(off, iv, rv, sem):
            pltpu.make_async_copy(table_hbm.at[iv], rv, sem).wait()
            pltpu.sync_copy(rv, out_hbm.at[pl.ds(off, SC_CHUNK)])

        start(base, i0, r0, s0)

        @pl.loop(0, per_w, step=2 * SC_CHUNK)
        def _(o):
            off = base + o
            start(off + SC_CHUNK, i1, r1, s1)
            finish(off, i0, r0, s0)

            @pl.when(o + 2 * SC_CHUNK < per_w)
            def _():
                start(off + 2 * SC_CHUNK, i0, r0, s0)

            finish(off + SC_CHUNK, i1, r1, s1)

    return pl.kernel(
        body, mesh=_sc_mesh(),
        out_type=jax.ShapeDtypeStruct((n, w), table.dtype),
        scratch_types=[pltpu.VMEM((SC_CHUNK,), jnp.int32), pltpu.VMEM((SC_CHUNK,), jnp.int32),
                       pltpu.VMEM((SC_CHUNK, w), table.dtype), pltpu.VMEM((SC_CHUNK, w), table.dtype),
                       pltpu.SemaphoreType.DMA, pltpu.SemaphoreType.DMA],
    )(table, idx)


def _sc_scatter_rows(src, pos3, n_out):
    t, w = src.shape
    kk = pos3.shape[1]
    per_w = t // SC_WORKERS
    assert t % SC_WORKERS == 0 and per_w % SC_CHUNK == 0 and pos3.shape == (t // SC_CHUNK, kk, SC_CHUNK)

    def body(src_hbm, pos_hbm, out_hbm, iv, rv, sem):
        base = _sc_worker() * per_w

        @pl.loop(0, per_w, step=SC_CHUNK)
        def _(o):
            off = base + o
            pltpu.sync_copy(pos_hbm.at[off // SC_CHUNK], iv)
            pltpu.sync_copy(src_hbm.at[pl.ds(off, SC_CHUNK)], rv)
            copies = [pltpu.async_copy(rv, out_hbm.at[iv.at[j]], sem) for j in range(kk)]
            for cp in copies:
                cp.wait()

    return pl.kernel(
        body, mesh=_sc_mesh(),
        out_type=jax.ShapeDtypeStruct((n_out, w), src.dtype),
        scratch_types=[pltpu.VMEM((kk, SC_CHUNK), jnp.int32), pltpu.VMEM((SC_CHUNK, w), src.dtype),
                       pltpu.SemaphoreType.DMA],
    )(src, pos3)


def _shift_down(a, k, row):
    return jnp.where(row >= k, pltpu.roll(a, k, axis=0), 0.0)


def _shift_up(a, k, row):
    n = a.shape[0]
    return jnp.where(row < n - k, pltpu.roll(a, n - k, axis=0), 0.0)


def _even_mix_kernel(x_ref, w_in_ref, pool_w_ref, pool_scale_ref, conv_w_ref, cat_ref, xb_ref):
    s = x_ref.shape[1]
    pool_width = pool_scale_ref.shape[1]
    gd = pool_width // len(POOL_WINDOWS)
    conv_width = conv_w_ref.shape[1]
    xb_ref[...] = x_ref[0].astype(BF16)
    xb = xb_ref[...]

    row = lax.broadcasted_iota(jnp.int32, (s, gd), 0)
    for g, w in enumerate(POOL_WINDOWS):
        lo = g * gd
        if g % 2 == 0:
            u2 = _dot(xb, w_in_ref[:, lo:lo + 2 * gd])
        u = u2[:, (g % 2) * gd:(g % 2 + 1) * gd]
        half = w // 2
        back, fwd, span = u, u, 1
        while span < half:
            back = back + _shift_down(back, span, row)
            fwd = fwd + _shift_up(fwd, span, row)
            span *= 2
        win = _shift_down(back, 1, row) + fwd
        cnt = (jnp.minimum(row + (w - half), s) - jnp.maximum(row - half, 0)).astype(F32)
        pooled = win / cnt - u
        mixed = _dot(pooled.astype(BF16), pool_w_ref[g])
        cat_ref[0, :, lo:lo + gd] = (mixed * pool_scale_ref[:, lo:lo + gd]).astype(BF16)

    cw = 256
    rowc = lax.broadcasted_iota(jnp.int32, (s, cw), 0)
    for j in range(conv_width // cw):
        c0 = j * cw
        b_gate = _dot(xb, w_in_ref[:, pool_width + c0:pool_width + c0 + cw])
        c_gate = _dot(xb, w_in_ref[:, pool_width + conv_width + c0:pool_width + conv_width + c0 + cw])
        v = _dot(xb, w_in_ref[:, pool_width + 2 * conv_width + c0:pool_width + 2 * conv_width + c0 + cw])
        u = c_gate * v
        conv = (_shift_down(u, 1, rowc) * conv_w_ref[0:1, c0:c0 + cw] + u * conv_w_ref[1:2, c0:c0 + cw]
                + _shift_up(u, 1, rowc) * conv_w_ref[2:3, c0:c0 + cw])
        cat_ref[0, :, pool_width + c0:pool_width + c0 + cw] = (b_gate * conv).astype(BF16)


def _even_mix(x, w_in, pool_w, pool_scale, conv_w):
    b, s, d = x.shape
    pool_width = pool_scale.shape[0]
    conv_width = conv_w.shape[1]
    assert conv_width % 256 == 0 and w_in.shape[1] == pool_width + 3 * conv_width
    cat_width = pool_width + conv_width
    return pl.pallas_call(
        _even_mix_kernel,
        grid=(b,),
        in_specs=[
            pl.BlockSpec((1, s, d), lambda i: (i, 0, 0)),
            pl.BlockSpec(w_in.shape, lambda i: (0, 0)),
            pl.BlockSpec(pool_w.shape, lambda i: (0, 0, 0)),
            pl.BlockSpec((1, pool_width), lambda i: (0, 0)),
            pl.BlockSpec(conv_w.shape, lambda i: (0, 0)),
        ],
        out_specs=pl.BlockSpec((1, s, cat_width), lambda i: (i, 0, 0)),
        out_shape=jax.ShapeDtypeStruct((b, s, cat_width), BF16),
        scratch_shapes=[pltpu.VMEM((s, d), BF16)],
        compiler_params=_params(1),
        name="even_mix",
    )(x, w_in.astype(BF16), pool_w.astype(BF16), pool_scale.reshape(1, pool_width), conv_w)


def _odd_proj_kernel(x_ref, w_ref, *refs, c_width, dils):
    outs, h_ref = refs[:-1], refs[-1]
    h = _dot(x_ref[...].astype(BF16), w_ref[...])
    for c in range(h_ref.shape[0]):
        h_ref[c] = h[:, c * LANES:(c + 1) * LANES]
    tm = x_ref.shape[0]
    chunks = c_width // LANES
    for pi, dil in enumerate(dils):
        n = tm // dil
        for r in range(dil):
            for j in range(3):
                for c in range(chunks):
                    src = h_ref[j * chunks + c] if dil == 1 else h_ref[j * chunks + c, pl.ds(r, n, stride=dil), :]
                    lo = r * c_width + c * LANES
                    outs[3 * pi + j][:, lo:lo + LANES] = src.astype(BF16)
    c0 = 3 * c_width
    for ref in outs[3 * len(dils):]:
        wd = ref.shape[1]
        ref[...] = h[:, c0:c0 + wd].astype(BF16)
        c0 += wd


def _odd_proj(xt, w_in, c_width, dq_width, dkv_width):
    t, d = xt.shape
    dils = tuple(dil for _, dil in C_PATTERNS)
    assert 3 * c_width + dq_width + 2 * dkv_width == w_in.shape[1] and c_width % LANES == 0
    col = jnp.arange(w_in.shape[1])
    is_q = (col < c_width) | ((col >= 3 * c_width) & (col < 3 * c_width + dq_width))
    w_in = w_in * jnp.where(is_q, HEAD_DIM ** -0.5 * LOG2E, 1.0)
    kv0 = 3 * c_width + dq_width
    twice = lambda w: jnp.repeat(w.reshape(d, -1, HEAD_DIM), LANES // HEAD_DIM, axis=1).reshape(d, -1)
    w_in = jnp.concatenate([w_in[:, :kv0], twice(w_in[:, kv0:kv0 + dkv_width]), twice(w_in[:, kv0 + dkv_width:])], axis=1)
    dkv_width = dkv_width * (LANES // HEAD_DIM)
    w_scaled = w_in.astype(BF16)
    shapes = [(t // dil, dil * c_width) for dil in dils for _ in range(3)]
    blocks = [(ROW_TILE // dil, dil * c_width) for dil in dils for _ in range(3)]
    for wd in (dq_width, dkv_width, dkv_width):
        shapes.append((t, wd))
        blocks.append((ROW_TILE, wd))
    outs = pl.pallas_call(
        functools.partial(_odd_proj_kernel, c_width=c_width, dils=dils),
        grid=(t // ROW_TILE,),
        in_specs=[pl.BlockSpec((ROW_TILE, d), lambda i: (i, 0)),
                  pl.BlockSpec(w_in.shape, lambda i: (0, 0))],
        out_specs=[pl.BlockSpec(blk, lambda i: (i, 0)) for blk in blocks],
        out_shape=[jax.ShapeDtypeStruct(shp, BF16) for shp in shapes],
        scratch_shapes=[pltpu.VMEM((w_in.shape[1] // LANES, ROW_TILE, LANES), F32)],
        compiler_params=_params(1),
        name="odd_proj",
    )(xt, w_scaled)
    qkv = [outs[3 * pi:3 * pi + 3] for pi in range(len(dils))]
    return qkv, outs[3 * len(dils):]


def _band_shape(length, radius):
    qb = min(ATTN_Q_BLOCK, length)
    span = min(length, qb + 2 * radius)
    n_blk = length // qb
    assert length % qb == 0 and (n_blk <= 2 or radius <= qb)
    return qb, span, n_blk, (0, -radius, qb - span)


def _fill_band_bias(bias_ref, length, radius, dist_unit, n_heads):
    qb, span, _, offsets = _band_shape(length, radius)
    rel = lax.broadcasted_iota(jnp.int32, (qb, span), 1) - lax.broadcasted_iota(jnp.int32, (qb, span), 0)
    for v, off in enumerate(offsets):
        dist = jnp.abs(rel + off)
        mask = jnp.where(dist <= radius, 0.0, NEG_INF).astype(F32)
        far = dist.astype(F32) * (dist_unit * LOG2E)
        for hh in range(n_heads):
            bias_ref[v, hh] = mask - 2.0 ** (-8.0 * (hh + 1) / n_heads) * far


def _band_block(i, length, radius):
    qb, span, n_blk, offsets = _band_shape(length, radius)
    q0 = pl.multiple_of(i * qb, qb)
    variant = jnp.where(i == 0, 0, jnp.where(i == n_blk - 1, 2, 1))
    off = jnp.where(i == 0, offsets[0], jnp.where(i == n_blk - 1, offsets[2], offsets[1]))
    align = 8
    for cand in (128, 64, 32, 16):
        if all(o % cand == 0 for o in offsets) and qb % cand == 0:
            align = cand
            break
    return q0, pl.multiple_of(q0 + off, align), variant


def _attend_pairs(q_slabs, k_slabs, v_slabs, biases):
    lower = lax.broadcasted_iota(jnp.int32, (1, LANES), 1) < HEAD_DIM
    scores = []
    for j, (q2, k2) in enumerate(zip(q_slabs, k_slabs)):
        zero = jnp.zeros_like(q2)
        scores.append(_dot_nt(jnp.where(lower, q2, zero), k2) + biases[2 * j])
        scores.append(_dot_nt(jnp.where(lower, zero, q2), k2) + biases[2 * j + 1])
    probs = []
    for s in scores:
        m = jnp.max(s, axis=-1, keepdims=True)
        probs.append((jnp.exp2(s - m).astype(BF16), m))
    ones = jnp.ones(v_slabs[0].shape, BF16)
    outs = []
    for j, v2 in enumerate(v_slabs):
        (p_lo, m_lo), (p_up, m_up) = probs[2 * j], probs[2 * j + 1]
        v_ones = jnp.concatenate([v2, ones], axis=1)
        r_lo, r_up = _dot(p_lo, v_ones), _dot(p_up, v_ones)
        l = jnp.where(lower, r_lo[:, LANES:], r_up[:, LANES:])
        o = jnp.where(lower, r_lo[:, :LANES], r_up[:, :LANES]) / l
        outs.append((o, (jnp.where(lower, m_lo, m_up) + jnp.log2(l)) * (1.0 / LOG2E)))
    return outs


def _attn_c_kernel(q_ref, k_ref, v_ref, o_ref, lse_ref, bias_ref, *, dil, radius, n_heads):
    length = q_ref.shape[1]
    qb, span, n_blk, _ = _band_shape(length, radius)
    width = n_heads * HEAD_DIM

    @pl.when((pl.program_id(0) == 0) & (pl.program_id(1) == 0))
    def _():
        _fill_band_bias(bias_ref, length, radius, float(dil), n_heads)

    def block(i, carry):
        q0, start, variant = _band_block(i, length, radius)
        biases = [bias_ref[variant, hh] for hh in range(n_heads)]
        for res in range(q_ref.shape[2] // width):
            slabs = [res * width + j * LANES for j in range(width // LANES)]
            outs = _attend_pairs([q_ref[0, pl.ds(q0, qb), lo:lo + LANES] for lo in slabs],
                                 [k_ref[0, pl.ds(start, span), lo:lo + LANES] for lo in slabs],
                                 [v_ref[0, pl.ds(start, span), lo:lo + LANES] for lo in slabs], biases)
            for lo, (o, lse) in zip(slabs, outs):
                o_ref[0, pl.ds(q0, qb), lo:lo + LANES] = o.astype(o_ref.dtype)
                lse_ref[0, pl.ds(q0, qb), lo:lo + LANES] = lse
        return carry

    lax.fori_loop(0, n_blk, block, 0)


def _attn_c(q, k, v, bsz, window, dil):
    rows, dw = q.shape
    w = dw // dil
    length = rows // bsz
    radius = window // 2 // dil
    qb, span, _, _ = _band_shape(length, radius)
    per_step = max(1, min(dil, ATTN_STEP_ROWS // length))
    assert dil % per_step == 0
    view = lambda a: a.reshape(bsz, length, dw)
    spec = pl.BlockSpec((1, length, per_step * w), lambda i, r: (i, 0, r))
    o, lse = pl.pallas_call(
        functools.partial(_attn_c_kernel, dil=dil, radius=radius, n_heads=w // HEAD_DIM),
        grid=(bsz, dil // per_step),
        in_specs=[spec, spec, spec],
        out_specs=[spec, spec],
        out_shape=[jax.ShapeDtypeStruct((bsz, length, dw), BF16),
                   jax.ShapeDtypeStruct((bsz, length, dw), F32)],
        scratch_shapes=[pltpu.VMEM((3, w // HEAD_DIM, qb, span), F32)],
        compiler_params=_params(2),
        name=f"attn_c_d{dil}",
    )(view(q), view(k), view(v))
    return o.reshape(rows, dw), lse.reshape(rows, dw)


def _attn_d_kernel(sink_ref, q_ref, k_ref, v_ref, y_ref, bias_ref, *, radius, n_heads, group):
    length = q_ref.shape[1]
    qb, span, n_blk, _ = _band_shape(length, radius)
    pairs = n_heads * HEAD_DIM // LANES
    lower = lax.broadcasted_iota(jnp.int32, (1, LANES), 1) < HEAD_DIM

    @pl.when(pl.program_id(0) == 0)
    def _():
        _fill_band_bias(bias_ref, length, radius, 1.0, n_heads)

    def block(i, carry):
        q0, start, variant = _band_block(i, length, radius)
        biases = [bias_ref[variant, hh] for hh in range(n_heads)]
        kv_lo = [(2 * j // group) * LANES for j in range(pairs)]
        outs = _attend_pairs([q_ref[0, pl.ds(q0, qb), j * LANES:(j + 1) * LANES] for j in range(pairs)],
                             [k_ref[0, pl.ds(start, span), lo:lo + LANES] for lo in kv_lo],
                             [v_ref[0, pl.ds(start, span), lo:lo + LANES] for lo in kv_lo], biases)
        for j, (o, lse) in enumerate(outs):
            sink = jnp.where(lower, sink_ref[2 * j], sink_ref[2 * j + 1])
            y_ref[0, pl.ds(q0, qb), j * LANES:(j + 1) * LANES] = (o * jax.nn.sigmoid(lse - sink)).astype(y_ref.dtype)
        return carry

    lax.fori_loop(0, n_blk, block, 0)


def _attn_d(q, k, v, sink):
    b, s, w = q.shape
    kvw = k.shape[2]
    n_heads = w // HEAD_DIM
    group = n_heads // D_KV_HEADS
    assert kvw == D_KV_HEADS * LANES and group % 2 == 0
    qb, span, _, _ = _band_shape(s, D_RADIUS)
    return pl.pallas_call(
        functools.partial(_attn_d_kernel, radius=D_RADIUS, n_heads=n_heads, group=group),
        grid=(b,),
        in_specs=[pl.BlockSpec(memory_space=pltpu.SMEM),
                  pl.BlockSpec((1, s, w), lambda i: (i, 0, 0)),
                  pl.BlockSpec((1, s, kvw), lambda i: (i, 0, 0)),
                  pl.BlockSpec((1, s, kvw), lambda i: (i, 0, 0))],
        out_specs=pl.BlockSpec((1, s, w), lambda i: (i, 0, 0)),
        out_shape=jax.ShapeDtypeStruct((b, s, w), BF16),
        scratch_shapes=[pltpu.VMEM((3, n_heads, qb, span), F32)],
        compiler_params=_params(1),
        name="attn_d",
    )(sink.astype(F32), q, k, v)


def _merge_kernel(*refs, dils, cw):
    n_pat = len(dils)
    o_refs, l_refs = refs[:n_pat], refs[n_pat:2 * n_pat]
    yd_ref, cat_ref = refs[2 * n_pat], refs[2 * n_pat + 1]
    scratch = list(refs[2 * n_pat + 2:])
    tm = cat_ref.shape[0]
    chunks = cw // LANES
    staged = []
    for p, dil in enumerate(dils):
        if dil == 1:
            staged.append(None)
            continue
        o_s, l_s = scratch.pop(0), scratch.pop(0)
        n = tm // dil
        for r in range(dil):
            for c in range(chunks):
                lo = r * cw + c * LANES
                o_s[c, pl.ds(r, n, stride=dil), :] = o_refs[p][:, lo:lo + LANES].astype(F32)
                l_s[c, pl.ds(r, n, stride=dil), :] = l_refs[p][:, lo:lo + LANES]
        staged.append((o_s, l_s))
    for c in range(chunks):
        cols = slice(c * LANES, (c + 1) * LANES)
        os_ = [o_refs[p][:, cols].astype(F32) if st is None else st[0][c] for p, st in enumerate(staged)]
        ls_ = [l_refs[p][:, cols] if st is None else st[1][c] for p, st in enumerate(staged)]
        m = functools.reduce(jnp.maximum, ls_)
        es = [jnp.exp(l - m) for l in ls_]
        num = functools.reduce(lambda a, b: a + b, [e * o for e, o in zip(es, os_)])
        den = functools.reduce(lambda a, b: a + b, es)
        cat_ref[:, cols] = (num / den).astype(BF16)
    cat_ref[:, cw:] = yd_ref[...]


def _merge(os_, lses, yd, dils):
    t, dw = yd.shape
    cw = os_[0].shape[1] // dils[0]
    vspecs = [pl.BlockSpec((ROW_TILE // dil, dil * cw), lambda i: (i, 0)) for dil in dils]
    n_scratch = 2 * sum(1 for dil in dils if dil != 1)
    return pl.pallas_call(
        functools.partial(_merge_kernel, dils=dils, cw=cw),
        grid=(t // ROW_TILE,),
        in_specs=vspecs + vspecs + [pl.BlockSpec((ROW_TILE, dw), lambda i: (i, 0))],
        out_specs=pl.BlockSpec((ROW_TILE, cw + dw), lambda i: (i, 0)),
        out_shape=jax.ShapeDtypeStruct((t, cw + dw), BF16),
        scratch_shapes=[pltpu.VMEM((cw // LANES, ROW_TILE, LANES), F32)] * n_scratch,
        compiler_params=_params(1),
        name="merge",
    )(*os_, *lses, yd)


def _layer_norm(z, g, b):
    mu = jnp.mean(z, axis=-1, keepdims=True)
    zc = z - mu
    var = jnp.mean(zc * zc, axis=-1, keepdims=True)
    return zc * lax.rsqrt(var + LN_EPS) * g + b


ROUTE_ROWS = 8


def _route_tile(logits, triu, count_ref):
    n_exp, tm = logits.shape
    sub = lax.broadcasted_iota(jnp.int32, (n_exp, tm), 0)
    out_row = lax.broadcasted_iota(jnp.int32, (ROUTE_ROWS, tm), 0)
    work = logits
    vals, onehots = [], []
    idx_out = jnp.zeros((ROUTE_ROWS, tm), jnp.int32)
    for k in range(TOP_K):
        m = jnp.max(work, axis=0, keepdims=True)
        idx = jnp.min(jnp.where(work == m, sub, n_exp), axis=0, keepdims=True)
        hot = sub == idx
        work = jnp.where(hot, -jnp.inf, work)
        vals.append(m)
        onehots.append(hot)
        idx_out = jnp.where(out_row == k, idx, idx_out)
    exps = [jnp.exp(v - vals[0]) for v in vals]
    den = functools.reduce(lambda a, b: a + b, exps)
    gate_out = jnp.zeros((ROUTE_ROWS, tm), F32)
    for k in range(TOP_K):
        gate_out = jnp.where(out_row == k, exps[k] / den, gate_out)
    multihot = functools.reduce(lambda a, b: a + b, [h.astype(F32) for h in onehots])
    before = _dot(multihot.astype(BF16), triu) + count_ref[...]
    rank_out = jnp.zeros((ROUTE_ROWS, tm), jnp.int32)
    for k in range(TOP_K):
        rank = jnp.sum(jnp.where(onehots[k], before, 0.0), axis=0, keepdims=True)
        rank_out = jnp.where(out_row == k, rank.astype(jnp.int32), rank_out)
    count_ref[...] += jnp.sum(multihot, axis=1, keepdims=True)
    return idx_out, gate_out, rank_out


def _out_ln_kernel(cat_ref, w_ref, x_ref, g_ref, b_ref, rw_hi_ref, rw_lo_ref, rb_ref, triu_ref,
                   x1_ref, x1p_ref, idx_ref, gate_ref, rank_ref, count_ref, *, alpha):
    @pl.when(pl.program_id(0) == 0)
    def _():
        count_ref[...] = jnp.zeros_like(count_ref)

    mix = _dot(cat_ref[...], w_ref[...])
    x1 = _layer_norm(alpha * x_ref[...] + mix, g_ref[...], b_ref[...])
    x1_ref[...] = x1
    x1p_ref[...] = _pack_pairs(x1)
    hi = x1.astype(BF16)
    lo = (x1 - hi.astype(F32)).astype(BF16)
    logits = (_dot_nt(rw_hi_ref[...], hi) + _dot_nt(rw_hi_ref[...], lo) + _dot_nt(rw_lo_ref[...], hi)
              + rb_ref[...])
    idx_ref[...], gate_ref[...], rank_ref[...] = _route_tile(logits, triu_ref[...], count_ref)


def _out_ln(cat, w_out, xt, g, b, router_w, router_b, alpha):
    t, d = xt.shape
    n_exp = router_w.shape[1]
    rw_t = router_w.T
    rw_hi = rw_t.astype(BF16)
    rw_lo = (rw_t - rw_hi.astype(F32)).astype(BF16)
    triu = jnp.triu(jnp.ones((ROW_TILE, ROW_TILE), BF16), 1)
    row = lambda wd: pl.BlockSpec((ROW_TILE, wd), lambda i: (i, 0))
    col = pl.BlockSpec((ROUTE_ROWS, ROW_TILE), lambda i: (0, i))
    full = lambda a: pl.BlockSpec(a.shape, lambda i: (0,) * a.ndim)
    args = (cat, w_out.astype(BF16), xt, g.reshape(1, d), b.reshape(1, d), rw_hi, rw_lo,
            router_b.reshape(n_exp, 1), triu)
    return pl.pallas_call(
        functools.partial(_out_ln_kernel, alpha=alpha),
        grid=(t // ROW_TILE,),
        in_specs=[row(cat.shape[1]), full(args[1]), row(d)] + [full(a) for a in args[3:]],
        out_specs=[row(d), row(d // 2), col, col, col, pl.BlockSpec((n_exp, 1), lambda i: (0, 0))],
        out_shape=[jax.ShapeDtypeStruct((t, d), F32), jax.ShapeDtypeStruct((t, d // 2), jnp.int32),
                   jax.ShapeDtypeStruct((ROUTE_ROWS, t), jnp.int32), jax.ShapeDtypeStruct((ROUTE_ROWS, t), F32),
                   jax.ShapeDtypeStruct((ROUTE_ROWS, t), jnp.int32), jax.ShapeDtypeStruct((n_exp, 1), F32)],
        compiler_params=_params(1),
        name="out_ln",
    )(*args)


def _experts_kernel(first_ref, blocks_ref, count_ref, x_hbm, wgu_ref, bgu_ref, wd_ref, bd_ref, perm_ref,
                    y_hbm, wgu_s, wd_s, x_buf, y_buf, x_sem, y_sem, zero_sem):
    e = pl.program_id(0)
    last = pl.num_programs(0) - 1
    n_blk = blocks_ref[e]
    first = first_ref[e]
    count = count_ref[e]
    used = first_ref[last] + blocks_ref[last]
    half = GU_BLOCK // 2

    def rows_of(g):
        return pl.ds(pl.multiple_of(g * MOE_BLOCK, MOE_BLOCK), MOE_BLOCK)

    def x_copy(g, slot):
        return pltpu.make_async_copy(x_hbm.at[rows_of(g)], x_buf.at[slot], x_sem.at[slot])

    def y_copy(g, slot):
        return pltpu.make_async_copy(y_buf.at[slot], y_hbm.at[rows_of(g)], y_sem.at[slot])

    def compute(g, slot):
        xb = _unpack_pairs(x_buf[slot])
        row = lax.broadcasted_iota(jnp.int32, xb.shape, 0)
        xb = jnp.where(row < count - (g - first) * MOE_BLOCK, xb, jnp.zeros_like(xb))
        h = _dot(xb, wgu_s[...]) + bgu_ref[...]
        acts = []
        for c in range(h.shape[1] // GU_BLOCK):
            glu = jnp.minimum(h[:, c * GU_BLOCK:c * GU_BLOCK + half], SWIGLU_LIMIT)
            lin = jnp.clip(h[:, c * GU_BLOCK + half:(c + 1) * GU_BLOCK], -SWIGLU_LIMIT, SWIGLU_LIMIT)
            acts.append((glu * jax.nn.sigmoid(SWIGLU_ALPHA * glu) * (lin + 1.0)).astype(BF16))
        act = jnp.concatenate(acts, axis=1)
        y_buf[slot] = _pack_pairs(_dot(act, wd_s[...]) + bd_ref[...])

    @pl.when((e == 0) & (used > 0))
    def _():
        x_copy(0, 0).start()

    @pl.when(n_blk > 0)
    def _():
        for c in range(wgu_ref.shape[1] // GU_BLOCK):
            cols = slice(c * GU_BLOCK, (c + 1) * GU_BLOCK)
            wgu_s[:, cols] = _dot(wgu_ref[:, cols].astype(BF16), perm_ref[...]).astype(BF16)
        wd_s[...] = wd_ref[...].astype(BF16)

    def step(g, carry):
        slot = g % 2
        x_copy(g, slot).wait()

        @pl.when(g + 1 < used)
        def _():
            x_copy(g + 1, 1 - slot).start()

        @pl.when(g >= 2)
        def _():
            y_copy(g - 2, slot).wait()

        compute(g, slot)
        y_copy(g, slot).start()
        return carry

    lax.fori_loop(first, first + n_blk, step, 0)

    @pl.when(e == last)
    def _():
        @pl.when(used >= 1)
        def _():
            y_copy(0, (used - 1) % 2).wait()

        @pl.when(used >= 2)
        def _():
            y_copy(0, used % 2).wait()

        total = y_hbm.shape[0] // MOE_BLOCK
        y_buf[0] = jnp.zeros(y_buf.shape[1:], y_buf.dtype)

        def tail_copy(g):
            return pltpu.make_async_copy(y_buf.at[0], y_hbm.at[rows_of(g)], zero_sem)

        def start_all(g, carry):
            tail_copy(g).start()
            return carry

        def wait_all(g, carry):
            tail_copy(g).wait()
            return carry

        lax.fori_loop(used, total, start_all, 0)
        lax.fori_loop(used, total, wait_all, 0)


def _experts(x_rows, first_blk, n_blk, counts, layer, w_gu, b_gu, w_down, b_down):
    rows, dp = x_rows.shape
    _, n_exp, d, de2 = w_gu.shape
    assert de2 % GU_BLOCK == 0 and dp * 2 == d and rows % MOE_BLOCK == 0
    half = GU_BLOCK // 2
    j = jnp.arange(GU_BLOCK)
    src = jnp.where(j < half, 2 * j, 2 * (j - half) + 1)
    perm = (jnp.arange(GU_BLOCK)[:, None] == src[None, :]).astype(BF16)
    wspec = lambda a: pl.BlockSpec((None, None) + a.shape[2:], lambda e, *_: (layer, e, 0, 0))
    grid_spec = pltpu.PrefetchScalarGridSpec(
        num_scalar_prefetch=3,
        grid=(n_exp,),
        in_specs=[pl.BlockSpec(memory_space=pl.ANY),
                  wspec(w_gu), wspec(b_gu), wspec(w_down), wspec(b_down),
                  pl.BlockSpec((GU_BLOCK, GU_BLOCK), lambda e, *_: (0, 0))],
        out_specs=pl.BlockSpec(memory_space=pl.ANY),
        scratch_shapes=[pltpu.VMEM(w_gu.shape[2:], BF16), pltpu.VMEM(w_down.shape[2:], BF16),
                        pltpu.VMEM((2, MOE_BLOCK, dp), jnp.int32), pltpu.VMEM((2, MOE_BLOCK, dp), jnp.int32),
                        pltpu.SemaphoreType.DMA((2,)), pltpu.SemaphoreType.DMA((2,)),
                        pltpu.SemaphoreType.DMA(())],
    )
    return pl.pallas_call(
        _experts_kernel,
        grid_spec=grid_spec,
        out_shape=jax.ShapeDtypeStruct((rows, dp), jnp.int32),
        compiler_params=_params(1),
        name="experts",
    )(first_blk, n_blk, counts, x_rows, w_gu, b_gu, w_down, b_down, perm)


def _combine_ln_kernel(x1_ref, yk_ref, gate_ref, g_ref, b_ref, x2_ref, *, alpha):
    gates = gate_ref[...]
    ffn = gates[:, 0:1] * _unpack_pairs(yk_ref[0]).astype(F32)
    for k in range(1, yk_ref.shape[0]):
        ffn = ffn + gates[:, k:k + 1] * _unpack_pairs(yk_ref[k]).astype(F32)
    x2_ref[...] = _layer_norm(alpha * x1_ref[...] + ffn, g_ref[...], b_ref[...])


def _combine_ln(x1, yk, gates, g, b, alpha):
    t, d = x1.shape
    k = yk.shape[0]
    return pl.pallas_call(
        functools.partial(_combine_ln_kernel, alpha=alpha),
        grid=(t // ROW_TILE,),
        in_specs=[pl.BlockSpec((ROW_TILE, d), lambda i: (i, 0)),
                  pl.BlockSpec((k, ROW_TILE, d // 2), lambda i: (0, i, 0)),
                  pl.BlockSpec((ROW_TILE, k), lambda i: (i, 0)),
                  pl.BlockSpec((1, d), lambda i: (0, 0)),
                  pl.BlockSpec((1, d), lambda i: (0, 0))],
        out_specs=pl.BlockSpec((ROW_TILE, d), lambda i: (i, 0)),
        out_shape=jax.ShapeDtypeStruct((t, d), F32),
        compiler_params=_params(1),
        name="combine_ln",
    )(x1, yk, gates, g.reshape(1, d), b.reshape(1, d))


def _route(top_idx, rank, counts):
    n_exp = counts.shape[0]
    experts = jnp.arange(n_exp, dtype=jnp.int32)
    counts = counts.reshape(n_exp).astype(jnp.int32)
    padded = (counts + MOE_BLOCK - 1) // MOE_BLOCK * MOE_BLOCK
    pad_end = jnp.cumsum(padded)
    pad_start = pad_end - padded
    start = jnp.sum(jnp.where(top_idx[None] == experts[:, None, None], pad_start[:, None, None], 0), axis=0)
    pos = rank + start
    first_blk = (pad_start // MOE_BLOCK).astype(jnp.int32)
    n_blk = (padded // MOE_BLOCK).astype(jnp.int32)
    return pos, first_blk, n_blk, counts


def _moe(x1, x1p, top_idx, gates, rank, counts, layer, w_gu, b_gu, w_down, b_down, g, b, alpha):
    t, d = x1.shape
    n_exp = counts.shape[0]
    pos, first_blk, n_blk, counts = _route(top_idx[:TOP_K], rank[:TOP_K], counts)
    rows = (-(-(t * TOP_K) // MOE_BLOCK) + n_exp) * MOE_BLOCK
    pos3 = pos.reshape(TOP_K, t // SC_CHUNK, SC_CHUNK).transpose(1, 0, 2)
    x_rows = _sc_scatter_rows(x1p, pos3, rows)
    y = _experts(x_rows, first_blk, n_blk, counts, layer, w_gu, b_gu, w_down, b_down)
    yk = _sc_gather_rows(y, pos.reshape(-1)).reshape(TOP_K, t, d // 2)
    return _combine_ln(x1, yk, gates[:TOP_K].T, g, b, alpha)


def kernel(x, ev_w_in, ev_pool_w, ev_pool_scale, ev_conv_w, ev_w_out, od_w_in, od_sink, od_w_out,
           router_w, router_b, exp_w_gu, exp_b_gu, exp_w_down, exp_b_down, ln_g, ln_b):
    bsz, seq, d = x.shape
    t = bsz * seq
    depth = ln_g.shape[0]
    alpha = (2 * depth) ** 0.25
    n_exp = router_w.shape[2]
    c_width = d // 2
    dq_width = d // 2
    dkv_width = D_KV_HEADS * HEAD_DIM
    assert t % ROW_TILE == 0

    de2 = exp_b_gu.shape[-1]
    b_gu = exp_b_gu.reshape(depth, n_exp, de2 // GU_BLOCK, GU_BLOCK // 2, 2)
    b_gu = jnp.swapaxes(b_gu, -1, -2).reshape(depth, n_exp, 1, de2)
    b_down = exp_b_down.reshape(depth, n_exp, 1, -1)

    xt = x.reshape(t, d)
    for layer in range(depth):
        i = layer // 2
        if layer % 2 == 0:
            cat = _even_mix(xt.reshape(bsz, seq, d), ev_w_in[i], ev_pool_w[i], ev_pool_scale[i], ev_conv_w[i])
            cat = cat.reshape(t, -1)
            w_out = ev_w_out[i]
        else:
            qkv, (qd, kd, vd) = _odd_proj(xt, od_w_in[i], c_width, dq_width, dkv_width)
            dils = tuple(dil for _, dil in C_PATTERNS)
            os_, lses = [], []
            for (window, dil), (qc, kc, vc) in zip(C_PATTERNS, qkv):
                o, lse = _attn_c(qc, kc, vc, bsz, window, dil)
                os_.append(o)
                lses.append(lse)
            seq3 = lambda a: a.reshape(bsz, seq, a.shape[1])
            yd = _attn_d(seq3(qd), seq3(kd), seq3(vd), od_sink[i]).reshape(t, dq_width)
            cat = _merge(os_, lses, yd, dils)
            w_out = od_w_out[i]
        x1, x1p, top_idx, gates, rank, counts = _out_ln(cat, w_out, xt, ln_g[layer, 0], ln_b[layer, 0],
                                                        router_w[layer], router_b[layer], alpha)
        xt = _moe(x1, x1p, top_idx, gates, rank, counts, layer, exp_w_gu, b_gu, exp_w_down, b_down,
                  ln_g[layer, 1], ln_b[layer, 1], alpha)
    return xt.reshape(bsz, seq, d)
```

```python
import functools

import jax
import jax.numpy as jnp
from jax import lax
from jax.experimental import pallas as pl
from jax.experimental.pallas import tpu as pltpu
from jax.experimental.pallas import tpu_sc as plsc

HEAD_DIM = 64
POOL_WINDOWS = (2, 4, 8, 16)
C_PATTERNS = ((128, 1), (512, 4), (2048, 16))
D_KV_HEADS = 2
D_RADIUS = 128
TOP_K = 4
SWIGLU_LIMIT = 7.0
SWIGLU_ALPHA = 1.702
MOE_BLOCK = 512
LN_EPS = 1e-5
NEG_INF = -1e30
LOG2E = 1.4426950408889634

ROW_TILE = 512
COMBINE_TILE = 1024
ATTN_Q_BLOCK = 128
ATTN_STEP_ROWS = 512
LANES = 128
GU_BLOCK = 256
VMEM_LIMIT_BYTES = 56 * 1024 * 1024

F32 = jnp.float32
BF16 = jnp.bfloat16


def _params(n_axes=1):
    return pltpu.CompilerParams(dimension_semantics=("arbitrary",) * n_axes,
                                vmem_limit_bytes=VMEM_LIMIT_BYTES)


def _dot(a, b):
    return jnp.dot(a, b, preferred_element_type=F32)


def _dot_nt(a, b):
    return lax.dot_general(a, b, (((1,), (1,)), ((), ())), preferred_element_type=F32)


def _pack_pairs(a):
    n = a.shape[1] // 2
    bits = lax.bitcast_convert_type(a.astype(BF16).astype(F32), jnp.int32)
    return bits[:, :n] | lax.shift_right_logical(bits[:, n:], 16)


def _unpack_pairs(w):
    hi = lax.bitcast_convert_type(w & jnp.int32(-65536), F32)
    lo = lax.bitcast_convert_type(lax.shift_left(w, 16), F32)
    return jnp.concatenate([hi, lo], axis=1).astype(BF16)


SC_CORES = 2
SC_SUBCORES = 16
SC_WORKERS = SC_CORES * SC_SUBCORES
SC_CHUNK = 64


def _sc_mesh():
    return plsc.VectorSubcoreMesh(core_axis_name="c", subcore_axis_name="s")


def _sc_worker():
    return lax.axis_index("s") * SC_CORES + lax.axis_index("c")


def _sc_gather_rows(table, idx):
    n, w = idx.shape[0], table.shape[1]
    per_w = n // SC_WORKERS
    assert n % SC_WORKERS == 0 and per_w % (2 * SC_CHUNK) == 0

    def body(table_hbm, idx_hbm, out_hbm, i0, i1, r0, r1, s0, s1):
        base = _sc_worker() * per_w

        def start(off, iv, rv, sem):
            pltpu.sync_copy(idx_hbm.at[pl.ds(off, SC_CHUNK)], iv)
            pltpu.async_copy(table_hbm.at[iv], rv, sem)

        def finish(off, iv, rv, sem):
            pltpu.make_async_copy(table_hbm.at[iv], rv, sem).wait()
            pltpu.sync_copy(rv, out_hbm.at[pl.ds(off, SC_CHUNK)])

        start(base, i0, r0, s0)

        @pl.loop(0, per_w, step=2 * SC_CHUNK)
        def _(o):
            off = base + o
            start(off + SC_CHUNK, i1, r1, s1)
            finish(off, i0, r0, s0)

            @pl.when(o + 2 * SC_CHUNK < per_w)
            def _():
                start(off + 2 * SC_CHUNK, i0, r0, s0)

            finish(off + SC_CHUNK, i1, r1, s1)

    return pl.kernel(
        body, mesh=_sc_mesh(),
        out_type=jax.ShapeDtypeStruct((n, w), table.dtype),
        scratch_types=[pltpu.VMEM((SC_CHUNK,), jnp.int32), pltpu.VMEM((SC_CHUNK,), jnp.int32),
                       pltpu.VMEM((SC_CHUNK, w), table.dtype), pltpu.VMEM((SC_CHUNK, w), table.dtype),
                       pltpu.SemaphoreType.DMA, pltpu.SemaphoreType.DMA],
    )(table, idx)


def _sc_scatter_rows(src, pos3, n_out):
    t, w = src.shape
    kk = pos3.shape[1]
    per_w = t // SC_WORKERS
    assert t % SC_WORKERS == 0 and per_w % SC_CHUNK == 0 and pos3.shape == (t // SC_CHUNK, kk, SC_CHUNK)

    def body(src_hbm, pos_hbm, out_hbm, iv, rv, sem):
        base = _sc_worker() * per_w

        @pl.loop(0, per_w, step=SC_CHUNK)
        def _(o):
            off = base + o
            pltpu.sync_copy(pos_hbm.at[off // SC_CHUNK], iv)
            pltpu.sync_copy(src_hbm.at[pl.ds(off, SC_CHUNK)], rv)
            copies = [pltpu.async_copy(rv, out_hbm.at[iv.at[j]], sem) for j in range(kk)]
            for cp in copies:
                cp.wait()

    return pl.kernel(
        body, mesh=_sc_mesh(),
        out_type=jax.ShapeDtypeStruct((n_out, w), src.dtype),
        scratch_types=[pltpu.VMEM((kk, SC_CHUNK), jnp.int32), pltpu.VMEM((SC_CHUNK, w), src.dtype),
                       pltpu.SemaphoreType.DMA],
    )(src, pos3)


def _shift_down(a, k, row):
    return jnp.where(row >= k, pltpu.roll(a, k, axis=0), 0.0)


def _shift_up(a, k, row):
    n = a.shape[0]
    return jnp.where(row < n - k, pltpu.roll(a, n - k, axis=0), 0.0)


def _even_mix_kernel(x_ref, w_in_ref, pool_w_ref, pool_scale_ref, conv_w_ref, cat_ref, xb_ref):
    s = x_ref.shape[1]
    pool_width = pool_scale_ref.shape[1]
    gd = pool_width // len(POOL_WINDOWS)
    conv_width = conv_w_ref.shape[1]
    xb_ref[...] = x_ref[0].astype(BF16)
    xb = xb_ref[...]

    row = lax.broadcasted_iota(jnp.int32, (s, gd), 0)
    for g, w in enumerate(POOL_WINDOWS):
        lo = g * gd
        if g % 2 == 0:
            u2 = _dot(xb, w_in_ref[:, lo:lo + 2 * gd])
        u = u2[:, (g % 2) * gd:(g % 2 + 1) * gd]
        half = w // 2
        back, fwd, span = u, u, 1
        while span < half:
            back = back + _shift_down(back, span, row)
            fwd = fwd + _shift_up(fwd, span, row)
            span *= 2
        win = _shift_down(back, 1, row) + fwd
        cnt = (jnp.minimum(row + (w - half), s) - jnp.maximum(row - half, 0)).astype(F32)
        pooled = win / cnt - u
        mixed = _dot(pooled.astype(BF16), pool_w_ref[g])
        cat_ref[0, :, lo:lo + gd] = (mixed * pool_scale_ref[:, lo:lo + gd]).astype(BF16)

    cw = 256
    rowc = lax.broadcasted_iota(jnp.int32, (s, cw), 0)
    for j in range(conv_width // cw):
        c0 = j * cw
        b_gate = _dot(xb, w_in_ref[:, pool_width + c0:pool_width + c0 + cw])
        c_gate = _dot(xb, w_in_ref[:, pool_width + conv_width + c0:pool_width + conv_width + c0 + cw])
        v = _dot(xb, w_in_ref[:, pool_width + 2 * conv_width + c0:pool_width + 2 * conv_width + c0 + cw])
        u = c_gate * v
        conv = (_shift_down(u, 1, rowc) * conv_w_ref[0:1, c0:c0 + cw] + u * conv_w_ref[1:2, c0:c0 + cw]
                + _shift_up(u, 1, rowc) * conv_w_ref[2:3, c0:c0 + cw])
        cat_ref[0, :, pool_width + c0:pool_width + c0 + cw] = (b_gate * conv).astype(BF16)


def _even_mix(x, w_in, pool_w, pool_scale, conv_w):
    b, s, d = x.shape
    pool_width = pool_scale.shape[0]
    conv_width = conv_w.shape[1]
    assert conv_width % 256 == 0 and w_in.shape[1] == pool_width + 3 * conv_width
    cat_width = pool_width + conv_width
    return pl.pallas_call(
        _even_mix_kernel,
        grid=(b,),
        in_specs=[
            pl.BlockSpec((1, s, d), lambda i: (i, 0, 0)),
            pl.BlockSpec(w_in.shape, lambda i: (0, 0)),
            pl.BlockSpec(pool_w.shape, lambda i: (0, 0, 0)),
            pl.BlockSpec((1, pool_width), lambda i: (0, 0)),
            pl.BlockSpec(conv_w.shape, lambda i: (0, 0)),
        ],
        out_specs=pl.BlockSpec((1, s, cat_width), lambda i: (i, 0, 0)),
        out_shape=jax.ShapeDtypeStruct((b, s, cat_width), BF16),
        scratch_shapes=[pltpu.VMEM((s, d), BF16)],
        compiler_params=_params(1),
        name="even_mix",
    )(x, w_in.astype(BF16), pool_w.astype(BF16), pool_scale.reshape(1, pool_width), conv_w)


def _odd_proj_kernel(x_ref, w_ref, *refs, c_width, dils):
    outs, h_ref = refs[:-1], refs[-1]
    xb = x_ref[...].astype(BF16)
    tm = x_ref.shape[0]
    chunks = c_width // LANES
    for j in range(3):
        h = _dot(xb, w_ref[:, j * c_width:(j + 1) * c_width])
        for c in range(chunks):
            h_ref[j, c] = h[:, c * LANES:(c + 1) * LANES]
        for pi, dil in enumerate(dils):
            n = tm // dil
            for r in range(dil):
                for c in range(chunks):
                    src = h_ref[j, c] if dil == 1 else h_ref[j, c, pl.ds(r, n, stride=dil), :]
                    lo = r * c_width + c * LANES
                    outs[3 * pi + j][:, lo:lo + LANES] = src.astype(BF16)
    c0 = 3 * c_width
    for ref in outs[3 * len(dils):]:
        wd = ref.shape[1]
        ref[...] = _dot(xb, w_ref[:, c0:c0 + wd]).astype(BF16)
        c0 += wd


def _odd_proj(xt, w_in, c_width, dq_width, dkv_width):
    t, d = xt.shape
    dils = tuple(dil for _, dil in C_PATTERNS)
    assert 3 * c_width + dq_width + 2 * dkv_width == w_in.shape[1] and c_width % LANES == 0
    col = jnp.arange(w_in.shape[1])
    is_q = (col < c_width) | ((col >= 3 * c_width) & (col < 3 * c_width + dq_width))
    w_in = w_in * jnp.where(is_q, HEAD_DIM ** -0.5 * LOG2E, 1.0)
    kv0 = 3 * c_width + dq_width
    twice = lambda w: jnp.repeat(w.reshape(d, -1, HEAD_DIM), LANES // HEAD_DIM, axis=1).reshape(d, -1)
    w_in = jnp.concatenate([w_in[:, :kv0], twice(w_in[:, kv0:kv0 + dkv_width]), twice(w_in[:, kv0 + dkv_width:])], axis=1)
    dkv_width = dkv_width * (LANES // HEAD_DIM)
    w_scaled = w_in.astype(BF16)
    shapes = [(t // dil, dil * c_width) for dil in dils for _ in range(3)]
    blocks = [(ROW_TILE // dil, dil * c_width) for dil in dils for _ in range(3)]
    for wd in (dq_width, dkv_width, dkv_width):
        shapes.append((t, wd))
        blocks.append((ROW_TILE, wd))
    outs = pl.pallas_call(
        functools.partial(_odd_proj_kernel, c_width=c_width, dils=dils),
        grid=(t // ROW_TILE,),
        in_specs=[pl.BlockSpec((ROW_TILE, d), lambda i: (i, 0)),
                  pl.BlockSpec(w_in.shape, lambda i: (0, 0))],
        out_specs=[pl.BlockSpec(blk, lambda i: (i, 0)) for blk in blocks],
        out_shape=[jax.ShapeDtypeStruct(shp, BF16) for shp in shapes],
        scratch_shapes=[pltpu.VMEM((3, c_width // LANES, ROW_TILE, LANES), F32)],
        compiler_params=_params(1),
        name="odd_proj",
    )(xt, w_scaled)
    qkv = [outs[3 * pi:3 * pi + 3] for pi in range(len(dils))]
    return qkv, outs[3 * len(dils):]


def _band_shape(length, radius):
    qb = min(ATTN_Q_BLOCK, length)
    span = min(length, qb + 2 * radius)
    n_blk = length // qb
    assert length % qb == 0 and (n_blk <= 2 or radius <= qb)
    return qb, span, n_blk, (0, -radius, qb - span)


def _fill_band_bias(bias_ref, length, radius, dist_unit, n_heads):
    qb, span, _, offsets = _band_shape(length, radius)
    rel = lax.broadcasted_iota(jnp.int32, (qb, span), 1) - lax.broadcasted_iota(jnp.int32, (qb, span), 0)
    for v, off in enumerate(offsets):
        dist = jnp.abs(rel + off)
        mask = jnp.where(dist <= radius, 0.0, NEG_INF).astype(F32)
        far = dist.astype(F32) * (dist_unit * LOG2E)
        for hh in range(n_heads):
            bias_ref[v, hh] = mask - 2.0 ** (-8.0 * (hh + 1) / n_heads) * far


def _band_block(i, length, radius):
    qb, span, n_blk, offsets = _band_shape(length, radius)
    q0 = pl.multiple_of(i * qb, qb)
    variant = jnp.where(i == 0, 0, jnp.where(i == n_blk - 1, 2, 1))
    off = jnp.where(i == 0, offsets[0], jnp.where(i == n_blk - 1, offsets[2], offsets[1]))
    align = 8
    for cand in (128, 64, 32, 16):
        if all(o % cand == 0 for o in offsets) and qb % cand == 0:
            align = cand
            break
    return q0, pl.multiple_of(q0 + off, align), variant


def _attend_pairs(q_slabs, k_slabs, v_slabs, biases):
    lower = lax.broadcasted_iota(jnp.int32, (1, LANES), 1) < HEAD_DIM
    scores = []
    for j, (q2, k2) in enumerate(zip(q_slabs, k_slabs)):
        zero = jnp.zeros_like(q2)
        scores.append(_dot_nt(jnp.where(lower, q2, zero), k2) + biases[2 * j])
        scores.append(_dot_nt(jnp.where(lower, zero, q2), k2) + biases[2 * j + 1])
    probs = []
    for s in scores:
        m = jnp.max(s, axis=-1, keepdims=True)
        probs.append((jnp.exp2(s - m).astype(BF16), m))
    ones = jnp.ones(v_slabs[0].shape, BF16)
    outs = []
    for j, v2 in enumerate(v_slabs):
        (p_lo, m_lo), (p_up, m_up) = probs[2 * j], probs[2 * j + 1]
        v_ones = jnp.concatenate([v2, ones], axis=1)
        r_lo, r_up = _dot(p_lo, v_ones), _dot(p_up, v_ones)
        l = jnp.where(lower, r_lo[:, LANES:], r_up[:, LANES:])
        o = jnp.where(lower, r_lo[:, :LANES], r_up[:, :LANES]) / l
        outs.append((o, (jnp.where(lower, m_lo, m_up) + jnp.log2(l)) * (1.0 / LOG2E)))
    return outs


def _attn_c_kernel(q_ref, k_ref, v_ref, o_ref, lse_ref, bias_ref, *, dil, radius, n_heads):
    length = q_ref.shape[1]
    qb, span, n_blk, _ = _band_shape(length, radius)
    width = n_heads * HEAD_DIM

    @pl.when((pl.program_id(0) == 0) & (pl.program_id(1) == 0))
    def _():
        _fill_band_bias(bias_ref, length, radius, float(dil), n_heads)

    def block(i, carry):
        q0, start, variant = _band_block(i, length, radius)
        biases = [bias_ref[variant, hh] for hh in range(n_heads)]
        for res in range(q_ref.shape[2] // width):
            slabs = [res * width + j * LANES for j in range(width // LANES)]
            outs = _attend_pairs([q_ref[0, pl.ds(q0, qb), lo:lo + LANES] for lo in slabs],
                                 [k_ref[0, pl.ds(start, span), lo:lo + LANES] for lo in slabs],
                                 [v_ref[0, pl.ds(start, span), lo:lo + LANES] for lo in slabs], biases)
            for lo, (o, lse) in zip(slabs, outs):
                o_ref[0, pl.ds(q0, qb), lo:lo + LANES] = o.astype(o_ref.dtype)
                lse_ref[0, pl.ds(q0, qb), lo:lo + LANES] = lse
        return carry

    lax.fori_loop(0, n_blk, block, 0)


def _attn_c(q, k, v, bsz, window, dil):
    rows, dw = q.shape
    w = dw // dil
    length = rows // bsz
    radius = window // 2 // dil
    qb, span, _, _ = _band_shape(length, radius)
    per_step = max(1, min(dil, ATTN_STEP_ROWS // length))
    assert dil % per_step == 0
    view = lambda a: a.reshape(bsz, length, dw)
    spec = pl.BlockSpec((1, length, per_step * w), lambda i, r: (i, 0, r))
    o, lse = pl.pallas_call(
        functools.partial(_attn_c_kernel, dil=dil, radius=radius, n_heads=w // HEAD_DIM),
        grid=(bsz, dil // per_step),
        in_specs=[spec, spec, spec],
        out_specs=[spec, spec],
        out_shape=[jax.ShapeDtypeStruct((bsz, length, dw), BF16),
                   jax.ShapeDtypeStruct((bsz, length, dw), F32)],
        scratch_shapes=[pltpu.VMEM((3, w // HEAD_DIM, qb, span), F32)],
        compiler_params=_params(2),
        name=f"attn_c_d{dil}",
    )(view(q), view(k), view(v))
    return o.reshape(rows, dw), lse.reshape(rows, dw)


def _attn_d_kernel(sink_ref, q_ref, k_ref, v_ref, y_ref, bias_ref, *, radius, n_heads, group):
    length = q_ref.shape[1]
    qb, span, n_blk, _ = _band_shape(length, radius)
    pairs = n_heads * HEAD_DIM // LANES
    lower = lax.broadcasted_iota(jnp.int32, (1, LANES), 1) < HEAD_DIM

    @pl.when(pl.program_id(0) == 0)
    def _():
        _fill_band_bias(bias_ref, length, radius, 1.0, n_heads)

    def block(i, carry):
        q0, start, variant = _band_block(i, length, radius)
        biases = [bias_ref[variant, hh] for hh in range(n_heads)]
        kv_lo = [(2 * j // group) * LANES for j in range(pairs)]
        outs = _attend_pairs([q_ref[0, pl.ds(q0, qb), j * LANES:(j + 1) * LANES] for j in range(pairs)],
                             [k_ref[0, pl.ds(start, span), lo:lo + LANES] for lo in kv_lo],
                             [v_ref[0, pl.ds(start, span), lo:lo + LANES] for lo in kv_lo], biases)
        for j, (o, lse) in enumerate(outs):
            sink = jnp.where(lower, sink_ref[2 * j], sink_ref[2 * j + 1])
            y_ref[0, pl.ds(q0, qb), j * LANES:(j + 1) * LANES] = (o * jax.nn.sigmoid(lse - sink)).astype(y_ref.dtype)
        return carry

    lax.fori_loop(0, n_blk, block, 0)


def _attn_d(q, k, v, sink):
    b, s, w = q.shape
    kvw = k.shape[2]
    n_heads = w // HEAD_DIM
    group = n_heads // D_KV_HEADS
    assert kvw == D_KV_HEADS * LANES and group % 2 == 0
    qb, span, _, _ = _band_shape(s, D_RADIUS)
    return pl.pallas_call(
        functools.partial(_attn_d_kernel, radius=D_RADIUS, n_heads=n_heads, group=group),
        grid=(b,),
        in_specs=[pl.BlockSpec(memory_space=pltpu.SMEM),
                  pl.BlockSpec((1, s, w), lambda i: (i, 0, 0)),
                  pl.BlockSpec((1, s, kvw), lambda i: (i, 0, 0)),
                  pl.BlockSpec((1, s, kvw), lambda i: (i, 0, 0))],
        out_specs=pl.BlockSpec((1, s, w), lambda i: (i, 0, 0)),
        out_shape=jax.ShapeDtypeStruct((b, s, w), BF16),
        scratch_shapes=[pltpu.VMEM((3, n_heads, qb, span), F32)],
        compiler_params=_params(1),
        name="attn_d",
    )(sink.astype(F32), q, k, v)


def _merge_kernel(*refs, dils, cw):
    n_pat = len(dils)
    o_refs, l_refs = refs[:n_pat], refs[n_pat:2 * n_pat]
    yd_ref, cat_ref = refs[2 * n_pat], refs[2 * n_pat + 1]
    scratch = list(refs[2 * n_pat + 2:])
    tm = cat_ref.shape[0]
    chunks = cw // LANES
    staged = []
    for p, dil in enumerate(dils):
        if dil == 1:
            staged.append(None)
            continue
        o_s, l_s = scratch.pop(0), scratch.pop(0)
        n = tm // dil
        for r in range(dil):
            for c in range(chunks):
                lo = r * cw + c * LANES
                o_s[c, pl.ds(r, n, stride=dil), :] = o_refs[p][:, lo:lo + LANES].astype(F32)
                l_s[c, pl.ds(r, n, stride=dil), :] = l_refs[p][:, lo:lo + LANES]
        staged.append((o_s, l_s))
    for c in range(chunks):
        cols = slice(c * LANES, (c + 1) * LANES)
        os_ = [o_refs[p][:, cols].astype(F32) if st is None else st[0][c] for p, st in enumerate(staged)]
        ls_ = [l_refs[p][:, cols] if st is None else st[1][c] for p, st in enumerate(staged)]
        m = functools.reduce(jnp.maximum, ls_)
        es = [jnp.exp(l - m) for l in ls_]
        num = functools.reduce(lambda a, b: a + b, [e * o for e, o in zip(es, os_)])
        den = functools.reduce(lambda a, b: a + b, es)
        cat_ref[:, cols] = (num / den).astype(BF16)
    cat_ref[:, cw:] = yd_ref[...]


def _merge(os_, lses, yd, dils):
    t, dw = yd.shape
    cw = os_[0].shape[1] // dils[0]
    vspecs = [pl.BlockSpec((ROW_TILE // dil, dil * cw), lambda i: (i, 0)) for dil in dils]
    n_scratch = 2 * sum(1 for dil in dils if dil != 1)
    return pl.pallas_call(
        functools.partial(_merge_kernel, dils=dils, cw=cw),
        grid=(t // ROW_TILE,),
        in_specs=vspecs + vspecs + [pl.BlockSpec((ROW_TILE, dw), lambda i: (i, 0))],
        out_specs=pl.BlockSpec((ROW_TILE, cw + dw), lambda i: (i, 0)),
        out_shape=jax.ShapeDtypeStruct((t, cw + dw), BF16),
        scratch_shapes=[pltpu.VMEM((cw // LANES, ROW_TILE, LANES), F32)] * n_scratch,
        compiler_params=_params(1),
        name="merge",
    )(*os_, *lses, yd)


def _layer_norm(z, g, b):
    mu = jnp.mean(z, axis=-1, keepdims=True)
    zc = z - mu
    var = jnp.mean(zc * zc, axis=-1, keepdims=True)
    return zc * lax.rsqrt(var + LN_EPS) * g + b


ROUTE_ROWS = 8


def _route_tile(logits, triu, count_ref):
    n_exp, tm = logits.shape
    sub = lax.broadcasted_iota(jnp.int32, (n_exp, tm), 0)
    out_row = lax.broadcasted_iota(jnp.int32, (ROUTE_ROWS, tm), 0)
    work = logits
    vals, onehots = [], []
    idx_out = jnp.zeros((ROUTE_ROWS, tm), jnp.int32)
    for k in range(TOP_K):
        m = jnp.max(work, axis=0, keepdims=True)
        idx = jnp.min(jnp.where(work == m, sub, n_exp), axis=0, keepdims=True)
        hot = sub == idx
        work = jnp.where(hot, -jnp.inf, work)
        vals.append(m)
        onehots.append(hot)
        idx_out = jnp.where(out_row == k, idx, idx_out)
    exps = [jnp.exp(v - vals[0]) for v in vals]
    den = functools.reduce(lambda a, b: a + b, exps)
    gate_out = jnp.zeros((ROUTE_ROWS, tm), F32)
    for k in range(TOP_K):
        gate_out = jnp.where(out_row == k, exps[k] / den, gate_out)
    multihot = functools.reduce(lambda a, b: a + b, [h.astype(F32) for h in onehots])
    before = _dot(multihot.astype(BF16), triu) + count_ref[...]
    rank_out = jnp.zeros((ROUTE_ROWS, tm), jnp.int32)
    for k in range(TOP_K):
        rank = jnp.sum(jnp.where(onehots[k], before, 0.0), axis=0, keepdims=True)
        rank_out = jnp.where(out_row == k, rank.astype(jnp.int32), rank_out)
    count_ref[...] += jnp.sum(multihot, axis=1, keepdims=True)
    return idx_out, gate_out, rank_out


def _out_ln_kernel(cat_ref, w_ref, x_ref, g_ref, b_ref, rw_hi_ref, rw_lo_ref, rb_ref, triu_ref,
                   x1_ref, x1p_ref, idx_ref, gate_ref, rank_ref, count_ref, *, alpha):
    @pl.when(pl.program_id(0) == 0)
    def _():
        count_ref[...] = jnp.zeros_like(count_ref)

    mix = _dot(cat_ref[...], w_ref[...])
    x1 = _layer_norm(alpha * x_ref[...] + mix, g_ref[...], b_ref[...])
    x1_ref[...] = x1
    x1p_ref[...] = _pack_pairs(x1)
    hi = x1.astype(BF16)
    lo = (x1 - hi.astype(F32)).astype(BF16)
    logits = (_dot_nt(rw_hi_ref[...], hi) + _dot_nt(rw_hi_ref[...], lo) + _dot_nt(rw_lo_ref[...], hi)
              + rb_ref[...])
    idx_ref[...], gate_ref[...], rank_ref[...] = _route_tile(logits, triu_ref[...], count_ref)


def _out_ln(cat, w_out, xt, g, b, router_w, router_b, alpha):
    t, d = xt.shape
    n_exp = router_w.shape[1]
    rw_t = router_w.T
    rw_hi = rw_t.astype(BF16)
    rw_lo = (rw_t - rw_hi.astype(F32)).astype(BF16)
    triu = jnp.triu(jnp.ones((ROW_TILE, ROW_TILE), BF16), 1)
    row = lambda wd: pl.BlockSpec((ROW_TILE, wd), lambda i: (i, 0))
    col = pl.BlockSpec((ROUTE_ROWS, ROW_TILE), lambda i: (0, i))
    full = lambda a: pl.BlockSpec(a.shape, lambda i: (0,) * a.ndim)
    args = (cat, w_out.astype(BF16), xt, g.reshape(1, d), b.reshape(1, d), rw_hi, rw_lo,
            router_b.reshape(n_exp, 1), triu)
    return pl.pallas_call(
        functools.partial(_out_ln_kernel, alpha=alpha),
        grid=(t // ROW_TILE,),
        in_specs=[row(cat.shape[1]), full(args[1]), row(d)] + [full(a) for a in args[3:]],
        out_specs=[row(d), row(d // 2), col, col, col, pl.BlockSpec((n_exp, 1), lambda i: (0, 0))],
        out_shape=[jax.ShapeDtypeStruct((t, d), F32), jax.ShapeDtypeStruct((t, d // 2), jnp.int32),
                   jax.ShapeDtypeStruct((ROUTE_ROWS, t), jnp.int32), jax.ShapeDtypeStruct((ROUTE_ROWS, t), F32),
                   jax.ShapeDtypeStruct((ROUTE_ROWS, t), jnp.int32), jax.ShapeDtypeStruct((n_exp, 1), F32)],
        compiler_params=_params(1),
        name="out_ln",
    )(*args)


def _experts_kernel(first_ref, blocks_ref, count_ref, x_hbm, wgu_ref, bgu_ref, wd_ref, bd_ref, perm_ref,
                    y_hbm, wgu_s, wd_s, x_buf, y_buf, x_sem, y_sem, zero_sem):
    e = pl.program_id(0)
    last = pl.num_programs(0) - 1
    n_blk = blocks_ref[e]
    first = first_ref[e]
    count = count_ref[e]
    used = first_ref[last] + blocks_ref[last]
    half = GU_BLOCK // 2

    def rows_of(g):
        return pl.ds(pl.multiple_of(g * MOE_BLOCK, MOE_BLOCK), MOE_BLOCK)

    def x_copy(g, slot):
        return pltpu.make_async_copy(x_hbm.at[rows_of(g)], x_buf.at[slot], x_sem.at[slot])

    def y_copy(g, slot):
        return pltpu.make_async_copy(y_buf.at[slot], y_hbm.at[rows_of(g)], y_sem.at[slot])

    def compute(g, slot):
        xb = _unpack_pairs(x_buf[slot])
        row = lax.broadcasted_iota(jnp.int32, xb.shape, 0)
        xb = jnp.where(row < count - (g - first) * MOE_BLOCK, xb, jnp.zeros_like(xb))
        h = _dot(xb, wgu_s[...]) + bgu_ref[...]
        acts = []
        for c in range(h.shape[1] // GU_BLOCK):
            glu = jnp.minimum(h[:, c * GU_BLOCK:c * GU_BLOCK + half], SWIGLU_LIMIT)
            lin = jnp.clip(h[:, c * GU_BLOCK + half:(c + 1) * GU_BLOCK], -SWIGLU_LIMIT, SWIGLU_LIMIT)
            acts.append((glu * jax.nn.sigmoid(SWIGLU_ALPHA * glu) * (lin + 1.0)).astype(BF16))
        act = jnp.concatenate(acts, axis=1)
        y_buf[slot] = _pack_pairs(_dot(act, wd_s[...]) + bd_ref[...])

    @pl.when((e == 0) & (used > 0))
    def _():
        x_copy(0, 0).start()

    @pl.when(n_blk > 0)
    def _():
        for c in range(wgu_ref.shape[1] // GU_BLOCK):
            cols = slice(c * GU_BLOCK, (c + 1) * GU_BLOCK)
            wgu_s[:, cols] = _dot(wgu_ref[:, cols].astype(BF16), perm_ref[...]).astype(BF16)
        wd_s[...] = wd_ref[...].astype(BF16)

    def step(g, carry):
        slot = g % 2
        x_copy(g, slot).wait()

        @pl.when(g + 1 < used)
        def _():
            x_copy(g + 1, 1 - slot).start()

        @pl.when(g >= 2)
        def _():
            y_copy(g - 2, slot).wait()

        compute(g, slot)
        y_copy(g, slot).start()
        return carry

    lax.fori_loop(first, first + n_blk, step, 0)

    @pl.when(e == last)
    def _():
        @pl.when(used >= 1)
        def _():
            y_copy(0, (used - 1) % 2).wait()

        @pl.when(used >= 2)
        def _():
            y_copy(0, used % 2).wait()

        total = y_hbm.shape[0] // MOE_BLOCK
        y_buf[0] = jnp.zeros(y_buf.shape[1:], y_buf.dtype)

        def tail_copy(g):
            return pltpu.make_async_copy(y_buf.at[0], y_hbm.at[rows_of(g)], zero_sem)

        def start_all(g, carry):
            tail_copy(g).start()
            return carry

        def wait_all(g, carry):
            tail_copy(g).wait()
            return carry

        lax.fori_loop(used, total, start_all, 0)
        lax.fori_loop(used, total, wait_all, 0)


def _experts(x_rows, first_blk, n_blk, counts, layer, w_gu, b_gu, w_down, b_down):
    rows, dp = x_rows.shape
    _, n_exp, d, de2 = w_gu.shape
    assert de2 % GU_BLOCK == 0 and dp * 2 == d and rows % MOE_BLOCK == 0
    half = GU_BLOCK // 2
    j = jnp.arange(GU_BLOCK)
    src = jnp.where(j < half, 2 * j, 2 * (j - half) + 1)
    perm = (jnp.arange(GU_BLOCK)[:, None] == src[None, :]).astype(BF16)
    wspec = lambda a: pl.BlockSpec((None, None) + a.shape[2:], lambda e, *_: (layer, e, 0, 0))
    grid_spec = pltpu.PrefetchScalarGridSpec(
        num_scalar_prefetch=3,
        grid=(n_exp,),
        in_specs=[pl.BlockSpec(memory_space=pl.ANY),
                  wspec(w_gu), wspec(b_gu), wspec(w_down), wspec(b_down),
                  pl.BlockSpec((GU_BLOCK, GU_BLOCK), lambda e, *_: (0, 0))],
        out_specs=pl.BlockSpec(memory_space=pl.ANY),
        scratch_shapes=[pltpu.VMEM(w_gu.shape[2:], BF16), pltpu.VMEM(w_down.shape[2:], BF16),
                        pltpu.VMEM((2, MOE_BLOCK, dp), jnp.int32), pltpu.VMEM((2, MOE_BLOCK, dp), jnp.int32),
                        pltpu.SemaphoreType.DMA((2,)), pltpu.SemaphoreType.DMA((2,)),
                        pltpu.SemaphoreType.DMA(())],
    )
    return pl.pallas_call(
        _experts_kernel,
        grid_spec=grid_spec,
        out_shape=jax.ShapeDtypeStruct((rows, dp), jnp.int32),
        compiler_params=_params(1),
        name="experts",
    )(first_blk, n_blk, counts, x_rows, w_gu, b_gu, w_down, b_down, perm)


def _combine_ln_kernel(x1_ref, yk_ref, gate_ref, g_ref, b_ref, x2_ref, *, alpha):
    gates = gate_ref[...]
    ffn = gates[:, 0:1] * _unpack_pairs(yk_ref[0]).astype(F32)
    for k in range(1, yk_ref.shape[0]):
        ffn = ffn + gates[:, k:k + 1] * _unpack_pairs(yk_ref[k]).astype(F32)
    x2_ref[...] = _layer_norm(alpha * x1_ref[...] + ffn, g_ref[...], b_ref[...])


def _combine_ln(x1, yk, gates, g, b, alpha):
    t, d = x1.shape
    k = yk.shape[0]
    tile = COMBINE_TILE if t % COMBINE_TILE == 0 else ROW_TILE
    return pl.pallas_call(
        functools.partial(_combine_ln_kernel, alpha=alpha),
        grid=(t // tile,),
        in_specs=[pl.BlockSpec((tile, d), lambda i: (i, 0)),
                  pl.BlockSpec((k, tile, d // 2), lambda i: (0, i, 0)),
                  pl.BlockSpec((tile, k), lambda i: (i, 0)),
                  pl.BlockSpec((1, d), lambda i: (0, 0)),
                  pl.BlockSpec((1, d), lambda i: (0, 0))],
        out_specs=pl.BlockSpec((tile, d), lambda i: (i, 0)),
        out_shape=jax.ShapeDtypeStruct((t, d), F32),
        compiler_params=_params(1),
        name="combine_ln",
    )(x1, yk, gates, g.reshape(1, d), b.reshape(1, d))


def _route(top_idx, rank, counts):
    n_exp = counts.shape[0]
    experts = jnp.arange(n_exp, dtype=jnp.int32)
    counts = counts.reshape(n_exp).astype(jnp.int32)
    padded = (counts + MOE_BLOCK - 1) // MOE_BLOCK * MOE_BLOCK
    pad_end = jnp.cumsum(padded)
    pad_start = pad_end - padded
    start = jnp.sum(jnp.where(top_idx[None] == experts[:, None, None], pad_start[:, None, None], 0), axis=0)
    pos = rank + start
    first_blk = (pad_start // MOE_BLOCK).astype(jnp.int32)
    n_blk = (padded // MOE_BLOCK).astype(jnp.int32)
    return pos, first_blk, n_blk, counts


def _moe(x1, x1p, top_idx, gates, rank, counts, layer, w_gu, b_gu, w_down, b_down, g, b, alpha):
    t, d = x1.shape
    n_exp = counts.shape[0]
    pos, first_blk, n_blk, counts = _route(top_idx[:TOP_K], rank[:TOP_K], counts)
    rows = (-(-(t * TOP_K) // MOE_BLOCK) + n_exp) * MOE_BLOCK
    pos3 = pos.reshape(TOP_K, t // SC_CHUNK, SC_CHUNK).transpose(1, 0, 2)
    x_rows = _sc_scatter_rows(x1p, pos3, rows)
    y = _experts(x_rows, first_blk, n_blk, counts, layer, w_gu, b_gu, w_down, b_down)
    yk = _sc_gather_rows(y, pos.reshape(-1)).reshape(TOP_K, t, d // 2)
    return _combine_ln(x1, yk, gates[:TOP_K].T, g, b, alpha)


def kernel(x, ev_w_in, ev_pool_w, ev_pool_scale, ev_conv_w, ev_w_out, od_w_in, od_sink, od_w_out,
           router_w, router_b, exp_w_gu, exp_b_gu, exp_w_down, exp_b_down, ln_g, ln_b):
    bsz, seq, d = x.shape
    t = bsz * seq
    depth = ln_g.shape[0]
    alpha = (2 * depth) ** 0.25
    n_exp = router_w.shape[2]
    c_width = d // 2
    dq_width = d // 2
    dkv_width = D_KV_HEADS * HEAD_DIM
    assert t % ROW_TILE == 0

    de2 = exp_b_gu.shape[-1]
    b_gu = exp_b_gu.reshape(depth, n_exp, de2 // GU_BLOCK, GU_BLOCK // 2, 2)
    b_gu = jnp.swapaxes(b_gu, -1, -2).reshape(depth, n_exp, 1, de2)
    b_down = exp_b_down.reshape(depth, n_exp, 1, -1)

    xt = x.reshape(t, d)
    for layer in range(depth):
        i = layer // 2
        if layer % 2 == 0:
            cat = _even_mix(xt.reshape(bsz, seq, d), ev_w_in[i], ev_pool_w[i], ev_pool_scale[i], ev_conv_w[i])
            cat = cat.reshape(t, -1)
            w_out = ev_w_out[i]
        else:
            qkv, (qd, kd, vd) = _odd_proj(xt, od_w_in[i], c_width, dq_width, dkv_width)
            dils = tuple(dil for _, dil in C_PATTERNS)
            os_, lses = [], []
            for (window, dil), (qc, kc, vc) in zip(C_PATTERNS, qkv):
                o, lse = _attn_c(qc, kc, vc, bsz, window, dil)
                os_.append(o)
                lses.append(lse)
            seq3 = lambda a: a.reshape(bsz, seq, a.shape[1])
            yd = _attn_d(seq3(qd), seq3(kd), seq3(vd), od_sink[i]).reshape(t, dq_width)
            cat = _merge(os_, lses, yd, dils)
            w_out = od_w_out[i]
        x1, x1p, top_idx, gates, rank, counts = _out_ln(cat, w_out, xt, ln_g[layer, 0], ln_b[layer, 0],
                                                        router_w[layer], router_b[layer], alpha)
        xt = _moe(x1, x1p, top_idx, gates, rank, counts, layer, exp_w_gu, b_gu, exp_w_down, b_down,
                  ln_g[layer, 1], ln_b[layer, 1], alpha)
    return xt.reshape(bsz, seq, d)
```

```python
import functools

import jax
import jax.numpy as jnp
from jax import lax
from jax.experimental import pallas as pl
from jax.experimental.pallas import tpu as pltpu
from jax.experimental.pallas import tpu_sc as plsc

HEAD_DIM = 64
POOL_WINDOWS = (2, 4, 8, 16)
C_PATTERNS = ((128, 1), (512, 4), (2048, 16))
D_KV_HEADS = 2
D_RADIUS = 128
TOP_K = 4
SWIGLU_LIMIT = 7.0
SWIGLU_ALPHA = 1.702
MOE_BLOCK = 512
LN_EPS = 1e-5
NEG_INF = -1e30
LOG2E = 1.4426950408889634

ROW_TILE = 512
N_STREAMS = 2
COMBINE_TILE = 1024
ATTN_Q_BLOCK = 128
ATTN_STEP_ROWS = 512
LANES = 128
GU_BLOCK = 256
VMEM_LIMIT_BYTES = 56 * 1024 * 1024

F32 = jnp.float32
BF16 = jnp.bfloat16


def _params(n_axes=1):
    return pltpu.CompilerParams(dimension_semantics=("arbitrary",) * n_axes,
                                vmem_limit_bytes=VMEM_LIMIT_BYTES)


def _dot(a, b):
    return jnp.dot(a, b, preferred_element_type=F32)


def _dot_nt(a, b):
    return lax.dot_general(a, b, (((1,), (1,)), ((), ())), preferred_element_type=F32)


def _pack_pairs(a):
    n = a.shape[1] // 2
    bits = lax.bitcast_convert_type(a.astype(BF16).astype(F32), jnp.int32)
    return bits[:, :n] | lax.shift_right_logical(bits[:, n:], 16)


def _unpack_pairs(w):
    hi = lax.bitcast_convert_type(w & jnp.int32(-65536), F32)
    lo = lax.bitcast_convert_type(lax.shift_left(w, 16), F32)
    return jnp.concatenate([hi, lo], axis=1).astype(BF16)


SC_CORES = 2
SC_SUBCORES = 16
SC_WORKERS = SC_CORES * SC_SUBCORES
SC_CHUNK = 64


def _sc_mesh():
    return plsc.VectorSubcoreMesh(core_axis_name="c", subcore_axis_name="s")


def _sc_worker():
    return lax.axis_index("s") * SC_CORES + lax.axis_index("c")


def _sc_gather_rows(table, idx):
    n, w = idx.shape[0], table.shape[1]
    per_w = n // SC_WORKERS
    assert n % SC_WORKERS == 0 and per_w % (2 * SC_CHUNK) == 0

    def body(table_hbm, idx_hbm, out_hbm, i0, i1, r0, r1, s0, s1):
        base = _sc_worker() * per_w

        def start(off, iv, rv, sem):
            pltpu.sync_copy(idx_hbm.at[pl.ds(off, SC_CHUNK)], iv)
            pltpu.async_copy(table_hbm.at[iv], rv, sem)

        def finish(off, iv, rv, sem):
            pltpu.make_async_copy(table_hbm.at[iv], rv, sem).wait()
            pltpu.sync_copy(rv, out_hbm.at[pl.ds(off, SC_CHUNK)])

        start(base, i0, r0, s0)

        @pl.loop(0, per_w, step=2 * SC_CHUNK)
        def _(o):
            off = base + o
            start(off + SC_CHUNK, i1, r1, s1)
            finish(off, i0, r0, s0)

            @pl.when(o + 2 * SC_CHUNK < per_w)
            def _():
                start(off + 2 * SC_CHUNK, i0, r0, s0)

            finish(off + SC_CHUNK, i1, r1, s1)

    return pl.kernel(
        body, mesh=_sc_mesh(),
        out_type=jax.ShapeDtypeStruct((n, w), table.dtype),
        scratch_types=[pltpu.VMEM((SC_CHUNK,), jnp.int32), pltpu.VMEM((SC_CHUNK,), jnp.int32),
                       pltpu.VMEM((SC_CHUNK, w), table.dtype), pltpu.VMEM((SC_CHUNK, w), table.dtype),
                       pltpu.SemaphoreType.DMA, pltpu.SemaphoreType.DMA],
    )(table, idx)


def _sc_scatter_rows(src, pos3, n_out):
    t, w = src.shape
    kk = pos3.shape[1]
    per_w = t // SC_WORKERS
    assert t % SC_WORKERS == 0 and per_w % SC_CHUNK == 0 and pos3.shape == (t // SC_CHUNK, kk, SC_CHUNK)

    def body(src_hbm, pos_hbm, out_hbm, iv, rv, sem):
        base = _sc_worker() * per_w

        @pl.loop(0, per_w, step=SC_CHUNK)
        def _(o):
            off = base + o
            pltpu.sync_copy(pos_hbm.at[off // SC_CHUNK], iv)
            pltpu.sync_copy(src_hbm.at[pl.ds(off, SC_CHUNK)], rv)
            copies = [pltpu.async_copy(rv, out_hbm.at[iv.at[j]], sem) for j in range(kk)]
            for cp in copies:
                cp.wait()

    return pl.kernel(
        body, mesh=_sc_mesh(),
        out_type=jax.ShapeDtypeStruct((n_out, w), src.dtype),
        scratch_types=[pltpu.VMEM((kk, SC_CHUNK), jnp.int32), pltpu.VMEM((SC_CHUNK, w), src.dtype),
                       pltpu.SemaphoreType.DMA],
    )(src, pos3)


def _shift_down(a, k, row):
    return jnp.where(row >= k, pltpu.roll(a, k, axis=0), 0.0)


def _shift_up(a, k, row):
    n = a.shape[0]
    return jnp.where(row < n - k, pltpu.roll(a, n - k, axis=0), 0.0)


def _even_mix_kernel(x_ref, w_in_ref, pool_w_ref, pool_scale_ref, conv_w_ref, cat_ref, xb_ref):
    s = x_ref.shape[1]
    pool_width = pool_scale_ref.shape[1]
    gd = pool_width // len(POOL_WINDOWS)
    conv_width = conv_w_ref.shape[1]
    xb_ref[...] = x_ref[0].astype(BF16)
    xb = xb_ref[...]

    row = lax.broadcasted_iota(jnp.int32, (s, gd), 0)
    for g, w in enumerate(POOL_WINDOWS):
        lo = g * gd
        if g % 2 == 0:
            u2 = _dot(xb, w_in_ref[:, lo:lo + 2 * gd])
        u = u2[:, (g % 2) * gd:(g % 2 + 1) * gd]
        half = w // 2
        back, fwd, span = u, u, 1
        while span < half:
            back = back + _shift_down(back, span, row)
            fwd = fwd + _shift_up(fwd, span, row)
            span *= 2
        win = _shift_down(back, 1, row) + fwd
        cnt = (jnp.minimum(row + (w - half), s) - jnp.maximum(row - half, 0)).astype(F32)
        pooled = win / cnt - u
        mixed = _dot(pooled.astype(BF16), pool_w_ref[g])
        cat_ref[0, :, lo:lo + gd] = (mixed * pool_scale_ref[:, lo:lo + gd]).astype(BF16)

    cw = 256
    rowc = lax.broadcasted_iota(jnp.int32, (s, cw), 0)
    for j in range(conv_width // cw):
        c0 = j * cw
        b_gate = _dot(xb, w_in_ref[:, pool_width + c0:pool_width + c0 + cw])
        c_gate = _dot(xb, w_in_ref[:, pool_width + conv_width + c0:pool_width + conv_width + c0 + cw])
        v = _dot(xb, w_in_ref[:, pool_width + 2 * conv_width + c0:pool_width + 2 * conv_width + c0 + cw])
        u = c_gate * v
        conv = (_shift_down(u, 1, rowc) * conv_w_ref[0:1, c0:c0 + cw] + u * conv_w_ref[1:2, c0:c0 + cw]
                + _shift_up(u, 1, rowc) * conv_w_ref[2:3, c0:c0 + cw])
        cat_ref[0, :, pool_width + c0:pool_width + c0 + cw] = (b_gate * conv).astype(BF16)


def _even_mix(x, w_in, pool_w, pool_scale, conv_w):
    b, s, d = x.shape
    pool_width = pool_scale.shape[0]
    conv_width = conv_w.shape[1]
    assert conv_width % 256 == 0 and w_in.shape[1] == pool_width + 3 * conv_width
    cat_width = pool_width + conv_width
    return pl.pallas_call(
        _even_mix_kernel,
        grid=(b,),
        in_specs=[
            pl.BlockSpec((1, s, d), lambda i: (i, 0, 0)),
            pl.BlockSpec(w_in.shape, lambda i: (0, 0)),
            pl.BlockSpec(pool_w.shape, lambda i: (0, 0, 0)),
            pl.BlockSpec((1, pool_width), lambda i: (0, 0)),
            pl.BlockSpec(conv_w.shape, lambda i: (0, 0)),
        ],
        out_specs=pl.BlockSpec((1, s, cat_width), lambda i: (i, 0, 0)),
        out_shape=jax.ShapeDtypeStruct((b, s, cat_width), BF16),
        scratch_shapes=[pltpu.VMEM((s, d), BF16)],
        compiler_params=_params(1),
        name="even_mix",
    )(x, w_in.astype(BF16), pool_w.astype(BF16), pool_scale.reshape(1, pool_width), conv_w)


def _odd_proj_kernel(x_ref, w_ref, *refs, c_width, dils):
    outs, h_ref = refs[:-1], refs[-1]
    xb = x_ref[...].astype(BF16)
    tm = x_ref.shape[0]
    chunks = c_width // LANES
    for j in range(3):
        h = _dot(xb, w_ref[:, j * c_width:(j + 1) * c_width])
        for c in range(chunks):
            h_ref[j, c] = h[:, c * LANES:(c + 1) * LANES]
        for pi, dil in enumerate(dils):
            n = tm // dil
            for r in range(dil):
                for c in range(chunks):
                    src = h_ref[j, c] if dil == 1 else h_ref[j, c, pl.ds(r, n, stride=dil), :]
                    lo = r * c_width + c * LANES
                    outs[3 * pi + j][:, lo:lo + LANES] = src.astype(BF16)
    c0 = 3 * c_width
    for ref in outs[3 * len(dils):]:
        wd = ref.shape[1]
        ref[...] = _dot(xb, w_ref[:, c0:c0 + wd]).astype(BF16)
        c0 += wd


def _odd_proj(xt, w_in, c_width, dq_width, dkv_width):
    t, d = xt.shape
    dils = tuple(dil for _, dil in C_PATTERNS)
    assert 3 * c_width + dq_width + 2 * dkv_width == w_in.shape[1] and c_width % LANES == 0
    col = jnp.arange(w_in.shape[1])
    is_q = (col < c_width) | ((col >= 3 * c_width) & (col < 3 * c_width + dq_width))
    w_in = w_in * jnp.where(is_q, HEAD_DIM ** -0.5 * LOG2E, 1.0)
    kv0 = 3 * c_width + dq_width
    twice = lambda w: jnp.repeat(w.reshape(d, -1, HEAD_DIM), LANES // HEAD_DIM, axis=1).reshape(d, -1)
    w_in = jnp.concatenate([w_in[:, :kv0], twice(w_in[:, kv0:kv0 + dkv_width]), twice(w_in[:, kv0 + dkv_width:])], axis=1)
    dkv_width = dkv_width * (LANES // HEAD_DIM)
    w_scaled = w_in.astype(BF16)
    shapes = [(t // dil, dil * c_width) for dil in dils for _ in range(3)]
    blocks = [(ROW_TILE // dil, dil * c_width) for dil in dils for _ in range(3)]
    for wd in (dq_width, dkv_width, dkv_width):
        shapes.append((t, wd))
        blocks.append((ROW_TILE, wd))
    outs = pl.pallas_call(
        functools.partial(_odd_proj_kernel, c_width=c_width, dils=dils),
        grid=(t // ROW_TILE,),
        in_specs=[pl.BlockSpec((ROW_TILE, d), lambda i: (i, 0)),
                  pl.BlockSpec(w_in.shape, lambda i: (0, 0))],
        out_specs=[pl.BlockSpec(blk, lambda i: (i, 0)) for blk in blocks],
        out_shape=[jax.ShapeDtypeStruct(shp, BF16) for shp in shapes],
        scratch_shapes=[pltpu.VMEM((3, c_width // LANES, ROW_TILE, LANES), F32)],
        compiler_params=_params(1),
        name="odd_proj",
    )(xt, w_scaled)
    qkv = [outs[3 * pi:3 * pi + 3] for pi in range(len(dils))]
    return qkv, outs[3 * len(dils):]


def _band_shape(length, radius):
    qb = min(ATTN_Q_BLOCK, length)
    span = min(length, qb + 2 * radius)
    n_blk = length // qb
    assert length % qb == 0 and (n_blk <= 2 or radius <= qb)
    return qb, span, n_blk, (0, -radius, qb - span)


def _fill_band_bias(bias_ref, length, radius, dist_unit, n_heads):
    qb, span, _, offsets = _band_shape(length, radius)
    rel = lax.broadcasted_iota(jnp.int32, (qb, span), 1) - lax.broadcasted_iota(jnp.int32, (qb, span), 0)
    for v, off in enumerate(offsets):
        dist = jnp.abs(rel + off)
        mask = jnp.where(dist <= radius, 0.0, NEG_INF).astype(F32)
        far = dist.astype(F32) * (dist_unit * LOG2E)
        for hh in range(n_heads):
            bias_ref[v, hh] = mask - 2.0 ** (-8.0 * (hh + 1) / n_heads) * far


def _band_block(i, length, radius):
    qb, span, n_blk, offsets = _band_shape(length, radius)
    q0 = pl.multiple_of(i * qb, qb)
    variant = jnp.where(i == 0, 0, jnp.where(i == n_blk - 1, 2, 1))
    off = jnp.where(i == 0, offsets[0], jnp.where(i == n_blk - 1, offsets[2], offsets[1]))
    align = 8
    for cand in (128, 64, 32, 16):
        if all(o % cand == 0 for o in offsets) and qb % cand == 0:
            align = cand
            break
    return q0, pl.multiple_of(q0 + off, align), variant


def _attend_pairs(q_slabs, k_slabs, v_slabs, biases):
    lower = lax.broadcasted_iota(jnp.int32, (1, LANES), 1) < HEAD_DIM
    scores = []
    for j, (q2, k2) in enumerate(zip(q_slabs, k_slabs)):
        zero = jnp.zeros_like(q2)
        scores.append(_dot_nt(jnp.where(lower, q2, zero), k2) + biases[2 * j])
        scores.append(_dot_nt(jnp.where(lower, zero, q2), k2) + biases[2 * j + 1])
    probs = []
    for s in scores:
        m = jnp.max(s, axis=-1, keepdims=True)
        probs.append((jnp.exp2(s - m).astype(BF16), m))
    ones = jnp.ones(v_slabs[0].shape, BF16)
    outs = []
    for j, v2 in enumerate(v_slabs):
        (p_lo, m_lo), (p_up, m_up) = probs[2 * j], probs[2 * j + 1]
        v_ones = jnp.concatenate([v2, ones], axis=1)
        r_lo, r_up = _dot(p_lo, v_ones), _dot(p_up, v_ones)
        l = jnp.where(lower, r_lo[:, LANES:], r_up[:, LANES:])
        o = jnp.where(lower, r_lo[:, :LANES], r_up[:, :LANES]) / l
        outs.append((o, (jnp.where(lower, m_lo, m_up) + jnp.log2(l)) * (1.0 / LOG2E)))
    return outs


def _attn_c_kernel(q_ref, k_ref, v_ref, o_ref, lse_ref, bias_ref, *, dil, radius, n_heads):
    length = q_ref.shape[1]
    qb, span, n_blk, _ = _band_shape(length, radius)
    width = n_heads * HEAD_DIM

    @pl.when((pl.program_id(0) == 0) & (pl.program_id(1) == 0))
    def _():
        _fill_band_bias(bias_ref, length, radius, float(dil), n_heads)

    def block(i, carry):
        q0, start, variant = _band_block(i, length, radius)
        biases = [bias_ref[variant, hh] for hh in range(n_heads)]
        for res in range(q_ref.shape[2] // width):
            slabs = [res * width + j * LANES for j in range(width // LANES)]
            outs = _attend_pairs([q_ref[0, pl.ds(q0, qb), lo:lo + LANES] for lo in slabs],
                                 [k_ref[0, pl.ds(start, span), lo:lo + LANES] for lo in slabs],
                                 [v_ref[0, pl.ds(start, span), lo:lo + LANES] for lo in slabs], biases)
            for lo, (o, lse) in zip(slabs, outs):
                o_ref[0, pl.ds(q0, qb), lo:lo + LANES] = o.astype(o_ref.dtype)
                lse_ref[0, pl.ds(q0, qb), lo:lo + LANES] = lse
        return carry

    lax.fori_loop(0, n_blk, block, 0)


def _attn_c(q, k, v, bsz, window, dil):
    rows, dw = q.shape
    w = dw // dil
    length = rows // bsz
    radius = window // 2 // dil
    qb, span, _, _ = _band_shape(length, radius)
    per_step = max(1, min(dil, ATTN_STEP_ROWS // length))
    assert dil % per_step == 0
    view = lambda a: a.reshape(bsz, length, dw)
    spec = pl.BlockSpec((1, length, per_step * w), lambda i, r: (i, 0, r))
    o, lse = pl.pallas_call(
        functools.partial(_attn_c_kernel, dil=dil, radius=radius, n_heads=w // HEAD_DIM),
        grid=(bsz, dil // per_step),
        in_specs=[spec, spec, spec],
        out_specs=[spec, spec],
        out_shape=[jax.ShapeDtypeStruct((bsz, length, dw), BF16),
                   jax.ShapeDtypeStruct((bsz, length, dw), F32)],
        scratch_shapes=[pltpu.VMEM((3, w // HEAD_DIM, qb, span), F32)],
        compiler_params=_params(2),
        name=f"attn_c_d{dil}",
    )(view(q), view(k), view(v))
    return o.reshape(rows, dw), lse.reshape(rows, dw)


def _attn_d_kernel(sink_ref, q_ref, k_ref, v_ref, y_ref, bias_ref, *, radius, n_heads, group):
    length = q_ref.shape[1]
    qb, span, n_blk, _ = _band_shape(length, radius)
    pairs = n_heads * HEAD_DIM // LANES
    lower = lax.broadcasted_iota(jnp.int32, (1, LANES), 1) < HEAD_DIM

    @pl.when(pl.program_id(0) == 0)
    def _():
        _fill_band_bias(bias_ref, length, radius, 1.0, n_heads)

    def block(i, carry):
        q0, start, variant = _band_block(i, length, radius)
        biases = [bias_ref[variant, hh] for hh in range(n_heads)]
        kv_lo = [(2 * j // group) * LANES for j in range(pairs)]
        outs = _attend_pairs([q_ref[0, pl.ds(q0, qb), j * LANES:(j + 1) * LANES] for j in range(pairs)],
                             [k_ref[0, pl.ds(start, span), lo:lo + LANES] for lo in kv_lo],
                             [v_ref[0, pl.ds(start, span), lo:lo + LANES] for lo in kv_lo], biases)
        for j, (o, lse) in enumerate(outs):
            sink = jnp.where(lower, sink_ref[2 * j], sink_ref[2 * j + 1])
            y_ref[0, pl.ds(q0, qb), j * LANES:(j + 1) * LANES] = (o * jax.nn.sigmoid(lse - sink)).astype(y_ref.dtype)
        return carry

    lax.fori_loop(0, n_blk, block, 0)


def _attn_d(q, k, v, sink):
    b, s, w = q.shape
    kvw = k.shape[2]
    n_heads = w // HEAD_DIM
    group = n_heads // D_KV_HEADS
    assert kvw == D_KV_HEADS * LANES and group % 2 == 0
    qb, span, _, _ = _band_shape(s, D_RADIUS)
    return pl.pallas_call(
        functools.partial(_attn_d_kernel, radius=D_RADIUS, n_heads=n_heads, group=group),
        grid=(b,),
        in_specs=[pl.BlockSpec(memory_space=pltpu.SMEM),
                  pl.BlockSpec((1, s, w), lambda i: (i, 0, 0)),
                  pl.BlockSpec((1, s, kvw), lambda i: (i, 0, 0)),
                  pl.BlockSpec((1, s, kvw), lambda i: (i, 0, 0))],
        out_specs=pl.BlockSpec((1, s, w), lambda i: (i, 0, 0)),
        out_shape=jax.ShapeDtypeStruct((b, s, w), BF16),
        scratch_shapes=[pltpu.VMEM((3, n_heads, qb, span), F32)],
        compiler_params=_params(1),
        name="attn_d",
    )(sink.astype(F32), q, k, v)


def _merge_kernel(*refs, dils, cw):
    n_pat = len(dils)
    o_refs, l_refs = refs[:n_pat], refs[n_pat:2 * n_pat]
    yd_ref, cat_ref = refs[2 * n_pat], refs[2 * n_pat + 1]
    scratch = list(refs[2 * n_pat + 2:])
    tm = cat_ref.shape[0]
    chunks = cw // LANES
    staged = []
    for p, dil in enumerate(dils):
        if dil == 1:
            staged.append(None)
            continue
        o_s, l_s = scratch.pop(0), scratch.pop(0)
        n = tm // dil
        for r in range(dil):
            for c in range(chunks):
                lo = r * cw + c * LANES
                o_s[c, pl.ds(r, n, stride=dil), :] = o_refs[p][:, lo:lo + LANES].astype(F32)
                l_s[c, pl.ds(r, n, stride=dil), :] = l_refs[p][:, lo:lo + LANES]
        staged.append((o_s, l_s))
    for c in range(chunks):
        cols = slice(c * LANES, (c + 1) * LANES)
        os_ = [o_refs[p][:, cols].astype(F32) if st is None else st[0][c] for p, st in enumerate(staged)]
        ls_ = [l_refs[p][:, cols] if st is None else st[1][c] for p, st in enumerate(staged)]
        m = functools.reduce(jnp.maximum, ls_)
        es = [jnp.exp(l - m) for l in ls_]
        num = functools.reduce(lambda a, b: a + b, [e * o for e, o in zip(es, os_)])
        den = functools.reduce(lambda a, b: a + b, es)
        cat_ref[:, cols] = (num / den).astype(BF16)
    cat_ref[:, cw:] = yd_ref[...]


def _merge(os_, lses, yd, dils):
    t, dw = yd.shape
    cw = os_[0].shape[1] // dils[0]
    vspecs = [pl.BlockSpec((ROW_TILE // dil, dil * cw), lambda i: (i, 0)) for dil in dils]
    n_scratch = 2 * sum(1 for dil in dils if dil != 1)
    return pl.pallas_call(
        functools.partial(_merge_kernel, dils=dils, cw=cw),
        grid=(t // ROW_TILE,),
        in_specs=vspecs + vspecs + [pl.BlockSpec((ROW_TILE, dw), lambda i: (i, 0))],
        out_specs=pl.BlockSpec((ROW_TILE, cw + dw), lambda i: (i, 0)),
        out_shape=jax.ShapeDtypeStruct((t, cw + dw), BF16),
        scratch_shapes=[pltpu.VMEM((cw // LANES, ROW_TILE, LANES), F32)] * n_scratch,
        compiler_params=_params(1),
        name="merge",
    )(*os_, *lses, yd)


def _layer_norm(z, g, b):
    mu = jnp.mean(z, axis=-1, keepdims=True)
    zc = z - mu
    var = jnp.mean(zc * zc, axis=-1, keepdims=True)
    return zc * lax.rsqrt(var + LN_EPS) * g + b


ROUTE_ROWS = 8


def _route_tile(logits, triu, count_ref):
    n_exp, tm = logits.shape
    sub = lax.broadcasted_iota(jnp.int32, (n_exp, tm), 0)
    out_row = lax.broadcasted_iota(jnp.int32, (ROUTE_ROWS, tm), 0)
    work = logits
    vals, onehots = [], []
    idx_out = jnp.zeros((ROUTE_ROWS, tm), jnp.int32)
    for k in range(TOP_K):
        m = jnp.max(work, axis=0, keepdims=True)
        idx = jnp.min(jnp.where(work == m, sub, n_exp), axis=0, keepdims=True)
        hot = sub == idx
        work = jnp.where(hot, -jnp.inf, work)
        vals.append(m)
        onehots.append(hot)
        idx_out = jnp.where(out_row == k, idx, idx_out)
    exps = [jnp.exp(v - vals[0]) for v in vals]
    den = functools.reduce(lambda a, b: a + b, exps)
    gate_out = jnp.zeros((ROUTE_ROWS, tm), F32)
    for k in range(TOP_K):
        gate_out = jnp.where(out_row == k, exps[k] / den, gate_out)
    multihot = functools.reduce(lambda a, b: a + b, [h.astype(F32) for h in onehots])
    before = _dot(multihot.astype(BF16), triu) + count_ref[...]
    rank_out = jnp.zeros((ROUTE_ROWS, tm), jnp.int32)
    for k in range(TOP_K):
        rank = jnp.sum(jnp.where(onehots[k], before, 0.0), axis=0, keepdims=True)
        rank_out = jnp.where(out_row == k, rank.astype(jnp.int32), rank_out)
    count_ref[...] += jnp.sum(multihot, axis=1, keepdims=True)
    return idx_out, gate_out, rank_out


def _out_ln_kernel(cat_ref, w_ref, x_ref, g_ref, b_ref, rw_hi_ref, rw_lo_ref, rb_ref, triu_ref,
                   x1_ref, x1p_ref, idx_ref, gate_ref, rank_ref, count_ref, *, alpha):
    @pl.when(pl.program_id(0) == 0)
    def _():
        count_ref[...] = jnp.zeros_like(count_ref)

    mix = _dot(cat_ref[...], w_ref[...])
    x1 = _layer_norm(alpha * x_ref[...] + mix, g_ref[...], b_ref[...])
    x1_ref[...] = x1
    x1p_ref[...] = _pack_pairs(x1)
    hi = x1.astype(BF16)
    lo = (x1 - hi.astype(F32)).astype(BF16)
    logits = (_dot_nt(rw_hi_ref[...], hi) + _dot_nt(rw_hi_ref[...], lo) + _dot_nt(rw_lo_ref[...], hi)
              + rb_ref[...])
    idx_ref[...], gate_ref[...], rank_ref[...] = _route_tile(logits, triu_ref[...], count_ref)


def _out_ln(cat, w_out, xt, g, b, router_w, router_b, alpha):
    t, d = xt.shape
    n_exp = router_w.shape[1]
    rw_t = router_w.T
    rw_hi = rw_t.astype(BF16)
    rw_lo = (rw_t - rw_hi.astype(F32)).astype(BF16)
    triu = jnp.triu(jnp.ones((ROW_TILE, ROW_TILE), BF16), 1)
    row = lambda wd: pl.BlockSpec((ROW_TILE, wd), lambda i: (i, 0))
    col = pl.BlockSpec((ROUTE_ROWS, ROW_TILE), lambda i: (0, i))
    full = lambda a: pl.BlockSpec(a.shape, lambda i: (0,) * a.ndim)
    args = (cat, w_out.astype(BF16), xt, g.reshape(1, d), b.reshape(1, d), rw_hi, rw_lo,
            router_b.reshape(n_exp, 1), triu)
    return pl.pallas_call(
        functools.partial(_out_ln_kernel, alpha=alpha),
        grid=(t // ROW_TILE,),
        in_specs=[row(cat.shape[1]), full(args[1]), row(d)] + [full(a) for a in args[3:]],
        out_specs=[row(d), row(d // 2), col, col, col, pl.BlockSpec((n_exp, 1), lambda i: (0, 0))],
        out_shape=[jax.ShapeDtypeStruct((t, d), F32), jax.ShapeDtypeStruct((t, d // 2), jnp.int32),
                   jax.ShapeDtypeStruct((ROUTE_ROWS, t), jnp.int32), jax.ShapeDtypeStruct((ROUTE_ROWS, t), F32),
                   jax.ShapeDtypeStruct((ROUTE_ROWS, t), jnp.int32), jax.ShapeDtypeStruct((n_exp, 1), F32)],
        compiler_params=_params(1),
        name="out_ln",
    )(*args)


def _experts_kernel(first_ref, blocks_ref, count_ref, x_hbm, wgu_ref, bgu_ref, wd_ref, bd_ref, perm_ref,
                    y_hbm, wgu_s, wd_s, x_buf, y_buf, x_sem, y_sem, zero_sem):
    e = pl.program_id(0)
    last = pl.num_programs(0) - 1
    n_blk = blocks_ref[e]
    first = first_ref[e]
    count = count_ref[e]
    used = first_ref[last] + blocks_ref[last]
    half = GU_BLOCK // 2

    def rows_of(g):
        return pl.ds(pl.multiple_of(g * MOE_BLOCK, MOE_BLOCK), MOE_BLOCK)

    def x_copy(g, slot):
        return pltpu.make_async_copy(x_hbm.at[rows_of(g)], x_buf.at[slot], x_sem.at[slot])

    def y_copy(g, slot):
        return pltpu.make_async_copy(y_buf.at[slot], y_hbm.at[rows_of(g)], y_sem.at[slot])

    def compute(g, slot):
        xb = _unpack_pairs(x_buf[slot])
        row = lax.broadcasted_iota(jnp.int32, xb.shape, 0)
        xb = jnp.where(row < count - (g - first) * MOE_BLOCK, xb, jnp.zeros_like(xb))
        h = _dot(xb, wgu_s[...]) + bgu_ref[...]
        acts = []
        for c in range(h.shape[1] // GU_BLOCK):
            glu = jnp.minimum(h[:, c * GU_BLOCK:c * GU_BLOCK + half], SWIGLU_LIMIT)
            lin = jnp.clip(h[:, c * GU_BLOCK + half:(c + 1) * GU_BLOCK], -SWIGLU_LIMIT, SWIGLU_LIMIT)
            acts.append((glu * jax.nn.sigmoid(SWIGLU_ALPHA * glu) * (lin + 1.0)).astype(BF16))
        act = jnp.concatenate(acts, axis=1)
        y_buf[slot] = _pack_pairs(_dot(act, wd_s[...]) + bd_ref[...])

    @pl.when((e == 0) & (used > 0))
    def _():
        x_copy(0, 0).start()

    @pl.when(n_blk > 0)
    def _():
        for c in range(wgu_ref.shape[1] // GU_BLOCK):
            cols = slice(c * GU_BLOCK, (c + 1) * GU_BLOCK)
            wgu_s[:, cols] = _dot(wgu_ref[:, cols].astype(BF16), perm_ref[...]).astype(BF16)
        wd_s[...] = wd_ref[...].astype(BF16)

    def step(g, carry):
        slot = g % 2
        x_copy(g, slot).wait()

        @pl.when(g + 1 < used)
        def _():
            x_copy(g + 1, 1 - slot).start()

        @pl.when(g >= 2)
        def _():
            y_copy(g - 2, slot).wait()

        compute(g, slot)
        y_copy(g, slot).start()
        return carry

    lax.fori_loop(first, first + n_blk, step, 0)

    @pl.when(e == last)
    def _():
        @pl.when(used >= 1)
        def _():
            y_copy(0, (used - 1) % 2).wait()

        @pl.when(used >= 2)
        def _():
            y_copy(0, used % 2).wait()

        total = y_hbm.shape[0] // MOE_BLOCK
        y_buf[0] = jnp.zeros(y_buf.shape[1:], y_buf.dtype)

        def tail_copy(g):
            return pltpu.make_async_copy(y_buf.at[0], y_hbm.at[rows_of(g)], zero_sem)

        def start_all(g, carry):
            tail_copy(g).start()
            return carry

        def wait_all(g, carry):
            tail_copy(g).wait()
            return carry

        lax.fori_loop(used, total, start_all, 0)
        lax.fori_loop(used, total, wait_all, 0)


def _experts(x_rows, first_blk, n_blk, counts, layer, w_gu, b_gu, w_down, b_down):
    rows, dp = x_rows.shape
    _, n_exp, d, de2 = w_gu.shape
    assert de2 % GU_BLOCK == 0 and dp * 2 == d and rows % MOE_BLOCK == 0
    half = GU_BLOCK // 2
    j = jnp.arange(GU_BLOCK)
    src = jnp.where(j < half, 2 * j, 2 * (j - half) + 1)
    perm = (jnp.arange(GU_BLOCK)[:, None] == src[None, :]).astype(BF16)
    wspec = lambda a: pl.BlockSpec((None, None) + a.shape[2:], lambda e, *_: (layer, e, 0, 0))
    grid_spec = pltpu.PrefetchScalarGridSpec(
        num_scalar_prefetch=3,
        grid=(n_exp,),
        in_specs=[pl.BlockSpec(memory_space=pl.ANY),
                  wspec(w_gu), wspec(b_gu), wspec(w_down), wspec(b_down),
                  pl.BlockSpec((GU_BLOCK, GU_BLOCK), lambda e, *_: (0, 0))],
        out_specs=pl.BlockSpec(memory_space=pl.ANY),
        scratch_shapes=[pltpu.VMEM(w_gu.shape[2:], BF16), pltpu.VMEM(w_down.shape[2:], BF16),
                        pltpu.VMEM((2, MOE_BLOCK, dp), jnp.int32), pltpu.VMEM((2, MOE_BLOCK, dp), jnp.int32),
                        pltpu.SemaphoreType.DMA((2,)), pltpu.SemaphoreType.DMA((2,)),
                        pltpu.SemaphoreType.DMA(())],
    )
    return pl.pallas_call(
        _experts_kernel,
        grid_spec=grid_spec,
        out_shape=jax.ShapeDtypeStruct((rows, dp), jnp.int32),
        compiler_params=_params(1),
        name="experts",
    )(first_blk, n_blk, counts, x_rows, w_gu, b_gu, w_down, b_down, perm)


def _combine_ln_kernel(x1_ref, yk_ref, gate_ref, g_ref, b_ref, x2_ref, *, alpha):
    gates = gate_ref[...]
    ffn = gates[:, 0:1] * _unpack_pairs(yk_ref[0]).astype(F32)
    for k in range(1, yk_ref.shape[0]):
        ffn = ffn + gates[:, k:k + 1] * _unpack_pairs(yk_ref[k]).astype(F32)
    x2_ref[...] = _layer_norm(alpha * x1_ref[...] + ffn, g_ref[...], b_ref[...])


def _combine_ln(x1, yk, gates, g, b, alpha):
    t, d = x1.shape
    k = yk.shape[0]
    tile = COMBINE_TILE if t % COMBINE_TILE == 0 else ROW_TILE
    return pl.pallas_call(
        functools.partial(_combine_ln_kernel, alpha=alpha),
        grid=(t // tile,),
        in_specs=[pl.BlockSpec((tile, d), lambda i: (i, 0)),
                  pl.BlockSpec((k, tile, d // 2), lambda i: (0, i, 0)),
                  pl.BlockSpec((tile, k), lambda i: (i, 0)),
                  pl.BlockSpec((1, d), lambda i: (0, 0)),
                  pl.BlockSpec((1, d), lambda i: (0, 0))],
        out_specs=pl.BlockSpec((tile, d), lambda i: (i, 0)),
        out_shape=jax.ShapeDtypeStruct((t, d), F32),
        compiler_params=_params(1),
        name="combine_ln",
    )(x1, yk, gates, g.reshape(1, d), b.reshape(1, d))


def _route(top_idx, rank, counts):
    n_exp = counts.shape[0]
    experts = jnp.arange(n_exp, dtype=jnp.int32)
    counts = counts.reshape(n_exp).astype(jnp.int32)
    padded = (counts + MOE_BLOCK - 1) // MOE_BLOCK * MOE_BLOCK
    pad_end = jnp.cumsum(padded)
    pad_start = pad_end - padded
    start = jnp.sum(jnp.where(top_idx[None] == experts[:, None, None], pad_start[:, None, None], 0), axis=0)
    pos = rank + start
    first_blk = (pad_start // MOE_BLOCK).astype(jnp.int32)
    n_blk = (padded // MOE_BLOCK).astype(jnp.int32)
    return pos, first_blk, n_blk, counts


def _moe(x1, x1p, top_idx, gates, rank, counts, layer, w_gu, b_gu, w_down, b_down, g, b, alpha):
    t, d = x1.shape
    n_exp = counts.shape[0]
    pos, first_blk, n_blk, counts = _route(top_idx[:TOP_K], rank[:TOP_K], counts)
    rows = (-(-(t * TOP_K) // MOE_BLOCK) + n_exp) * MOE_BLOCK
    pos3 = pos.reshape(TOP_K, t // SC_CHUNK, SC_CHUNK).transpose(1, 0, 2)
    x_rows = _sc_scatter_rows(x1p, pos3, rows)
    y = _experts(x_rows, first_blk, n_blk, counts, layer, w_gu, b_gu, w_down, b_down)
    yk = _sc_gather_rows(y, pos.reshape(-1)).reshape(TOP_K, t, d // 2)
    return _combine_ln(x1, yk, gates[:TOP_K].T, g, b, alpha)


def kernel(x, ev_w_in, ev_pool_w, ev_pool_scale, ev_conv_w, ev_w_out, od_w_in, od_sink, od_w_out,
           router_w, router_b, exp_w_gu, exp_b_gu, exp_w_down, exp_b_down, ln_g, ln_b):
    bsz, seq, d = x.shape
    t = bsz * seq
    depth = ln_g.shape[0]
    alpha = (2 * depth) ** 0.25
    n_exp = router_w.shape[2]
    c_width = d // 2
    dq_width = d // 2
    dkv_width = D_KV_HEADS * HEAD_DIM

    de2 = exp_b_gu.shape[-1]
    b_gu = exp_b_gu.reshape(depth, n_exp, de2 // GU_BLOCK, GU_BLOCK // 2, 2)
    b_gu = jnp.swapaxes(b_gu, -1, -2).reshape(depth, n_exp, 1, de2)
    b_down = exp_b_down.reshape(depth, n_exp, 1, -1)

    def run_layer(xt, nb, layer):
        i = layer // 2
        rows = nb * seq
        if layer % 2 == 0:
            cat = _even_mix(xt.reshape(nb, seq, d), ev_w_in[i], ev_pool_w[i], ev_pool_scale[i], ev_conv_w[i])
            cat = cat.reshape(rows, -1)
            w_out = ev_w_out[i]
        else:
            qkv, (qd, kd, vd) = _odd_proj(xt, od_w_in[i], c_width, dq_width, dkv_width)
            dils = tuple(dil for _, dil in C_PATTERNS)
            os_, lses = [], []
            for (window, dil), (qc, kc, vc) in zip(C_PATTERNS, qkv):
                o, lse = _attn_c(qc, kc, vc, nb, window, dil)
                os_.append(o)
                lses.append(lse)
            seq3 = lambda a: a.reshape(nb, seq, a.shape[1])
            yd = _attn_d(seq3(qd), seq3(kd), seq3(vd), od_sink[i]).reshape(rows, dq_width)
            cat = _merge(os_, lses, yd, dils)
            w_out = od_w_out[i]
        x1, x1p, top_idx, gates, rank, counts = _out_ln(cat, w_out, xt, ln_g[layer, 0], ln_b[layer, 0],
                                                        router_w[layer], router_b[layer], alpha)
        return _moe(x1, x1p, top_idx, gates, rank, counts, layer, exp_w_gu, b_gu, exp_w_down, b_down,
                    ln_g[layer, 1], ln_b[layer, 1], alpha)

    n_streams = N_STREAMS if bsz % N_STREAMS == 0 else 1
    nb = bsz // n_streams
    streams = [x[j * nb:(j + 1) * nb].reshape(nb * seq, d) for j in range(n_streams)]
    for layer in range(depth):
        streams = [run_layer(xt, nb, layer) for xt in streams]
    return jnp.concatenate(streams, axis=0).reshape(bsz, seq, d)
```

```python
import functools

import jax
import jax.numpy as jnp
from jax import lax
from jax.experimental import pallas as pl
from jax.experimental.pallas import tpu as pltpu
from jax.experimental.pallas import tpu_sc as plsc

HEAD_DIM = 64
POOL_WINDOWS = (2, 4, 8, 16)
C_PATTERNS = ((128, 1), (512, 4), (2048, 16))
D_KV_HEADS = 2
D_RADIUS = 128
TOP_K = 4
SWIGLU_LIMIT = 7.0
SWIGLU_ALPHA = 1.702
MOE_BLOCK = 512
LN_EPS = 1e-5
NEG_INF = -1e30
LOG2E = 1.4426950408889634

ROW_TILE = 512
COMBINE_TILE = 1024
ATTN_Q_BLOCK = 128
ATTN_STEP_ROWS = 512
LANES = 128
GU_BLOCK = 256
VMEM_LIMIT_BYTES = 56 * 1024 * 1024

F32 = jnp.float32
BF16 = jnp.bfloat16


def _params(n_axes=1):
    return pltpu.CompilerParams(dimension_semantics=("arbitrary",) * n_axes,
                                vmem_limit_bytes=VMEM_LIMIT_BYTES)


def _dot(a, b):
    return jnp.dot(a, b, preferred_element_type=F32)


def _dot_nt(a, b):
    return lax.dot_general(a, b, (((1,), (1,)), ((), ())), preferred_element_type=F32)


def _pack_pairs(a):
    n = a.shape[1] // 2
    bits = lax.bitcast_convert_type(a.astype(BF16).astype(F32), jnp.int32)
    return bits[:, :n] | lax.shift_right_logical(bits[:, n:], 16)


def _unpack_pairs(w):
    hi = lax.bitcast_convert_type(w & jnp.int32(-65536), F32)
    lo = lax.bitcast_convert_type(lax.shift_left(w, 16), F32)
    return jnp.concatenate([hi, lo], axis=1).astype(BF16)


SC_CORES = 2
SC_SUBCORES = 16
SC_WORKERS = SC_CORES * SC_SUBCORES
SC_CHUNK = 64


def _sc_mesh():
    return plsc.VectorSubcoreMesh(core_axis_name="c", subcore_axis_name="s")


def _sc_worker():
    return lax.axis_index("s") * SC_CORES + lax.axis_index("c")


def _sc_gather_rows(table, idx):
    n, w = idx.shape[0], table.shape[1]
    per_w = n // SC_WORKERS
    assert n % SC_WORKERS == 0 and per_w % (2 * SC_CHUNK) == 0

    def body(table_hbm, idx_hbm, out_hbm, i0, i1, r0, r1, s0, s1):
        base = _sc_worker() * per_w

        def start(off, iv, rv, sem):
            pltpu.sync_copy(idx_hbm.at[pl.ds(off, SC_CHUNK)], iv)
            pltpu.async_copy(table_hbm.at[iv], rv, sem)

        def finish(off, iv, rv, sem):
            pltpu.make_async_copy(table_hbm.at[iv], rv, sem).wait()
            pltpu.sync_copy(rv, out_hbm.at[pl.ds(off, SC_CHUNK)])

        start(base, i0, r0, s0)

        @pl.loop(0, per_w, step=2 * SC_CHUNK)
        def _(o):
            off = base + o
            start(off + SC_CHUNK, i1, r1, s1)
            finish(off, i0, r0, s0)

            @pl.when(o + 2 * SC_CHUNK < per_w)
            def _():
                start(off + 2 * SC_CHUNK, i0, r0, s0)

            finish(off + SC_CHUNK, i1, r1, s1)

    return pl.kernel(
        body, mesh=_sc_mesh(),
        out_type=jax.ShapeDtypeStruct((n, w), table.dtype),
        scratch_types=[pltpu.VMEM((SC_CHUNK,), jnp.int32), pltpu.VMEM((SC_CHUNK,), jnp.int32),
                       pltpu.VMEM((SC_CHUNK, w), table.dtype), pltpu.VMEM((SC_CHUNK, w), table.dtype),
                       pltpu.SemaphoreType.DMA, pltpu.SemaphoreType.DMA],
    )(table, idx)


def _sc_scatter_rows(src, pos3, n_out):
    t, w = src.shape
    kk = pos3.shape[1]
    per_w = t // SC_WORKERS
    assert t % SC_WORKERS == 0 and per_w % SC_CHUNK == 0 and pos3.shape == (t // SC_CHUNK, kk, SC_CHUNK)

    def body(src_hbm, pos_hbm, out_hbm, iv, rv, sem):
        base = _sc_worker() * per_w

        @pl.loop(0, per_w, step=SC_CHUNK)
        def _(o):
            off = base + o
            pltpu.sync_copy(pos_hbm.at[off // SC_CHUNK], iv)
            pltpu.sync_copy(src_hbm.at[pl.ds(off, SC_CHUNK)], rv)
            copies = [pltpu.async_copy(rv, out_hbm.at[iv.at[j]], sem) for j in range(kk)]
            for cp in copies:
                cp.wait()

    return pl.kernel(
        body, mesh=_sc_mesh(),
        out_type=jax.ShapeDtypeStruct((n_out, w), src.dtype),
        scratch_types=[pltpu.VMEM((kk, SC_CHUNK), jnp.int32), pltpu.VMEM((SC_CHUNK, w), src.dtype),
                       pltpu.SemaphoreType.DMA],
    )(src, pos3)


def _shift_down(a, k, row):
    return jnp.where(row >= k, pltpu.roll(a, k, axis=0), 0.0)


def _shift_up(a, k, row):
    n = a.shape[0]
    return jnp.where(row < n - k, pltpu.roll(a, n - k, axis=0), 0.0)


def _even_mix_kernel(x_ref, w_in_ref, pool_w_ref, pool_scale_ref, conv_w_ref, cat_ref, xb_ref):
    s = x_ref.shape[1]
    pool_width = pool_scale_ref.shape[1]
    gd = pool_width // len(POOL_WINDOWS)
    conv_width = conv_w_ref.shape[1]
    xb_ref[...] = x_ref[0].astype(BF16)
    xb = xb_ref[...]

    row = lax.broadcasted_iota(jnp.int32, (s, gd), 0)
    for g, w in enumerate(POOL_WINDOWS):
        lo = g * gd
        if g % 2 == 0:
            u2 = _dot(xb, w_in_ref[:, lo:lo + 2 * gd])
        u = u2[:, (g % 2) * gd:(g % 2 + 1) * gd]
        half = w // 2
        back, fwd, span = u, u, 1
        while span < half:
            back = back + _shift_down(back, span, row)
            fwd = fwd + _shift_up(fwd, span, row)
            span *= 2
        win = _shift_down(back, 1, row) + fwd
        cnt = (jnp.minimum(row + (w - half), s) - jnp.maximum(row - half, 0)).astype(F32)
        pooled = win / cnt - u
        mixed = _dot(pooled.astype(BF16), pool_w_ref[g])
        cat_ref[0, :, lo:lo + gd] = (mixed * pool_scale_ref[:, lo:lo + gd]).astype(BF16)

    cw = 256
    rowc = lax.broadcasted_iota(jnp.int32, (s, cw), 0)
    for j in range(conv_width // cw):
        c0 = j * cw
        b_gate = _dot(xb, w_in_ref[:, pool_width + c0:pool_width + c0 + cw])
        c_gate = _dot(xb, w_in_ref[:, pool_width + conv_width + c0:pool_width + conv_width + c0 + cw])
        v = _dot(xb, w_in_ref[:, pool_width + 2 * conv_width + c0:pool_width + 2 * conv_width + c0 + cw])
        u = c_gate * v
        conv = (_shift_down(u, 1, rowc) * conv_w_ref[0:1, c0:c0 + cw] + u * conv_w_ref[1:2, c0:c0 + cw]
                + _shift_up(u, 1, rowc) * conv_w_ref[2:3, c0:c0 + cw])
        cat_ref[0, :, pool_width + c0:pool_width + c0 + cw] = (b_gate * conv).astype(BF16)


def _even_mix(x, w_in, pool_w, pool_scale, conv_w):
    b, s, d = x.shape
    pool_width = pool_scale.shape[0]
    conv_width = conv_w.shape[1]
    assert conv_width % 256 == 0 and w_in.shape[1] == pool_width + 3 * conv_width
    cat_width = pool_width + conv_width
    return pl.pallas_call(
        _even_mix_kernel,
        grid=(b,),
        in_specs=[
            pl.BlockSpec((1, s, d), lambda i: (i, 0, 0)),
            pl.BlockSpec(w_in.shape, lambda i: (0, 0)),
            pl.BlockSpec(pool_w.shape, lambda i: (0, 0, 0)),
            pl.BlockSpec((1, pool_width), lambda i: (0, 0)),
            pl.BlockSpec(conv_w.shape, lambda i: (0, 0)),
        ],
        out_specs=pl.BlockSpec((1, s, cat_width), lambda i: (i, 0, 0)),
        out_shape=jax.ShapeDtypeStruct((b, s, cat_width), BF16),
        scratch_shapes=[pltpu.VMEM((s, d), BF16)],
        compiler_params=_params(1),
        name="even_mix",
    )(x, w_in.astype(BF16), pool_w.astype(BF16), pool_scale.reshape(1, pool_width), conv_w)


def _odd_proj_kernel(x_ref, w_ref, *refs, c_width, dils):
    outs, h_ref = refs[:-1], refs[-1]
    xb = x_ref[...].astype(BF16)
    tm = x_ref.shape[0]
    chunks = c_width // LANES
    for j in range(3):
        h = _dot(xb, w_ref[:, j * c_width:(j + 1) * c_width])
        for c in range(chunks):
            h_ref[j, c] = h[:, c * LANES:(c + 1) * LANES]
        for pi, dil in enumerate(dils):
            n = tm // dil
            for r in range(dil):
                for c in range(chunks):
                    src = h_ref[j, c] if dil == 1 else h_ref[j, c, pl.ds(r, n, stride=dil), :]
                    lo = r * c_width + c * LANES
                    outs[3 * pi + j][:, lo:lo + LANES] = src.astype(BF16)
    c0 = 3 * c_width
    for ref in outs[3 * len(dils):]:
        wd = ref.shape[1]
        ref[...] = _dot(xb, w_ref[:, c0:c0 + wd]).astype(BF16)
        c0 += wd


def _odd_proj(xt, w_in, c_width, dq_width, dkv_width):
    t, d = xt.shape
    dils = tuple(dil for _, dil in C_PATTERNS)
    assert 3 * c_width + dq_width + 2 * dkv_width == w_in.shape[1] and c_width % LANES == 0
    col = jnp.arange(w_in.shape[1])
    is_q = (col < c_width) | ((col >= 3 * c_width) & (col < 3 * c_width + dq_width))
    w_in = w_in * jnp.where(is_q, HEAD_DIM ** -0.5 * LOG2E, 1.0)
    kv0 = 3 * c_width + dq_width
    twice = lambda w: jnp.repeat(w.reshape(d, -1, HEAD_DIM), LANES // HEAD_DIM, axis=1).reshape(d, -1)
    w_in = jnp.concatenate([w_in[:, :kv0], twice(w_in[:, kv0:kv0 + dkv_width]), twice(w_in[:, kv0 + dkv_width:])], axis=1)
    dkv_width = dkv_width * (LANES // HEAD_DIM)
    w_scaled = w_in.astype(BF16)
    shapes = [(t // dil, dil * c_width) for dil in dils for _ in range(3)]
    blocks = [(ROW_TILE // dil, dil * c_width) for dil in dils for _ in range(3)]
    for wd in (dq_width, dkv_width, dkv_width):
        shapes.append((t, wd))
        blocks.append((ROW_TILE, wd))
    outs = pl.pallas_call(
        functools.partial(_odd_proj_kernel, c_width=c_width, dils=dils),
        grid=(t // ROW_TILE,),
        in_specs=[pl.BlockSpec((ROW_TILE, d), lambda i: (i, 0)),
                  pl.BlockSpec(w_in.shape, lambda i: (0, 0))],
        out_specs=[pl.BlockSpec(blk, lambda i: (i, 0)) for blk in blocks],
        out_shape=[jax.ShapeDtypeStruct(shp, BF16) for shp in shapes],
        scratch_shapes=[pltpu.VMEM((3, c_width // LANES, ROW_TILE, LANES), F32)],
        compiler_params=_params(1),
        name="odd_proj",
    )(xt, w_scaled)
    qkv = [outs[3 * pi:3 * pi + 3] for pi in range(len(dils))]
    return qkv, outs[3 * len(dils):]


def _band_shape(length, radius):
    qb = min(ATTN_Q_BLOCK, length)
    span = min(length, qb + 2 * radius)
    n_blk = length // qb
    assert length % qb == 0 and (n_blk <= 2 or radius <= qb)
    return qb, span, n_blk, (0, -radius, qb - span)


def _fill_band_bias(bias_ref, length, radius, dist_unit, n_heads):
    qb, span, _, offsets = _band_shape(length, radius)
    rel = lax.broadcasted_iota(jnp.int32, (qb, span), 1) - lax.broadcasted_iota(jnp.int32, (qb, span), 0)
    for v, off in enumerate(offsets):
        dist = jnp.abs(rel + off)
        mask = jnp.where(dist <= radius, 0.0, NEG_INF).astype(F32)
        far = dist.astype(F32) * (dist_unit * LOG2E)
        for hh in range(n_heads):
            bias_ref[v, hh] = mask - 2.0 ** (-8.0 * (hh + 1) / n_heads) * far


def _band_block(i, length, radius):
    qb, span, n_blk, offsets = _band_shape(length, radius)
    q0 = pl.multiple_of(i * qb, qb)
    variant = jnp.where(i == 0, 0, jnp.where(i == n_blk - 1, 2, 1))
    off = jnp.where(i == 0, offsets[0], jnp.where(i == n_blk - 1, offsets[2], offsets[1]))
    align = 8
    for cand in (128, 64, 32, 16):
        if all(o % cand == 0 for o in offsets) and qb % cand == 0:
            align = cand
            break
    return q0, pl.multiple_of(q0 + off, align), variant


def _attend_pairs(q_slabs, k_slabs, v_slabs, biases):
    lower = lax.broadcasted_iota(jnp.int32, (1, LANES), 1) < HEAD_DIM
    scores = []
    for j, (q2, k2) in enumerate(zip(q_slabs, k_slabs)):
        zero = jnp.zeros_like(q2)
        scores.append(_dot_nt(jnp.where(lower, q2, zero), k2) + biases[2 * j])
        scores.append(_dot_nt(jnp.where(lower, zero, q2), k2) + biases[2 * j + 1])
    probs = []
    for s in scores:
        m = jnp.max(s, axis=-1, keepdims=True)
        probs.append((jnp.exp2(s - m).astype(BF16), m))
    ones = jnp.ones(v_slabs[0].shape, BF16)
    outs = []
    for j, v2 in enumerate(v_slabs):
        (p_lo, m_lo), (p_up, m_up) = probs[2 * j], probs[2 * j + 1]
        v_ones = jnp.concatenate([v2, ones], axis=1)
        r_lo, r_up = _dot(p_lo, v_ones), _dot(p_up, v_ones)
        l = jnp.where(lower, r_lo[:, LANES:], r_up[:, LANES:])
        o = jnp.where(lower, r_lo[:, :LANES], r_up[:, :LANES]) / l
        outs.append((o, (jnp.where(lower, m_lo, m_up) + jnp.log2(l)) * (1.0 / LOG2E)))
    return outs


def _attn_c_kernel(q_ref, k_ref, v_ref, o_ref, lse_ref, bias_ref, *, dil, radius, n_heads):
    length = q_ref.shape[1]
    qb, span, n_blk, _ = _band_shape(length, radius)
    width = n_heads * HEAD_DIM

    @pl.when((pl.program_id(0) == 0) & (pl.program_id(1) == 0))
    def _():
        _fill_band_bias(bias_ref, length, radius, float(dil), n_heads)

    def block(i, carry):
        q0, start, variant = _band_block(i, length, radius)
        biases = [bias_ref[variant, hh] for hh in range(n_heads)]
        for res in range(q_ref.shape[2] // width):
            slabs = [res * width + j * LANES for j in range(width // LANES)]
            outs = _attend_pairs([q_ref[0, pl.ds(q0, qb), lo:lo + LANES] for lo in slabs],
                                 [k_ref[0, pl.ds(start, span), lo:lo + LANES] for lo in slabs],
                                 [v_ref[0, pl.ds(start, span), lo:lo + LANES] for lo in slabs], biases)
            for lo, (o, lse) in zip(slabs, outs):
                o_ref[0, pl.ds(q0, qb), lo:lo + LANES] = o.astype(o_ref.dtype)
                lse_ref[0, pl.ds(q0, qb), lo:lo + LANES] = lse
        return carry

    lax.fori_loop(0, n_blk, block, 0)


def _attn_c(q, k, v, bsz, window, dil):
    rows, dw = q.shape
    w = dw // dil
    length = rows // bsz
    radius = window // 2 // dil
    qb, span, _, _ = _band_shape(length, radius)
    per_step = max(1, min(dil, ATTN_STEP_ROWS // length))
    assert dil % per_step == 0
    view = lambda a: a.reshape(bsz, length, dw)
    spec = pl.BlockSpec((1, length, per_step * w), lambda i, r: (i, 0, r))
    o, lse = pl.pallas_call(
        functools.partial(_attn_c_kernel, dil=dil, radius=radius, n_heads=w // HEAD_DIM),
        grid=(bsz, dil // per_step),
        in_specs=[spec, spec, spec],
        out_specs=[spec, spec],
        out_shape=[jax.ShapeDtypeStruct((bsz, length, dw), BF16),
                   jax.ShapeDtypeStruct((bsz, length, dw), F32)],
        scratch_shapes=[pltpu.VMEM((3, w // HEAD_DIM, qb, span), F32)],
        compiler_params=_params(2),
        name=f"attn_c_d{dil}",
    )(view(q), view(k), view(v))
    return o.reshape(rows, dw), lse.reshape(rows, dw)


def _attn_d_kernel(sink_ref, q_ref, k_ref, v_ref, y_ref, bias_ref, *, radius, n_heads, group):
    length = q_ref.shape[1]
    qb, span, n_blk, _ = _band_shape(length, radius)
    pairs = n_heads * HEAD_DIM // LANES
    lower = lax.broadcasted_iota(jnp.int32, (1, LANES), 1) < HEAD_DIM

    @pl.when(pl.program_id(0) == 0)
    def _():
        _fill_band_bias(bias_ref, length, radius, 1.0, n_heads)

    def block(i, carry):
        q0, start, variant = _band_block(i, length, radius)
        biases = [bias_ref[variant, hh] for hh in range(n_heads)]
        kv_lo = [(2 * j // group) * LANES for j in range(pairs)]
        outs = _attend_pairs([q_ref[0, pl.ds(q0, qb), j * LANES:(j + 1) * LANES] for j in range(pairs)],
                             [k_ref[0, pl.ds(start, span), lo:lo + LANES] for lo in kv_lo],
                             [v_ref[0, pl.ds(start, span), lo:lo + LANES] for lo in kv_lo], biases)
        for j, (o, lse) in enumerate(outs):
            sink = jnp.where(lower, sink_ref[2 * j], sink_ref[2 * j + 1])
            y_ref[0, pl.ds(q0, qb), j * LANES:(j + 1) * LANES] = (o * jax.nn.sigmoid(lse - sink)).astype(y_ref.dtype)
        return carry

    lax.fori_loop(0, n_blk, block, 0)


def _attn_d(q, k, v, sink):
    b, s, w = q.shape
    kvw = k.shape[2]
    n_heads = w // HEAD_DIM
    group = n_heads // D_KV_HEADS
    assert kvw == D_KV_HEADS * LANES and group % 2 == 0
    qb, span, _, _ = _band_shape(s, D_RADIUS)
    return pl.pallas_call(
        functools.partial(_attn_d_kernel, radius=D_RADIUS, n_heads=n_heads, group=group),
        grid=(b,),
        in_specs=[pl.BlockSpec(memory_space=pltpu.SMEM),
                  pl.BlockSpec((1, s, w), lambda i: (i, 0, 0)),
                  pl.BlockSpec((1, s, kvw), lambda i: (i, 0, 0)),
                  pl.BlockSpec((1, s, kvw), lambda i: (i, 0, 0))],
        out_specs=pl.BlockSpec((1, s, w), lambda i: (i, 0, 0)),
        out_shape=jax.ShapeDtypeStruct((b, s, w), BF16),
        scratch_shapes=[pltpu.VMEM((3, n_heads, qb, span), F32)],
        compiler_params=_params(1),
        name="attn_d",
    )(sink.astype(F32), q, k, v)


def _merge_kernel(*refs, dils, cw):
    n_pat = len(dils)
    o_refs, l_refs = refs[:n_pat], refs[n_pat:2 * n_pat]
    yd_ref, cat_ref = refs[2 * n_pat], refs[2 * n_pat + 1]
    scratch = list(refs[2 * n_pat + 2:])
    tm = cat_ref.shape[0]
    chunks = cw // LANES
    staged = []
    for p, dil in enumerate(dils):
        if dil == 1:
            staged.append(None)
            continue
        o_s, l_s = scratch.pop(0), scratch.pop(0)
        n = tm // dil
        for r in range(dil):
            for c in range(chunks):
                lo = r * cw + c * LANES
                o_s[c, pl.ds(r, n, stride=dil), :] = o_refs[p][:, lo:lo + LANES].astype(F32)
                l_s[c, pl.ds(r, n, stride=dil), :] = l_refs[p][:, lo:lo + LANES]
        staged.append((o_s, l_s))
    for c in range(chunks):
        cols = slice(c * LANES, (c + 1) * LANES)
        os_ = [o_refs[p][:, cols].astype(F32) if st is None else st[0][c] for p, st in enumerate(staged)]
        ls_ = [l_refs[p][:, cols] if st is None else st[1][c] for p, st in enumerate(staged)]
        m = functools.reduce(jnp.maximum, ls_)
        es = [jnp.exp(l - m) for l in ls_]
        num = functools.reduce(lambda a, b: a + b, [e * o for e, o in zip(es, os_)])
        den = functools.reduce(lambda a, b: a + b, es)
        cat_ref[:, cols] = (num / den).astype(BF16)
    cat_ref[:, cw:] = yd_ref[...]


def _merge(os_, lses, yd, dils):
    t, dw = yd.shape
    cw = os_[0].shape[1] // dils[0]
    vspecs = [pl.BlockSpec((ROW_TILE // dil, dil * cw), lambda i: (i, 0)) for dil in dils]
    n_scratch = 2 * sum(1 for dil in dils if dil != 1)
    return pl.pallas_call(
        functools.partial(_merge_kernel, dils=dils, cw=cw),
        grid=(t // ROW_TILE,),
        in_specs=vspecs + vspecs + [pl.BlockSpec((ROW_TILE, dw), lambda i: (i, 0))],
        out_specs=pl.BlockSpec((ROW_TILE, cw + dw), lambda i: (i, 0)),
        out_shape=jax.ShapeDtypeStruct((t, cw + dw), BF16),
        scratch_shapes=[pltpu.VMEM((cw // LANES, ROW_TILE, LANES), F32)] * n_scratch,
        compiler_params=_params(1),
        name="merge",
    )(*os_, *lses, yd)


def _layer_norm(z, g, b):
    mu = jnp.mean(z, axis=-1, keepdims=True)
    zc = z - mu
    var = jnp.mean(zc * zc, axis=-1, keepdims=True)
    return zc * lax.rsqrt(var + LN_EPS) * g + b


ROUTE_ROWS = 8


def _route_tile(logits, triu, count_ref):
    n_exp, tm = logits.shape
    sub = lax.broadcasted_iota(jnp.int32, (n_exp, tm), 0)
    out_row = lax.broadcasted_iota(jnp.int32, (ROUTE_ROWS, tm), 0)
    work = logits
    vals, onehots = [], []
    idx_out = jnp.zeros((ROUTE_ROWS, tm), jnp.int32)
    for k in range(TOP_K):
        m = jnp.max(work, axis=0, keepdims=True)
        idx = jnp.min(jnp.where(work == m, sub, n_exp), axis=0, keepdims=True)
        hot = sub == idx
        work = jnp.where(hot, -jnp.inf, work)
        vals.append(m)
        onehots.append(hot)
        idx_out = jnp.where(out_row == k, idx, idx_out)
    exps = [jnp.exp(v - vals[0]) for v in vals]
    den = functools.reduce(lambda a, b: a + b, exps)
    gate_out = jnp.zeros((ROUTE_ROWS, tm), F32)
    for k in range(TOP_K):
        gate_out = jnp.where(out_row == k, exps[k] / den, gate_out)
    multihot = functools.reduce(lambda a, b: a + b, [h.astype(F32) for h in onehots])
    before = _dot(multihot.astype(BF16), triu) + count_ref[...]
    rank_out = jnp.zeros((ROUTE_ROWS, tm), jnp.int32)
    for k in range(TOP_K):
        rank = jnp.sum(jnp.where(onehots[k], before, 0.0), axis=0, keepdims=True)
        rank_out = jnp.where(out_row == k, rank.astype(jnp.int32), rank_out)
    count_ref[...] += jnp.sum(multihot, axis=1, keepdims=True)
    return idx_out, gate_out, rank_out


def _out_ln_kernel(cat_ref, w_ref, x_ref, g_ref, b_ref, rw_hi_ref, rw_lo_ref, rb_ref, triu_ref,
                   x1_ref, x1p_ref, idx_ref, gate_ref, rank_ref, count_ref, *, alpha):
    @pl.when(pl.program_id(0) == 0)
    def _():
        count_ref[...] = jnp.zeros_like(count_ref)

    mix = _dot(cat_ref[...], w_ref[...])
    x1 = _layer_norm(alpha * x_ref[...] + mix, g_ref[...], b_ref[...])
    x1_ref[...] = x1
    x1p_ref[...] = _pack_pairs(x1)
    hi = x1.astype(BF16)
    lo = (x1 - hi.astype(F32)).astype(BF16)
    logits = (_dot_nt(rw_hi_ref[...], hi) + _dot_nt(rw_hi_ref[...], lo) + _dot_nt(rw_lo_ref[...], hi)
              + rb_ref[...])
    idx_ref[...], gate_ref[...], rank_ref[...] = _route_tile(logits, triu_ref[...], count_ref)


def _out_ln(cat, w_out, xt, g, b, router_w, router_b, alpha):
    t, d = xt.shape
    n_exp = router_w.shape[1]
    rw_t = router_w.T
    rw_hi = rw_t.astype(BF16)
    rw_lo = (rw_t - rw_hi.astype(F32)).astype(BF16)
    triu = jnp.triu(jnp.ones((ROW_TILE, ROW_TILE), BF16), 1)
    row = lambda wd: pl.BlockSpec((ROW_TILE, wd), lambda i: (i, 0))
    col = pl.BlockSpec((ROUTE_ROWS, ROW_TILE), lambda i: (0, i))
    full = lambda a: pl.BlockSpec(a.shape, lambda i: (0,) * a.ndim)
    args = (cat, w_out.astype(BF16), xt, g.reshape(1, d), b.reshape(1, d), rw_hi, rw_lo,
            router_b.reshape(n_exp, 1), triu)
    return pl.pallas_call(
        functools.partial(_out_ln_kernel, alpha=alpha),
        grid=(t // ROW_TILE,),
        in_specs=[row(cat.shape[1]), full(args[1]), row(d)] + [full(a) for a in args[3:]],
        out_specs=[row(d), row(d // 2), col, col, col, pl.BlockSpec((n_exp, 1), lambda i: (0, 0))],
        out_shape=[jax.ShapeDtypeStruct((t, d), F32), jax.ShapeDtypeStruct((t, d // 2), jnp.int32),
                   jax.ShapeDtypeStruct((ROUTE_ROWS, t), jnp.int32), jax.ShapeDtypeStruct((ROUTE_ROWS, t), F32),
                   jax.ShapeDtypeStruct((ROUTE_ROWS, t), jnp.int32), jax.ShapeDtypeStruct((n_exp, 1), F32)],
        compiler_params=_params(1),
        name="out_ln",
    )(*args)


def _experts_kernel(first_ref, blocks_ref, count_ref, x_hbm, wgu_ref, bgu_ref, wd_ref, bd_ref, perm_ref,
                    y_hbm, wgu_s, wd_s, x_buf, y_buf, x_sem, y_sem, zero_sem):
    e = pl.program_id(0)
    last = pl.num_programs(0) - 1
    n_blk = blocks_ref[e]
    first = first_ref[e]
    count = count_ref[e]
    used = first_ref[last] + blocks_ref[last]
    half = GU_BLOCK // 2

    def rows_of(g):
        return pl.ds(pl.multiple_of(g * MOE_BLOCK, MOE_BLOCK), MOE_BLOCK)

    def x_copy(g, slot):
        return pltpu.make_async_copy(x_hbm.at[rows_of(g)], x_buf.at[slot], x_sem.at[slot])

    def y_copy(g, slot):
        return pltpu.make_async_copy(y_buf.at[slot], y_hbm.at[rows_of(g)], y_sem.at[slot])

    def compute(g, slot):
        xb = _unpack_pairs(x_buf[slot])
        row = lax.broadcasted_iota(jnp.int32, xb.shape, 0)
        xb = jnp.where(row < count - (g - first) * MOE_BLOCK, xb, jnp.zeros_like(xb))
        h = _dot(xb, wgu_s[...]) + bgu_ref[...]
        acts = []
        for c in range(h.shape[1] // GU_BLOCK):
            glu = jnp.minimum(h[:, c * GU_BLOCK:c * GU_BLOCK + half], SWIGLU_LIMIT)
            lin = jnp.clip(h[:, c * GU_BLOCK + half:(c + 1) * GU_BLOCK], -SWIGLU_LIMIT, SWIGLU_LIMIT)
            acts.append((glu * jax.nn.sigmoid(SWIGLU_ALPHA * glu) * (lin + 1.0)).astype(BF16))
        act = jnp.concatenate(acts, axis=1)
        y_buf[slot] = _pack_pairs(_dot(act, wd_s[...]) + bd_ref[...])

    @pl.when((e == 0) & (used > 0))
    def _():
        x_copy(0, 0).start()

    @pl.when(n_blk > 0)
    def _():
        for c in range(wgu_ref.shape[1] // GU_BLOCK):
            cols = slice(c * GU_BLOCK, (c + 1) * GU_BLOCK)
            wgu_s[:, cols] = _dot(wgu_ref[:, cols].astype(BF16), perm_ref[...]).astype(BF16)
        wd_s[...] = wd_ref[...].astype(BF16)

    def step(g, carry):
        slot = g % 2
        x_copy(g, slot).wait()

        @pl.when(g + 1 < used)
        def _():
            x_copy(g + 1, 1 - slot).start()

        @pl.when(g >= 2)
        def _():
            y_copy(g - 2, slot).wait()

        compute(g, slot)
        y_copy(g, slot).start()
        return carry

    lax.fori_loop(first, first + n_blk, step, 0)

    @pl.when(e == last)
    def _():
        @pl.when(used >= 1)
        def _():
            y_copy(0, (used - 1) % 2).wait()

        @pl.when(used >= 2)
        def _():
            y_copy(0, used % 2).wait()

        total = y_hbm.shape[0] // MOE_BLOCK
        y_buf[0] = jnp.zeros(y_buf.shape[1:], y_buf.dtype)

        def tail_copy(g):
            return pltpu.make_async_copy(y_buf.at[0], y_hbm.at[rows_of(g)], zero_sem)

        def start_all(g, carry):
            tail_copy(g).start()
            return carry

        def wait_all(g, carry):
            tail_copy(g).wait()
            return carry

        lax.fori_loop(used, total, start_all, 0)
        lax.fori_loop(used, total, wait_all, 0)


def _experts(x_rows, first_blk, n_blk, counts, layer, w_gu, b_gu, w_down, b_down):
    rows, dp = x_rows.shape
    _, n_exp, d, de2 = w_gu.shape
    assert de2 % GU_BLOCK == 0 and dp * 2 == d and rows % MOE_BLOCK == 0
    half = GU_BLOCK // 2
    j = jnp.arange(GU_BLOCK)
    src = jnp.where(j < half, 2 * j, 2 * (j - half) + 1)
    perm = (jnp.arange(GU_BLOCK)[:, None] == src[None, :]).astype(BF16)
    wspec = lambda a: pl.BlockSpec((None, None) + a.shape[2:], lambda e, *_: (layer, e, 0, 0))
    grid_spec = pltpu.PrefetchScalarGridSpec(
        num_scalar_prefetch=3,
        grid=(n_exp,),
        in_specs=[pl.BlockSpec(memory_space=pl.ANY),
                  wspec(w_gu), wspec(b_gu), wspec(w_down), wspec(b_down),
                  pl.BlockSpec((GU_BLOCK, GU_BLOCK), lambda e, *_: (0, 0))],
        out_specs=pl.BlockSpec(memory_space=pl.ANY),
        scratch_shapes=[pltpu.VMEM(w_gu.shape[2:], BF16), pltpu.VMEM(w_down.shape[2:], BF16),
                        pltpu.VMEM((2, MOE_BLOCK, dp), jnp.int32), pltpu.VMEM((2, MOE_BLOCK, dp), jnp.int32),
                        pltpu.SemaphoreType.DMA((2,)), pltpu.SemaphoreType.DMA((2,)),
                        pltpu.SemaphoreType.DMA(())],
    )
    return pl.pallas_call(
        _experts_kernel,
        grid_spec=grid_spec,
        out_shape=jax.ShapeDtypeStruct((rows, dp), jnp.int32),
        compiler_params=_params(1),
        name="experts",
    )(first_blk, n_blk, counts, x_rows, w_gu, b_gu, w_down, b_down, perm)


def _combine_ln_kernel(x1_ref, yk_ref, gate_ref, g_ref, b_ref, x2_ref, *, alpha):
    gates = gate_ref[...].T
    ffn = gates[:, 0:1] * _unpack_pairs(yk_ref[0]).astype(F32)
    for k in range(1, yk_ref.shape[0]):
        ffn = ffn + gates[:, k:k + 1] * _unpack_pairs(yk_ref[k]).astype(F32)
    x2_ref[...] = _layer_norm(alpha * x1_ref[...] + ffn, g_ref[...], b_ref[...])


def _combine_ln(x1, yk, gates, g, b, alpha):
    t, d = x1.shape
    k = yk.shape[0]
    tile = COMBINE_TILE if t % COMBINE_TILE == 0 else ROW_TILE
    return pl.pallas_call(
        functools.partial(_combine_ln_kernel, alpha=alpha),
        grid=(t // tile,),
        in_specs=[pl.BlockSpec((tile, d), lambda i: (i, 0)),
                  pl.BlockSpec((k, tile, d // 2), lambda i: (0, i, 0)),
                  pl.BlockSpec((gates.shape[0], tile), lambda i: (0, i)),
                  pl.BlockSpec((1, d), lambda i: (0, 0)),
                  pl.BlockSpec((1, d), lambda i: (0, 0))],
        out_specs=pl.BlockSpec((tile, d), lambda i: (i, 0)),
        out_shape=jax.ShapeDtypeStruct((t, d), F32),
        compiler_params=_params(1),
        name="combine_ln",
    )(x1, yk, gates, g.reshape(1, d), b.reshape(1, d))


def _route(top_idx, rank, counts):
    n_exp = counts.shape[0]
    experts = jnp.arange(n_exp, dtype=jnp.int32)
    counts = counts.reshape(n_exp).astype(jnp.int32)
    padded = (counts + MOE_BLOCK - 1) // MOE_BLOCK * MOE_BLOCK
    pad_end = jnp.cumsum(padded)
    pad_start = pad_end - padded
    start = jnp.sum(jnp.where(top_idx[None] == experts[:, None, None], pad_start[:, None, None], 0), axis=0)
    pos = rank + start
    first_blk = (pad_start // MOE_BLOCK).astype(jnp.int32)
    n_blk = (padded // MOE_BLOCK).astype(jnp.int32)
    return pos, first_blk, n_blk, counts


def _moe(x1, x1p, top_idx, gates, rank, counts, layer, w_gu, b_gu, w_down, b_down, g, b, alpha):
    t, d = x1.shape
    n_exp = counts.shape[0]
    pos, first_blk, n_blk, counts = _route(top_idx[:TOP_K], rank[:TOP_K], counts)
    rows = (-(-(t * TOP_K) // MOE_BLOCK) + n_exp) * MOE_BLOCK
    pos3 = pos.reshape(TOP_K, t // SC_CHUNK, SC_CHUNK).transpose(1, 0, 2)
    x_rows = _sc_scatter_rows(x1p, pos3, rows)
    y = _experts(x_rows, first_blk, n_blk, counts, layer, w_gu, b_gu, w_down, b_down)
    yk = _sc_gather_rows(y, pos.reshape(-1)).reshape(TOP_K, t, d // 2)
    return _combine_ln(x1, yk, gates, g, b, alpha)


def kernel(x, ev_w_in, ev_pool_w, ev_pool_scale, ev_conv_w, ev_w_out, od_w_in, od_sink, od_w_out,
           router_w, router_b, exp_w_gu, exp_b_gu, exp_w_down, exp_b_down, ln_g, ln_b):
    bsz, seq, d = x.shape
    t = bsz * seq
    depth = ln_g.shape[0]
    alpha = (2 * depth) ** 0.25
    n_exp = router_w.shape[2]
    c_width = d // 2
    dq_width = d // 2
    dkv_width = D_KV_HEADS * HEAD_DIM
    assert t % ROW_TILE == 0

    de2 = exp_b_gu.shape[-1]
    b_gu = exp_b_gu.reshape(depth, n_exp, de2 // GU_BLOCK, GU_BLOCK // 2, 2)
    b_gu = jnp.swapaxes(b_gu, -1, -2).reshape(depth, n_exp, 1, de2)
    b_down = exp_b_down.reshape(depth, n_exp, 1, -1)

    xt = x.reshape(t, d)
    for layer in range(depth):
        i = layer // 2
        if layer % 2 == 0:
            cat = _even_mix(xt.reshape(bsz, seq, d), ev_w_in[i], ev_pool_w[i], ev_pool_scale[i], ev_conv_w[i])
            cat = cat.reshape(t, -1)
            w_out = ev_w_out[i]
        else:
            qkv, (qd, kd, vd) = _odd_proj(xt, od_w_in[i], c_width, dq_width, dkv_width)
            dils = tuple(dil for _, dil in C_PATTERNS)
            os_, lses = [], []
            for (window, dil), (qc, kc, vc) in zip(C_PATTERNS, qkv):
                o, lse = _attn_c(qc, kc, vc, bsz, window, dil)
                os_.append(o)
                lses.append(lse)
            seq3 = lambda a: a.reshape(bsz, seq, a.shape[1])
            yd = _attn_d(seq3(qd), seq3(kd), seq3(vd), od_sink[i]).reshape(t, dq_width)
            cat = _merge(os_, lses, yd, dils)
            w_out = od_w_out[i]
        x1, x1p, top_idx, gates, rank, counts = _out_ln(cat, w_out, xt, ln_g[layer, 0], ln_b[layer, 0],
                                                        router_w[layer], router_b[layer], alpha)
        xt = _moe(x1, x1p, top_idx, gates, rank, counts, layer, exp_w_gu, b_gu, exp_w_down, b_down,
                  ln_g[layer, 1], ln_b[layer, 1], alpha)
    return xt.reshape(bsz, seq, d)
```

```python
import functools

import jax
import jax.numpy as jnp
from jax import lax
from jax.experimental import pallas as pl
from jax.experimental.pallas import tpu as pltpu
from jax.experimental.pallas import tpu_sc as plsc

HEAD_DIM = 64
POOL_WINDOWS = (2, 4, 8, 16)
C_PATTERNS = ((128, 1), (512, 4), (2048, 16))
D_KV_HEADS = 2
D_RADIUS = 128
TOP_K = 4
SWIGLU_LIMIT = 7.0
SWIGLU_ALPHA = 1.702
MOE_BLOCK = 512
LN_EPS = 1e-5
NEG_INF = -1e30
LOG2E = 1.4426950408889634

ROW_TILE = 512
COMBINE_TILE = 1024
ATTN_Q_BLOCK = 128
ATTN_BLOCKS_PER_ITER = 4
ATTN_STEP_ROWS = 512
LANES = 128
GU_BLOCK = 256
VMEM_LIMIT_BYTES = 56 * 1024 * 1024

F32 = jnp.float32
BF16 = jnp.bfloat16


def _params(n_axes=1):
    return pltpu.CompilerParams(dimension_semantics=("arbitrary",) * n_axes,
                                vmem_limit_bytes=VMEM_LIMIT_BYTES)


def _dot(a, b):
    return jnp.dot(a, b, preferred_element_type=F32)


def _dot_nt(a, b):
    return lax.dot_general(a, b, (((1,), (1,)), ((), ())), preferred_element_type=F32)


def _pack_pairs(a):
    n = a.shape[1] // 2
    bits = lax.bitcast_convert_type(a.astype(BF16).astype(F32), jnp.int32)
    return bits[:, :n] | lax.shift_right_logical(bits[:, n:], 16)


def _unpack_pairs(w):
    hi = lax.bitcast_convert_type(w & jnp.int32(-65536), F32)
    lo = lax.bitcast_convert_type(lax.shift_left(w, 16), F32)
    return jnp.concatenate([hi, lo], axis=1).astype(BF16)


SC_CORES = 2
SC_SUBCORES = 16
SC_WORKERS = SC_CORES * SC_SUBCORES
SC_CHUNK = 64


def _sc_mesh():
    return plsc.VectorSubcoreMesh(core_axis_name="c", subcore_axis_name="s")


def _sc_worker():
    return lax.axis_index("s") * SC_CORES + lax.axis_index("c")


def _sc_gather_rows(table, idx):
    n, w = idx.shape[0], table.shape[1]
    per_w = n // SC_WORKERS
    assert n % SC_WORKERS == 0 and per_w % (2 * SC_CHUNK) == 0

    def body(table_hbm, idx_hbm, out_hbm, i0, i1, r0, r1, s0, s1):
        base = _sc_worker() * per_w

        def start(off, iv, rv, sem):
            pltpu.sync_copy(idx_hbm.at[pl.ds(off, SC_CHUNK)], iv)
            pltpu.async_copy(table_hbm.at[iv], rv, sem)

        def finish(off, iv, rv, sem):
            pltpu.make_async_copy(table_hbm.at[iv], rv, sem).wait()
            pltpu.sync_copy(rv, out_hbm.at[pl.ds(off, SC_CHUNK)])

        start(base, i0, r0, s0)

        @pl.loop(0, per_w, step=2 * SC_CHUNK)
        def _(o):
            off = base + o
            start(off + SC_CHUNK, i1, r1, s1)
            finish(off, i0, r0, s0)

            @pl.when(o + 2 * SC_CHUNK < per_w)
            def _():
                start(off + 2 * SC_CHUNK, i0, r0, s0)

            finish(off + SC_CHUNK, i1, r1, s1)

    return pl.kernel(
        body, mesh=_sc_mesh(),
        out_type=jax.ShapeDtypeStruct((n, w), table.dtype),
        scratch_types=[pltpu.VMEM((SC_CHUNK,), jnp.int32), pltpu.VMEM((SC_CHUNK,), jnp.int32),
                       pltpu.VMEM((SC_CHUNK, w), table.dtype), pltpu.VMEM((SC_CHUNK, w), table.dtype),
                       pltpu.SemaphoreType.DMA, pltpu.SemaphoreType.DMA],
    )(table, idx)


def _sc_scatter_rows(src, pos3, n_out):
    t, w = src.shape
    kk = pos3.shape[1]
    per_w = t // SC_WORKERS
    assert t % SC_WORKERS == 0 and per_w % SC_CHUNK == 0 and pos3.shape == (t // SC_CHUNK, kk, SC_CHUNK)

    def body(src_hbm, pos_hbm, out_hbm, iv, rv, sem):
        base = _sc_worker() * per_w

        @pl.loop(0, per_w, step=SC_CHUNK)
        def _(o):
            off = base + o
            pltpu.sync_copy(pos_hbm.at[off // SC_CHUNK], iv)
            pltpu.sync_copy(src_hbm.at[pl.ds(off, SC_CHUNK)], rv)
            copies = [pltpu.async_copy(rv, out_hbm.at[iv.at[j]], sem) for j in range(kk)]
            for cp in copies:
                cp.wait()

    return pl.kernel(
        body, mesh=_sc_mesh(),
        out_type=jax.ShapeDtypeStruct((n_out, w), src.dtype),
        scratch_types=[pltpu.VMEM((kk, SC_CHUNK), jnp.int32), pltpu.VMEM((SC_CHUNK, w), src.dtype),
                       pltpu.SemaphoreType.DMA],
    )(src, pos3)


def _shift_down(a, k, row):
    return jnp.where(row >= k, pltpu.roll(a, k, axis=0), 0.0)


def _shift_up(a, k, row):
    n = a.shape[0]
    return jnp.where(row < n - k, pltpu.roll(a, n - k, axis=0), 0.0)


def _even_mix_kernel(x_ref, w_in_ref, pool_w_ref, pool_scale_ref, conv_w_ref, cat_ref, xb_ref):
    s = x_ref.shape[1]
    pool_width = pool_scale_ref.shape[1]
    gd = pool_width // len(POOL_WINDOWS)
    conv_width = conv_w_ref.shape[1]
    xb_ref[...] = x_ref[0].astype(BF16)
    xb = xb_ref[...]

    row = lax.broadcasted_iota(jnp.int32, (s, gd), 0)
    for g, w in enumerate(POOL_WINDOWS):
        lo = g * gd
        if g % 2 == 0:
            u2 = _dot(xb, w_in_ref[:, lo:lo + 2 * gd])
        u = u2[:, (g % 2) * gd:(g % 2 + 1) * gd]
        half = w // 2
        back, fwd, span = u, u, 1
        while span < half:
            back = back + _shift_down(back, span, row)
            fwd = fwd + _shift_up(fwd, span, row)
            span *= 2
        win = _shift_down(back, 1, row) + fwd
        cnt = (jnp.minimum(row + (w - half), s) - jnp.maximum(row - half, 0)).astype(F32)
        pooled = win / cnt - u
        mixed = _dot(pooled.astype(BF16), pool_w_ref[g])
        cat_ref[0, :, lo:lo + gd] = (mixed * pool_scale_ref[:, lo:lo + gd]).astype(BF16)

    cw = 256
    rowc = lax.broadcasted_iota(jnp.int32, (s, cw), 0)
    for j in range(conv_width // cw):
        c0 = j * cw
        b_gate = _dot(xb, w_in_ref[:, pool_width + c0:pool_width + c0 + cw])
        c_gate = _dot(xb, w_in_ref[:, pool_width + conv_width + c0:pool_width + conv_width + c0 + cw])
        v = _dot(xb, w_in_ref[:, pool_width + 2 * conv_width + c0:pool_width + 2 * conv_width + c0 + cw])
        u = c_gate * v
        conv = (_shift_down(u, 1, rowc) * conv_w_ref[0:1, c0:c0 + cw] + u * conv_w_ref[1:2, c0:c0 + cw]
                + _shift_up(u, 1, rowc) * conv_w_ref[2:3, c0:c0 + cw])
        cat_ref[0, :, pool_width + c0:pool_width + c0 + cw] = (b_gate * conv).astype(BF16)


def _even_mix(x, w_in, pool_w, pool_scale, conv_w):
    b, s, d = x.shape
    pool_width = pool_scale.shape[0]
    conv_width = conv_w.shape[1]
    assert conv_width % 256 == 0 and w_in.shape[1] == pool_width + 3 * conv_width
    cat_width = pool_width + conv_width
    return pl.pallas_call(
        _even_mix_kernel,
        grid=(b,),
        in_specs=[
            pl.BlockSpec((1, s, d), lambda i: (i, 0, 0)),
            pl.BlockSpec(w_in.shape, lambda i: (0, 0)),
            pl.BlockSpec(pool_w.shape, lambda i: (0, 0, 0)),
            pl.BlockSpec((1, pool_width), lambda i: (0, 0)),
            pl.BlockSpec(conv_w.shape, lambda i: (0, 0)),
        ],
        out_specs=pl.BlockSpec((1, s, cat_width), lambda i: (i, 0, 0)),
        out_shape=jax.ShapeDtypeStruct((b, s, cat_width), BF16),
        scratch_shapes=[pltpu.VMEM((s, d), BF16)],
        compiler_params=_params(1),
        name="even_mix",
    )(x, w_in.astype(BF16), pool_w.astype(BF16), pool_scale.reshape(1, pool_width), conv_w)


def _odd_proj_kernel(x_ref, w_ref, *refs, c_width, dils):
    outs, h_ref = refs[:-1], refs[-1]
    xb = x_ref[...].astype(BF16)
    tm = x_ref.shape[0]
    chunks = c_width // LANES
    for j in range(3):
        h = _dot(xb, w_ref[:, j * c_width:(j + 1) * c_width])
        for c in range(chunks):
            h_ref[j, c] = h[:, c * LANES:(c + 1) * LANES]
        for pi, dil in enumerate(dils):
            n = tm // dil
            for r in range(dil):
                for c in range(chunks):
                    src = h_ref[j, c] if dil == 1 else h_ref[j, c, pl.ds(r, n, stride=dil), :]
                    lo = r * c_width + c * LANES
                    outs[3 * pi + j][:, lo:lo + LANES] = src.astype(BF16)
    c0 = 3 * c_width
    for ref in outs[3 * len(dils):]:
        wd = ref.shape[1]
        ref[...] = _dot(xb, w_ref[:, c0:c0 + wd]).astype(BF16)
        c0 += wd


def _odd_proj(xt, w_in, c_width, dq_width, dkv_width):
    t, d = xt.shape
    dils = tuple(dil for _, dil in C_PATTERNS)
    assert 3 * c_width + dq_width + 2 * dkv_width == w_in.shape[1] and c_width % LANES == 0
    col = jnp.arange(w_in.shape[1])
    is_q = (col < c_width) | ((col >= 3 * c_width) & (col < 3 * c_width + dq_width))
    w_in = w_in * jnp.where(is_q, HEAD_DIM ** -0.5 * LOG2E, 1.0)
    kv0 = 3 * c_width + dq_width
    twice = lambda w: jnp.repeat(w.reshape(d, -1, HEAD_DIM), LANES // HEAD_DIM, axis=1).reshape(d, -1)
    w_in = jnp.concatenate([w_in[:, :kv0], twice(w_in[:, kv0:kv0 + dkv_width]), twice(w_in[:, kv0 + dkv_width:])], axis=1)
    dkv_width = dkv_width * (LANES // HEAD_DIM)
    w_scaled = w_in.astype(BF16)
    shapes = [(t // dil, dil * c_width) for dil in dils for _ in range(3)]
    blocks = [(ROW_TILE // dil, dil * c_width) for dil in dils for _ in range(3)]
    for wd in (dq_width, dkv_width, dkv_width):
        shapes.append((t, wd))
        blocks.append((ROW_TILE, wd))
    outs = pl.pallas_call(
        functools.partial(_odd_proj_kernel, c_width=c_width, dils=dils),
        grid=(t // ROW_TILE,),
        in_specs=[pl.BlockSpec((ROW_TILE, d), lambda i: (i, 0)),
                  pl.BlockSpec(w_in.shape, lambda i: (0, 0))],
        out_specs=[pl.BlockSpec(blk, lambda i: (i, 0)) for blk in blocks],
        out_shape=[jax.ShapeDtypeStruct(shp, BF16) for shp in shapes],
        scratch_shapes=[pltpu.VMEM((3, c_width // LANES, ROW_TILE, LANES), F32)],
        compiler_params=_params(1),
        name="odd_proj",
    )(xt, w_scaled)
    qkv = [outs[3 * pi:3 * pi + 3] for pi in range(len(dils))]
    return qkv, outs[3 * len(dils):]


def _band_shape(length, radius):
    qb = min(ATTN_Q_BLOCK, length)
    span = min(length, qb + 2 * radius)
    n_blk = length // qb
    assert length % qb == 0 and (n_blk <= 2 or radius <= qb)
    return qb, span, n_blk, (0, -radius, qb - span)


def _fill_band_bias(bias_ref, length, radius, dist_unit, n_heads):
    qb, span, _, offsets = _band_shape(length, radius)
    rel = lax.broadcasted_iota(jnp.int32, (qb, span), 1) - lax.broadcasted_iota(jnp.int32, (qb, span), 0)
    for v, off in enumerate(offsets):
        dist = jnp.abs(rel + off)
        mask = jnp.where(dist <= radius, 0.0, NEG_INF).astype(F32)
        far = dist.astype(F32) * (dist_unit * LOG2E)
        for hh in range(n_heads):
            bias_ref[v, hh] = mask - 2.0 ** (-8.0 * (hh + 1) / n_heads) * far


def _band_block(i, length, radius):
    qb, span, n_blk, offsets = _band_shape(length, radius)
    q0 = pl.multiple_of(i * qb, qb)
    variant = jnp.where(i == 0, 0, jnp.where(i == n_blk - 1, 2, 1))
    off = jnp.where(i == 0, offsets[0], jnp.where(i == n_blk - 1, offsets[2], offsets[1]))
    align = 8
    for cand in (128, 64, 32, 16):
        if all(o % cand == 0 for o in offsets) and qb % cand == 0:
            align = cand
            break
    return q0, pl.multiple_of(q0 + off, align), variant


def _attend_pairs(q_slabs, k_slabs, v_slabs, biases):
    lower = lax.broadcasted_iota(jnp.int32, (1, LANES), 1) < HEAD_DIM
    scores = []
    for j, (q2, k2) in enumerate(zip(q_slabs, k_slabs)):
        zero = jnp.zeros_like(q2)
        scores.append(_dot_nt(jnp.where(lower, q2, zero), k2) + biases[2 * j])
        scores.append(_dot_nt(jnp.where(lower, zero, q2), k2) + biases[2 * j + 1])
    probs = []
    for s in scores:
        m = jnp.max(s, axis=-1, keepdims=True)
        probs.append((jnp.exp2(s - m).astype(BF16), m))
    ones = jnp.ones(v_slabs[0].shape, BF16)
    outs = []
    for j, v2 in enumerate(v_slabs):
        (p_lo, m_lo), (p_up, m_up) = probs[2 * j], probs[2 * j + 1]
        v_ones = jnp.concatenate([v2, ones], axis=1)
        r_lo, r_up = _dot(p_lo, v_ones), _dot(p_up, v_ones)
        l = jnp.where(lower, r_lo[:, LANES:], r_up[:, LANES:])
        o = jnp.where(lower, r_lo[:, :LANES], r_up[:, :LANES]) / l
        outs.append((o, (jnp.where(lower, m_lo, m_up) + jnp.log2(l)) * (1.0 / LOG2E)))
    return outs


def _attn_c_kernel(q_ref, k_ref, v_ref, o_ref, lse_ref, bias_ref, *, dil, radius, n_heads):
    length = q_ref.shape[1]
    qb, span, n_blk, _ = _band_shape(length, radius)
    width = n_heads * HEAD_DIM

    @pl.when((pl.program_id(0) == 0) & (pl.program_id(1) == 0))
    def _():
        _fill_band_bias(bias_ref, length, radius, float(dil), n_heads)

    per_iter = ATTN_BLOCKS_PER_ITER if n_blk % ATTN_BLOCKS_PER_ITER == 0 else 1

    def blocks(it, carry):
        geo = [_band_block(it * per_iter + b, length, radius) for b in range(per_iter)]
        slabs = [(q0, start, variant, lo) for q0, start, variant in geo for lo in range(0, q_ref.shape[2], LANES)]
        biases = [bias_ref[variant, (lo % width) // HEAD_DIM + half] for _, _, variant, lo in slabs for half in (0, 1)]
        outs = _attend_pairs([q_ref[0, pl.ds(q0, qb), lo:lo + LANES] for q0, _, _, lo in slabs],
                             [k_ref[0, pl.ds(start, span), lo:lo + LANES] for _, start, _, lo in slabs],
                             [v_ref[0, pl.ds(start, span), lo:lo + LANES] for _, start, _, lo in slabs], biases)
        for (q0, _, _, lo), (o, lse) in zip(slabs, outs):
            o_ref[0, pl.ds(q0, qb), lo:lo + LANES] = o.astype(o_ref.dtype)
            lse_ref[0, pl.ds(q0, qb), lo:lo + LANES] = lse
        return carry

    lax.fori_loop(0, n_blk // per_iter, blocks, 0)


def _attn_c(q, k, v, bsz, window, dil):
    rows, dw = q.shape
    w = dw // dil
    length = rows // bsz
    radius = window // 2 // dil
    qb, span, _, _ = _band_shape(length, radius)
    per_step = max(1, min(dil, ATTN_STEP_ROWS // length))
    assert dil % per_step == 0
    view = lambda a: a.reshape(bsz, length, dw)
    spec = pl.BlockSpec((1, length, per_step * w), lambda i, r: (i, 0, r))
    o, lse = pl.pallas_call(
        functools.partial(_attn_c_kernel, dil=dil, radius=radius, n_heads=w // HEAD_DIM),
        grid=(bsz, dil // per_step),
        in_specs=[spec, spec, spec],
        out_specs=[spec, spec],
        out_shape=[jax.ShapeDtypeStruct((bsz, length, dw), BF16),
                   jax.ShapeDtypeStruct((bsz, length, dw), F32)],
        scratch_shapes=[pltpu.VMEM((3, w // HEAD_DIM, qb, span), F32)],
        compiler_params=_params(2),
        name=f"attn_c_d{dil}",
    )(view(q), view(k), view(v))
    return o.reshape(rows, dw), lse.reshape(rows, dw)


def _attn_d_kernel(sink_ref, q_ref, k_ref, v_ref, y_ref, bias_ref, *, radius, n_heads, group):
    length = q_ref.shape[1]
    qb, span, n_blk, _ = _band_shape(length, radius)
    pairs = n_heads * HEAD_DIM // LANES
    lower = lax.broadcasted_iota(jnp.int32, (1, LANES), 1) < HEAD_DIM

    @pl.when(pl.program_id(0) == 0)
    def _():
        _fill_band_bias(bias_ref, length, radius, 1.0, n_heads)

    per_iter = ATTN_BLOCKS_PER_ITER if n_blk % ATTN_BLOCKS_PER_ITER == 0 else 1

    def blocks(it, carry):
        geo = [_band_block(it * per_iter + b, length, radius) for b in range(per_iter)]
        slabs = [(q0, start, variant, j) for q0, start, variant in geo for j in range(pairs)]
        kv_lo = [(2 * j // group) * LANES for _, _, _, j in slabs]
        biases = [bias_ref[variant, 2 * j + half] for _, _, variant, j in slabs for half in (0, 1)]
        outs = _attend_pairs([q_ref[0, pl.ds(q0, qb), j * LANES:(j + 1) * LANES] for q0, _, _, j in slabs],
                             [k_ref[0, pl.ds(start, span), lo:lo + LANES] for (_, start, _, _), lo in zip(slabs, kv_lo)],
                             [v_ref[0, pl.ds(start, span), lo:lo + LANES] for (_, start, _, _), lo in zip(slabs, kv_lo)],
                             biases)
        for (q0, _, _, j), (o, lse) in zip(slabs, outs):
            sink = jnp.where(lower, sink_ref[2 * j], sink_ref[2 * j + 1])
            y_ref[0, pl.ds(q0, qb), j * LANES:(j + 1) * LANES] = (o * jax.nn.sigmoid(lse - sink)).astype(y_ref.dtype)
        return carry

    lax.fori_loop(0, n_blk // per_iter, blocks, 0)


def _attn_d(q, k, v, sink):
    b, s, w = q.shape
    kvw = k.shape[2]
    n_heads = w // HEAD_DIM
    group = n_heads // D_KV_HEADS
    assert kvw == D_KV_HEADS * LANES and group % 2 == 0
    qb, span, _, _ = _band_shape(s, D_RADIUS)
    return pl.pallas_call(
        functools.partial(_attn_d_kernel, radius=D_RADIUS, n_heads=n_heads, group=group),
        grid=(b,),
        in_specs=[pl.BlockSpec(memory_space=pltpu.SMEM),
                  pl.BlockSpec((1, s, w), lambda i: (i, 0, 0)),
                  pl.BlockSpec((1, s, kvw), lambda i: (i, 0, 0)),
                  pl.BlockSpec((1, s, kvw), lambda i: (i, 0, 0))],
        out_specs=pl.BlockSpec((1, s, w), lambda i: (i, 0, 0)),
        out_shape=jax.ShapeDtypeStruct((b, s, w), BF16),
        scratch_shapes=[pltpu.VMEM((3, n_heads, qb, span), F32)],
        compiler_params=_params(1),
        name="attn_d",
    )(sink.astype(F32), q, k, v)


def _merge_kernel(*refs, dils, cw):
    n_pat = len(dils)
    o_refs, l_refs = refs[:n_pat], refs[n_pat:2 * n_pat]
    yd_ref, cat_ref = refs[2 * n_pat], refs[2 * n_pat + 1]
    scratch = list(refs[2 * n_pat + 2:])
    tm = cat_ref.shape[0]
    chunks = cw // LANES
    staged = []
    for p, dil in enumerate(dils):
        if dil == 1:
            staged.append(None)
            continue
        o_s, l_s = scratch.pop(0), scratch.pop(0)
        n = tm // dil
        for r in range(dil):
            for c in range(chunks):
                lo = r * cw + c * LANES
                o_s[c, pl.ds(r, n, stride=dil), :] = o_refs[p][:, lo:lo + LANES].astype(F32)
                l_s[c, pl.ds(r, n, stride=dil), :] = l_refs[p][:, lo:lo + LANES]
        staged.append((o_s, l_s))
    for c in range(chunks):
        cols = slice(c * LANES, (c + 1) * LANES)
        os_ = [o_refs[p][:, cols].astype(F32) if st is None else st[0][c] for p, st in enumerate(staged)]
        ls_ = [l_refs[p][:, cols] if st is None else st[1][c] for p, st in enumerate(staged)]
        m = functools.reduce(jnp.maximum, ls_)
        es = [jnp.exp(l - m) for l in ls_]
        num = functools.reduce(lambda a, b: a + b, [e * o for e, o in zip(es, os_)])
        den = functools.reduce(lambda a, b: a + b, es)
        cat_ref[:, cols] = (num / den).astype(BF16)
    cat_ref[:, cw:] = yd_ref[...]


def _merge(os_, lses, yd, dils):
    t, dw = yd.shape
    cw = os_[0].shape[1] // dils[0]
    vspecs = [pl.BlockSpec((ROW_TILE // dil, dil * cw), lambda i: (i, 0)) for dil in dils]
    n_scratch = 2 * sum(1 for dil in dils if dil != 1)
    return pl.pallas_call(
        functools.partial(_merge_kernel, dils=dils, cw=cw),
        grid=(t // ROW_TILE,),
        in_specs=vspecs + vspecs + [pl.BlockSpec((ROW_TILE, dw), lambda i: (i, 0))],
        out_specs=pl.BlockSpec((ROW_TILE, cw + dw), lambda i: (i, 0)),
        out_shape=jax.ShapeDtypeStruct((t, cw + dw), BF16),
        scratch_shapes=[pltpu.VMEM((cw // LANES, ROW_TILE, LANES), F32)] * n_scratch,
        compiler_params=_params(1),
        name="merge",
    )(*os_, *lses, yd)


def _layer_norm(z, g, b):
    mu = jnp.mean(z, axis=-1, keepdims=True)
    zc = z - mu
    var = jnp.mean(zc * zc, axis=-1, keepdims=True)
    return zc * lax.rsqrt(var + LN_EPS) * g + b


ROUTE_ROWS = 8


def _route_tile(logits, triu, count_ref):
    n_exp, tm = logits.shape
    sub = lax.broadcasted_iota(jnp.int32, (n_exp, tm), 0)
    out_row = lax.broadcasted_iota(jnp.int32, (ROUTE_ROWS, tm), 0)
    work = logits
    vals, onehots = [], []
    idx_out = jnp.zeros((ROUTE_ROWS, tm), jnp.int32)
    for k in range(TOP_K):
        m = jnp.max(work, axis=0, keepdims=True)
        idx = jnp.min(jnp.where(work == m, sub, n_exp), axis=0, keepdims=True)
        hot = sub == idx
        work = jnp.where(hot, -jnp.inf, work)
        vals.append(m)
        onehots.append(hot)
        idx_out = jnp.where(out_row == k, idx, idx_out)
    exps = [jnp.exp(v - vals[0]) for v in vals]
    den = functools.reduce(lambda a, b: a + b, exps)
    gate_out = jnp.zeros((ROUTE_ROWS, tm), F32)
    for k in range(TOP_K):
        gate_out = jnp.where(out_row == k, exps[k] / den, gate_out)
    multihot = functools.reduce(lambda a, b: a + b, [h.astype(F32) for h in onehots])
    before = _dot(multihot.astype(BF16), triu) + count_ref[...]
    rank_out = jnp.zeros((ROUTE_ROWS, tm), jnp.int32)
    for k in range(TOP_K):
        rank = jnp.sum(jnp.where(onehots[k], before, 0.0), axis=0, keepdims=True)
        rank_out = jnp.where(out_row == k, rank.astype(jnp.int32), rank_out)
    count_ref[...] += jnp.sum(multihot, axis=1, keepdims=True)
    return idx_out, gate_out, rank_out


def _out_ln_kernel(cat_ref, w_ref, x_ref, g_ref, b_ref, rw_hi_ref, rw_lo_ref, rb_ref, triu_ref,
                   x1_ref, x1p_ref, idx_ref, gate_ref, rank_ref, count_ref, *, alpha):
    @pl.when(pl.program_id(0) == 0)
    def _():
        count_ref[...] = jnp.zeros_like(count_ref)

    mix = _dot(cat_ref[...], w_ref[...])
    x1 = _layer_norm(alpha * x_ref[...] + mix, g_ref[...], b_ref[...])
    x1_ref[...] = x1
    x1p_ref[...] = _pack_pairs(x1)
    hi = x1.astype(BF16)
    lo = (x1 - hi.astype(F32)).astype(BF16)
    logits = (_dot_nt(rw_hi_ref[...], hi) + _dot_nt(rw_hi_ref[...], lo) + _dot_nt(rw_lo_ref[...], hi)
              + rb_ref[...])
    idx_ref[...], gate_ref[...], rank_ref[...] = _route_tile(logits, triu_ref[...], count_ref)


def _out_ln(cat, w_out, xt, g, b, router_w, router_b, alpha):
    t, d = xt.shape
    n_exp = router_w.shape[1]
    rw_t = router_w.T
    rw_hi = rw_t.astype(BF16)
    rw_lo = (rw_t - rw_hi.astype(F32)).astype(BF16)
    triu = jnp.triu(jnp.ones((ROW_TILE, ROW_TILE), BF16), 1)
    row = lambda wd: pl.BlockSpec((ROW_TILE, wd), lambda i: (i, 0))
    col = pl.BlockSpec((ROUTE_ROWS, ROW_TILE), lambda i: (0, i))
    full = lambda a: pl.BlockSpec(a.shape, lambda i: (0,) * a.ndim)
    args = (cat, w_out.astype(BF16), xt, g.reshape(1, d), b.reshape(1, d), rw_hi, rw_lo,
            router_b.reshape(n_exp, 1), triu)
    return pl.pallas_call(
        functools.partial(_out_ln_kernel, alpha=alpha),
        grid=(t // ROW_TILE,),
        in_specs=[row(cat.shape[1]), full(args[1]), row(d)] + [full(a) for a in args[3:]],
        out_specs=[row(d), row(d // 2), col, col, col, pl.BlockSpec((n_exp, 1), lambda i: (0, 0))],
        out_shape=[jax.ShapeDtypeStruct((t, d), F32), jax.ShapeDtypeStruct((t, d // 2), jnp.int32),
                   jax.ShapeDtypeStruct((ROUTE_ROWS, t), jnp.int32), jax.ShapeDtypeStruct((ROUTE_ROWS, t), F32),
                   jax.ShapeDtypeStruct((ROUTE_ROWS, t), jnp.int32), jax.ShapeDtypeStruct((n_exp, 1), F32)],
        compiler_params=_params(1),
        name="out_ln",
    )(*args)


def _experts_kernel(first_ref, blocks_ref, count_ref, x_hbm, wgu_ref, bgu_ref, wd_ref, bd_ref, perm_ref,
                    y_hbm, wgu_s, wd_s, x_buf, y_buf, x_sem, y_sem, zero_sem):
    e = pl.program_id(0)
    last = pl.num_programs(0) - 1
    n_blk = blocks_ref[e]
    first = first_ref[e]
    count = count_ref[e]
    used = first_ref[last] + blocks_ref[last]
    half = GU_BLOCK // 2

    def rows_of(g):
        return pl.ds(pl.multiple_of(g * MOE_BLOCK, MOE_BLOCK), MOE_BLOCK)

    def x_copy(g, slot):
        return pltpu.make_async_copy(x_hbm.at[rows_of(g)], x_buf.at[slot], x_sem.at[slot])

    def y_copy(g, slot):
        return pltpu.make_async_copy(y_buf.at[slot], y_hbm.at[rows_of(g)], y_sem.at[slot])

    def compute(g, slot):
        xb = _unpack_pairs(x_buf[slot])
        row = lax.broadcasted_iota(jnp.int32, xb.shape, 0)
        xb = jnp.where(row < count - (g - first) * MOE_BLOCK, xb, jnp.zeros_like(xb))
        h = _dot(xb, wgu_s[...]) + bgu_ref[...]
        acts = []
        for c in range(h.shape[1] // GU_BLOCK):
            glu = jnp.minimum(h[:, c * GU_BLOCK:c * GU_BLOCK + half], SWIGLU_LIMIT)
            lin = jnp.clip(h[:, c * GU_BLOCK + half:(c + 1) * GU_BLOCK], -SWIGLU_LIMIT, SWIGLU_LIMIT)
            acts.append((glu * jax.nn.sigmoid(SWIGLU_ALPHA * glu) * (lin + 1.0)).astype(BF16))
        act = jnp.concatenate(acts, axis=1)
        y_buf[slot] = _pack_pairs(_dot(act, wd_s[...]) + bd_ref[...])

    @pl.when((e == 0) & (used > 0))
    def _():
        x_copy(0, 0).start()

    @pl.when(n_blk > 0)
    def _():
        for c in range(wgu_ref.shape[1] // GU_BLOCK):
            cols = slice(c * GU_BLOCK, (c + 1) * GU_BLOCK)
            wgu_s[:, cols] = _dot(wgu_ref[:, cols].astype(BF16), perm_ref[...]).astype(BF16)
        wd_s[...] = wd_ref[...].astype(BF16)

    def step(g, carry):
        slot = g % 2
        x_copy(g, slot).wait()

        @pl.when(g + 1 < used)
        def _():
            x_copy(g + 1, 1 - slot).start()

        @pl.when(g >= 2)
        def _():
            y_copy(g - 2, slot).wait()

        compute(g, slot)
        y_copy(g, slot).start()
        return carry

    lax.fori_loop(first, first + n_blk, step, 0)

    @pl.when(e == last)
    def _():
        @pl.when(used >= 1)
        def _():
            y_copy(0, (used - 1) % 2).wait()

        @pl.when(used >= 2)
        def _():
            y_copy(0, used % 2).wait()

        total = y_hbm.shape[0] // MOE_BLOCK
        y_buf[0] = jnp.zeros(y_buf.shape[1:], y_buf.dtype)

        def tail_copy(g):
            return pltpu.make_async_copy(y_buf.at[0], y_hbm.at[rows_of(g)], zero_sem)

        def start_all(g, carry):
            tail_copy(g).start()
            return carry

        def wait_all(g, carry):
            tail_copy(g).wait()
            return carry

        lax.fori_loop(used, total, start_all, 0)
        lax.fori_loop(used, total, wait_all, 0)


def _experts(x_rows, first_blk, n_blk, counts, layer, w_gu, b_gu, w_down, b_down):
    rows, dp = x_rows.shape
    _, n_exp, d, de2 = w_gu.shape
    assert de2 % GU_BLOCK == 0 and dp * 2 == d and rows % MOE_BLOCK == 0
    half = GU_BLOCK // 2
    j = jnp.arange(GU_BLOCK)
    src = jnp.where(j < half, 2 * j, 2 * (j - half) + 1)
    perm = (jnp.arange(GU_BLOCK)[:, None] == src[None, :]).astype(BF16)
    wspec = lambda a: pl.BlockSpec((None, None) + a.shape[2:], lambda e, *_: (layer, e, 0, 0))
    grid_spec = pltpu.PrefetchScalarGridSpec(
        num_scalar_prefetch=3,
        grid=(n_exp,),
        in_specs=[pl.BlockSpec(memory_space=pl.ANY),
                  wspec(w_gu), wspec(b_gu), wspec(w_down), wspec(b_down),
                  pl.BlockSpec((GU_BLOCK, GU_BLOCK), lambda e, *_: (0, 0))],
        out_specs=pl.BlockSpec(memory_space=pl.ANY),
        scratch_shapes=[pltpu.VMEM(w_gu.shape[2:], BF16), pltpu.VMEM(w_down.shape[2:], BF16),
                        pltpu.VMEM((2, MOE_BLOCK, dp), jnp.int32), pltpu.VMEM((2, MOE_BLOCK, dp), jnp.int32),
                        pltpu.SemaphoreType.DMA((2,)), pltpu.SemaphoreType.DMA((2,)),
                        pltpu.SemaphoreType.DMA(())],
    )
    return pl.pallas_call(
        _experts_kernel,
        grid_spec=grid_spec,
        out_shape=jax.ShapeDtypeStruct((rows, dp), jnp.int32),
        compiler_params=_params(1),
        name="experts",
    )(first_blk, n_blk, counts, x_rows, w_gu, b_gu, w_down, b_down, perm)


def _combine_ln_kernel(x1_ref, yk_ref, gate_ref, g_ref, b_ref, x2_ref, *, alpha):
    gates = gate_ref[...].T
    ffn = gates[:, 0:1] * _unpack_pairs(yk_ref[0]).astype(F32)
    for k in range(1, yk_ref.shape[0]):
        ffn = ffn + gates[:, k:k + 1] * _unpack_pairs(yk_ref[k]).astype(F32)
    x2_ref[...] = _layer_norm(alpha * x1_ref[...] + ffn, g_ref[...], b_ref[...])


def _combine_ln(x1, yk, gates, g, b, alpha):
    t, d = x1.shape
    k = yk.shape[0]
    tile = COMBINE_TILE if t % COMBINE_TILE == 0 else ROW_TILE
    return pl.pallas_call(
        functools.partial(_combine_ln_kernel, alpha=alpha),
        grid=(t // tile,),
        in_specs=[pl.BlockSpec((tile, d), lambda i: (i, 0)),
                  pl.BlockSpec((k, tile, d // 2), lambda i: (0, i, 0)),
                  pl.BlockSpec((gates.shape[0], tile), lambda i: (0, i)),
                  pl.BlockSpec((1, d), lambda i: (0, 0)),
                  pl.BlockSpec((1, d), lambda i: (0, 0))],
        out_specs=pl.BlockSpec((tile, d), lambda i: (i, 0)),
        out_shape=jax.ShapeDtypeStruct((t, d), F32),
        compiler_params=_params(1),
        name="combine_ln",
    )(x1, yk, gates, g.reshape(1, d), b.reshape(1, d))


def _route(top_idx, rank, counts):
    n_exp = counts.shape[0]
    experts = jnp.arange(n_exp, dtype=jnp.int32)
    counts = counts.reshape(n_exp).astype(jnp.int32)
    padded = (counts + MOE_BLOCK - 1) // MOE_BLOCK * MOE_BLOCK
    pad_end = jnp.cumsum(padded)
    pad_start = pad_end - padded
    start = jnp.sum(jnp.where(top_idx[None] == experts[:, None, None], pad_start[:, None, None], 0), axis=0)
    pos = rank + start
    first_blk = (pad_start // MOE_BLOCK).astype(jnp.int32)
    n_blk = (padded // MOE_BLOCK).astype(jnp.int32)
    return pos, first_blk, n_blk, counts


def _moe(x1, x1p, top_idx, gates, rank, counts, layer, w_gu, b_gu, w_down, b_down, g, b, alpha):
    t, d = x1.shape
    n_exp = counts.shape[0]
    pos, first_blk, n_blk, counts = _route(top_idx[:TOP_K], rank[:TOP_K], counts)
    rows = (-(-(t * TOP_K) // MOE_BLOCK) + n_exp) * MOE_BLOCK
    pos3 = pos.reshape(TOP_K, t // SC_CHUNK, SC_CHUNK).transpose(1, 0, 2)
    x_rows = _sc_scatter_rows(x1p, pos3, rows)
    y = _experts(x_rows, first_blk, n_blk, counts, layer, w_gu, b_gu, w_down, b_down)
    yk = _sc_gather_rows(y, pos.reshape(-1)).reshape(TOP_K, t, d // 2)
    return _combine_ln(x1, yk, gates, g, b, alpha)


def kernel(x, ev_w_in, ev_pool_w, ev_pool_scale, ev_conv_w, ev_w_out, od_w_in, od_sink, od_w_out,
           router_w, router_b, exp_w_gu, exp_b_gu, exp_w_down, exp_b_down, ln_g, ln_b):
    bsz, seq, d = x.shape
    t = bsz * seq
    depth = ln_g.shape[0]
    alpha = (2 * depth) ** 0.25
    n_exp = router_w.shape[2]
    c_width = d // 2
    dq_width = d // 2
    dkv_width = D_KV_HEADS * HEAD_DIM
    assert t % ROW_TILE == 0

    de2 = exp_b_gu.shape[-1]
    b_gu = exp_b_gu.reshape(depth, n_exp, de2 // GU_BLOCK, GU_BLOCK // 2, 2)
    b_gu = jnp.swapaxes(b_gu, -1, -2).reshape(depth, n_exp, 1, de2)
    b_down = exp_b_down.reshape(depth, n_exp, 1, -1)

    xt = x.reshape(t, d)
    for layer in range(depth):
        i = layer // 2
        if layer % 2 == 0:
            cat = _even_mix(xt.reshape(bsz, seq, d), ev_w_in[i], ev_pool_w[i], ev_pool_scale[i], ev_conv_w[i])
            cat = cat.reshape(t, -1)
            w_out = ev_w_out[i]
        else:
            qkv, (qd, kd, vd) = _odd_proj(xt, od_w_in[i], c_width, dq_width, dkv_width)
            dils = tuple(dil for _, dil in C_PATTERNS)
            os_, lses = [], []
            for (window, dil), (qc, kc, vc) in zip(C_PATTERNS, qkv):
                o, lse = _attn_c(qc, kc, vc, bsz, window, dil)
                os_.append(o)
                lses.append(lse)
            seq3 = lambda a: a.reshape(bsz, seq, a.shape[1])
            yd = _attn_d(seq3(qd), seq3(kd), seq3(vd), od_sink[i]).reshape(t, dq_width)
            cat = _merge(os_, lses, yd, dils)
            w_out = od_w_out[i]
        x1, x1p, top_idx, gates, rank, counts = _out_ln(cat, w_out, xt, ln_g[layer, 0], ln_b[layer, 0],
                                                        router_w[layer], router_b[layer], alpha)
        xt = _moe(x1, x1p, top_idx, gates, rank, counts, layer, exp_w_gu, b_gu, exp_w_down, b_down,
                  ln_g[layer, 1], ln_b[layer, 1], alpha)
    return xt.reshape(bsz, seq, d)
```

```python
import functools

import jax
import jax.numpy as jnp
from jax import lax
from jax.experimental import pallas as pl
from jax.experimental.pallas import tpu as pltpu
from jax.experimental.pallas import tpu_sc as plsc

HEAD_DIM = 64
POOL_WINDOWS = (2, 4, 8, 16)
C_PATTERNS = ((128, 1), (512, 4), (2048, 16))
D_KV_HEADS = 2
D_RADIUS = 128
TOP_K = 4
SWIGLU_LIMIT = 7.0
SWIGLU_ALPHA = 1.702
MOE_BLOCK = 512
LN_EPS = 1e-5
NEG_INF = -1e30
LOG2E = 1.4426950408889634

ROW_TILE = 512
OUT_LN_SUBTILES = 2
COMBINE_TILE = 1024
ATTN_Q_BLOCK = 128
ATTN_BLOCKS_PER_ITER = 4
ATTN_STEP_ROWS = 512
LANES = 128
GU_BLOCK = 256
VMEM_LIMIT_BYTES = 56 * 1024 * 1024

F32 = jnp.float32
BF16 = jnp.bfloat16


def _params(n_axes=1):
    return pltpu.CompilerParams(dimension_semantics=("arbitrary",) * n_axes,
                                vmem_limit_bytes=VMEM_LIMIT_BYTES)


def _dot(a, b):
    return jnp.dot(a, b, preferred_element_type=F32)


def _dot_nt(a, b):
    return lax.dot_general(a, b, (((1,), (1,)), ((), ())), preferred_element_type=F32)


def _pack_pairs(a):
    n = a.shape[1] // 2
    bits = lax.bitcast_convert_type(a.astype(BF16).astype(F32), jnp.int32)
    return bits[:, :n] | lax.shift_right_logical(bits[:, n:], 16)


def _unpack_pairs(w):
    hi = lax.bitcast_convert_type(w & jnp.int32(-65536), F32)
    lo = lax.bitcast_convert_type(lax.shift_left(w, 16), F32)
    return jnp.concatenate([hi, lo], axis=1).astype(BF16)


SC_CORES = 2
SC_SUBCORES = 16
SC_WORKERS = SC_CORES * SC_SUBCORES
SC_CHUNK = 64


def _sc_mesh():
    return plsc.VectorSubcoreMesh(core_axis_name="c", subcore_axis_name="s")


def _sc_worker():
    return lax.axis_index("s") * SC_CORES + lax.axis_index("c")


def _sc_gather_rows(table, idx):
    n, w = idx.shape[0], table.shape[1]
    per_w = n // SC_WORKERS
    assert n % SC_WORKERS == 0 and per_w % (2 * SC_CHUNK) == 0

    def body(table_hbm, idx_hbm, out_hbm, i0, i1, r0, r1, s0, s1):
        base = _sc_worker() * per_w

        def start(off, iv, rv, sem):
            pltpu.sync_copy(idx_hbm.at[pl.ds(off, SC_CHUNK)], iv)
            pltpu.async_copy(table_hbm.at[iv], rv, sem)

        def finish(off, iv, rv, sem):
            pltpu.make_async_copy(table_hbm.at[iv], rv, sem).wait()
            pltpu.sync_copy(rv, out_hbm.at[pl.ds(off, SC_CHUNK)])

        start(base, i0, r0, s0)

        @pl.loop(0, per_w, step=2 * SC_CHUNK)
        def _(o):
            off = base + o
            start(off + SC_CHUNK, i1, r1, s1)
            finish(off, i0, r0, s0)

            @pl.when(o + 2 * SC_CHUNK < per_w)
            def _():
                start(off + 2 * SC_CHUNK, i0, r0, s0)

            finish(off + SC_CHUNK, i1, r1, s1)

    return pl.kernel(
        body, mesh=_sc_mesh(),
        out_type=jax.ShapeDtypeStruct((n, w), table.dtype),
        scratch_types=[pltpu.VMEM((SC_CHUNK,), jnp.int32), pltpu.VMEM((SC_CHUNK,), jnp.int32),
                       pltpu.VMEM((SC_CHUNK, w), table.dtype), pltpu.VMEM((SC_CHUNK, w), table.dtype),
                       pltpu.SemaphoreType.DMA, pltpu.SemaphoreType.DMA],
    )(table, idx)


def _sc_scatter_rows(src, pos3, n_out):
    t, w = src.shape
    kk = pos3.shape[1]
    per_w = t // SC_WORKERS
    assert t % SC_WORKERS == 0 and per_w % SC_CHUNK == 0 and pos3.shape == (t // SC_CHUNK, kk, SC_CHUNK)

    def body(src_hbm, pos_hbm, out_hbm, iv, rv, sem):
        base = _sc_worker() * per_w

        @pl.loop(0, per_w, step=SC_CHUNK)
        def _(o):
            off = base + o
            pltpu.sync_copy(pos_hbm.at[off // SC_CHUNK], iv)
            pltpu.sync_copy(src_hbm.at[pl.ds(off, SC_CHUNK)], rv)
            copies = [pltpu.async_copy(rv, out_hbm.at[iv.at[j]], sem) for j in range(kk)]
            for cp in copies:
                cp.wait()

    return pl.kernel(
        body, mesh=_sc_mesh(),
        out_type=jax.ShapeDtypeStruct((n_out, w), src.dtype),
        scratch_types=[pltpu.VMEM((kk, SC_CHUNK), jnp.int32), pltpu.VMEM((SC_CHUNK, w), src.dtype),
                       pltpu.SemaphoreType.DMA],
    )(src, pos3)


def _shift_down(a, k, row):
    return jnp.where(row >= k, pltpu.roll(a, k, axis=0), 0.0)


def _shift_up(a, k, row):
    n = a.shape[0]
    return jnp.where(row < n - k, pltpu.roll(a, n - k, axis=0), 0.0)


def _even_mix_kernel(x_ref, w_in_ref, pool_w_ref, pool_scale_ref, conv_w_ref, cat_ref, xb_ref):
    s = x_ref.shape[1]
    pool_width = pool_scale_ref.shape[1]
    gd = pool_width // len(POOL_WINDOWS)
    conv_width = conv_w_ref.shape[1]
    xb_ref[...] = x_ref[0].astype(BF16)
    xb = xb_ref[...]

    row = lax.broadcasted_iota(jnp.int32, (s, gd), 0)
    for g, w in enumerate(POOL_WINDOWS):
        lo = g * gd
        if g % 2 == 0:
            u2 = _dot(xb, w_in_ref[:, lo:lo + 2 * gd])
        u = u2[:, (g % 2) * gd:(g % 2 + 1) * gd]
        half = w // 2
        back, fwd, span = u, u, 1
        while span < half:
            back = back + _shift_down(back, span, row)
            fwd = fwd + _shift_up(fwd, span, row)
            span *= 2
        win = _shift_down(back, 1, row) + fwd
        cnt = (jnp.minimum(row + (w - half), s) - jnp.maximum(row - half, 0)).astype(F32)
        pooled = win / cnt - u
        mixed = _dot(pooled.astype(BF16), pool_w_ref[g])
        cat_ref[0, :, lo:lo + gd] = (mixed * pool_scale_ref[:, lo:lo + gd]).astype(BF16)

    cw = 256
    rowc = lax.broadcasted_iota(jnp.int32, (s, cw), 0)
    for j in range(conv_width // cw):
        c0 = j * cw
        b_gate = _dot(xb, w_in_ref[:, pool_width + c0:pool_width + c0 + cw])
        c_gate = _dot(xb, w_in_ref[:, pool_width + conv_width + c0:pool_width + conv_width + c0 + cw])
        v = _dot(xb, w_in_ref[:, pool_width + 2 * conv_width + c0:pool_width + 2 * conv_width + c0 + cw])
        u = c_gate * v
        conv = (_shift_down(u, 1, rowc) * conv_w_ref[0:1, c0:c0 + cw] + u * conv_w_ref[1:2, c0:c0 + cw]
                + _shift_up(u, 1, rowc) * conv_w_ref[2:3, c0:c0 + cw])
        cat_ref[0, :, pool_width + c0:pool_width + c0 + cw] = (b_gate * conv).astype(BF16)


def _even_mix(x, w_in, pool_w, pool_scale, conv_w):
    b, s, d = x.shape
    pool_width = pool_scale.shape[0]
    conv_width = conv_w.shape[1]
    assert conv_width % 256 == 0 and w_in.shape[1] == pool_width + 3 * conv_width
    cat_width = pool_width + conv_width
    return pl.pallas_call(
        _even_mix_kernel,
        grid=(b,),
        in_specs=[
            pl.BlockSpec((1, s, d), lambda i: (i, 0, 0)),
            pl.BlockSpec(w_in.shape, lambda i: (0, 0)),
            pl.BlockSpec(pool_w.shape, lambda i: (0, 0, 0)),
            pl.BlockSpec((1, pool_width), lambda i: (0, 0)),
            pl.BlockSpec(conv_w.shape, lambda i: (0, 0)),
        ],
        out_specs=pl.BlockSpec((1, s, cat_width), lambda i: (i, 0, 0)),
        out_shape=jax.ShapeDtypeStruct((b, s, cat_width), BF16),
        scratch_shapes=[pltpu.VMEM((s, d), BF16)],
        compiler_params=_params(1),
        name="even_mix",
    )(x, w_in.astype(BF16), pool_w.astype(BF16), pool_scale.reshape(1, pool_width), conv_w)


def _odd_proj_kernel(x_ref, w_ref, *refs, c_width, dils):
    outs, h_ref = refs[:-1], refs[-1]
    xb = x_ref[...].astype(BF16)
    tm = x_ref.shape[0]
    chunks = c_width // LANES
    for j in range(3):
        h = _dot(xb, w_ref[:, j * c_width:(j + 1) * c_width])
        for c in range(chunks):
            h_ref[j, c] = h[:, c * LANES:(c + 1) * LANES]
        for pi, dil in enumerate(dils):
            n = tm // dil
            for r in range(dil):
                for c in range(chunks):
                    src = h_ref[j, c] if dil == 1 else h_ref[j, c, pl.ds(r, n, stride=dil), :]
                    lo = r * c_width + c * LANES
                    outs[3 * pi + j][:, lo:lo + LANES] = src.astype(BF16)
    c0 = 3 * c_width
    for ref in outs[3 * len(dils):]:
        wd = ref.shape[1]
        ref[...] = _dot(xb, w_ref[:, c0:c0 + wd]).astype(BF16)
        c0 += wd


def _odd_proj(xt, w_in, c_width, dq_width, dkv_width):
    t, d = xt.shape
    dils = tuple(dil for _, dil in C_PATTERNS)
    assert 3 * c_width + dq_width + 2 * dkv_width == w_in.shape[1] and c_width % LANES == 0
    col = jnp.arange(w_in.shape[1])
    is_q = (col < c_width) | ((col >= 3 * c_width) & (col < 3 * c_width + dq_width))
    w_in = w_in * jnp.where(is_q, HEAD_DIM ** -0.5 * LOG2E, 1.0)
    kv0 = 3 * c_width + dq_width
    twice = lambda w: jnp.repeat(w.reshape(d, -1, HEAD_DIM), LANES // HEAD_DIM, axis=1).reshape(d, -1)
    w_in = jnp.concatenate([w_in[:, :kv0], twice(w_in[:, kv0:kv0 + dkv_width]), twice(w_in[:, kv0 + dkv_width:])], axis=1)
    dkv_width = dkv_width * (LANES // HEAD_DIM)
    w_scaled = w_in.astype(BF16)
    shapes = [(t // dil, dil * c_width) for dil in dils for _ in range(3)]
    blocks = [(ROW_TILE // dil, dil * c_width) for dil in dils for _ in range(3)]
    for wd in (dq_width, dkv_width, dkv_width):
        shapes.append((t, wd))
        blocks.append((ROW_TILE, wd))
    outs = pl.pallas_call(
        functools.partial(_odd_proj_kernel, c_width=c_width, dils=dils),
        grid=(t // ROW_TILE,),
        in_specs=[pl.BlockSpec((ROW_TILE, d), lambda i: (i, 0)),
                  pl.BlockSpec(w_in.shape, lambda i: (0, 0))],
        out_specs=[pl.BlockSpec(blk, lambda i: (i, 0)) for blk in blocks],
        out_shape=[jax.ShapeDtypeStruct(shp, BF16) for shp in shapes],
        scratch_shapes=[pltpu.VMEM((3, c_width // LANES, ROW_TILE, LANES), F32)],
        compiler_params=_params(1),
        name="odd_proj",
    )(xt, w_scaled)
    qkv = [outs[3 * pi:3 * pi + 3] for pi in range(len(dils))]
    return qkv, outs[3 * len(dils):]


def _band_shape(length, radius):
    qb = min(ATTN_Q_BLOCK, length)
    span = min(length, qb + 2 * radius)
    n_blk = length // qb
    assert length % qb == 0 and (n_blk <= 2 or radius <= qb)
    return qb, span, n_blk, (0, -radius, qb - span)


def _fill_band_bias(bias_ref, length, radius, dist_unit, n_heads):
    qb, span, _, offsets = _band_shape(length, radius)
    rel = lax.broadcasted_iota(jnp.int32, (qb, span), 1) - lax.broadcasted_iota(jnp.int32, (qb, span), 0)
    for v, off in enumerate(offsets):
        dist = jnp.abs(rel + off)
        mask = jnp.where(dist <= radius, 0.0, NEG_INF).astype(F32)
        far = dist.astype(F32) * (dist_unit * LOG2E)
        for hh in range(n_heads):
            bias_ref[v, hh] = mask - 2.0 ** (-8.0 * (hh + 1) / n_heads) * far


def _band_block(i, length, radius):
    qb, span, n_blk, offsets = _band_shape(length, radius)
    q0 = pl.multiple_of(i * qb, qb)
    variant = jnp.where(i == 0, 0, jnp.where(i == n_blk - 1, 2, 1))
    off = jnp.where(i == 0, offsets[0], jnp.where(i == n_blk - 1, offsets[2], offsets[1]))
    align = 8
    for cand in (128, 64, 32, 16):
        if all(o % cand == 0 for o in offsets) and qb % cand == 0:
            align = cand
            break
    return q0, pl.multiple_of(q0 + off, align), variant


def _attend_pairs(q_slabs, k_slabs, v_slabs, biases):
    lower = lax.broadcasted_iota(jnp.int32, (1, LANES), 1) < HEAD_DIM
    scores = []
    for j, (q2, k2) in enumerate(zip(q_slabs, k_slabs)):
        zero = jnp.zeros_like(q2)
        scores.append(_dot_nt(jnp.where(lower, q2, zero), k2) + biases[2 * j])
        scores.append(_dot_nt(jnp.where(lower, zero, q2), k2) + biases[2 * j + 1])
    probs = []
    for s in scores:
        m = jnp.max(s, axis=-1, keepdims=True)
        probs.append((jnp.exp2(s - m).astype(BF16), m))
    ones = jnp.ones(v_slabs[0].shape, BF16)
    outs = []
    for j, v2 in enumerate(v_slabs):
        (p_lo, m_lo), (p_up, m_up) = probs[2 * j], probs[2 * j + 1]
        v_ones = jnp.concatenate([v2, ones], axis=1)
        r_lo, r_up = _dot(p_lo, v_ones), _dot(p_up, v_ones)
        l = jnp.where(lower, r_lo[:, LANES:], r_up[:, LANES:])
        o = jnp.where(lower, r_lo[:, :LANES], r_up[:, :LANES]) / l
        outs.append((o, (jnp.where(lower, m_lo, m_up) + jnp.log2(l)) * (1.0 / LOG2E)))
    return outs


def _attn_c_kernel(q_ref, k_ref, v_ref, o_ref, lse_ref, bias_ref, *, dil, radius, n_heads):
    length = q_ref.shape[1]
    qb, span, n_blk, _ = _band_shape(length, radius)
    width = n_heads * HEAD_DIM

    @pl.when((pl.program_id(0) == 0) & (pl.program_id(1) == 0))
    def _():
        _fill_band_bias(bias_ref, length, radius, float(dil), n_heads)

    per_iter = ATTN_BLOCKS_PER_ITER if n_blk % ATTN_BLOCKS_PER_ITER == 0 else 1

    def blocks(it, carry):
        geo = [_band_block(it * per_iter + b, length, radius) for b in range(per_iter)]
        slabs = [(q0, start, variant, lo) for q0, start, variant in geo for lo in range(0, q_ref.shape[2], LANES)]
        biases = [bias_ref[variant, (lo % width) // HEAD_DIM + half] for _, _, variant, lo in slabs for half in (0, 1)]
        outs = _attend_pairs([q_ref[0, pl.ds(q0, qb), lo:lo + LANES] for q0, _, _, lo in slabs],
                             [k_ref[0, pl.ds(start, span), lo:lo + LANES] for _, start, _, lo in slabs],
                             [v_ref[0, pl.ds(start, span), lo:lo + LANES] for _, start, _, lo in slabs], biases)
        for (q0, _, _, lo), (o, lse) in zip(slabs, outs):
            o_ref[0, pl.ds(q0, qb), lo:lo + LANES] = o.astype(o_ref.dtype)
            lse_ref[0, pl.ds(q0, qb), lo:lo + LANES] = lse
        return carry

    lax.fori_loop(0, n_blk // per_iter, blocks, 0)


def _attn_c(q, k, v, bsz, window, dil):
    rows, dw = q.shape
    w = dw // dil
    length = rows // bsz
    radius = window // 2 // dil
    qb, span, _, _ = _band_shape(length, radius)
    per_step = max(1, min(dil, ATTN_STEP_ROWS // length))
    assert dil % per_step == 0
    view = lambda a: a.reshape(bsz, length, dw)
    spec = pl.BlockSpec((1, length, per_step * w), lambda i, r: (i, 0, r))
    o, lse = pl.pallas_call(
        functools.partial(_attn_c_kernel, dil=dil, radius=radius, n_heads=w // HEAD_DIM),
        grid=(bsz, dil // per_step),
        in_specs=[spec, spec, spec],
        out_specs=[spec, spec],
        out_shape=[jax.ShapeDtypeStruct((bsz, length, dw), BF16),
                   jax.ShapeDtypeStruct((bsz, length, dw), F32)],
        scratch_shapes=[pltpu.VMEM((3, w // HEAD_DIM, qb, span), F32)],
        compiler_params=_params(2),
        name=f"attn_c_d{dil}",
    )(view(q), view(k), view(v))
    return o.reshape(rows, dw), lse.reshape(rows, dw)


def _attn_d_kernel(sink_ref, q_ref, k_ref, v_ref, y_ref, bias_ref, *, radius, n_heads, group):
    length = q_ref.shape[1]
    qb, span, n_blk, _ = _band_shape(length, radius)
    pairs = n_heads * HEAD_DIM // LANES
    lower = lax.broadcasted_iota(jnp.int32, (1, LANES), 1) < HEAD_DIM

    @pl.when(pl.program_id(0) == 0)
    def _():
        _fill_band_bias(bias_ref, length, radius, 1.0, n_heads)

    per_iter = ATTN_BLOCKS_PER_ITER if n_blk % ATTN_BLOCKS_PER_ITER == 0 else 1

    def blocks(it, carry):
        geo = [_band_block(it * per_iter + b, length, radius) for b in range(per_iter)]
        slabs = [(q0, start, variant, j) for q0, start, variant in geo for j in range(pairs)]
        kv_lo = [(2 * j // group) * LANES for _, _, _, j in slabs]
        biases = [bias_ref[variant, 2 * j + half] for _, _, variant, j in slabs for half in (0, 1)]
        outs = _attend_pairs([q_ref[0, pl.ds(q0, qb), j * LANES:(j + 1) * LANES] for q0, _, _, j in slabs],
                             [k_ref[0, pl.ds(start, span), lo:lo + LANES] for (_, start, _, _), lo in zip(slabs, kv_lo)],
                             [v_ref[0, pl.ds(start, span), lo:lo + LANES] for (_, start, _, _), lo in zip(slabs, kv_lo)],
                             biases)
        for (q0, _, _, j), (o, lse) in zip(slabs, outs):
            sink = jnp.where(lower, sink_ref[2 * j], sink_ref[2 * j + 1])
            y_ref[0, pl.ds(q0, qb), j * LANES:(j + 1) * LANES] = (o * jax.nn.sigmoid(lse - sink)).astype(y_ref.dtype)
        return carry

    lax.fori_loop(0, n_blk // per_iter, blocks, 0)


def _attn_d(q, k, v, sink):
    b, s, w = q.shape
    kvw = k.shape[2]
    n_heads = w // HEAD_DIM
    group = n_heads // D_KV_HEADS
    assert kvw == D_KV_HEADS * LANES and group % 2 == 0
    qb, span, _, _ = _band_shape(s, D_RADIUS)
    return pl.pallas_call(
        functools.partial(_attn_d_kernel, radius=D_RADIUS, n_heads=n_heads, group=group),
        grid=(b,),
        in_specs=[pl.BlockSpec(memory_space=pltpu.SMEM),
                  pl.BlockSpec((1, s, w), lambda i: (i, 0, 0)),
                  pl.BlockSpec((1, s, kvw), lambda i: (i, 0, 0)),
                  pl.BlockSpec((1, s, kvw), lambda i: (i, 0, 0))],
        out_specs=pl.BlockSpec((1, s, w), lambda i: (i, 0, 0)),
        out_shape=jax.ShapeDtypeStruct((b, s, w), BF16),
        scratch_shapes=[pltpu.VMEM((3, n_heads, qb, span), F32)],
        compiler_params=_params(1),
        name="attn_d",
    )(sink.astype(F32), q, k, v)


def _merge_kernel(*refs, dils, cw):
    n_pat = len(dils)
    o_refs, l_refs = refs[:n_pat], refs[n_pat:2 * n_pat]
    yd_ref, cat_ref = refs[2 * n_pat], refs[2 * n_pat + 1]
    scratch = list(refs[2 * n_pat + 2:])
    tm = cat_ref.shape[0]
    chunks = cw // LANES
    staged = []
    for p, dil in enumerate(dils):
        if dil == 1:
            staged.append(None)
            continue
        o_s, l_s = scratch.pop(0), scratch.pop(0)
        n = tm // dil
        for r in range(dil):
            for c in range(chunks):
                lo = r * cw + c * LANES
                o_s[c, pl.ds(r, n, stride=dil), :] = o_refs[p][:, lo:lo + LANES].astype(F32)
                l_s[c, pl.ds(r, n, stride=dil), :] = l_refs[p][:, lo:lo + LANES]
        staged.append((o_s, l_s))
    for c in range(chunks):
        cols = slice(c * LANES, (c + 1) * LANES)
        os_ = [o_refs[p][:, cols].astype(F32) if st is None else st[0][c] for p, st in enumerate(staged)]
        ls_ = [l_refs[p][:, cols] if st is None else st[1][c] for p, st in enumerate(staged)]
        m = functools.reduce(jnp.maximum, ls_)
        es = [jnp.exp(l - m) for l in ls_]
        num = functools.reduce(lambda a, b: a + b, [e * o for e, o in zip(es, os_)])
        den = functools.reduce(lambda a, b: a + b, es)
        cat_ref[:, cols] = (num / den).astype(BF16)
    cat_ref[:, cw:] = yd_ref[...]


def _merge(os_, lses, yd, dils):
    t, dw = yd.shape
    cw = os_[0].shape[1] // dils[0]
    vspecs = [pl.BlockSpec((ROW_TILE // dil, dil * cw), lambda i: (i, 0)) for dil in dils]
    n_scratch = 2 * sum(1 for dil in dils if dil != 1)
    return pl.pallas_call(
        functools.partial(_merge_kernel, dils=dils, cw=cw),
        grid=(t // ROW_TILE,),
        in_specs=vspecs + vspecs + [pl.BlockSpec((ROW_TILE, dw), lambda i: (i, 0))],
        out_specs=pl.BlockSpec((ROW_TILE, cw + dw), lambda i: (i, 0)),
        out_shape=jax.ShapeDtypeStruct((t, cw + dw), BF16),
        scratch_shapes=[pltpu.VMEM((cw // LANES, ROW_TILE, LANES), F32)] * n_scratch,
        compiler_params=_params(1),
        name="merge",
    )(*os_, *lses, yd)


def _layer_norm(z, g, b):
    mu = jnp.mean(z, axis=-1, keepdims=True)
    zc = z - mu
    var = jnp.mean(zc * zc, axis=-1, keepdims=True)
    return zc * lax.rsqrt(var + LN_EPS) * g + b


ROUTE_ROWS = 8


def _route_tile(logits, triu, count_ref):
    n_exp, tm = logits.shape
    sub = lax.broadcasted_iota(jnp.int32, (n_exp, tm), 0)
    out_row = lax.broadcasted_iota(jnp.int32, (ROUTE_ROWS, tm), 0)
    work = logits
    vals, onehots = [], []
    idx_out = jnp.zeros((ROUTE_ROWS, tm), jnp.int32)
    for k in range(TOP_K):
        m = jnp.max(work, axis=0, keepdims=True)
        idx = jnp.min(jnp.where(work == m, sub, n_exp), axis=0, keepdims=True)
        hot = sub == idx
        work = jnp.where(hot, -jnp.inf, work)
        vals.append(m)
        onehots.append(hot)
        idx_out = jnp.where(out_row == k, idx, idx_out)
    exps = [jnp.exp(v - vals[0]) for v in vals]
    den = functools.reduce(lambda a, b: a + b, exps)
    gate_out = jnp.zeros((ROUTE_ROWS, tm), F32)
    for k in range(TOP_K):
        gate_out = jnp.where(out_row == k, exps[k] / den, gate_out)
    multihot = functools.reduce(lambda a, b: a + b, [h.astype(F32) for h in onehots])
    before = _dot(multihot.astype(BF16), triu) + count_ref[...]
    rank_out = jnp.zeros((ROUTE_ROWS, tm), jnp.int32)
    for k in range(TOP_K):
        rank = jnp.sum(jnp.where(onehots[k], before, 0.0), axis=0, keepdims=True)
        rank_out = jnp.where(out_row == k, rank.astype(jnp.int32), rank_out)
    count_ref[...] += jnp.sum(multihot, axis=1, keepdims=True)
    return idx_out, gate_out, rank_out


def _out_ln_kernel(cat_ref, w_ref, x_ref, g_ref, b_ref, rw_hi_ref, rw_lo_ref, rb_ref, triu_ref,
                   x1_ref, x1p_ref, idx_ref, gate_ref, rank_ref, count_ref, *, alpha):
    @pl.when(pl.program_id(0) == 0)
    def _():
        count_ref[...] = jnp.zeros_like(count_ref)

    sub = triu_ref.shape[0]
    tiles = [slice(a, a + sub) for a in range(0, x_ref.shape[0], sub)]
    mixes = [_dot(cat_ref[rows, :], w_ref[...]) for rows in tiles]
    splits = []
    for rows, mix in zip(tiles, mixes):
        x1 = _layer_norm(alpha * x_ref[rows, :] + mix, g_ref[...], b_ref[...])
        x1_ref[rows, :] = x1
        x1p_ref[rows, :] = _pack_pairs(x1)
        hi = x1.astype(BF16)
        splits.append((hi, (x1 - hi.astype(F32)).astype(BF16)))
    logits = [_dot_nt(rw_hi_ref[...], hi) + _dot_nt(rw_hi_ref[...], lo) + _dot_nt(rw_lo_ref[...], hi) + rb_ref[...]
              for hi, lo in splits]
    for rows, lg in zip(tiles, logits):
        idx_ref[:, rows], gate_ref[:, rows], rank_ref[:, rows] = _route_tile(lg, triu_ref[...], count_ref)


def _out_ln(cat, w_out, xt, g, b, router_w, router_b, alpha):
    t, d = xt.shape
    n_exp = router_w.shape[1]
    rw_t = router_w.T
    rw_hi = rw_t.astype(BF16)
    rw_lo = (rw_t - rw_hi.astype(F32)).astype(BF16)
    triu = jnp.triu(jnp.ones((ROW_TILE, ROW_TILE), BF16), 1)
    step = OUT_LN_SUBTILES * ROW_TILE if t % (OUT_LN_SUBTILES * ROW_TILE) == 0 else ROW_TILE
    row = lambda wd: pl.BlockSpec((step, wd), lambda i: (i, 0))
    col = pl.BlockSpec((ROUTE_ROWS, step), lambda i: (0, i))
    full = lambda a: pl.BlockSpec(a.shape, lambda i: (0,) * a.ndim)
    args = (cat, w_out.astype(BF16), xt, g.reshape(1, d), b.reshape(1, d), rw_hi, rw_lo,
            router_b.reshape(n_exp, 1), triu)
    return pl.pallas_call(
        functools.partial(_out_ln_kernel, alpha=alpha),
        grid=(t // step,),
        in_specs=[row(cat.shape[1]), full(args[1]), row(d)] + [full(a) for a in args[3:]],
        out_specs=[row(d), row(d // 2), col, col, col, pl.BlockSpec((n_exp, 1), lambda i: (0, 0))],
        out_shape=[jax.ShapeDtypeStruct((t, d), F32), jax.ShapeDtypeStruct((t, d // 2), jnp.int32),
                   jax.ShapeDtypeStruct((ROUTE_ROWS, t), jnp.int32), jax.ShapeDtypeStruct((ROUTE_ROWS, t), F32),
                   jax.ShapeDtypeStruct((ROUTE_ROWS, t), jnp.int32), jax.ShapeDtypeStruct((n_exp, 1), F32)],
        compiler_params=_params(1),
        name="out_ln",
    )(*args)


def _experts_kernel(first_ref, blocks_ref, count_ref, x_hbm, wgu_ref, bgu_ref, wd_ref, bd_ref, perm_ref,
                    y_hbm, wgu_s, wd_s, x_buf, y_buf, x_sem, y_sem, zero_sem):
    e = pl.program_id(0)
    last = pl.num_programs(0) - 1
    n_blk = blocks_ref[e]
    first = first_ref[e]
    count = count_ref[e]
    used = first_ref[last] + blocks_ref[last]
    half = GU_BLOCK // 2

    def rows_of(g):
        return pl.ds(pl.multiple_of(g * MOE_BLOCK, MOE_BLOCK), MOE_BLOCK)

    def x_copy(g, slot):
        return pltpu.make_async_copy(x_hbm.at[rows_of(g)], x_buf.at[slot], x_sem.at[slot])

    def y_copy(g, slot):
        return pltpu.make_async_copy(y_buf.at[slot], y_hbm.at[rows_of(g)], y_sem.at[slot])

    def compute(g, slot):
        xb = _unpack_pairs(x_buf[slot])
        row = lax.broadcasted_iota(jnp.int32, xb.shape, 0)
        xb = jnp.where(row < count - (g - first) * MOE_BLOCK, xb, jnp.zeros_like(xb))
        h = _dot(xb, wgu_s[...]) + bgu_ref[...]
        acts = []
        for c in range(h.shape[1] // GU_BLOCK):
            glu = jnp.minimum(h[:, c * GU_BLOCK:c * GU_BLOCK + half], SWIGLU_LIMIT)
            lin = jnp.clip(h[:, c * GU_BLOCK + half:(c + 1) * GU_BLOCK], -SWIGLU_LIMIT, SWIGLU_LIMIT)
            acts.append((glu * jax.nn.sigmoid(SWIGLU_ALPHA * glu) * (lin + 1.0)).astype(BF16))
        act = jnp.concatenate(acts, axis=1)
        y_buf[slot] = _pack_pairs(_dot(act, wd_s[...]) + bd_ref[...])

    @pl.when((e == 0) & (used > 0))
    def _():
        x_copy(0, 0).start()

    @pl.when(n_blk > 0)
    def _():
        for c in range(wgu_ref.shape[1] // GU_BLOCK):
            cols = slice(c * GU_BLOCK, (c + 1) * GU_BLOCK)
            wgu_s[:, cols] = _dot(wgu_ref[:, cols].astype(BF16), perm_ref[...]).astype(BF16)
        wd_s[...] = wd_ref[...].astype(BF16)

    def step(g, carry):
        slot = g % 2
        x_copy(g, slot).wait()

        @pl.when(g + 1 < used)
        def _():
            x_copy(g + 1, 1 - slot).start()

        @pl.when(g >= 2)
        def _():
            y_copy(g - 2, slot).wait()

        compute(g, slot)
        y_copy(g, slot).start()
        return carry

    lax.fori_loop(first, first + n_blk, step, 0)

    @pl.when(e == last)
    def _():
        @pl.when(used >= 1)
        def _():
            y_copy(0, (used - 1) % 2).wait()

        @pl.when(used >= 2)
        def _():
            y_copy(0, used % 2).wait()

        total = y_hbm.shape[0] // MOE_BLOCK
        y_buf[0] = jnp.zeros(y_buf.shape[1:], y_buf.dtype)

        def tail_copy(g):
            return pltpu.make_async_copy(y_buf.at[0], y_hbm.at[rows_of(g)], zero_sem)

        def start_all(g, carry):
            tail_copy(g).start()
            return carry

        def wait_all(g, carry):
            tail_copy(g).wait()
            return carry

        lax.fori_loop(used, total, start_all, 0)
        lax.fori_loop(used, total, wait_all, 0)


def _experts(x_rows, first_blk, n_blk, counts, layer, w_gu, b_gu, w_down, b_down):
    rows, dp = x_rows.shape
    _, n_exp, d, de2 = w_gu.shape
    assert de2 % GU_BLOCK == 0 and dp * 2 == d and rows % MOE_BLOCK == 0
    half = GU_BLOCK // 2
    j = jnp.arange(GU_BLOCK)
    src = jnp.where(j < half, 2 * j, 2 * (j - half) + 1)
    perm = (jnp.arange(GU_BLOCK)[:, None] == src[None, :]).astype(BF16)
    wspec = lambda a: pl.BlockSpec((None, None) + a.shape[2:], lambda e, *_: (layer, e, 0, 0))
    grid_spec = pltpu.PrefetchScalarGridSpec(
        num_scalar_prefetch=3,
        grid=(n_exp,),
        in_specs=[pl.BlockSpec(memory_space=pl.ANY),
                  wspec(w_gu), wspec(b_gu), wspec(w_down), wspec(b_down),
                  pl.BlockSpec((GU_BLOCK, GU_BLOCK), lambda e, *_: (0, 0))],
        out_specs=pl.BlockSpec(memory_space=pl.ANY),
        scratch_shapes=[pltpu.VMEM(w_gu.shape[2:], BF16), pltpu.VMEM(w_down.shape[2:], BF16),
                        pltpu.VMEM((2, MOE_BLOCK, dp), jnp.int32), pltpu.VMEM((2, MOE_BLOCK, dp), jnp.int32),
                        pltpu.SemaphoreType.DMA((2,)), pltpu.SemaphoreType.DMA((2,)),
                        pltpu.SemaphoreType.DMA(())],
    )
    return pl.pallas_call(
        _experts_kernel,
        grid_spec=grid_spec,
        out_shape=jax.ShapeDtypeStruct((rows, dp), jnp.int32),
        compiler_params=_params(1),
        name="experts",
    )(first_blk, n_blk, counts, x_rows, w_gu, b_gu, w_down, b_down, perm)


def _combine_ln_kernel(x1_ref, yk_ref, gate_ref, g_ref, b_ref, x2_ref, *, alpha):
    gates = gate_ref[...].T
    ffn = gates[:, 0:1] * _unpack_pairs(yk_ref[0]).astype(F32)
    for k in range(1, yk_ref.shape[0]):
        ffn = ffn + gates[:, k:k + 1] * _unpack_pairs(yk_ref[k]).astype(F32)
    x2_ref[...] = _layer_norm(alpha * x1_ref[...] + ffn, g_ref[...], b_ref[...])


def _combine_ln(x1, yk, gates, g, b, alpha):
    t, d = x1.shape
    k = yk.shape[0]
    tile = COMBINE_TILE if t % COMBINE_TILE == 0 else ROW_TILE
    return pl.pallas_call(
        functools.partial(_combine_ln_kernel, alpha=alpha),
        grid=(t // tile,),
        in_specs=[pl.BlockSpec((tile, d), lambda i: (i, 0)),
                  pl.BlockSpec((k, tile, d // 2), lambda i: (0, i, 0)),
                  pl.BlockSpec((gates.shape[0], tile), lambda i: (0, i)),
                  pl.BlockSpec((1, d), lambda i: (0, 0)),
                  pl.BlockSpec((1, d), lambda i: (0, 0))],
        out_specs=pl.BlockSpec((tile, d), lambda i: (i, 0)),
        out_shape=jax.ShapeDtypeStruct((t, d), F32),
        compiler_params=_params(1),
        name="combine_ln",
    )(x1, yk, gates, g.reshape(1, d), b.reshape(1, d))


def _route(top_idx, rank, counts):
    n_exp = counts.shape[0]
    experts = jnp.arange(n_exp, dtype=jnp.int32)
    counts = counts.reshape(n_exp).astype(jnp.int32)
    padded = (counts + MOE_BLOCK - 1) // MOE_BLOCK * MOE_BLOCK
    pad_end = jnp.cumsum(padded)
    pad_start = pad_end - padded
    start = jnp.sum(jnp.where(top_idx[None] == experts[:, None, None], pad_start[:, None, None], 0), axis=0)
    pos = rank + start
    first_blk = (pad_start // MOE_BLOCK).astype(jnp.int32)
    n_blk = (padded // MOE_BLOCK).astype(jnp.int32)
    return pos, first_blk, n_blk, counts


def _moe(x1, x1p, top_idx, gates, rank, counts, layer, w_gu, b_gu, w_down, b_down, g, b, alpha):
    t, d = x1.shape
    n_exp = counts.shape[0]
    pos, first_blk, n_blk, counts = _route(top_idx[:TOP_K], rank[:TOP_K], counts)
    rows = (-(-(t * TOP_K) // MOE_BLOCK) + n_exp) * MOE_BLOCK
    pos3 = pos.reshape(TOP_K, t // SC_CHUNK, SC_CHUNK).transpose(1, 0, 2)
    x_rows = _sc_scatter_rows(x1p, pos3, rows)
    y = _experts(x_rows, first_blk, n_blk, counts, layer, w_gu, b_gu, w_down, b_down)
    yk = _sc_gather_rows(y, pos.reshape(-1)).reshape(TOP_K, t, d // 2)
    return _combine_ln(x1, yk, gates, g, b, alpha)


def kernel(x, ev_w_in, ev_pool_w, ev_pool_scale, ev_conv_w, ev_w_out, od_w_in, od_sink, od_w_out,
           router_w, router_b, exp_w_gu, exp_b_gu, exp_w_down, exp_b_down, ln_g, ln_b):
    bsz, seq, d = x.shape
    t = bsz * seq
    depth = ln_g.shape[0]
    alpha = (2 * depth) ** 0.25
    n_exp = router_w.shape[2]
    c_width = d // 2
    dq_width = d // 2
    dkv_width = D_KV_HEADS * HEAD_DIM
    assert t % ROW_TILE == 0

    de2 = exp_b_gu.shape[-1]
    b_gu = exp_b_gu.reshape(depth, n_exp, de2 // GU_BLOCK, GU_BLOCK // 2, 2)
    b_gu = jnp.swapaxes(b_gu, -1, -2).reshape(depth, n_exp, 1, de2)
    b_down = exp_b_down.reshape(depth, n_exp, 1, -1)

    xt = x.reshape(t, d)
    for layer in range(depth):
        i = layer // 2
        if layer % 2 == 0:
            cat = _even_mix(xt.reshape(bsz, seq, d), ev_w_in[i], ev_pool_w[i], ev_pool_scale[i], ev_conv_w[i])
            cat = cat.reshape(t, -1)
            w_out = ev_w_out[i]
        else:
            qkv, (qd, kd, vd) = _odd_proj(xt, od_w_in[i], c_width, dq_width, dkv_width)
            dils = tuple(dil for _, dil in C_PATTERNS)
            os_, lses = [], []
            for (window, dil), (qc, kc, vc) in zip(C_PATTERNS, qkv):
                o, lse = _attn_c(qc, kc, vc, bsz, window, dil)
                os_.append(o)
                lses.append(lse)
            seq3 = lambda a: a.reshape(bsz, seq, a.shape[1])
            yd = _attn_d(seq3(qd), seq3(kd), seq3(vd), od_sink[i]).reshape(t, dq_width)
            cat = _merge(os_, lses, yd, dils)
            w_out = od_w_out[i]
        x1, x1p, top_idx, gates, rank, counts = _out_ln(cat, w_out, xt, ln_g[layer, 0], ln_b[layer, 0],
                                                        router_w[layer], router_b[layer], alpha)
        xt = _moe(x1, x1p, top_idx, gates, rank, counts, layer, exp_w_gu, b_gu, exp_w_down, b_down,
                  ln_g[layer, 1], ln_b[layer, 1], alpha)
    return xt.reshape(bsz, seq, d)
```

```python
import functools

import jax
import jax.numpy as jnp
from jax import lax
from jax.experimental import pallas as pl
from jax.experimental.pallas import tpu as pltpu
from jax.experimental.pallas import tpu_sc as plsc

HEAD_DIM = 64
POOL_WINDOWS = (2, 4, 8, 16)
C_PATTERNS = ((128, 1), (512, 4), (2048, 16))
D_KV_HEADS = 2
D_RADIUS = 128
TOP_K = 4
SWIGLU_LIMIT = 7.0
SWIGLU_ALPHA = 1.702
MOE_BLOCK = 512
LN_EPS = 1e-5
NEG_INF = -1e30
LOG2E = 1.4426950408889634

ROW_TILE = 512
ODD_PROJ_TILE = 1024
OUT_LN_SUBTILES = 2
COMBINE_TILE = 1024
ATTN_Q_BLOCK = 128
ATTN_BLOCKS_PER_ITER = 8
ATTN_STEP_ROWS = 512
LANES = 128
GU_BLOCK = 256
VMEM_LIMIT_BYTES = 56 * 1024 * 1024

F32 = jnp.float32
BF16 = jnp.bfloat16


def _params(n_axes=1):
    return pltpu.CompilerParams(dimension_semantics=("arbitrary",) * n_axes,
                                vmem_limit_bytes=VMEM_LIMIT_BYTES)


def _dot(a, b):
    return jnp.dot(a, b, preferred_element_type=F32)


def _dot_nt(a, b):
    return lax.dot_general(a, b, (((1,), (1,)), ((), ())), preferred_element_type=F32)


def _pack_pairs(a):
    n = a.shape[1] // 2
    bits = lax.bitcast_convert_type(a.astype(BF16).astype(F32), jnp.int32)
    return bits[:, :n] | lax.shift_right_logical(bits[:, n:], 16)


def _unpack_pairs(w):
    hi = lax.bitcast_convert_type(w & jnp.int32(-65536), F32)
    lo = lax.bitcast_convert_type(lax.shift_left(w, 16), F32)
    return jnp.concatenate([hi, lo], axis=1).astype(BF16)


SC_CORES = 2
SC_SUBCORES = 16
SC_WORKERS = SC_CORES * SC_SUBCORES
SC_CHUNK = 64


def _sc_mesh():
    return plsc.VectorSubcoreMesh(core_axis_name="c", subcore_axis_name="s")


def _sc_worker():
    return lax.axis_index("s") * SC_CORES + lax.axis_index("c")


def _sc_gather_rows(table, idx):
    n, w = idx.shape[0], table.shape[1]
    per_w = n // SC_WORKERS
    assert n % SC_WORKERS == 0 and per_w % (2 * SC_CHUNK) == 0

    def body(table_hbm, idx_hbm, out_hbm, i0, i1, r0, r1, s0, s1):
        base = _sc_worker() * per_w

        def start(off, iv, rv, sem):
            pltpu.sync_copy(idx_hbm.at[pl.ds(off, SC_CHUNK)], iv)
            pltpu.async_copy(table_hbm.at[iv], rv, sem)

        def finish(off, iv, rv, sem):
            pltpu.make_async_copy(table_hbm.at[iv], rv, sem).wait()
            pltpu.sync_copy(rv, out_hbm.at[pl.ds(off, SC_CHUNK)])

        start(base, i0, r0, s0)

        @pl.loop(0, per_w, step=2 * SC_CHUNK)
        def _(o):
            off = base + o
            start(off + SC_CHUNK, i1, r1, s1)
            finish(off, i0, r0, s0)

            @pl.when(o + 2 * SC_CHUNK < per_w)
            def _():
                start(off + 2 * SC_CHUNK, i0, r0, s0)

            finish(off + SC_CHUNK, i1, r1, s1)

    return pl.kernel(
        body, mesh=_sc_mesh(),
        out_type=jax.ShapeDtypeStruct((n, w), table.dtype),
        scratch_types=[pltpu.VMEM((SC_CHUNK,), jnp.int32), pltpu.VMEM((SC_CHUNK,), jnp.int32),
                       pltpu.VMEM((SC_CHUNK, w), table.dtype), pltpu.VMEM((SC_CHUNK, w), table.dtype),
                       pltpu.SemaphoreType.DMA, pltpu.SemaphoreType.DMA],
    )(table, idx)


def _sc_scatter_rows(src, pos3, n_out):
    t, w = src.shape
    kk = pos3.shape[1]
    per_w = t // SC_WORKERS
    assert t % SC_WORKERS == 0 and per_w % SC_CHUNK == 0 and pos3.shape == (t // SC_CHUNK, kk, SC_CHUNK)

    def body(src_hbm, pos_hbm, out_hbm, iv, rv, sem):
        base = _sc_worker() * per_w

        @pl.loop(0, per_w, step=SC_CHUNK)
        def _(o):
            off = base + o
            pltpu.sync_copy(pos_hbm.at[off // SC_CHUNK], iv)
            pltpu.sync_copy(src_hbm.at[pl.ds(off, SC_CHUNK)], rv)
            copies = [pltpu.async_copy(rv, out_hbm.at[iv.at[j]], sem) for j in range(kk)]
            for cp in copies:
                cp.wait()

    return pl.kernel(
        body, mesh=_sc_mesh(),
        out_type=jax.ShapeDtypeStruct((n_out, w), src.dtype),
        scratch_types=[pltpu.VMEM((kk, SC_CHUNK), jnp.int32), pltpu.VMEM((SC_CHUNK, w), src.dtype),
                       pltpu.SemaphoreType.DMA],
    )(src, pos3)


def _shift_down(a, k, row):
    return jnp.where(row >= k, pltpu.roll(a, k, axis=0), 0.0)


def _shift_up(a, k, row):
    n = a.shape[0]
    return jnp.where(row < n - k, pltpu.roll(a, n - k, axis=0), 0.0)


def _even_mix_kernel(x_ref, w_in_ref, pool_w_ref, pool_scale_ref, conv_w_ref, cat_ref, xb_ref):
    s = x_ref.shape[1]
    pool_width = pool_scale_ref.shape[1]
    gd = pool_width // len(POOL_WINDOWS)
    conv_width = conv_w_ref.shape[1]
    xb_ref[...] = x_ref[0].astype(BF16)
    xb = xb_ref[...]

    row = lax.broadcasted_iota(jnp.int32, (s, gd), 0)
    for g, w in enumerate(POOL_WINDOWS):
        lo = g * gd
        if g % 2 == 0:
            u2 = _dot(xb, w_in_ref[:, lo:lo + 2 * gd])
        u = u2[:, (g % 2) * gd:(g % 2 + 1) * gd]
        half = w // 2
        back, fwd, span = u, u, 1
        while span < half:
            back = back + _shift_down(back, span, row)
            fwd = fwd + _shift_up(fwd, span, row)
            span *= 2
        win = _shift_down(back, 1, row) + fwd
        cnt = (jnp.minimum(row + (w - half), s) - jnp.maximum(row - half, 0)).astype(F32)
        pooled = win / cnt - u
        mixed = _dot(pooled.astype(BF16), pool_w_ref[g])
        cat_ref[0, :, lo:lo + gd] = (mixed * pool_scale_ref[:, lo:lo + gd]).astype(BF16)

    cw = 256
    rowc = lax.broadcasted_iota(jnp.int32, (s, cw), 0)
    for j in range(conv_width // cw):
        c0 = j * cw
        b_gate = _dot(xb, w_in_ref[:, pool_width + c0:pool_width + c0 + cw])
        c_gate = _dot(xb, w_in_ref[:, pool_width + conv_width + c0:pool_width + conv_width + c0 + cw])
        v = _dot(xb, w_in_ref[:, pool_width + 2 * conv_width + c0:pool_width + 2 * conv_width + c0 + cw])
        u = c_gate * v
        conv = (_shift_down(u, 1, rowc) * conv_w_ref[0:1, c0:c0 + cw] + u * conv_w_ref[1:2, c0:c0 + cw]
                + _shift_up(u, 1, rowc) * conv_w_ref[2:3, c0:c0 + cw])
        cat_ref[0, :, pool_width + c0:pool_width + c0 + cw] = (b_gate * conv).astype(BF16)


def _even_mix(x, w_in, pool_w, pool_scale, conv_w):
    b, s, d = x.shape
    pool_width = pool_scale.shape[0]
    conv_width = conv_w.shape[1]
    assert conv_width % 256 == 0 and w_in.shape[1] == pool_width + 3 * conv_width
    cat_width = pool_width + conv_width
    return pl.pallas_call(
        _even_mix_kernel,
        grid=(b,),
        in_specs=[
            pl.BlockSpec((1, s, d), lambda i: (i, 0, 0)),
            pl.BlockSpec(w_in.shape, lambda i: (0, 0)),
            pl.BlockSpec(pool_w.shape, lambda i: (0, 0, 0)),
            pl.BlockSpec((1, pool_width), lambda i: (0, 0)),
            pl.BlockSpec(conv_w.shape, lambda i: (0, 0)),
        ],
        out_specs=pl.BlockSpec((1, s, cat_width), lambda i: (i, 0, 0)),
        out_shape=jax.ShapeDtypeStruct((b, s, cat_width), BF16),
        scratch_shapes=[pltpu.VMEM((s, d), BF16)],
        compiler_params=_params(1),
        name="even_mix",
    )(x, w_in.astype(BF16), pool_w.astype(BF16), pool_scale.reshape(1, pool_width), conv_w)


def _odd_proj_kernel(x_ref, w_ref, *refs, c_width, dils):
    outs, h_ref = refs[:-1], refs[-1]
    xb = x_ref[...].astype(BF16)
    tm = x_ref.shape[0]
    chunks = c_width // LANES
    for j in range(3):
        h = _dot(xb, w_ref[:, j * c_width:(j + 1) * c_width])
        for c in range(chunks):
            h_ref[j, c] = h[:, c * LANES:(c + 1) * LANES]
        for pi, dil in enumerate(dils):
            n = tm // dil
            for r in range(dil):
                for c in range(chunks):
                    src = h_ref[j, c] if dil == 1 else h_ref[j, c, pl.ds(r, n, stride=dil), :]
                    lo = r * c_width + c * LANES
                    outs[3 * pi + j][:, lo:lo + LANES] = src.astype(BF16)
    c0 = 3 * c_width
    for ref in outs[3 * len(dils):]:
        wd = ref.shape[1]
        ref[...] = _dot(xb, w_ref[:, c0:c0 + wd]).astype(BF16)
        c0 += wd


def _odd_proj(xt, w_in, c_width, dq_width, dkv_width):
    t, d = xt.shape
    dils = tuple(dil for _, dil in C_PATTERNS)
    tile = ODD_PROJ_TILE if t % ODD_PROJ_TILE == 0 else ROW_TILE
    assert 3 * c_width + dq_width + 2 * dkv_width == w_in.shape[1] and c_width % LANES == 0
    col = jnp.arange(w_in.shape[1])
    is_q = (col < c_width) | ((col >= 3 * c_width) & (col < 3 * c_width + dq_width))
    w_in = w_in * jnp.where(is_q, HEAD_DIM ** -0.5 * LOG2E, 1.0)
    kv0 = 3 * c_width + dq_width
    twice = lambda w: jnp.repeat(w.reshape(d, -1, HEAD_DIM), LANES // HEAD_DIM, axis=1).reshape(d, -1)
    w_in = jnp.concatenate([w_in[:, :kv0], twice(w_in[:, kv0:kv0 + dkv_width]), twice(w_in[:, kv0 + dkv_width:])], axis=1)
    dkv_width = dkv_width * (LANES // HEAD_DIM)
    w_scaled = w_in.astype(BF16)
    shapes = [(t // dil, dil * c_width) for dil in dils for _ in range(3)]
    blocks = [(tile // dil, dil * c_width) for dil in dils for _ in range(3)]
    for wd in (dq_width, dkv_width, dkv_width):
        shapes.append((t, wd))
        blocks.append((tile, wd))
    outs = pl.pallas_call(
        functools.partial(_odd_proj_kernel, c_width=c_width, dils=dils),
        grid=(t // tile,),
        in_specs=[pl.BlockSpec((tile, d), lambda i: (i, 0)),
                  pl.BlockSpec(w_in.shape, lambda i: (0, 0))],
        out_specs=[pl.BlockSpec(blk, lambda i: (i, 0)) for blk in blocks],
        out_shape=[jax.ShapeDtypeStruct(shp, BF16) for shp in shapes],
        scratch_shapes=[pltpu.VMEM((3, c_width // LANES, tile, LANES), F32)],
        compiler_params=_params(1),
        name="odd_proj",
    )(xt, w_scaled)
    qkv = [outs[3 * pi:3 * pi + 3] for pi in range(len(dils))]
    return qkv, outs[3 * len(dils):]


def _band_shape(length, radius):
    qb = min(ATTN_Q_BLOCK, length)
    span = min(length, qb + 2 * radius)
    n_blk = length // qb
    assert length % qb == 0 and (n_blk <= 2 or radius <= qb)
    return qb, span, n_blk, (0, -radius, qb - span)


def _fill_band_bias(bias_ref, length, radius, dist_unit, n_heads):
    qb, span, _, offsets = _band_shape(length, radius)
    rel = lax.broadcasted_iota(jnp.int32, (qb, span), 1) - lax.broadcasted_iota(jnp.int32, (qb, span), 0)
    for v, off in enumerate(offsets):
        dist = jnp.abs(rel + off)
        mask = jnp.where(dist <= radius, 0.0, NEG_INF).astype(F32)
        far = dist.astype(F32) * (dist_unit * LOG2E)
        for hh in range(n_heads):
            bias_ref[v, hh] = mask - 2.0 ** (-8.0 * (hh + 1) / n_heads) * far


def _band_block(i, length, radius):
    qb, span, n_blk, offsets = _band_shape(length, radius)
    q0 = pl.multiple_of(i * qb, qb)
    variant = jnp.where(i == 0, 0, jnp.where(i == n_blk - 1, 2, 1))
    off = jnp.where(i == 0, offsets[0], jnp.where(i == n_blk - 1, offsets[2], offsets[1]))
    align = 8
    for cand in (128, 64, 32, 16):
        if all(o % cand == 0 for o in offsets) and qb % cand == 0:
            align = cand
            break
    return q0, pl.multiple_of(q0 + off, align), variant


def _attend_pairs(q_slabs, k_slabs, v_slabs, biases):
    lower = lax.broadcasted_iota(jnp.int32, (1, LANES), 1) < HEAD_DIM
    scores = []
    for j, (q2, k2) in enumerate(zip(q_slabs, k_slabs)):
        zero = jnp.zeros_like(q2)
        scores.append(_dot_nt(jnp.where(lower, q2, zero), k2) + biases[2 * j])
        scores.append(_dot_nt(jnp.where(lower, zero, q2), k2) + biases[2 * j + 1])
    probs = []
    for s in scores:
        m = jnp.max(s, axis=-1, keepdims=True)
        probs.append((jnp.exp2(s - m).astype(BF16), m))
    ones = jnp.ones(v_slabs[0].shape, BF16)
    outs = []
    for j, v2 in enumerate(v_slabs):
        (p_lo, m_lo), (p_up, m_up) = probs[2 * j], probs[2 * j + 1]
        v_ones = jnp.concatenate([v2, ones], axis=1)
        r_lo, r_up = _dot(p_lo, v_ones), _dot(p_up, v_ones)
        l = jnp.where(lower, r_lo[:, LANES:], r_up[:, LANES:])
        o = jnp.where(lower, r_lo[:, :LANES], r_up[:, :LANES]) / l
        outs.append((o, (jnp.where(lower, m_lo, m_up) + jnp.log2(l)) * (1.0 / LOG2E)))
    return outs


def _attn_c_kernel(q_ref, k_ref, v_ref, o_ref, lse_ref, bias_ref, *, dil, radius, n_heads):
    length = q_ref.shape[1]
    qb, span, n_blk, _ = _band_shape(length, radius)
    width = n_heads * HEAD_DIM

    @pl.when((pl.program_id(0) == 0) & (pl.program_id(1) == 0))
    def _():
        _fill_band_bias(bias_ref, length, radius, float(dil), n_heads)

    per_iter = max(p for p in range(1, ATTN_BLOCKS_PER_ITER + 1) if n_blk % p == 0)

    def blocks(it, carry):
        geo = [_band_block(it * per_iter + b, length, radius) for b in range(per_iter)]
        slabs = [(q0, start, variant, lo) for q0, start, variant in geo for lo in range(0, q_ref.shape[2], LANES)]
        biases = [bias_ref[variant, (lo % width) // HEAD_DIM + half] for _, _, variant, lo in slabs for half in (0, 1)]
        outs = _attend_pairs([q_ref[0, pl.ds(q0, qb), lo:lo + LANES] for q0, _, _, lo in slabs],
                             [k_ref[0, pl.ds(start, span), lo:lo + LANES] for _, start, _, lo in slabs],
                             [v_ref[0, pl.ds(start, span), lo:lo + LANES] for _, start, _, lo in slabs], biases)
        for (q0, _, _, lo), (o, lse) in zip(slabs, outs):
            o_ref[0, pl.ds(q0, qb), lo:lo + LANES] = o.astype(o_ref.dtype)
            lse_ref[0, pl.ds(q0, qb), lo:lo + LANES] = lse
        return carry

    lax.fori_loop(0, n_blk // per_iter, blocks, 0)


def _attn_c(q, k, v, bsz, window, dil):
    rows, dw = q.shape
    w = dw // dil
    length = rows // bsz
    radius = window // 2 // dil
    qb, span, _, _ = _band_shape(length, radius)
    per_step = max(1, min(dil, ATTN_STEP_ROWS // length))
    assert dil % per_step == 0
    view = lambda a: a.reshape(bsz, length, dw)
    spec = pl.BlockSpec((1, length, per_step * w), lambda i, r: (i, 0, r))
    o, lse = pl.pallas_call(
        functools.partial(_attn_c_kernel, dil=dil, radius=radius, n_heads=w // HEAD_DIM),
        grid=(bsz, dil // per_step),
        in_specs=[spec, spec, spec],
        out_specs=[spec, spec],
        out_shape=[jax.ShapeDtypeStruct((bsz, length, dw), BF16),
                   jax.ShapeDtypeStruct((bsz, length, dw), F32)],
        scratch_shapes=[pltpu.VMEM((3, w // HEAD_DIM, qb, span), F32)],
        compiler_params=_params(2),
        name=f"attn_c_d{dil}",
    )(view(q), view(k), view(v))
    return o.reshape(rows, dw), lse.reshape(rows, dw)


def _attn_d_kernel(sink_ref, q_ref, k_ref, v_ref, y_ref, bias_ref, *, radius, n_heads, group):
    length = q_ref.shape[1]
    qb, span, n_blk, _ = _band_shape(length, radius)
    pairs = n_heads * HEAD_DIM // LANES
    lower = lax.broadcasted_iota(jnp.int32, (1, LANES), 1) < HEAD_DIM

    @pl.when(pl.program_id(0) == 0)
    def _():
        _fill_band_bias(bias_ref, length, radius, 1.0, n_heads)

    per_iter = max(p for p in range(1, ATTN_BLOCKS_PER_ITER + 1) if n_blk % p == 0)

    def blocks(it, carry):
        geo = [_band_block(it * per_iter + b, length, radius) for b in range(per_iter)]
        slabs = [(q0, start, variant, j) for q0, start, variant in geo for j in range(pairs)]
        kv_lo = [(2 * j // group) * LANES for _, _, _, j in slabs]
        biases = [bias_ref[variant, 2 * j + half] for _, _, variant, j in slabs for half in (0, 1)]
        outs = _attend_pairs([q_ref[0, pl.ds(q0, qb), j * LANES:(j + 1) * LANES] for q0, _, _, j in slabs],
                             [k_ref[0, pl.ds(start, span), lo:lo + LANES] for (_, start, _, _), lo in zip(slabs, kv_lo)],
                             [v_ref[0, pl.ds(start, span), lo:lo + LANES] for (_, start, _, _), lo in zip(slabs, kv_lo)],
                             biases)
        for (q0, _, _, j), (o, lse) in zip(slabs, outs):
            sink = jnp.where(lower, sink_ref[2 * j], sink_ref[2 * j + 1])
            y_ref[0, pl.ds(q0, qb), j * LANES:(j + 1) * LANES] = (o * jax.nn.sigmoid(lse - sink)).astype(y_ref.dtype)
        return carry

    lax.fori_loop(0, n_blk // per_iter, blocks, 0)


def _attn_d(q, k, v, sink):
    b, s, w = q.shape
    kvw = k.shape[2]
    n_heads = w // HEAD_DIM
    group = n_heads // D_KV_HEADS
    assert kvw == D_KV_HEADS * LANES and group % 2 == 0
    qb, span, _, _ = _band_shape(s, D_RADIUS)
    return pl.pallas_call(
        functools.partial(_attn_d_kernel, radius=D_RADIUS, n_heads=n_heads, group=group),
        grid=(b,),
        in_specs=[pl.BlockSpec(memory_space=pltpu.SMEM),
                  pl.BlockSpec((1, s, w), lambda i: (i, 0, 0)),
                  pl.BlockSpec((1, s, kvw), lambda i: (i, 0, 0)),
                  pl.BlockSpec((1, s, kvw), lambda i: (i, 0, 0))],
        out_specs=pl.BlockSpec((1, s, w), lambda i: (i, 0, 0)),
        out_shape=jax.ShapeDtypeStruct((b, s, w), BF16),
        scratch_shapes=[pltpu.VMEM((3, n_heads, qb, span), F32)],
        compiler_params=_params(1),
        name="attn_d",
    )(sink.astype(F32), q, k, v)


def _merge_kernel(*refs, dils, cw):
    n_pat = len(dils)
    o_refs, l_refs = refs[:n_pat], refs[n_pat:2 * n_pat]
    yd_ref, cat_ref = refs[2 * n_pat], refs[2 * n_pat + 1]
    scratch = list(refs[2 * n_pat + 2:])
    tm = cat_ref.shape[0]
    chunks = cw // LANES
    staged = []
    for p, dil in enumerate(dils):
        if dil == 1:
            staged.append(None)
            continue
        o_s, l_s = scratch.pop(0), scratch.pop(0)
        n = tm // dil
        for r in range(dil):
            for c in range(chunks):
                lo = r * cw + c * LANES
                o_s[c, pl.ds(r, n, stride=dil), :] = o_refs[p][:, lo:lo + LANES].astype(F32)
                l_s[c, pl.ds(r, n, stride=dil), :] = l_refs[p][:, lo:lo + LANES]
        staged.append((o_s, l_s))
    for c in range(chunks):
        cols = slice(c * LANES, (c + 1) * LANES)
        os_ = [o_refs[p][:, cols].astype(F32) if st is None else st[0][c] for p, st in enumerate(staged)]
        ls_ = [l_refs[p][:, cols] if st is None else st[1][c] for p, st in enumerate(staged)]
        m = functools.reduce(jnp.maximum, ls_)
        es = [jnp.exp(l - m) for l in ls_]
        num = functools.reduce(lambda a, b: a + b, [e * o for e, o in zip(es, os_)])
        den = functools.reduce(lambda a, b: a + b, es)
        cat_ref[:, cols] = (num / den).astype(BF16)
    cat_ref[:, cw:] = yd_ref[...]


def _merge(os_, lses, yd, dils):
    t, dw = yd.shape
    cw = os_[0].shape[1] // dils[0]
    vspecs = [pl.BlockSpec((ROW_TILE // dil, dil * cw), lambda i: (i, 0)) for dil in dils]
    n_scratch = 2 * sum(1 for dil in dils if dil != 1)
    return pl.pallas_call(
        functools.partial(_merge_kernel, dils=dils, cw=cw),
        grid=(t // ROW_TILE,),
        in_specs=vspecs + vspecs + [pl.BlockSpec((ROW_TILE, dw), lambda i: (i, 0))],
        out_specs=pl.BlockSpec((ROW_TILE, cw + dw), lambda i: (i, 0)),
        out_shape=jax.ShapeDtypeStruct((t, cw + dw), BF16),
        scratch_shapes=[pltpu.VMEM((cw // LANES, ROW_TILE, LANES), F32)] * n_scratch,
        compiler_params=_params(1),
        name="merge",
    )(*os_, *lses, yd)


def _layer_norm(z, g, b):
    mu = jnp.mean(z, axis=-1, keepdims=True)
    zc = z - mu
    var = jnp.mean(zc * zc, axis=-1, keepdims=True)
    return zc * lax.rsqrt(var + LN_EPS) * g + b


ROUTE_ROWS = 8


def _route_tile(logits, triu, count_ref):
    n_exp, tm = logits.shape
    sub = lax.broadcasted_iota(jnp.int32, (n_exp, tm), 0)
    out_row = lax.broadcasted_iota(jnp.int32, (ROUTE_ROWS, tm), 0)
    work = logits
    vals, onehots = [], []
    idx_out = jnp.zeros((ROUTE_ROWS, tm), jnp.int32)
    for k in range(TOP_K):
        m = jnp.max(work, axis=0, keepdims=True)
        idx = jnp.min(jnp.where(work == m, sub, n_exp), axis=0, keepdims=True)
        hot = sub == idx
        work = jnp.where(hot, -jnp.inf, work)
        vals.append(m)
        onehots.append(hot)
        idx_out = jnp.where(out_row == k, idx, idx_out)
    exps = [jnp.exp(v - vals[0]) for v in vals]
    den = functools.reduce(lambda a, b: a + b, exps)
    gate_out = jnp.zeros((ROUTE_ROWS, tm), F32)
    for k in range(TOP_K):
        gate_out = jnp.where(out_row == k, exps[k] / den, gate_out)
    multihot = functools.reduce(lambda a, b: a + b, [h.astype(F32) for h in onehots])
    before = _dot(multihot.astype(BF16), triu) + count_ref[...]
    rank_out = jnp.zeros((ROUTE_ROWS, tm), jnp.int32)
    for k in range(TOP_K):
        rank = jnp.sum(jnp.where(onehots[k], before, 0.0), axis=0, keepdims=True)
        rank_out = jnp.where(out_row == k, rank.astype(jnp.int32), rank_out)
    count_ref[...] += jnp.sum(multihot, axis=1, keepdims=True)
    return idx_out, gate_out, rank_out


def _out_ln_kernel(cat_ref, w_ref, x_ref, g_ref, b_ref, rw_hi_ref, rw_lo_ref, rb_ref, triu_ref,
                   x1_ref, x1p_ref, idx_ref, gate_ref, rank_ref, count_ref, *, alpha):
    @pl.when(pl.program_id(0) == 0)
    def _():
        count_ref[...] = jnp.zeros_like(count_ref)

    sub = triu_ref.shape[0]
    tiles = [slice(a, a + sub) for a in range(0, x_ref.shape[0], sub)]
    mixes = [_dot(cat_ref[rows, :], w_ref[...]) for rows in tiles]
    splits = []
    for rows, mix in zip(tiles, mixes):
        x1 = _layer_norm(alpha * x_ref[rows, :] + mix, g_ref[...], b_ref[...])
        x1_ref[rows, :] = x1
        x1p_ref[rows, :] = _pack_pairs(x1)
        hi = x1.astype(BF16)
        splits.append((hi, (x1 - hi.astype(F32)).astype(BF16)))
    logits = [_dot_nt(rw_hi_ref[...], hi) + _dot_nt(rw_hi_ref[...], lo) + _dot_nt(rw_lo_ref[...], hi) + rb_ref[...]
              for hi, lo in splits]
    for rows, lg in zip(tiles, logits):
        idx_ref[:, rows], gate_ref[:, rows], rank_ref[:, rows] = _route_tile(lg, triu_ref[...], count_ref)


def _out_ln(cat, w_out, xt, g, b, router_w, router_b, alpha):
    t, d = xt.shape
    n_exp = router_w.shape[1]
    rw_t = router_w.T
    rw_hi = rw_t.astype(BF16)
    rw_lo = (rw_t - rw_hi.astype(F32)).astype(BF16)
    triu = jnp.triu(jnp.ones((ROW_TILE, ROW_TILE), BF16), 1)
    step = OUT_LN_SUBTILES * ROW_TILE if t % (OUT_LN_SUBTILES * ROW_TILE) == 0 else ROW_TILE
    row = lambda wd: pl.BlockSpec((step, wd), lambda i: (i, 0))
    col = pl.BlockSpec((ROUTE_ROWS, step), lambda i: (0, i))
    full = lambda a: pl.BlockSpec(a.shape, lambda i: (0,) * a.ndim)
    args = (cat, w_out.astype(BF16), xt, g.reshape(1, d), b.reshape(1, d), rw_hi, rw_lo,
            router_b.reshape(n_exp, 1), triu)
    return pl.pallas_call(
        functools.partial(_out_ln_kernel, alpha=alpha),
        grid=(t // step,),
        in_specs=[row(cat.shape[1]), full(args[1]), row(d)] + [full(a) for a in args[3:]],
        out_specs=[row(d), row(d // 2), col, col, col, pl.BlockSpec((n_exp, 1), lambda i: (0, 0))],
        out_shape=[jax.ShapeDtypeStruct((t, d), F32), jax.ShapeDtypeStruct((t, d // 2), jnp.int32),
                   jax.ShapeDtypeStruct((ROUTE_ROWS, t), jnp.int32), jax.ShapeDtypeStruct((ROUTE_ROWS, t), F32),
                   jax.ShapeDtypeStruct((ROUTE_ROWS, t), jnp.int32), jax.ShapeDtypeStruct((n_exp, 1), F32)],
        compiler_params=_params(1),
        name="out_ln",
    )(*args)


def _experts_kernel(first_ref, blocks_ref, count_ref, x_hbm, wgu_ref, bgu_ref, wd_ref, bd_ref, perm_ref,
                    y_hbm, wgu_s, wd_s, x_buf, y_buf, x_sem, y_sem, zero_sem):
    e = pl.program_id(0)
    last = pl.num_programs(0) - 1
    n_blk = blocks_ref[e]
    first = first_ref[e]
    count = count_ref[e]
    used = first_ref[last] + blocks_ref[last]
    half = GU_BLOCK // 2

    def rows_of(g):
        return pl.ds(pl.multiple_of(g * MOE_BLOCK, MOE_BLOCK), MOE_BLOCK)

    def x_copy(g, slot):
        return pltpu.make_async_copy(x_hbm.at[rows_of(g)], x_buf.at[slot], x_sem.at[slot])

    def y_copy(g, slot):
        return pltpu.make_async_copy(y_buf.at[slot], y_hbm.at[rows_of(g)], y_sem.at[slot])

    def compute(g, slot):
        xb = _unpack_pairs(x_buf[slot])
        row = lax.broadcasted_iota(jnp.int32, xb.shape, 0)
        xb = jnp.where(row < count - (g - first) * MOE_BLOCK, xb, jnp.zeros_like(xb))
        h = _dot(xb, wgu_s[...]) + bgu_ref[...]
        acts = []
        for c in range(h.shape[1] // GU_BLOCK):
            glu = jnp.minimum(h[:, c * GU_BLOCK:c * GU_BLOCK + half], SWIGLU_LIMIT)
            lin = jnp.clip(h[:, c * GU_BLOCK + half:(c + 1) * GU_BLOCK], -SWIGLU_LIMIT, SWIGLU_LIMIT)
            acts.append((glu * jax.nn.sigmoid(SWIGLU_ALPHA * glu) * (lin + 1.0)).astype(BF16))
        act = jnp.concatenate(acts, axis=1)
        y_buf[slot] = _pack_pairs(_dot(act, wd_s[...]) + bd_ref[...])

    @pl.when((e == 0) & (used > 0))
    def _():
        x_copy(0, 0).start()

    @pl.when(n_blk > 0)
    def _():
        for c in range(wgu_ref.shape[1] // GU_BLOCK):
            cols = slice(c * GU_BLOCK, (c + 1) * GU_BLOCK)
            wgu_s[:, cols] = _dot(wgu_ref[:, cols].astype(BF16), perm_ref[...]).astype(BF16)
        wd_s[...] = wd_ref[...].astype(BF16)

    def step(g, carry):
        slot = g % 2
        x_copy(g, slot).wait()

        @pl.when(g + 1 < used)
        def _():
            x_copy(g + 1, 1 - slot).start()

        @pl.when(g >= 2)
        def _():
            y_copy(g - 2, slot).wait()

        compute(g, slot)
        y_copy(g, slot).start()
        return carry

    lax.fori_loop(first, first + n_blk, step, 0)

    @pl.when(e == last)
    def _():
        @pl.when(used >= 1)
        def _():
            y_copy(0, (used - 1) % 2).wait()

        @pl.when(used >= 2)
        def _():
            y_copy(0, used % 2).wait()

        total = y_hbm.shape[0] // MOE_BLOCK
        y_buf[0] = jnp.zeros(y_buf.shape[1:], y_buf.dtype)

        def tail_copy(g):
            return pltpu.make_async_copy(y_buf.at[0], y_hbm.at[rows_of(g)], zero_sem)

        def start_all(g, carry):
            tail_copy(g).start()
            return carry

        def wait_all(g, carry):
            tail_copy(g).wait()
            return carry

        lax.fori_loop(used, total, start_all, 0)
        lax.fori_loop(used, total, wait_all, 0)


def _experts(x_rows, first_blk, n_blk, counts, layer, w_gu, b_gu, w_down, b_down):
    rows, dp = x_rows.shape
    _, n_exp, d, de2 = w_gu.shape
    assert de2 % GU_BLOCK == 0 and dp * 2 == d and rows % MOE_BLOCK == 0
    half = GU_BLOCK // 2
    j = jnp.arange(GU_BLOCK)
    src = jnp.where(j < half, 2 * j, 2 * (j - half) + 1)
    perm = (jnp.arange(GU_BLOCK)[:, None] == src[None, :]).astype(BF16)
    wspec = lambda a: pl.BlockSpec((None, None) + a.shape[2:], lambda e, *_: (layer, e, 0, 0))
    grid_spec = pltpu.PrefetchScalarGridSpec(
        num_scalar_prefetch=3,
        grid=(n_exp,),
        in_specs=[pl.BlockSpec(memory_space=pl.ANY),
                  wspec(w_gu), wspec(b_gu), wspec(w_down), wspec(b_down),
                  pl.BlockSpec((GU_BLOCK, GU_BLOCK), lambda e, *_: (0, 0))],
        out_specs=pl.BlockSpec(memory_space=pl.ANY),
        scratch_shapes=[pltpu.VMEM(w_gu.shape[2:], BF16), pltpu.VMEM(w_down.shape[2:], BF16),
                        pltpu.VMEM((2, MOE_BLOCK, dp), jnp.int32), pltpu.VMEM((2, MOE_BLOCK, dp), jnp.int32),
                        pltpu.SemaphoreType.DMA((2,)), pltpu.SemaphoreType.DMA((2,)),
                        pltpu.SemaphoreType.DMA(())],
    )
    return pl.pallas_call(
        _experts_kernel,
        grid_spec=grid_spec,
        out_shape=jax.ShapeDtypeStruct((rows, dp), jnp.int32),
        compiler_params=_params(1),
        name="experts",
    )(first_blk, n_blk, counts, x_rows, w_gu, b_gu, w_down, b_down, perm)


def _combine_ln_kernel(x1_ref, yk_ref, gate_ref, g_ref, b_ref, x2_ref, *, alpha):
    gates = gate_ref[...].T
    ffn = gates[:, 0:1] * _unpack_pairs(yk_ref[0]).astype(F32)
    for k in range(1, yk_ref.shape[0]):
        ffn = ffn + gates[:, k:k + 1] * _unpack_pairs(yk_ref[k]).astype(F32)
    x2_ref[...] = _layer_norm(alpha * x1_ref[...] + ffn, g_ref[...], b_ref[...])


def _combine_ln(x1, yk, gates, g, b, alpha):
    t, d = x1.shape
    k = yk.shape[0]
    tile = COMBINE_TILE if t % COMBINE_TILE == 0 else ROW_TILE
    return pl.pallas_call(
        functools.partial(_combine_ln_kernel, alpha=alpha),
        grid=(t // tile,),
        in_specs=[pl.BlockSpec((tile, d), lambda i: (i, 0)),
                  pl.BlockSpec((k, tile, d // 2), lambda i: (0, i, 0)),
                  pl.BlockSpec((gates.shape[0], tile), lambda i: (0, i)),
                  pl.BlockSpec((1, d), lambda i: (0, 0)),
                  pl.BlockSpec((1, d), lambda i: (0, 0))],
        out_specs=pl.BlockSpec((tile, d), lambda i: (i, 0)),
        out_shape=jax.ShapeDtypeStruct((t, d), F32),
        compiler_params=_params(1),
        name="combine_ln",
    )(x1, yk, gates, g.reshape(1, d), b.reshape(1, d))


def _route(top_idx, rank, counts):
    n_exp = counts.shape[0]
    experts = jnp.arange(n_exp, dtype=jnp.int32)
    counts = counts.reshape(n_exp).astype(jnp.int32)
    padded = (counts + MOE_BLOCK - 1) // MOE_BLOCK * MOE_BLOCK
    pad_end = jnp.cumsum(padded)
    pad_start = pad_end - padded
    start = jnp.sum(jnp.where(top_idx[None] == experts[:, None, None], pad_start[:, None, None], 0), axis=0)
    pos = rank + start
    first_blk = (pad_start // MOE_BLOCK).astype(jnp.int32)
    n_blk = (padded // MOE_BLOCK).astype(jnp.int32)
    return pos, first_blk, n_blk, counts


def _moe(x1, x1p, top_idx, gates, rank, counts, layer, w_gu, b_gu, w_down, b_down, g, b, alpha):
    t, d = x1.shape
    n_exp = counts.shape[0]
    pos, first_blk, n_blk, counts = _route(top_idx[:TOP_K], rank[:TOP_K], counts)
    rows = (-(-(t * TOP_K) // MOE_BLOCK) + n_exp) * MOE_BLOCK
    pos3 = pos.reshape(TOP_K, t // SC_CHUNK, SC_CHUNK).transpose(1, 0, 2)
    x_rows = _sc_scatter_rows(x1p, pos3, rows)
    y = _experts(x_rows, first_blk, n_blk, counts, layer, w_gu, b_gu, w_down, b_down)
    yk = _sc_gather_rows(y, pos.reshape(-1)).reshape(TOP_K, t, d // 2)
    return _combine_ln(x1, yk, gates, g, b, alpha)


def kernel(x, ev_w_in, ev_pool_w, ev_pool_scale, ev_conv_w, ev_w_out, od_w_in, od_sink, od_w_out,
           router_w, router_b, exp_w_gu, exp_b_gu, exp_w_down, exp_b_down, ln_g, ln_b):
    bsz, seq, d = x.shape
    t = bsz * seq
    depth = ln_g.shape[0]
    alpha = (2 * depth) ** 0.25
    n_exp = router_w.shape[2]
    c_width = d // 2
    dq_width = d // 2
    dkv_width = D_KV_HEADS * HEAD_DIM
    assert t % ROW_TILE == 0

    de2 = exp_b_gu.shape[-1]
    b_gu = exp_b_gu.reshape(depth, n_exp, de2 // GU_BLOCK, GU_BLOCK // 2, 2)
    b_gu = jnp.swapaxes(b_gu, -1, -2).reshape(depth, n_exp, 1, de2)
    b_down = exp_b_down.reshape(depth, n_exp, 1, -1)

    xt = x.reshape(t, d)
    for layer in range(depth):
        i = layer // 2
        if layer % 2 == 0:
            cat = _even_mix(xt.reshape(bsz, seq, d), ev_w_in[i], ev_pool_w[i], ev_pool_scale[i], ev_conv_w[i])
            cat = cat.reshape(t, -1)
            w_out = ev_w_out[i]
        else:
            qkv, (qd, kd, vd) = _odd_proj(xt, od_w_in[i], c_width, dq_width, dkv_width)
            dils = tuple(dil for _, dil in C_PATTERNS)
            os_, lses = [], []
            for (window, dil), (qc, kc, vc) in zip(C_PATTERNS, qkv):
                o, lse = _attn_c(qc, kc, vc, bsz, window, dil)
                os_.append(o)
                lses.append(lse)
            seq3 = lambda a: a.reshape(bsz, seq, a.shape[1])
            yd = _attn_d(seq3(qd), seq3(kd), seq3(vd), od_sink[i]).reshape(t, dq_width)
            cat = _merge(os_, lses, yd, dils)
            w_out = od_w_out[i]
        x1, x1p, top_idx, gates, rank, counts = _out_ln(cat, w_out, xt, ln_g[layer, 0], ln_b[layer, 0],
                                                        router_w[layer], router_b[layer], alpha)
        xt = _moe(x1, x1p, top_idx, gates, rank, counts, layer, exp_w_gu, b_gu, exp_w_down, b_down,
                  ln_g[layer, 1], ln_b[layer, 1], alpha)
    return xt.reshape(bsz, seq, d)
```

```python
import functools

import jax
import jax.numpy as jnp
from jax import lax
from jax.experimental import pallas as pl
from jax.experimental.pallas import tpu as pltpu
from jax.experimental.pallas import tpu_sc as plsc

HEAD_DIM = 64
POOL_WINDOWS = (2, 4, 8, 16)
C_PATTERNS = ((128, 1), (512, 4), (2048, 16))
D_KV_HEADS = 2
D_RADIUS = 128
TOP_K = 4
SWIGLU_LIMIT = 7.0
SWIGLU_ALPHA = 1.702
MOE_BLOCK = 512
LN_EPS = 1e-5
NEG_INF = -1e30
LOG2E = 1.4426950408889634

ROW_TILE = 512
ODD_PROJ_TILE = 1024
OUT_LN_SUBTILES = 2
COMBINE_TILE = 1024
ATTN_Q_BLOCK = 128
ATTN_BLOCKS_PER_ITER = 8
ATTN_STEP_ROWS = 512
LANES = 128
GU_BLOCK = 256
VMEM_LIMIT_BYTES = 56 * 1024 * 1024

F32 = jnp.float32
BF16 = jnp.bfloat16


def _params(n_axes=1):
    return pltpu.CompilerParams(dimension_semantics=("arbitrary",) * n_axes,
                                vmem_limit_bytes=VMEM_LIMIT_BYTES)


def _dot(a, b):
    return jnp.dot(a, b, preferred_element_type=F32)


def _dot_nt(a, b):
    return lax.dot_general(a, b, (((1,), (1,)), ((), ())), preferred_element_type=F32)


def _pack_pairs(a):
    n = a.shape[1] // 2
    bits = lax.bitcast_convert_type(a.astype(BF16).astype(F32), jnp.int32)
    return bits[:, :n] | lax.shift_right_logical(bits[:, n:], 16)


def _unpack_pairs(w):
    hi = lax.bitcast_convert_type(w & jnp.int32(-65536), F32)
    lo = lax.bitcast_convert_type(lax.shift_left(w, 16), F32)
    return jnp.concatenate([hi, lo], axis=1).astype(BF16)


SC_CORES = 2
SC_SUBCORES = 16
SC_WORKERS = SC_CORES * SC_SUBCORES
SC_CHUNK = 64


def _sc_mesh():
    return plsc.VectorSubcoreMesh(core_axis_name="c", subcore_axis_name="s")


def _sc_worker():
    return lax.axis_index("s") * SC_CORES + lax.axis_index("c")


def _sc_gather_rows(table, idx):
    n, w = idx.shape[0], table.shape[1]
    per_w = n // SC_WORKERS
    assert n % SC_WORKERS == 0 and per_w % (2 * SC_CHUNK) == 0

    def body(table_hbm, idx_hbm, out_hbm, i0, i1, r0, r1, s0, s1):
        base = _sc_worker() * per_w

        def start(off, iv, rv, sem):
            pltpu.sync_copy(idx_hbm.at[pl.ds(off, SC_CHUNK)], iv)
            pltpu.async_copy(table_hbm.at[iv], rv, sem)

        def finish(off, iv, rv, sem):
            pltpu.make_async_copy(table_hbm.at[iv], rv, sem).wait()
            pltpu.sync_copy(rv, out_hbm.at[pl.ds(off, SC_CHUNK)])

        start(base, i0, r0, s0)

        @pl.loop(0, per_w, step=2 * SC_CHUNK)
        def _(o):
            off = base + o
            start(off + SC_CHUNK, i1, r1, s1)
            finish(off, i0, r0, s0)

            @pl.when(o + 2 * SC_CHUNK < per_w)
            def _():
                start(off + 2 * SC_CHUNK, i0, r0, s0)

            finish(off + SC_CHUNK, i1, r1, s1)

    return pl.kernel(
        body, mesh=_sc_mesh(),
        out_type=jax.ShapeDtypeStruct((n, w), table.dtype),
        scratch_types=[pltpu.VMEM((SC_CHUNK,), jnp.int32), pltpu.VMEM((SC_CHUNK,), jnp.int32),
                       pltpu.VMEM((SC_CHUNK, w), table.dtype), pltpu.VMEM((SC_CHUNK, w), table.dtype),
                       pltpu.SemaphoreType.DMA, pltpu.SemaphoreType.DMA],
    )(table, idx)


def _sc_scatter_rows(src, pos3, n_out):
    t, w = src.shape
    kk = pos3.shape[1]
    per_w = t // SC_WORKERS
    assert t % SC_WORKERS == 0 and per_w % SC_CHUNK == 0 and pos3.shape == (t // SC_CHUNK, kk, SC_CHUNK)

    def body(src_hbm, pos_hbm, out_hbm, iv, rv, sem):
        base = _sc_worker() * per_w

        @pl.loop(0, per_w, step=SC_CHUNK)
        def _(o):
            off = base + o
            pltpu.sync_copy(pos_hbm.at[off // SC_CHUNK], iv)
            pltpu.sync_copy(src_hbm.at[pl.ds(off, SC_CHUNK)], rv)
            copies = [pltpu.async_copy(rv, out_hbm.at[iv.at[j]], sem) for j in range(kk)]
            for cp in copies:
                cp.wait()

    return pl.kernel(
        body, mesh=_sc_mesh(),
        out_type=jax.ShapeDtypeStruct((n_out, w), src.dtype),
        scratch_types=[pltpu.VMEM((kk, SC_CHUNK), jnp.int32), pltpu.VMEM((SC_CHUNK, w), src.dtype),
                       pltpu.SemaphoreType.DMA],
    )(src, pos3)


def _shift_down(a, k, row):
    return jnp.where(row >= k, pltpu.roll(a, k, axis=0), 0.0)


def _shift_up(a, k, row):
    n = a.shape[0]
    return jnp.where(row < n - k, pltpu.roll(a, n - k, axis=0), 0.0)


def _even_mix_kernel(x_ref, w_in_ref, pool_w_ref, pool_scale_ref, conv_w_ref, cat_ref, xb_ref):
    s = x_ref.shape[1]
    pool_width = pool_scale_ref.shape[1]
    gd = pool_width // len(POOL_WINDOWS)
    conv_width = conv_w_ref.shape[1]
    xb_ref[...] = x_ref[0].astype(BF16)
    xb = xb_ref[...]

    row = lax.broadcasted_iota(jnp.int32, (s, gd), 0)
    cw = 256
    rowc = lax.broadcasted_iota(jnp.int32, (s, cw), 0)

    def pool_group(g, u):
        w, lo = POOL_WINDOWS[g], g * gd
        half = w // 2
        back, fwd, span = u, u, 1
        while span < half:
            back = back + _shift_down(back, span, row)
            fwd = fwd + _shift_up(fwd, span, row)
            span *= 2
        win = _shift_down(back, 1, row) + fwd
        cnt = (jnp.minimum(row + (w - half), s) - jnp.maximum(row - half, 0)).astype(F32)
        pooled = win / cnt - u
        mixed = _dot(pooled.astype(BF16), pool_w_ref[g])
        cat_ref[0, :, lo:lo + gd] = (mixed * pool_scale_ref[:, lo:lo + gd]).astype(BF16)

    def conv_project(j):
        c0 = j * cw
        return tuple(_dot(xb, w_in_ref[:, pool_width + k * conv_width + c0:pool_width + k * conv_width + c0 + cw])
                     for k in range(3))

    def conv_chunk(j, b_gate, c_gate, v):
        c0 = j * cw
        u = c_gate * v
        conv = (_shift_down(u, 1, rowc) * conv_w_ref[0:1, c0:c0 + cw] + u * conv_w_ref[1:2, c0:c0 + cw]
                + _shift_up(u, 1, rowc) * conv_w_ref[2:3, c0:c0 + cw])
        cat_ref[0, :, pool_width + c0:pool_width + c0 + cw] = (b_gate * conv).astype(BF16)

    n_conv = conv_width // cw
    pool_u = [_dot(xb, w_in_ref[:, lo:lo + 2 * gd]) for lo in range(0, pool_width, 2 * gd)]
    ahead = conv_project(0)
    for g in range(len(POOL_WINDOWS)):
        pool_group(g, pool_u[g // 2][:, (g % 2) * gd:(g % 2 + 1) * gd])
    for j in range(n_conv):
        cur, ahead = ahead, (conv_project(j + 1) if j + 1 < n_conv else None)
        conv_chunk(j, *cur)


def _even_mix(x, w_in, pool_w, pool_scale, conv_w):
    b, s, d = x.shape
    pool_width = pool_scale.shape[0]
    conv_width = conv_w.shape[1]
    assert conv_width % 256 == 0 and w_in.shape[1] == pool_width + 3 * conv_width
    cat_width = pool_width + conv_width
    return pl.pallas_call(
        _even_mix_kernel,
        grid=(b,),
        in_specs=[
            pl.BlockSpec((1, s, d), lambda i: (i, 0, 0)),
            pl.BlockSpec(w_in.shape, lambda i: (0, 0)),
            pl.BlockSpec(pool_w.shape, lambda i: (0, 0, 0)),
            pl.BlockSpec((1, pool_width), lambda i: (0, 0)),
            pl.BlockSpec(conv_w.shape, lambda i: (0, 0)),
        ],
        out_specs=pl.BlockSpec((1, s, cat_width), lambda i: (i, 0, 0)),
        out_shape=jax.ShapeDtypeStruct((b, s, cat_width), BF16),
        scratch_shapes=[pltpu.VMEM((s, d), BF16)],
        compiler_params=_params(1),
        name="even_mix",
    )(x, w_in.astype(BF16), pool_w.astype(BF16), pool_scale.reshape(1, pool_width), conv_w)


def _odd_proj_kernel(x_ref, w_ref, *refs, c_width, dils):
    outs, h_ref = refs[:-1], refs[-1]
    xb = x_ref[...].astype(BF16)
    tm = x_ref.shape[0]
    chunks = c_width // LANES
    for j in range(3):
        h = _dot(xb, w_ref[:, j * c_width:(j + 1) * c_width])
        for c in range(chunks):
            h_ref[j, c] = h[:, c * LANES:(c + 1) * LANES]
        for pi, dil in enumerate(dils):
            n = tm // dil
            for r in range(dil):
                for c in range(chunks):
                    src = h_ref[j, c] if dil == 1 else h_ref[j, c, pl.ds(r, n, stride=dil), :]
                    lo = r * c_width + c * LANES
                    outs[3 * pi + j][:, lo:lo + LANES] = src.astype(BF16)
    c0 = 3 * c_width
    for ref in outs[3 * len(dils):]:
        wd = ref.shape[1]
        ref[...] = _dot(xb, w_ref[:, c0:c0 + wd]).astype(BF16)
        c0 += wd


def _odd_proj(xt, w_in, c_width, dq_width, dkv_width):
    t, d = xt.shape
    dils = tuple(dil for _, dil in C_PATTERNS)
    tile = ODD_PROJ_TILE if t % ODD_PROJ_TILE == 0 else ROW_TILE
    assert 3 * c_width + dq_width + 2 * dkv_width == w_in.shape[1] and c_width % LANES == 0
    col = jnp.arange(w_in.shape[1])
    is_q = (col < c_width) | ((col >= 3 * c_width) & (col < 3 * c_width + dq_width))
    w_in = w_in * jnp.where(is_q, HEAD_DIM ** -0.5 * LOG2E, 1.0)
    kv0 = 3 * c_width + dq_width
    twice = lambda w: jnp.repeat(w.reshape(d, -1, HEAD_DIM), LANES // HEAD_DIM, axis=1).reshape(d, -1)
    w_in = jnp.concatenate([w_in[:, :kv0], twice(w_in[:, kv0:kv0 + dkv_width]), twice(w_in[:, kv0 + dkv_width:])], axis=1)
    dkv_width = dkv_width * (LANES // HEAD_DIM)
    w_scaled = w_in.astype(BF16)
    shapes = [(t // dil, dil * c_width) for dil in dils for _ in range(3)]
    blocks = [(tile // dil, dil * c_width) for dil in dils for _ in range(3)]
    for wd in (dq_width, dkv_width, dkv_width):
        shapes.append((t, wd))
        blocks.append((tile, wd))
    outs = pl.pallas_call(
        functools.partial(_odd_proj_kernel, c_width=c_width, dils=dils),
        grid=(t // tile,),
        in_specs=[pl.BlockSpec((tile, d), lambda i: (i, 0)),
                  pl.BlockSpec(w_in.shape, lambda i: (0, 0))],
        out_specs=[pl.BlockSpec(blk, lambda i: (i, 0)) for blk in blocks],
        out_shape=[jax.ShapeDtypeStruct(shp, BF16) for shp in shapes],
        scratch_shapes=[pltpu.VMEM((3, c_width // LANES, tile, LANES), F32)],
        compiler_params=_params(1),
        name="odd_proj",
    )(xt, w_scaled)
    qkv = [outs[3 * pi:3 * pi + 3] for pi in range(len(dils))]
    return qkv, outs[3 * len(dils):]


def _band_shape(length, radius):
    qb = min(ATTN_Q_BLOCK, length)
    span = min(length, qb + 2 * radius)
    n_blk = length // qb
    assert length % qb == 0 and (n_blk <= 2 or radius <= qb)
    return qb, span, n_blk, (0, -radius, qb - span)


def _fill_band_bias(bias_ref, length, radius, dist_unit, n_heads):
    qb, span, _, offsets = _band_shape(length, radius)
    rel = lax.broadcasted_iota(jnp.int32, (qb, span), 1) - lax.broadcasted_iota(jnp.int32, (qb, span), 0)
    for v, off in enumerate(offsets):
        dist = jnp.abs(rel + off)
        mask = jnp.where(dist <= radius, 0.0, NEG_INF).astype(F32)
        far = dist.astype(F32) * (dist_unit * LOG2E)
        for hh in range(n_heads):
            bias_ref[v, hh] = mask - 2.0 ** (-8.0 * (hh + 1) / n_heads) * far


def _band_block(i, length, radius):
    qb, span, n_blk, offsets = _band_shape(length, radius)
    q0 = pl.multiple_of(i * qb, qb)
    variant = jnp.where(i == 0, 0, jnp.where(i == n_blk - 1, 2, 1))
    off = jnp.where(i == 0, offsets[0], jnp.where(i == n_blk - 1, offsets[2], offsets[1]))
    align = 8
    for cand in (128, 64, 32, 16):
        if all(o % cand == 0 for o in offsets) and qb % cand == 0:
            align = cand
            break
    return q0, pl.multiple_of(q0 + off, align), variant


def _attend_pairs(q_slabs, k_slabs, v_slabs, biases):
    lower = lax.broadcasted_iota(jnp.int32, (1, LANES), 1) < HEAD_DIM
    scores = []
    for j, (q2, k2) in enumerate(zip(q_slabs, k_slabs)):
        zero = jnp.zeros_like(q2)
        scores.append(_dot_nt(jnp.where(lower, q2, zero), k2) + biases[2 * j])
        scores.append(_dot_nt(jnp.where(lower, zero, q2), k2) + biases[2 * j + 1])
    probs = []
    for s in scores:
        m = jnp.max(s, axis=-1, keepdims=True)
        probs.append((jnp.exp2(s - m).astype(BF16), m))
    ones = jnp.ones(v_slabs[0].shape, BF16)
    outs = []
    for j, v2 in enumerate(v_slabs):
        (p_lo, m_lo), (p_up, m_up) = probs[2 * j], probs[2 * j + 1]
        v_ones = jnp.concatenate([v2, ones], axis=1)
        r_lo, r_up = _dot(p_lo, v_ones), _dot(p_up, v_ones)
        l = jnp.where(lower, r_lo[:, LANES:], r_up[:, LANES:])
        o = jnp.where(lower, r_lo[:, :LANES], r_up[:, :LANES]) / l
        outs.append((o, (jnp.where(lower, m_lo, m_up) + jnp.log2(l)) * (1.0 / LOG2E)))
    return outs


def _attn_c_kernel(q_ref, k_ref, v_ref, o_ref, lse_ref, bias_ref, *, dil, radius, n_heads):
    length = q_ref.shape[1]
    qb, span, n_blk, _ = _band_shape(length, radius)
    width = n_heads * HEAD_DIM

    @pl.when((pl.program_id(0) == 0) & (pl.program_id(1) == 0))
    def _():
        _fill_band_bias(bias_ref, length, radius, float(dil), n_heads)

    per_iter = max(p for p in range(1, ATTN_BLOCKS_PER_ITER + 1) if n_blk % p == 0)

    def blocks(it, carry):
        geo = [_band_block(it * per_iter + b, length, radius) for b in range(per_iter)]
        slabs = [(q0, start, variant, lo) for q0, start, variant in geo for lo in range(0, q_ref.shape[2], LANES)]
        biases = [bias_ref[variant, (lo % width) // HEAD_DIM + half] for _, _, variant, lo in slabs for half in (0, 1)]
        outs = _attend_pairs([q_ref[0, pl.ds(q0, qb), lo:lo + LANES] for q0, _, _, lo in slabs],
                             [k_ref[0, pl.ds(start, span), lo:lo + LANES] for _, start, _, lo in slabs],
                             [v_ref[0, pl.ds(start, span), lo:lo + LANES] for _, start, _, lo in slabs], biases)
        for (q0, _, _, lo), (o, lse) in zip(slabs, outs):
            o_ref[0, pl.ds(q0, qb), lo:lo + LANES] = o.astype(o_ref.dtype)
            lse_ref[0, pl.ds(q0, qb), lo:lo + LANES] = lse
        return carry

    lax.fori_loop(0, n_blk // per_iter, blocks, 0)


def _attn_c(q, k, v, bsz, window, dil):
    rows, dw = q.shape
    w = dw // dil
    length = rows // bsz
    radius = window // 2 // dil
    qb, span, _, _ = _band_shape(length, radius)
    per_step = max(1, min(dil, ATTN_STEP_ROWS // length))
    assert dil % per_step == 0
    view = lambda a: a.reshape(bsz, length, dw)
    spec = pl.BlockSpec((1, length, per_step * w), lambda i, r: (i, 0, r))
    o, lse = pl.pallas_call(
        functools.partial(_attn_c_kernel, dil=dil, radius=radius, n_heads=w // HEAD_DIM),
        grid=(bsz, dil // per_step),
        in_specs=[spec, spec, spec],
        out_specs=[spec, spec],
        out_shape=[jax.ShapeDtypeStruct((bsz, length, dw), BF16),
                   jax.ShapeDtypeStruct((bsz, length, dw), F32)],
        scratch_shapes=[pltpu.VMEM((3, w // HEAD_DIM, qb, span), F32)],
        compiler_params=_params(2),
        name=f"attn_c_d{dil}",
    )(view(q), view(k), view(v))
    return o.reshape(rows, dw), lse.reshape(rows, dw)


def _attn_d_kernel(sink_ref, q_ref, k_ref, v_ref, y_ref, bias_ref, *, radius, n_heads, group):
    length = q_ref.shape[1]
    qb, span, n_blk, _ = _band_shape(length, radius)
    pairs = n_heads * HEAD_DIM // LANES
    lower = lax.broadcasted_iota(jnp.int32, (1, LANES), 1) < HEAD_DIM

    @pl.when(pl.program_id(0) == 0)
    def _():
        _fill_band_bias(bias_ref, length, radius, 1.0, n_heads)

    per_iter = max(p for p in range(1, ATTN_BLOCKS_PER_ITER + 1) if n_blk % p == 0)

    def blocks(it, carry):
        geo = [_band_block(it * per_iter + b, length, radius) for b in range(per_iter)]
        slabs = [(q0, start, variant, j) for q0, start, variant in geo for j in range(pairs)]
        kv_lo = [(2 * j // group) * LANES for _, _, _, j in slabs]
        biases = [bias_ref[variant, 2 * j + half] for _, _, variant, j in slabs for half in (0, 1)]
        outs = _attend_pairs([q_ref[0, pl.ds(q0, qb), j * LANES:(j + 1) * LANES] for q0, _, _, j in slabs],
                             [k_ref[0, pl.ds(start, span), lo:lo + LANES] for (_, start, _, _), lo in zip(slabs, kv_lo)],
                             [v_ref[0, pl.ds(start, span), lo:lo + LANES] for (_, start, _, _), lo in zip(slabs, kv_lo)],
                             biases)
        for (q0, _, _, j), (o, lse) in zip(slabs, outs):
            sink = jnp.where(lower, sink_ref[2 * j], sink_ref[2 * j + 1])
            y_ref[0, pl.ds(q0, qb), j * LANES:(j + 1) * LANES] = (o * jax.nn.sigmoid(lse - sink)).astype(y_ref.dtype)
        return carry

    lax.fori_loop(0, n_blk // per_iter, blocks, 0)


def _attn_d(q, k, v, sink):
    b, s, w = q.shape
    kvw = k.shape[2]
    n_heads = w // HEAD_DIM
    group = n_heads // D_KV_HEADS
    assert kvw == D_KV_HEADS * LANES and group % 2 == 0
    qb, span, _, _ = _band_shape(s, D_RADIUS)
    return pl.pallas_call(
        functools.partial(_attn_d_kernel, radius=D_RADIUS, n_heads=n_heads, group=group),
        grid=(b,),
        in_specs=[pl.BlockSpec(memory_space=pltpu.SMEM),
                  pl.BlockSpec((1, s, w), lambda i: (i, 0, 0)),
                  pl.BlockSpec((1, s, kvw), lambda i: (i, 0, 0)),
                  pl.BlockSpec((1, s, kvw), lambda i: (i, 0, 0))],
        out_specs=pl.BlockSpec((1, s, w), lambda i: (i, 0, 0)),
        out_shape=jax.ShapeDtypeStruct((b, s, w), BF16),
        scratch_shapes=[pltpu.VMEM((3, n_heads, qb, span), F32)],
        compiler_params=_params(1),
        name="attn_d",
    )(sink.astype(F32), q, k, v)


def _merge_kernel(*refs, dils, cw):
    n_pat = len(dils)
    o_refs, l_refs = refs[:n_pat], refs[n_pat:2 * n_pat]
    yd_ref, cat_ref = refs[2 * n_pat], refs[2 * n_pat + 1]
    scratch = list(refs[2 * n_pat + 2:])
    tm = cat_ref.shape[0]
    chunks = cw // LANES
    staged = []
    for p, dil in enumerate(dils):
        if dil == 1:
            staged.append(None)
            continue
        o_s, l_s = scratch.pop(0), scratch.pop(0)
        n = tm // dil
        for r in range(dil):
            for c in range(chunks):
                lo = r * cw + c * LANES
                o_s[c, pl.ds(r, n, stride=dil), :] = o_refs[p][:, lo:lo + LANES].astype(F32)
                l_s[c, pl.ds(r, n, stride=dil), :] = l_refs[p][:, lo:lo + LANES]
        staged.append((o_s, l_s))
    for c in range(chunks):
        cols = slice(c * LANES, (c + 1) * LANES)
        os_ = [o_refs[p][:, cols].astype(F32) if st is None else st[0][c] for p, st in enumerate(staged)]
        ls_ = [l_refs[p][:, cols] if st is None else st[1][c] for p, st in enumerate(staged)]
        m = functools.reduce(jnp.maximum, ls_)
        es = [jnp.exp(l - m) for l in ls_]
        num = functools.reduce(lambda a, b: a + b, [e * o for e, o in zip(es, os_)])
        den = functools.reduce(lambda a, b: a + b, es)
        cat_ref[:, cols] = (num / den).astype(BF16)
    cat_ref[:, cw:] = yd_ref[...]


def _merge(os_, lses, yd, dils):
    t, dw = yd.shape
    cw = os_[0].shape[1] // dils[0]
    vspecs = [pl.BlockSpec((ROW_TILE // dil, dil * cw), lambda i: (i, 0)) for dil in dils]
    n_scratch = 2 * sum(1 for dil in dils if dil != 1)
    return pl.pallas_call(
        functools.partial(_merge_kernel, dils=dils, cw=cw),
        grid=(t // ROW_TILE,),
        in_specs=vspecs + vspecs + [pl.BlockSpec((ROW_TILE, dw), lambda i: (i, 0))],
        out_specs=pl.BlockSpec((ROW_TILE, cw + dw), lambda i: (i, 0)),
        out_shape=jax.ShapeDtypeStruct((t, cw + dw), BF16),
        scratch_shapes=[pltpu.VMEM((cw // LANES, ROW_TILE, LANES), F32)] * n_scratch,
        compiler_params=_params(1),
        name="merge",
    )(*os_, *lses, yd)


def _layer_norm(z, g, b):
    mu = jnp.mean(z, axis=-1, keepdims=True)
    zc = z - mu
    var = jnp.mean(zc * zc, axis=-1, keepdims=True)
    return zc * lax.rsqrt(var + LN_EPS) * g + b


ROUTE_ROWS = 8


def _route_tile(logits, triu, count_ref):
    n_exp, tm = logits.shape
    sub = lax.broadcasted_iota(jnp.int32, (n_exp, tm), 0)
    out_row = lax.broadcasted_iota(jnp.int32, (ROUTE_ROWS, tm), 0)
    work = logits
    vals, onehots = [], []
    idx_out = jnp.zeros((ROUTE_ROWS, tm), jnp.int32)
    for k in range(TOP_K):
        m = jnp.max(work, axis=0, keepdims=True)
        idx = jnp.min(jnp.where(work == m, sub, n_exp), axis=0, keepdims=True)
        hot = sub == idx
        work = jnp.where(hot, -jnp.inf, work)
        vals.append(m)
        onehots.append(hot)
        idx_out = jnp.where(out_row == k, idx, idx_out)
    exps = [jnp.exp(v - vals[0]) for v in vals]
    den = functools.reduce(lambda a, b: a + b, exps)
    gate_out = jnp.zeros((ROUTE_ROWS, tm), F32)
    for k in range(TOP_K):
        gate_out = jnp.where(out_row == k, exps[k] / den, gate_out)
    multihot = functools.reduce(lambda a, b: a + b, [h.astype(F32) for h in onehots])
    before = _dot(multihot.astype(BF16), triu) + count_ref[...]
    rank_out = jnp.zeros((ROUTE_ROWS, tm), jnp.int32)
    for k in range(TOP_K):
        rank = jnp.sum(jnp.where(onehots[k], before, 0.0), axis=0, keepdims=True)
        rank_out = jnp.where(out_row == k, rank.astype(jnp.int32), rank_out)
    count_ref[...] += jnp.sum(multihot, axis=1, keepdims=True)
    return idx_out, gate_out, rank_out


def _out_ln_kernel(cat_ref, w_ref, x_ref, g_ref, b_ref, rw_hi_ref, rw_lo_ref, rb_ref, triu_ref,
                   x1_ref, x1p_ref, idx_ref, gate_ref, rank_ref, count_ref, *, alpha):
    @pl.when(pl.program_id(0) == 0)
    def _():
        count_ref[...] = jnp.zeros_like(count_ref)

    sub = triu_ref.shape[0]
    tiles = [slice(a, a + sub) for a in range(0, x_ref.shape[0], sub)]
    mixes = [_dot(cat_ref[rows, :], w_ref[...]) for rows in tiles]
    splits = []
    for rows, mix in zip(tiles, mixes):
        x1 = _layer_norm(alpha * x_ref[rows, :] + mix, g_ref[...], b_ref[...])
        x1_ref[rows, :] = x1
        x1p_ref[rows, :] = _pack_pairs(x1)
        hi = x1.astype(BF16)
        splits.append((hi, (x1 - hi.astype(F32)).astype(BF16)))
    logits = [_dot_nt(rw_hi_ref[...], hi) + _dot_nt(rw_hi_ref[...], lo) + _dot_nt(rw_lo_ref[...], hi) + rb_ref[...]
              for hi, lo in splits]
    for rows, lg in zip(tiles, logits):
        idx_ref[:, rows], gate_ref[:, rows], rank_ref[:, rows] = _route_tile(lg, triu_ref[...], count_ref)


def _out_ln(cat, w_out, xt, g, b, router_w, router_b, alpha):
    t, d = xt.shape
    n_exp = router_w.shape[1]
    rw_t = router_w.T
    rw_hi = rw_t.astype(BF16)
    rw_lo = (rw_t - rw_hi.astype(F32)).astype(BF16)
    triu = jnp.triu(jnp.ones((ROW_TILE, ROW_TILE), BF16), 1)
    step = OUT_LN_SUBTILES * ROW_TILE if t % (OUT_LN_SUBTILES * ROW_TILE) == 0 else ROW_TILE
    row = lambda wd: pl.BlockSpec((step, wd), lambda i: (i, 0))
    col = pl.BlockSpec((ROUTE_ROWS, step), lambda i: (0, i))
    full = lambda a: pl.BlockSpec(a.shape, lambda i: (0,) * a.ndim)
    args = (cat, w_out.astype(BF16), xt, g.reshape(1, d), b.reshape(1, d), rw_hi, rw_lo,
            router_b.reshape(n_exp, 1), triu)
    return pl.pallas_call(
        functools.partial(_out_ln_kernel, alpha=alpha),
        grid=(t // step,),
        in_specs=[row(cat.shape[1]), full(args[1]), row(d)] + [full(a) for a in args[3:]],
        out_specs=[row(d), row(d // 2), col, col, col, pl.BlockSpec((n_exp, 1), lambda i: (0, 0))],
        out_shape=[jax.ShapeDtypeStruct((t, d), F32), jax.ShapeDtypeStruct((t, d // 2), jnp.int32),
                   jax.ShapeDtypeStruct((ROUTE_ROWS, t), jnp.int32), jax.ShapeDtypeStruct((ROUTE_ROWS, t), F32),
                   jax.ShapeDtypeStruct((ROUTE_ROWS, t), jnp.int32), jax.ShapeDtypeStruct((n_exp, 1), F32)],
        compiler_params=_params(1),
        name="out_ln",
    )(*args)


def _experts_kernel(first_ref, blocks_ref, count_ref, x_hbm, wgu_ref, bgu_ref, wd_ref, bd_ref, perm_ref,
                    y_hbm, wgu_s, wd_s, x_buf, y_buf, x_sem, y_sem, zero_sem):
    e = pl.program_id(0)
    last = pl.num_programs(0) - 1
    n_blk = blocks_ref[e]
    first = first_ref[e]
    count = count_ref[e]
    used = first_ref[last] + blocks_ref[last]
    half = GU_BLOCK // 2

    def rows_of(g):
        return pl.ds(pl.multiple_of(g * MOE_BLOCK, MOE_BLOCK), MOE_BLOCK)

    def x_copy(g, slot):
        return pltpu.make_async_copy(x_hbm.at[rows_of(g)], x_buf.at[slot], x_sem.at[slot])

    def y_copy(g, slot):
        return pltpu.make_async_copy(y_buf.at[slot], y_hbm.at[rows_of(g)], y_sem.at[slot])

    def compute(g, slot):
        xb = _unpack_pairs(x_buf[slot])
        row = lax.broadcasted_iota(jnp.int32, xb.shape, 0)
        xb = jnp.where(row < count - (g - first) * MOE_BLOCK, xb, jnp.zeros_like(xb))
        h = _dot(xb, wgu_s[...]) + bgu_ref[...]
        acts = []
        for c in range(h.shape[1] // GU_BLOCK):
            glu = jnp.minimum(h[:, c * GU_BLOCK:c * GU_BLOCK + half], SWIGLU_LIMIT)
            lin = jnp.clip(h[:, c * GU_BLOCK + half:(c + 1) * GU_BLOCK], -SWIGLU_LIMIT, SWIGLU_LIMIT)
            acts.append((glu * jax.nn.sigmoid(SWIGLU_ALPHA * glu) * (lin + 1.0)).astype(BF16))
        act = jnp.concatenate(acts, axis=1)
        y_buf[slot] = _pack_pairs(_dot(act, wd_s[...]) + bd_ref[...])

    @pl.when((e == 0) & (used > 0))
    def _():
        x_copy(0, 0).start()

    @pl.when(n_blk > 0)
    def _():
        for c in range(wgu_ref.shape[1] // GU_BLOCK):
            cols = slice(c * GU_BLOCK, (c + 1) * GU_BLOCK)
            wgu_s[:, cols] = _dot(wgu_ref[:, cols].astype(BF16), perm_ref[...]).astype(BF16)
        wd_s[...] = wd_ref[...].astype(BF16)

    def step(g, carry):
        slot = g % 2
        x_copy(g, slot).wait()

        @pl.when(g + 1 < used)
        def _():
            x_copy(g + 1, 1 - slot).start()

        @pl.when(g >= 2)
        def _():
            y_copy(g - 2, slot).wait()

        compute(g, slot)
        y_copy(g, slot).start()
        return carry

    lax.fori_loop(first, first + n_blk, step, 0)

    @pl.when(e == last)
    def _():
        @pl.when(used >= 1)
        def _():
            y_copy(0, (used - 1) % 2).wait()

        @pl.when(used >= 2)
        def _():
            y_copy(0, used % 2).wait()

        total = y_hbm.shape[0] // MOE_BLOCK
        y_buf[0] = jnp.zeros(y_buf.shape[1:], y_buf.dtype)

        def tail_copy(g):
            return pltpu.make_async_copy(y_buf.at[0], y_hbm.at[rows_of(g)], zero_sem)

        def start_all(g, carry):
            tail_copy(g).start()
            return carry

        def wait_all(g, carry):
            tail_copy(g).wait()
            return carry

        lax.fori_loop(used, total, start_all, 0)
        lax.fori_loop(used, total, wait_all, 0)


def _experts(x_rows, first_blk, n_blk, counts, layer, w_gu, b_gu, w_down, b_down):
    rows, dp = x_rows.shape
    _, n_exp, d, de2 = w_gu.shape
    assert de2 % GU_BLOCK == 0 and dp * 2 == d and rows % MOE_BLOCK == 0
    half = GU_BLOCK // 2
    j = jnp.arange(GU_BLOCK)
    src = jnp.where(j < half, 2 * j, 2 * (j - half) + 1)
    perm = (jnp.arange(GU_BLOCK)[:, None] == src[None, :]).astype(BF16)
    wspec = lambda a: pl.BlockSpec((None, None) + a.shape[2:], lambda e, *_: (layer, e, 0, 0))
    grid_spec = pltpu.PrefetchScalarGridSpec(
        num_scalar_prefetch=3,
        grid=(n_exp,),
        in_specs=[pl.BlockSpec(memory_space=pl.ANY),
                  wspec(w_gu), wspec(b_gu), wspec(w_down), wspec(b_down),
                  pl.BlockSpec((GU_BLOCK, GU_BLOCK), lambda e, *_: (0, 0))],
        out_specs=pl.BlockSpec(memory_space=pl.ANY),
        scratch_shapes=[pltpu.VMEM(w_gu.shape[2:], BF16), pltpu.VMEM(w_down.shape[2:], BF16),
                        pltpu.VMEM((2, MOE_BLOCK, dp), jnp.int32), pltpu.VMEM((2, MOE_BLOCK, dp), jnp.int32),
                        pltpu.SemaphoreType.DMA((2,)), pltpu.SemaphoreType.DMA((2,)),
                        pltpu.SemaphoreType.DMA(())],
    )
    return pl.pallas_call(
        _experts_kernel,
        grid_spec=grid_spec,
        out_shape=jax.ShapeDtypeStruct((rows, dp), jnp.int32),
        compiler_params=_params(1),
        name="experts",
    )(first_blk, n_blk, counts, x_rows, w_gu, b_gu, w_down, b_down, perm)


def _combine_ln_kernel(x1_ref, yk_ref, gate_ref, g_ref, b_ref, x2_ref, *, alpha):
    gates = gate_ref[...].T
    ffn = gates[:, 0:1] * _unpack_pairs(yk_ref[0]).astype(F32)
    for k in range(1, yk_ref.shape[0]):
        ffn = ffn + gates[:, k:k + 1] * _unpack_pairs(yk_ref[k]).astype(F32)
    x2_ref[...] = _layer_norm(alpha * x1_ref[...] + ffn, g_ref[...], b_ref[...])


def _combine_ln(x1, yk, gates, g, b, alpha):
    t, d = x1.shape
    k = yk.shape[0]
    tile = COMBINE_TILE if t % COMBINE_TILE == 0 else ROW_TILE
    return pl.pallas_call(
        functools.partial(_combine_ln_kernel, alpha=alpha),
        grid=(t // tile,),
        in_specs=[pl.BlockSpec((tile, d), lambda i: (i, 0)),
                  pl.BlockSpec((k, tile, d // 2), lambda i: (0, i, 0)),
                  pl.BlockSpec((gates.shape[0], tile), lambda i: (0, i)),
                  pl.BlockSpec((1, d), lambda i: (0, 0)),
                  pl.BlockSpec((1, d), lambda i: (0, 0))],
        out_specs=pl.BlockSpec((tile, d), lambda i: (i, 0)),
        out_shape=jax.ShapeDtypeStruct((t, d), F32),
        compiler_params=_params(1),
        name="combine_ln",
    )(x1, yk, gates, g.reshape(1, d), b.reshape(1, d))


def _route(top_idx, rank, counts):
    n_exp = counts.shape[0]
    experts = jnp.arange(n_exp, dtype=jnp.int32)
    counts = counts.reshape(n_exp).astype(jnp.int32)
    padded = (counts + MOE_BLOCK - 1) // MOE_BLOCK * MOE_BLOCK
    pad_end = jnp.cumsum(padded)
    pad_start = pad_end - padded
    start = jnp.sum(jnp.where(top_idx[None] == experts[:, None, None], pad_start[:, None, None], 0), axis=0)
    pos = rank + start
    first_blk = (pad_start // MOE_BLOCK).astype(jnp.int32)
    n_blk = (padded // MOE_BLOCK).astype(jnp.int32)
    return pos, first_blk, n_blk, counts


def _moe(x1, x1p, top_idx, gates, rank, counts, layer, w_gu, b_gu, w_down, b_down, g, b, alpha):
    t, d = x1.shape
    n_exp = counts.shape[0]
    pos, first_blk, n_blk, counts = _route(top_idx[:TOP_K], rank[:TOP_K], counts)
    rows = (-(-(t * TOP_K) // MOE_BLOCK) + n_exp) * MOE_BLOCK
    pos3 = pos.reshape(TOP_K, t // SC_CHUNK, SC_CHUNK).transpose(1, 0, 2)
    x_rows = _sc_scatter_rows(x1p, pos3, rows)
    y = _experts(x_rows, first_blk, n_blk, counts, layer, w_gu, b_gu, w_down, b_down)
    yk = _sc_gather_rows(y, pos.reshape(-1)).reshape(TOP_K, t, d // 2)
    return _combine_ln(x1, yk, gates, g, b, alpha)


def kernel(x, ev_w_in, ev_pool_w, ev_pool_scale, ev_conv_w, ev_w_out, od_w_in, od_sink, od_w_out,
           router_w, router_b, exp_w_gu, exp_b_gu, exp_w_down, exp_b_down, ln_g, ln_b):
    bsz, seq, d = x.shape
    t = bsz * seq
    depth = ln_g.shape[0]
    alpha = (2 * depth) ** 0.25
    n_exp = router_w.shape[2]
    c_width = d // 2
    dq_width = d // 2
    dkv_width = D_KV_HEADS * HEAD_DIM
    assert t % ROW_TILE == 0

    de2 = exp_b_gu.shape[-1]
    b_gu = exp_b_gu.reshape(depth, n_exp, de2 // GU_BLOCK, GU_BLOCK // 2, 2)
    b_gu = jnp.swapaxes(b_gu, -1, -2).reshape(depth, n_exp, 1, de2)
    b_down = exp_b_down.reshape(depth, n_exp, 1, -1)

    xt = x.reshape(t, d)
    for layer in range(depth):
        i = layer // 2
        if layer % 2 == 0:
            cat = _even_mix(xt.reshape(bsz, seq, d), ev_w_in[i], ev_pool_w[i], ev_pool_scale[i], ev_conv_w[i])
            cat = cat.reshape(t, -1)
            w_out = ev_w_out[i]
        else:
            qkv, (qd, kd, vd) = _odd_proj(xt, od_w_in[i], c_width, dq_width, dkv_width)
            dils = tuple(dil for _, dil in C_PATTERNS)
            os_, lses = [], []
            for (window, dil), (qc, kc, vc) in zip(C_PATTERNS, qkv):
                o, lse = _attn_c(qc, kc, vc, bsz, window, dil)
                os_.append(o)
                lses.append(lse)
            seq3 = lambda a: a.reshape(bsz, seq, a.shape[1])
            yd = _attn_d(seq3(qd), seq3(kd), seq3(vd), od_sink[i]).reshape(t, dq_width)
            cat = _merge(os_, lses, yd, dils)
            w_out = od_w_out[i]
        x1, x1p, top_idx, gates, rank, counts = _out_ln(cat, w_out, xt, ln_g[layer, 0], ln_b[layer, 0],
                                                        router_w[layer], router_b[layer], alpha)
        xt = _moe(x1, x1p, top_idx, gates, rank, counts, layer, exp_w_gu, b_gu, exp_w_down, b_down,
                  ln_g[layer, 1], ln_b[layer, 1], alpha)
    return xt.reshape(bsz, seq, d)
```

```python
import functools

import jax
import jax.numpy as jnp
from jax import lax
from jax.experimental import pallas as pl
from jax.experimental.pallas import tpu as pltpu
from jax.experimental.pallas import tpu_sc as plsc

HEAD_DIM = 64
POOL_WINDOWS = (2, 4, 8, 16)
C_PATTERNS = ((128, 1), (512, 4), (2048, 16))
D_KV_HEADS = 2
D_RADIUS = 128
TOP_K = 4
SWIGLU_LIMIT = 7.0
SWIGLU_ALPHA = 1.702
MOE_BLOCK = 512
LN_EPS = 1e-5
NEG_INF = -1e30
LOG2E = 1.4426950408889634

ROW_TILE = 512
ODD_PROJ_TILE = 1024
OUT_LN_SUBTILES = 2
COMBINE_TILE = 1024
ATTN_Q_BLOCK = 128
ATTN_BLOCKS_PER_ITER = 8
ATTN_STEP_ROWS = 512
LANES = 128
ROW_DMA_PRIORITY = 1
GU_BLOCK = 256
VMEM_LIMIT_BYTES = 56 * 1024 * 1024

F32 = jnp.float32
BF16 = jnp.bfloat16


def _params(n_axes=1):
    return pltpu.CompilerParams(dimension_semantics=("arbitrary",) * n_axes,
                                vmem_limit_bytes=VMEM_LIMIT_BYTES)


def _dot(a, b):
    return jnp.dot(a, b, preferred_element_type=F32)


def _dot_nt(a, b):
    return lax.dot_general(a, b, (((1,), (1,)), ((), ())), preferred_element_type=F32)


def _pack_pairs(a):
    n = a.shape[1] // 2
    bits = lax.bitcast_convert_type(a.astype(BF16).astype(F32), jnp.int32)
    return bits[:, :n] | lax.shift_right_logical(bits[:, n:], 16)


def _unpack_pairs(w):
    hi = lax.bitcast_convert_type(w & jnp.int32(-65536), F32)
    lo = lax.bitcast_convert_type(lax.shift_left(w, 16), F32)
    return jnp.concatenate([hi, lo], axis=1).astype(BF16)


SC_CORES = 2
SC_SUBCORES = 16
SC_WORKERS = SC_CORES * SC_SUBCORES
SC_CHUNK = 64


def _sc_mesh():
    return plsc.VectorSubcoreMesh(core_axis_name="c", subcore_axis_name="s")


def _sc_worker():
    return lax.axis_index("s") * SC_CORES + lax.axis_index("c")


def _sc_gather_rows(table, idx):
    n, w = idx.shape[0], table.shape[1]
    per_w = n // SC_WORKERS
    assert n % SC_WORKERS == 0 and per_w % (2 * SC_CHUNK) == 0

    def body(table_hbm, idx_hbm, out_hbm, i0, i1, r0, r1, s0, s1):
        base = _sc_worker() * per_w

        def start(off, iv, rv, sem):
            pltpu.sync_copy(idx_hbm.at[pl.ds(off, SC_CHUNK)], iv)
            pltpu.async_copy(table_hbm.at[iv], rv, sem)

        def finish(off, iv, rv, sem):
            pltpu.make_async_copy(table_hbm.at[iv], rv, sem).wait()
            pltpu.sync_copy(rv, out_hbm.at[pl.ds(off, SC_CHUNK)])

        start(base, i0, r0, s0)

        @pl.loop(0, per_w, step=2 * SC_CHUNK)
        def _(o):
            off = base + o
            start(off + SC_CHUNK, i1, r1, s1)
            finish(off, i0, r0, s0)

            @pl.when(o + 2 * SC_CHUNK < per_w)
            def _():
                start(off + 2 * SC_CHUNK, i0, r0, s0)

            finish(off + SC_CHUNK, i1, r1, s1)

    return pl.kernel(
        body, mesh=_sc_mesh(),
        out_type=jax.ShapeDtypeStruct((n, w), table.dtype),
        scratch_types=[pltpu.VMEM((SC_CHUNK,), jnp.int32), pltpu.VMEM((SC_CHUNK,), jnp.int32),
                       pltpu.VMEM((SC_CHUNK, w), table.dtype), pltpu.VMEM((SC_CHUNK, w), table.dtype),
                       pltpu.SemaphoreType.DMA, pltpu.SemaphoreType.DMA],
    )(table, idx)


def _sc_scatter_rows(src, pos3, n_out):
    t, w = src.shape
    kk = pos3.shape[1]
    per_w = t // SC_WORKERS
    assert t % SC_WORKERS == 0 and per_w % SC_CHUNK == 0 and pos3.shape == (t // SC_CHUNK, kk, SC_CHUNK)

    def body(src_hbm, pos_hbm, out_hbm, iv, rv, sem):
        base = _sc_worker() * per_w

        @pl.loop(0, per_w, step=SC_CHUNK)
        def _(o):
            off = base + o
            pltpu.sync_copy(pos_hbm.at[off // SC_CHUNK], iv)
            pltpu.sync_copy(src_hbm.at[pl.ds(off, SC_CHUNK)], rv)
            copies = [pltpu.async_copy(rv, out_hbm.at[iv.at[j]], sem) for j in range(kk)]
            for cp in copies:
                cp.wait()

    return pl.kernel(
        body, mesh=_sc_mesh(),
        out_type=jax.ShapeDtypeStruct((n_out, w), src.dtype),
        scratch_types=[pltpu.VMEM((kk, SC_CHUNK), jnp.int32), pltpu.VMEM((SC_CHUNK, w), src.dtype),
                       pltpu.SemaphoreType.DMA],
    )(src, pos3)


def _shift_down(a, k, row):
    return jnp.where(row >= k, pltpu.roll(a, k, axis=0), 0.0)


def _shift_up(a, k, row):
    n = a.shape[0]
    return jnp.where(row < n - k, pltpu.roll(a, n - k, axis=0), 0.0)


def _even_mix_kernel(x_ref, w_in_ref, pool_w_ref, pool_scale_ref, conv_w_ref, cat_ref, xb_ref):
    s = x_ref.shape[1]
    pool_width = pool_scale_ref.shape[1]
    gd = pool_width // len(POOL_WINDOWS)
    conv_width = conv_w_ref.shape[1]
    xb_ref[...] = x_ref[0].astype(BF16)
    xb = xb_ref[...]

    row = lax.broadcasted_iota(jnp.int32, (s, gd), 0)
    cw = 256
    rowc = lax.broadcasted_iota(jnp.int32, (s, cw), 0)

    def pool_group(g, u):
        w, lo = POOL_WINDOWS[g], g * gd
        half = w // 2
        back, fwd, span = u, u, 1
        while span < half:
            back = back + _shift_down(back, span, row)
            fwd = fwd + _shift_up(fwd, span, row)
            span *= 2
        win = _shift_down(back, 1, row) + fwd
        cnt = (jnp.minimum(row + (w - half), s) - jnp.maximum(row - half, 0)).astype(F32)
        pooled = win / cnt - u
        mixed = _dot(pooled.astype(BF16), pool_w_ref[g])
        cat_ref[0, :, lo:lo + gd] = (mixed * pool_scale_ref[:, lo:lo + gd]).astype(BF16)

    def conv_project(j):
        c0 = j * cw
        return tuple(_dot(xb, w_in_ref[:, pool_width + k * conv_width + c0:pool_width + k * conv_width + c0 + cw])
                     for k in range(3))

    def conv_chunk(j, b_gate, c_gate, v):
        c0 = j * cw
        u = c_gate * v
        conv = (_shift_down(u, 1, rowc) * conv_w_ref[0:1, c0:c0 + cw] + u * conv_w_ref[1:2, c0:c0 + cw]
                + _shift_up(u, 1, rowc) * conv_w_ref[2:3, c0:c0 + cw])
        cat_ref[0, :, pool_width + c0:pool_width + c0 + cw] = (b_gate * conv).astype(BF16)

    n_conv = conv_width // cw
    pool_u = [_dot(xb, w_in_ref[:, lo:lo + 2 * gd]) for lo in range(0, pool_width, 2 * gd)]
    ahead = conv_project(0)
    for g in range(len(POOL_WINDOWS)):
        pool_group(g, pool_u[g // 2][:, (g % 2) * gd:(g % 2 + 1) * gd])
    for j in range(n_conv):
        cur, ahead = ahead, (conv_project(j + 1) if j + 1 < n_conv else None)
        conv_chunk(j, *cur)


def _even_mix(x, w_in, pool_w, pool_scale, conv_w):
    b, s, d = x.shape
    pool_width = pool_scale.shape[0]
    conv_width = conv_w.shape[1]
    assert conv_width % 256 == 0 and w_in.shape[1] == pool_width + 3 * conv_width
    cat_width = pool_width + conv_width
    return pl.pallas_call(
        _even_mix_kernel,
        grid=(b,),
        in_specs=[
            pl.BlockSpec((1, s, d), lambda i: (i, 0, 0)),
            pl.BlockSpec(w_in.shape, lambda i: (0, 0)),
            pl.BlockSpec(pool_w.shape, lambda i: (0, 0, 0)),
            pl.BlockSpec((1, pool_width), lambda i: (0, 0)),
            pl.BlockSpec(conv_w.shape, lambda i: (0, 0)),
        ],
        out_specs=pl.BlockSpec((1, s, cat_width), lambda i: (i, 0, 0)),
        out_shape=jax.ShapeDtypeStruct((b, s, cat_width), BF16),
        scratch_shapes=[pltpu.VMEM((s, d), BF16)],
        compiler_params=_params(1),
        name="even_mix",
    )(x, w_in.astype(BF16), pool_w.astype(BF16), pool_scale.reshape(1, pool_width), conv_w)


def _odd_proj_kernel(x_ref, w_ref, *refs, c_width, dils):
    outs, h_ref = refs[:-1], refs[-1]
    xb = x_ref[...].astype(BF16)
    tm = x_ref.shape[0]
    chunks = c_width // LANES
    for j in range(3):
        h = _dot(xb, w_ref[:, j * c_width:(j + 1) * c_width])
        for c in range(chunks):
            h_ref[j, c] = h[:, c * LANES:(c + 1) * LANES]
        for pi, dil in enumerate(dils):
            n = tm // dil
            for r in range(dil):
                for c in range(chunks):
                    src = h_ref[j, c] if dil == 1 else h_ref[j, c, pl.ds(r, n, stride=dil), :]
                    lo = r * c_width + c * LANES
                    outs[3 * pi + j][:, lo:lo + LANES] = src.astype(BF16)
    c0 = 3 * c_width
    for ref in outs[3 * len(dils):]:
        wd = ref.shape[1]
        ref[...] = _dot(xb, w_ref[:, c0:c0 + wd]).astype(BF16)
        c0 += wd


def _odd_proj(xt, w_in, c_width, dq_width, dkv_width):
    t, d = xt.shape
    dils = tuple(dil for _, dil in C_PATTERNS)
    tile = ODD_PROJ_TILE if t % ODD_PROJ_TILE == 0 else ROW_TILE
    assert 3 * c_width + dq_width + 2 * dkv_width == w_in.shape[1] and c_width % LANES == 0
    col = jnp.arange(w_in.shape[1])
    is_q = (col < c_width) | ((col >= 3 * c_width) & (col < 3 * c_width + dq_width))
    w_in = w_in * jnp.where(is_q, HEAD_DIM ** -0.5 * LOG2E, 1.0)
    kv0 = 3 * c_width + dq_width
    twice = lambda w: jnp.repeat(w.reshape(d, -1, HEAD_DIM), LANES // HEAD_DIM, axis=1).reshape(d, -1)
    w_in = jnp.concatenate([w_in[:, :kv0], twice(w_in[:, kv0:kv0 + dkv_width]), twice(w_in[:, kv0 + dkv_width:])], axis=1)
    dkv_width = dkv_width * (LANES // HEAD_DIM)
    w_scaled = w_in.astype(BF16)
    shapes = [(t // dil, dil * c_width) for dil in dils for _ in range(3)]
    blocks = [(tile // dil, dil * c_width) for dil in dils for _ in range(3)]
    for wd in (dq_width, dkv_width, dkv_width):
        shapes.append((t, wd))
        blocks.append((tile, wd))
    outs = pl.pallas_call(
        functools.partial(_odd_proj_kernel, c_width=c_width, dils=dils),
        grid=(t // tile,),
        in_specs=[pl.BlockSpec((tile, d), lambda i: (i, 0)),
                  pl.BlockSpec(w_in.shape, lambda i: (0, 0))],
        out_specs=[pl.BlockSpec(blk, lambda i: (i, 0)) for blk in blocks],
        out_shape=[jax.ShapeDtypeStruct(shp, BF16) for shp in shapes],
        scratch_shapes=[pltpu.VMEM((3, c_width // LANES, tile, LANES), F32)],
        compiler_params=_params(1),
        name="odd_proj",
    )(xt, w_scaled)
    qkv = [outs[3 * pi:3 * pi + 3] for pi in range(len(dils))]
    return qkv, outs[3 * len(dils):]


def _band_shape(length, radius):
    qb = min(ATTN_Q_BLOCK, length)
    span = min(length, qb + 2 * radius)
    n_blk = length // qb
    assert length % qb == 0 and (n_blk <= 2 or radius <= qb)
    return qb, span, n_blk, (0, -radius, qb - span)


def _fill_band_bias(bias_ref, length, radius, dist_unit, n_heads):
    qb, span, _, offsets = _band_shape(length, radius)
    rel = lax.broadcasted_iota(jnp.int32, (qb, span), 1) - lax.broadcasted_iota(jnp.int32, (qb, span), 0)
    for v, off in enumerate(offsets):
        dist = jnp.abs(rel + off)
        mask = jnp.where(dist <= radius, 0.0, NEG_INF).astype(F32)
        far = dist.astype(F32) * (dist_unit * LOG2E)
        for hh in range(n_heads):
            bias_ref[v, hh] = mask - 2.0 ** (-8.0 * (hh + 1) / n_heads) * far


def _band_block(i, length, radius):
    qb, span, n_blk, offsets = _band_shape(length, radius)
    q0 = pl.multiple_of(i * qb, qb)
    variant = jnp.where(i == 0, 0, jnp.where(i == n_blk - 1, 2, 1))
    off = jnp.where(i == 0, offsets[0], jnp.where(i == n_blk - 1, offsets[2], offsets[1]))
    align = 8
    for cand in (128, 64, 32, 16):
        if all(o % cand == 0 for o in offsets) and qb % cand == 0:
            align = cand
            break
    return q0, pl.multiple_of(q0 + off, align), variant


def _attend_pairs(q_slabs, k_slabs, v_slabs, biases):
    lower = lax.broadcasted_iota(jnp.int32, (1, LANES), 1) < HEAD_DIM
    scores = []
    for j, (q2, k2) in enumerate(zip(q_slabs, k_slabs)):
        zero = jnp.zeros_like(q2)
        scores.append(_dot_nt(jnp.where(lower, q2, zero), k2) + biases[2 * j])
        scores.append(_dot_nt(jnp.where(lower, zero, q2), k2) + biases[2 * j + 1])
    probs = []
    for s in scores:
        m = jnp.max(s, axis=-1, keepdims=True)
        probs.append((jnp.exp2(s - m).astype(BF16), m))
    ones = jnp.ones(v_slabs[0].shape, BF16)
    outs = []
    for j, v2 in enumerate(v_slabs):
        (p_lo, m_lo), (p_up, m_up) = probs[2 * j], probs[2 * j + 1]
        v_ones = jnp.concatenate([v2, ones], axis=1)
        r_lo, r_up = _dot(p_lo, v_ones), _dot(p_up, v_ones)
        l = jnp.where(lower, r_lo[:, LANES:], r_up[:, LANES:])
        o = jnp.where(lower, r_lo[:, :LANES], r_up[:, :LANES]) / l
        outs.append((o, (jnp.where(lower, m_lo, m_up) + jnp.log2(l)) * (1.0 / LOG2E)))
    return outs


def _attn_c_kernel(q_ref, k_ref, v_ref, o_ref, lse_ref, bias_ref, *, dil, radius, n_heads):
    length = q_ref.shape[1]
    qb, span, n_blk, _ = _band_shape(length, radius)
    width = n_heads * HEAD_DIM

    @pl.when((pl.program_id(0) == 0) & (pl.program_id(1) == 0))
    def _():
        _fill_band_bias(bias_ref, length, radius, float(dil), n_heads)

    per_iter = max(p for p in range(1, ATTN_BLOCKS_PER_ITER + 1) if n_blk % p == 0)

    def blocks(it, carry):
        geo = [_band_block(it * per_iter + b, length, radius) for b in range(per_iter)]
        slabs = [(q0, start, variant, lo) for q0, start, variant in geo for lo in range(0, q_ref.shape[2], LANES)]
        biases = [bias_ref[variant, (lo % width) // HEAD_DIM + half] for _, _, variant, lo in slabs for half in (0, 1)]
        outs = _attend_pairs([q_ref[0, pl.ds(q0, qb), lo:lo + LANES] for q0, _, _, lo in slabs],
                             [k_ref[0, pl.ds(start, span), lo:lo + LANES] for _, start, _, lo in slabs],
                             [v_ref[0, pl.ds(start, span), lo:lo + LANES] for _, start, _, lo in slabs], biases)
        for (q0, _, _, lo), (o, lse) in zip(slabs, outs):
            o_ref[0, pl.ds(q0, qb), lo:lo + LANES] = o.astype(o_ref.dtype)
            lse_ref[0, pl.ds(q0, qb), lo:lo + LANES] = lse
        return carry

    lax.fori_loop(0, n_blk // per_iter, blocks, 0)


def _attn_c(q, k, v, bsz, window, dil):
    rows, dw = q.shape
    w = dw // dil
    length = rows // bsz
    radius = window // 2 // dil
    qb, span, _, _ = _band_shape(length, radius)
    per_step = max(1, min(dil, ATTN_STEP_ROWS // length))
    assert dil % per_step == 0
    view = lambda a: a.reshape(bsz, length, dw)
    spec = pl.BlockSpec((1, length, per_step * w), lambda i, r: (i, 0, r))
    o, lse = pl.pallas_call(
        functools.partial(_attn_c_kernel, dil=dil, radius=radius, n_heads=w // HEAD_DIM),
        grid=(bsz, dil // per_step),
        in_specs=[spec, spec, spec],
        out_specs=[spec, spec],
        out_shape=[jax.ShapeDtypeStruct((bsz, length, dw), BF16),
                   jax.ShapeDtypeStruct((bsz, length, dw), F32)],
        scratch_shapes=[pltpu.VMEM((3, w // HEAD_DIM, qb, span), F32)],
        compiler_params=_params(2),
        name=f"attn_c_d{dil}",
    )(view(q), view(k), view(v))
    return o.reshape(rows, dw), lse.reshape(rows, dw)


def _attn_d_kernel(sink_ref, q_ref, k_ref, v_ref, y_ref, bias_ref, *, radius, n_heads, group):
    length = q_ref.shape[1]
    qb, span, n_blk, _ = _band_shape(length, radius)
    pairs = n_heads * HEAD_DIM // LANES
    lower = lax.broadcasted_iota(jnp.int32, (1, LANES), 1) < HEAD_DIM

    @pl.when(pl.program_id(0) == 0)
    def _():
        _fill_band_bias(bias_ref, length, radius, 1.0, n_heads)

    per_iter = max(p for p in range(1, ATTN_BLOCKS_PER_ITER + 1) if n_blk % p == 0)

    def blocks(it, carry):
        geo = [_band_block(it * per_iter + b, length, radius) for b in range(per_iter)]
        slabs = [(q0, start, variant, j) for q0, start, variant in geo for j in range(pairs)]
        kv_lo = [(2 * j // group) * LANES for _, _, _, j in slabs]
        biases = [bias_ref[variant, 2 * j + half] for _, _, variant, j in slabs for half in (0, 1)]
        outs = _attend_pairs([q_ref[0, pl.ds(q0, qb), j * LANES:(j + 1) * LANES] for q0, _, _, j in slabs],
                             [k_ref[0, pl.ds(start, span), lo:lo + LANES] for (_, start, _, _), lo in zip(slabs, kv_lo)],
                             [v_ref[0, pl.ds(start, span), lo:lo + LANES] for (_, start, _, _), lo in zip(slabs, kv_lo)],
                             biases)
        for (q0, _, _, j), (o, lse) in zip(slabs, outs):
            sink = jnp.where(lower, sink_ref[2 * j], sink_ref[2 * j + 1])
            y_ref[0, pl.ds(q0, qb), j * LANES:(j + 1) * LANES] = (o * jax.nn.sigmoid(lse - sink)).astype(y_ref.dtype)
        return carry

    lax.fori_loop(0, n_blk // per_iter, blocks, 0)


def _attn_d(q, k, v, sink):
    b, s, w = q.shape
    kvw = k.shape[2]
    n_heads = w // HEAD_DIM
    group = n_heads // D_KV_HEADS
    assert kvw == D_KV_HEADS * LANES and group % 2 == 0
    qb, span, _, _ = _band_shape(s, D_RADIUS)
    return pl.pallas_call(
        functools.partial(_attn_d_kernel, radius=D_RADIUS, n_heads=n_heads, group=group),
        grid=(b,),
        in_specs=[pl.BlockSpec(memory_space=pltpu.SMEM),
                  pl.BlockSpec((1, s, w), lambda i: (i, 0, 0)),
                  pl.BlockSpec((1, s, kvw), lambda i: (i, 0, 0)),
                  pl.BlockSpec((1, s, kvw), lambda i: (i, 0, 0))],
        out_specs=pl.BlockSpec((1, s, w), lambda i: (i, 0, 0)),
        out_shape=jax.ShapeDtypeStruct((b, s, w), BF16),
        scratch_shapes=[pltpu.VMEM((3, n_heads, qb, span), F32)],
        compiler_params=_params(1),
        name="attn_d",
    )(sink.astype(F32), q, k, v)


def _merge_kernel(*refs, dils, cw):
    n_pat = len(dils)
    o_refs, l_refs = refs[:n_pat], refs[n_pat:2 * n_pat]
    yd_ref, cat_ref = refs[2 * n_pat], refs[2 * n_pat + 1]
    scratch = list(refs[2 * n_pat + 2:])
    tm = cat_ref.shape[0]
    chunks = cw // LANES
    staged = []
    for p, dil in enumerate(dils):
        if dil == 1:
            staged.append(None)
            continue
        o_s, l_s = scratch.pop(0), scratch.pop(0)
        n = tm // dil
        for r in range(dil):
            for c in range(chunks):
                lo = r * cw + c * LANES
                o_s[c, pl.ds(r, n, stride=dil), :] = o_refs[p][:, lo:lo + LANES].astype(F32)
                l_s[c, pl.ds(r, n, stride=dil), :] = l_refs[p][:, lo:lo + LANES]
        staged.append((o_s, l_s))
    for c in range(chunks):
        cols = slice(c * LANES, (c + 1) * LANES)
        os_ = [o_refs[p][:, cols].astype(F32) if st is None else st[0][c] for p, st in enumerate(staged)]
        ls_ = [l_refs[p][:, cols] if st is None else st[1][c] for p, st in enumerate(staged)]
        m = functools.reduce(jnp.maximum, ls_)
        es = [jnp.exp(l - m) for l in ls_]
        num = functools.reduce(lambda a, b: a + b, [e * o for e, o in zip(es, os_)])
        den = functools.reduce(lambda a, b: a + b, es)
        cat_ref[:, cols] = (num / den).astype(BF16)
    cat_ref[:, cw:] = yd_ref[...]


def _merge(os_, lses, yd, dils):
    t, dw = yd.shape
    cw = os_[0].shape[1] // dils[0]
    vspecs = [pl.BlockSpec((ROW_TILE // dil, dil * cw), lambda i: (i, 0)) for dil in dils]
    n_scratch = 2 * sum(1 for dil in dils if dil != 1)
    return pl.pallas_call(
        functools.partial(_merge_kernel, dils=dils, cw=cw),
        grid=(t // ROW_TILE,),
        in_specs=vspecs + vspecs + [pl.BlockSpec((ROW_TILE, dw), lambda i: (i, 0))],
        out_specs=pl.BlockSpec((ROW_TILE, cw + dw), lambda i: (i, 0)),
        out_shape=jax.ShapeDtypeStruct((t, cw + dw), BF16),
        scratch_shapes=[pltpu.VMEM((cw // LANES, ROW_TILE, LANES), F32)] * n_scratch,
        compiler_params=_params(1),
        name="merge",
    )(*os_, *lses, yd)


def _layer_norm(z, g, b):
    mu = jnp.mean(z, axis=-1, keepdims=True)
    zc = z - mu
    var = jnp.mean(zc * zc, axis=-1, keepdims=True)
    return zc * lax.rsqrt(var + LN_EPS) * g + b


ROUTE_ROWS = 8


def _route_tile(logits, triu, count_ref):
    n_exp, tm = logits.shape
    sub = lax.broadcasted_iota(jnp.int32, (n_exp, tm), 0)
    out_row = lax.broadcasted_iota(jnp.int32, (ROUTE_ROWS, tm), 0)
    work = logits
    vals, onehots = [], []
    idx_out = jnp.zeros((ROUTE_ROWS, tm), jnp.int32)
    for k in range(TOP_K):
        m = jnp.max(work, axis=0, keepdims=True)
        idx = jnp.min(jnp.where(work == m, sub, n_exp), axis=0, keepdims=True)
        hot = sub == idx
        work = jnp.where(hot, -jnp.inf, work)
        vals.append(m)
        onehots.append(hot)
        idx_out = jnp.where(out_row == k, idx, idx_out)
    exps = [jnp.exp(v - vals[0]) for v in vals]
    den = functools.reduce(lambda a, b: a + b, exps)
    gate_out = jnp.zeros((ROUTE_ROWS, tm), F32)
    for k in range(TOP_K):
        gate_out = jnp.where(out_row == k, exps[k] / den, gate_out)
    multihot = functools.reduce(lambda a, b: a + b, [h.astype(F32) for h in onehots])
    before = _dot(multihot.astype(BF16), triu) + count_ref[...]
    rank_out = jnp.zeros((ROUTE_ROWS, tm), jnp.int32)
    for k in range(TOP_K):
        rank = jnp.sum(jnp.where(onehots[k], before, 0.0), axis=0, keepdims=True)
        rank_out = jnp.where(out_row == k, rank.astype(jnp.int32), rank_out)
    count_ref[...] += jnp.sum(multihot, axis=1, keepdims=True)
    return idx_out, gate_out, rank_out


def _out_ln_kernel(cat_ref, w_ref, x_ref, g_ref, b_ref, rw_hi_ref, rw_lo_ref, rb_ref, triu_ref,
                   x1_ref, x1p_ref, idx_ref, gate_ref, rank_ref, count_ref, *, alpha):
    @pl.when(pl.program_id(0) == 0)
    def _():
        count_ref[...] = jnp.zeros_like(count_ref)

    sub = triu_ref.shape[0]
    tiles = [slice(a, a + sub) for a in range(0, x_ref.shape[0], sub)]
    mixes = [_dot(cat_ref[rows, :], w_ref[...]) for rows in tiles]
    splits = []
    for rows, mix in zip(tiles, mixes):
        x1 = _layer_norm(alpha * x_ref[rows, :] + mix, g_ref[...], b_ref[...])
        x1_ref[rows, :] = x1
        x1p_ref[rows, :] = _pack_pairs(x1)
        hi = x1.astype(BF16)
        splits.append((hi, (x1 - hi.astype(F32)).astype(BF16)))
    logits = [_dot_nt(rw_hi_ref[...], hi) + _dot_nt(rw_hi_ref[...], lo) + _dot_nt(rw_lo_ref[...], hi) + rb_ref[...]
              for hi, lo in splits]
    for rows, lg in zip(tiles, logits):
        idx_ref[:, rows], gate_ref[:, rows], rank_ref[:, rows] = _route_tile(lg, triu_ref[...], count_ref)


def _out_ln(cat, w_out, xt, g, b, router_w, router_b, alpha):
    t, d = xt.shape
    n_exp = router_w.shape[1]
    rw_t = router_w.T
    rw_hi = rw_t.astype(BF16)
    rw_lo = (rw_t - rw_hi.astype(F32)).astype(BF16)
    triu = jnp.triu(jnp.ones((ROW_TILE, ROW_TILE), BF16), 1)
    step = OUT_LN_SUBTILES * ROW_TILE if t % (OUT_LN_SUBTILES * ROW_TILE) == 0 else ROW_TILE
    row = lambda wd: pl.BlockSpec((step, wd), lambda i: (i, 0))
    col = pl.BlockSpec((ROUTE_ROWS, step), lambda i: (0, i))
    full = lambda a: pl.BlockSpec(a.shape, lambda i: (0,) * a.ndim)
    args = (cat, w_out.astype(BF16), xt, g.reshape(1, d), b.reshape(1, d), rw_hi, rw_lo,
            router_b.reshape(n_exp, 1), triu)
    return pl.pallas_call(
        functools.partial(_out_ln_kernel, alpha=alpha),
        grid=(t // step,),
        in_specs=[row(cat.shape[1]), full(args[1]), row(d)] + [full(a) for a in args[3:]],
        out_specs=[row(d), row(d // 2), col, col, col, pl.BlockSpec((n_exp, 1), lambda i: (0, 0))],
        out_shape=[jax.ShapeDtypeStruct((t, d), F32), jax.ShapeDtypeStruct((t, d // 2), jnp.int32),
                   jax.ShapeDtypeStruct((ROUTE_ROWS, t), jnp.int32), jax.ShapeDtypeStruct((ROUTE_ROWS, t), F32),
                   jax.ShapeDtypeStruct((ROUTE_ROWS, t), jnp.int32), jax.ShapeDtypeStruct((n_exp, 1), F32)],
        compiler_params=_params(1),
        name="out_ln",
    )(*args)


def _experts_kernel(first_ref, blocks_ref, count_ref, x_hbm, wgu_ref, bgu_ref, wd_ref, bd_ref, perm_ref,
                    y_hbm, wgu_s, wd_s, x_buf, y_buf, x_sem, y_sem, zero_sem):
    e = pl.program_id(0)
    last = pl.num_programs(0) - 1
    n_blk = blocks_ref[e]
    first = first_ref[e]
    count = count_ref[e]
    used = first_ref[last] + blocks_ref[last]
    half = GU_BLOCK // 2

    def rows_of(g):
        return pl.ds(pl.multiple_of(g * MOE_BLOCK, MOE_BLOCK), MOE_BLOCK)

    def x_copy(g, slot):
        return pltpu.make_async_copy(x_hbm.at[rows_of(g)], x_buf.at[slot], x_sem.at[slot])

    def y_copy(g, slot):
        return pltpu.make_async_copy(y_buf.at[slot], y_hbm.at[rows_of(g)], y_sem.at[slot])

    def compute(g, slot):
        xb = _unpack_pairs(x_buf[slot])
        row = lax.broadcasted_iota(jnp.int32, xb.shape, 0)
        xb = jnp.where(row < count - (g - first) * MOE_BLOCK, xb, jnp.zeros_like(xb))
        h = _dot(xb, wgu_s[...]) + bgu_ref[...]
        acts = []
        for c in range(h.shape[1] // GU_BLOCK):
            glu = jnp.minimum(h[:, c * GU_BLOCK:c * GU_BLOCK + half], SWIGLU_LIMIT)
            lin = jnp.clip(h[:, c * GU_BLOCK + half:(c + 1) * GU_BLOCK], -SWIGLU_LIMIT, SWIGLU_LIMIT)
            acts.append((glu * jax.nn.sigmoid(SWIGLU_ALPHA * glu) * (lin + 1.0)).astype(BF16))
        act = jnp.concatenate(acts, axis=1)
        y_buf[slot] = _pack_pairs(_dot(act, wd_s[...]) + bd_ref[...])

    @pl.when((e == 0) & (used > 0))
    def _():
        x_copy(0, 0).start(priority=ROW_DMA_PRIORITY)

    @pl.when(n_blk > 0)
    def _():
        for c in range(wgu_ref.shape[1] // GU_BLOCK):
            cols = slice(c * GU_BLOCK, (c + 1) * GU_BLOCK)
            wgu_s[:, cols] = _dot(wgu_ref[:, cols].astype(BF16), perm_ref[...]).astype(BF16)
        wd_s[...] = wd_ref[...].astype(BF16)

    def step(g, carry):
        slot = g % 2
        x_copy(g, slot).wait()

        @pl.when(g + 1 < used)
        def _():
            x_copy(g + 1, 1 - slot).start(priority=ROW_DMA_PRIORITY)

        @pl.when(g >= 2)
        def _():
            y_copy(g - 2, slot).wait()

        compute(g, slot)
        y_copy(g, slot).start(priority=ROW_DMA_PRIORITY)
        return carry

    lax.fori_loop(first, first + n_blk, step, 0)

    @pl.when(e == last)
    def _():
        @pl.when(used >= 1)
        def _():
            y_copy(0, (used - 1) % 2).wait()

        @pl.when(used >= 2)
        def _():
            y_copy(0, used % 2).wait()

        total = y_hbm.shape[0] // MOE_BLOCK
        y_buf[0] = jnp.zeros(y_buf.shape[1:], y_buf.dtype)

        def tail_copy(g):
            return pltpu.make_async_copy(y_buf.at[0], y_hbm.at[rows_of(g)], zero_sem)

        def start_all(g, carry):
            tail_copy(g).start()
            return carry

        def wait_all(g, carry):
            tail_copy(g).wait()
            return carry

        lax.fori_loop(used, total, start_all, 0)
        lax.fori_loop(used, total, wait_all, 0)


def _experts(x_rows, first_blk, n_blk, counts, layer, w_gu, b_gu, w_down, b_down):
    rows, dp = x_rows.shape
    _, n_exp, d, de2 = w_gu.shape
    assert de2 % GU_BLOCK == 0 and dp * 2 == d and rows % MOE_BLOCK == 0
    half = GU_BLOCK // 2
    j = jnp.arange(GU_BLOCK)
    src = jnp.where(j < half, 2 * j, 2 * (j - half) + 1)
    perm = (jnp.arange(GU_BLOCK)[:, None] == src[None, :]).astype(BF16)
    wspec = lambda a: pl.BlockSpec((None, None) + a.shape[2:], lambda e, *_: (layer, e, 0, 0))
    grid_spec = pltpu.PrefetchScalarGridSpec(
        num_scalar_prefetch=3,
        grid=(n_exp,),
        in_specs=[pl.BlockSpec(memory_space=pl.ANY),
                  wspec(w_gu), wspec(b_gu), wspec(w_down), wspec(b_down),
                  pl.BlockSpec((GU_BLOCK, GU_BLOCK), lambda e, *_: (0, 0))],
        out_specs=pl.BlockSpec(memory_space=pl.ANY),
        scratch_shapes=[pltpu.VMEM(w_gu.shape[2:], BF16), pltpu.VMEM(w_down.shape[2:], BF16),
                        pltpu.VMEM((2, MOE_BLOCK, dp), jnp.int32), pltpu.VMEM((2, MOE_BLOCK, dp), jnp.int32),
                        pltpu.SemaphoreType.DMA((2,)), pltpu.SemaphoreType.DMA((2,)),
                        pltpu.SemaphoreType.DMA(())],
    )
    return pl.pallas_call(
        _experts_kernel,
        grid_spec=grid_spec,
        out_shape=jax.ShapeDtypeStruct((rows, dp), jnp.int32),
        compiler_params=_params(1),
        name="experts",
    )(first_blk, n_blk, counts, x_rows, w_gu, b_gu, w_down, b_down, perm)


def _combine_ln_kernel(x1_ref, yk_ref, gate_ref, g_ref, b_ref, x2_ref, *, alpha):
    gates = gate_ref[...].T
    ffn = gates[:, 0:1] * _unpack_pairs(yk_ref[0]).astype(F32)
    for k in range(1, yk_ref.shape[0]):
        ffn = ffn + gates[:, k:k + 1] * _unpack_pairs(yk_ref[k]).astype(F32)
    x2_ref[...] = _layer_norm(alpha * x1_ref[...] + ffn, g_ref[...], b_ref[...])


def _combine_ln(x1, yk, gates, g, b, alpha):
    t, d = x1.shape
    k = yk.shape[0]
    tile = COMBINE_TILE if t % COMBINE_TILE == 0 else ROW_TILE
    return pl.pallas_call(
        functools.partial(_combine_ln_kernel, alpha=alpha),
        grid=(t // tile,),
        in_specs=[pl.BlockSpec((tile, d), lambda i: (i, 0)),
                  pl.BlockSpec((k, tile, d // 2), lambda i: (0, i, 0)),
                  pl.BlockSpec((gates.shape[0], tile), lambda i: (0, i)),
                  pl.BlockSpec((1, d), lambda i: (0, 0)),
                  pl.BlockSpec((1, d), lambda i: (0, 0))],
        out_specs=pl.BlockSpec((tile, d), lambda i: (i, 0)),
        out_shape=jax.ShapeDtypeStruct((t, d), F32),
        compiler_params=_params(1),
        name="combine_ln",
    )(x1, yk, gates, g.reshape(1, d), b.reshape(1, d))


def _route(top_idx, rank, counts):
    n_exp = counts.shape[0]
    experts = jnp.arange(n_exp, dtype=jnp.int32)
    counts = counts.reshape(n_exp).astype(jnp.int32)
    padded = (counts + MOE_BLOCK - 1) // MOE_BLOCK * MOE_BLOCK
    pad_end = jnp.cumsum(padded)
    pad_start = pad_end - padded
    start = jnp.sum(jnp.where(top_idx[None] == experts[:, None, None], pad_start[:, None, None], 0), axis=0)
    pos = rank + start
    first_blk = (pad_start // MOE_BLOCK).astype(jnp.int32)
    n_blk = (padded // MOE_BLOCK).astype(jnp.int32)
    return pos, first_blk, n_blk, counts


def _moe(x1, x1p, top_idx, gates, rank, counts, layer, w_gu, b_gu, w_down, b_down, g, b, alpha):
    t, d = x1.shape
    n_exp = counts.shape[0]
    pos, first_blk, n_blk, counts = _route(top_idx[:TOP_K], rank[:TOP_K], counts)
    rows = (-(-(t * TOP_K) // MOE_BLOCK) + n_exp) * MOE_BLOCK
    pos3 = pos.reshape(TOP_K, t // SC_CHUNK, SC_CHUNK).transpose(1, 0, 2)
    x_rows = _sc_scatter_rows(x1p, pos3, rows)
    y = _experts(x_rows, first_blk, n_blk, counts, layer, w_gu, b_gu, w_down, b_down)
    yk = _sc_gather_rows(y, pos.reshape(-1)).reshape(TOP_K, t, d // 2)
    return _combine_ln(x1, yk, gates, g, b, alpha)


def kernel(x, ev_w_in, ev_pool_w, ev_pool_scale, ev_conv_w, ev_w_out, od_w_in, od_sink, od_w_out,
           router_w, router_b, exp_w_gu, exp_b_gu, exp_w_down, exp_b_down, ln_g, ln_b):
    bsz, seq, d = x.shape
    t = bsz * seq
    depth = ln_g.shape[0]
    alpha = (2 * depth) ** 0.25
    n_exp = router_w.shape[2]
    c_width = d // 2
    dq_width = d // 2
    dkv_width = D_KV_HEADS * HEAD_DIM
    assert t % ROW_TILE == 0

    de2 = exp_b_gu.shape[-1]
    b_gu = exp_b_gu.reshape(depth, n_exp, de2 // GU_BLOCK, GU_BLOCK // 2, 2)
    b_gu = jnp.swapaxes(b_gu, -1, -2).reshape(depth, n_exp, 1, de2)
    b_down = exp_b_down.reshape(depth, n_exp, 1, -1)

    xt = x.reshape(t, d)
    for layer in range(depth):
        i = layer // 2
        if layer % 2 == 0:
            cat = _even_mix(xt.reshape(bsz, seq, d), ev_w_in[i], ev_pool_w[i], ev_pool_scale[i], ev_conv_w[i])
            cat = cat.reshape(t, -1)
            w_out = ev_w_out[i]
        else:
            qkv, (qd, kd, vd) = _odd_proj(xt, od_w_in[i], c_width, dq_width, dkv_width)
            dils = tuple(dil for _, dil in C_PATTERNS)
            os_, lses = [], []
            for (window, dil), (qc, kc, vc) in zip(C_PATTERNS, qkv):
                o, lse = _attn_c(qc, kc, vc, bsz, window, dil)
                os_.append(o)
                lses.append(lse)
            seq3 = lambda a: a.reshape(bsz, seq, a.shape[1])
            yd = _attn_d(seq3(qd), seq3(kd), seq3(vd), od_sink[i]).reshape(t, dq_width)
            cat = _merge(os_, lses, yd, dils)
            w_out = od_w_out[i]
        x1, x1p, top_idx, gates, rank, counts = _out_ln(cat, w_out, xt, ln_g[layer, 0], ln_b[layer, 0],
                                                        router_w[layer], router_b[layer], alpha)
        xt = _moe(x1, x1p, top_idx, gates, rank, counts, layer, exp_w_gu, b_gu, exp_w_down, b_down,
                  ln_g[layer, 1], ln_b[layer, 1], alpha)
    return xt.reshape(bsz, seq, d)
```
